```python
import math
import jax, jax.numpy as jnp
from jax import lax
import numpy as np

D_MODEL = 1024
BATCH = 8
SEQ = 2048
DEPTH = 2
DEC_BATCH = 8
DEC_SEQ = 8192
PAST_LEN = 128

HEAD_DIM = 64
A_HEADS = 8
A_PATTERNS = ((128, 1), (512, 4), (2048, 16))
B_HEADS = 8
B_KV_HEADS = 2
Q_BLOCK = 128
GRID_W = 64
ROPE_THETA = 500000.0
ROPE_DIM = HEAD_DIM // 4
AXIAL_THETA = 10000.0
C_HEADS = 8
C_KEY = 128
C_VAL = 128
C_CHUNK = 64
D_FF = 2816
CONV_W = 3
ALPHA = (2 * DEPTH) ** 0.25
BETA = (8 * DEPTH) ** -0.25
LN_EPS = 1e-5
RMS_EPS = 1e-6
N_AB = (DEPTH + 1) // 2
N_C = DEPTH // 2
A_W = A_HEADS * HEAD_DIM
B_QW = B_HEADS * HEAD_DIM
B_KVW = B_KV_HEADS * HEAD_DIM
AB_IN = 3 * A_W + B_QW + 2 * B_KVW
AB_OUT = A_W + B_QW
C_W = C_HEADS * C_KEY
C_VW = C_HEADS * C_VAL
C_IN = 3 * C_W + 2 * C_VW

kernel_name = "hybrid_dilated_gqa_hgrn2_encoder"


def _layer_norm(x, g, b):
    xf = x.astype(jnp.float32)
    mu = jnp.mean(xf, -1, keepdims=True)
    var = jnp.mean(jnp.square(xf - mu), -1, keepdims=True)
    return ((xf - mu) * lax.rsqrt(var + LN_EPS) * g + b).astype(x.dtype)


def _rms_norm(x, g):
    xf = x.astype(jnp.float32)
    return (xf * lax.rsqrt(jnp.mean(xf * xf, -1, keepdims=True) + RMS_EPS) * g).astype(x.dtype)


def _rope_angles(pos, dim, theta):
    freqs = theta ** (-(jnp.arange(0, dim, 2, dtype=jnp.float32) / dim))
    ang = pos.astype(jnp.float32)[:, None] * freqs[None, :]
    return jnp.cos(ang), jnp.sin(ang)


def _apply_rope(x, cos, sin):
    half = x.shape[-1] // 2
    xf = x.astype(jnp.float32)
    x1, x2 = xf[..., :half], xf[..., half:]
    c = cos[None, :, None, :]
    s = sin[None, :, None, :]
    return jnp.concatenate([x1 * c - x2 * s, x2 * c + x1 * s], -1).astype(x.dtype)


def _dilated_branch(q, k, v, dil, n):
    B, S, H, D = q.shape
    m = S // dil
    L = n
    nb = -(-m // L)
    M = nb * L
    Bp = B * dil

    def to_cls(t):
        return t.reshape(B, m, dil, H, D).transpose(0, 2, 1, 3, 4).reshape(Bp, m, H, D)

    qc = jnp.pad(to_cls(q), ((0, 0), (0, M - m), (0, 0), (0, 0))).reshape(Bp, nb, L, H, D)

    def band(t):
        tp = jnp.pad(to_cls(t), ((0, 0), (L, M - m + L), (0, 0), (0, 0))).reshape(Bp, nb + 2, L, H, D)
        return jnp.concatenate([tp[:, :-2], tp[:, 1:-1], tp[:, 2:]], axis=2)

    kb, vb = band(k), band(v)
    s = jnp.einsum('bnqhd,bnkhd->bnhqk', qc, kb).astype(jnp.float32) * (D ** -0.5)
    qi = jnp.arange(nb)[:, None] * L + jnp.arange(L)[None, :]
    ki = jnp.arange(nb)[:, None] * L - L + jnp.arange(3 * L)[None, :]
    rel = ki[:, None, :] - qi[:, :, None]
    valid = ((jnp.abs(rel) <= n) & (ki[:, None, :] >= 0) & (ki[:, None, :] < m)) | (rel == 0)
    s = jnp.where(valid[None, :, None], s, -jnp.inf)
    lse = jax.nn.logsumexp(s, axis=-1)
    p = jnp.exp(s - lse[..., None]).astype(v.dtype)
    o = jnp.einsum('bnhqk,bnkhd->bnqhd', p, vb).reshape(Bp, M, H, D)[:, :m]
    lse = lse.transpose(0, 1, 3, 2).reshape(Bp, M, H)[:, :m]
    o = o.reshape(B, dil, m, H, D).transpose(0, 2, 1, 3, 4).reshape(B, S, H, D)
    lse = lse.reshape(B, dil, m, H).transpose(0, 2, 1, 3).reshape(B, S, H)
    return o, lse


def _dilated_attention(q, k, v):
    outs, lses = [], []
    for window, dil in A_PATTERNS:
        o, l = _dilated_branch(q, k, v, dil, window // (2 * dil))
        outs.append(o)
        lses.append(l)
    w = jax.nn.softmax(jnp.stack(lses, 0), axis=0).astype(q.dtype)
    return jnp.einsum('gbsh,gbshd->bshd', w, jnp.stack(outs, 0))


def _gqa_blocks(q, k, v):
    B, S, Hq, D = q.shape
    Hkv = k.shape[2]
    G = Hq // Hkv
    nqb = S // Q_BLOCK
    qb = q.reshape(B, nqb, Q_BLOCK, Hkv, G, D).transpose(1, 0, 2, 3, 4, 5)

    def one_block(qblk):
        s = jnp.einsum('bqkgd,bskd->bkgqs', qblk, k).astype(jnp.float32) * (D ** -0.5)
        p = jax.nn.softmax(s, axis=-1).astype(v.dtype)
        return jnp.einsum('bkgqs,bskd->bqkgd', p, v)

    o = lax.map(one_block, qb)
    return o.transpose(1, 0, 2, 3, 4, 5).reshape(B, S, Hq * D)


def _mixer_ab(x, w_in, w_out, qn, kn):
    B, S, _ = x.shape
    h = x @ w_in
    o1, o2, o3 = A_W, 2 * A_W, 3 * A_W
    o4 = o3 + B_QW
    o5 = o4 + B_KVW
    qa = h[..., :o1].reshape(B, S, A_HEADS, HEAD_DIM)
    ka = h[..., o1:o2].reshape(B, S, A_HEADS, HEAD_DIM)
    va = h[..., o2:o3].reshape(B, S, A_HEADS, HEAD_DIM)
    qb = h[..., o3:o4].reshape(B, S, B_HEADS, HEAD_DIM)
    kb = h[..., o4:o5].reshape(B, S, B_KV_HEADS, HEAD_DIM)
    vb = h[..., o5:].reshape(B, S, B_KV_HEADS, HEAD_DIM)

    cos, sin = _rope_angles(jnp.arange(S), ROPE_DIM, ROPE_THETA)

    def prope(t):
        return jnp.concatenate([_apply_rope(t[..., :ROPE_DIM], cos, sin), t[..., ROPE_DIM:]], -1)

    ya = _dilated_attention(prope(qa), prope(ka), va).reshape(B, S, A_W)

    rows = S // GRID_W
    row = jnp.broadcast_to(jnp.arange(rows)[:, None], (rows, GRID_W)).reshape(S)
    col = jnp.broadcast_to(jnp.arange(GRID_W)[None, :], (rows, GRID_W)).reshape(S)
    half = HEAD_DIM // 2
    cr, sr = _rope_angles(row, half, AXIAL_THETA)
    cc, sc = _rope_angles(col, half, AXIAL_THETA)

    def arope(t):
        return jnp.concatenate([_apply_rope(t[..., :half], cr, sr), _apply_rope(t[..., half:], cc, sc)], -1)

    qb = arope(_rms_norm(qb, qn))
    kb = arope(_rms_norm(kb, kn))
    yb = _gqa_blocks(qb, kb, vb)
    return jnp.concatenate([ya, yb], -1) @ w_out


def _hgrn2_scan(q, k, v, logf):
    B, S, H, K = q.shape
    V = v.shape[-1]
    nc = S // C_CHUNK

    def chunks(t):
        return t.reshape(B, nc, C_CHUNK, H, t.shape[-1]).swapaxes(0, 1)

    lower = jnp.tril(jnp.ones((C_CHUNK, C_CHUNK), dtype=bool))[None, :, :, None, None]

    def step(state, inp):
        qc, kc, vc, gc = inp
        b = jnp.cumsum(gc, axis=1)
        o_inter = jnp.einsum('bthk,bhkv->bthv', qc * jnp.exp(b), state)
        dec = jnp.where(lower, jnp.exp(jnp.minimum(b[:, :, None] - b[:, None, :], 0.0)), 0.0)
        attn = jnp.einsum('bthk,btshk,bshk->bths', qc, dec, kc)
        o_intra = jnp.einsum('bths,bshv->bthv', attn, vc)
        b_last = b[:, -1]
        state = jnp.exp(b_last)[..., None] * state + jnp.einsum(
            'bshk,bshv->bhkv', kc * jnp.exp(b_last[:, None] - b), vc)
        return state, o_inter + o_intra

    s0 = jnp.zeros((B, H, K, V), jnp.float32)
    _, o = lax.scan(step, s0, (chunks(q), chunks(k), chunks(v), chunks(logf)))
    return o.swapaxes(0, 1).reshape(B, S, H, V)


def _mixer_hgrn2(x, w_in, w_out, lb_fwd, lb_bwd, gn, layer):
    B, S, _ = x.shape
    h = x @ w_in
    q = h[..., :C_W].reshape(B, S, C_HEADS, C_KEY).astype(jnp.float32)
    zf = h[..., C_W:2 * C_W].reshape(B, S, C_HEADS, C_KEY).astype(jnp.float32)
    zb = h[..., 2 * C_W:3 * C_W].reshape(B, S, C_HEADS, C_KEY).astype(jnp.float32)
    i = h[..., 3 * C_W:3 * C_W + C_VW].reshape(B, S, C_HEADS, C_VAL).astype(jnp.float32)
    g = h[..., 3 * C_W + C_VW:]

    def lower_bound(tbl):
        p = jax.nn.softmax(tbl.astype(jnp.float32), axis=0)
        return (jnp.cumsum(p, axis=0) - p[0])[layer].reshape(C_HEADS, C_KEY)

    def gates(z, lb):
        f = lb + (1.0 - lb) * jax.nn.sigmoid(z)
        return jnp.log(f), 1.0 - f

    v = jax.nn.silu(i)
    logf_f, k_f = gates(zf, lower_bound(lb_fwd))
    logf_b, k_b = gates(zb, lower_bound(lb_bwd))
    o_f = _hgrn2_scan(q, k_f, v, logf_f)
    flip = lambda t: jnp.flip(t, axis=1)
    o_b = flip(_hgrn2_scan(flip(q), flip(k_b), flip(v), flip(logf_b)))
    o = _rms_norm(o_f + o_b, gn.reshape(C_HEADS, C_VAL)).astype(x.dtype)
    o = o.reshape(B, S, C_VW) * jax.nn.silu(g)
    return o @ w_out


def _conv_ffn(x, w_up, conv_w, conv_b, w_down):
    h = x @ w_up
    u, g = h[..., :D_FF], h[..., D_FF:]
    gp = jnp.pad(g, ((0, 0), (1, 1), (0, 0)))
    g = gp[:, :-2] * conv_w[0] + gp[:, 1:-1] * conv_w[1] + gp[:, 2:] * conv_w[2] + conv_b
    return (jax.nn.gelu(g) * u) @ w_down


def _trunk(x, w_in_ab, w_out_ab, qn_ab, kn_ab, w_in_c, w_out_c, lb_fwd, lb_bwd, gn_c,
           ln_mix_g, ln_mix_b, ln_ffn_g, ln_ffn_b, ffn_w_up, ffn_conv_w, ffn_conv_b, ffn_w_down):
    for l in range(DEPTH):
        j = l // 2
        if l % 2 == 0:
            y = _mixer_ab(x, w_in_ab[j], w_out_ab[j], qn_ab[j], kn_ab[j])
        else:
            y = _mixer_hgrn2(x, w_in_c[j], w_out_c[j], lb_fwd, lb_bwd, gn_c[j], l)
        x = _layer_norm(ALPHA * x + y, ln_mix_g[l], ln_mix_b[l])
        y = _conv_ffn(x, ffn_w_up[l], ffn_conv_w[l], ffn_conv_b[l], ffn_w_down[l])
        x = _layer_norm(ALPHA * x + y, ln_ffn_g[l], ln_ffn_b[l])
    return x


def setup_inputs(seed: int = 0) -> dict:
    key = jax.random.key(seed)
    ks = jax.random.split(key, 24)

    def nrm(k, shape, scale):
        return jax.random.normal(k, shape, jnp.float32) * scale

    ab_cols = jnp.ones((AB_IN,), jnp.float32).at[2 * A_W:3 * A_W].set(BETA).at[3 * A_W + B_QW + B_KVW:].set(BETA)
    c_cols = jnp.ones((C_IN,), jnp.float32).at[3 * C_W:3 * C_W + C_VW].set(BETA)
    return {
        "x_prompt": nrm(ks[0], (BATCH, SEQ, D_MODEL), 1.0),
        "x_sample": nrm(ks[1], (DEC_BATCH, DEC_SEQ, D_MODEL), 1.0),
        "w_in_ab": nrm(ks[2], (N_AB, D_MODEL, AB_IN), D_MODEL ** -0.5) * ab_cols,
        "w_out_ab": nrm(ks[3], (N_AB, AB_OUT, D_MODEL), BETA * AB_OUT ** -0.5),
        "qn_ab": 1.0 + nrm(ks[4], (N_AB, HEAD_DIM), 0.01),
        "kn_ab": 1.0 + nrm(ks[5], (N_AB, HEAD_DIM), 0.01),
        "w_in_c": nrm(ks[6], (N_C, D_MODEL, C_IN), D_MODEL ** -0.5) * c_cols,
        "w_out_c": nrm(ks[7], (N_C, C_VW, D_MODEL), BETA * C_VW ** -0.5),
        "lb_fwd": nrm(ks[8], (DEPTH, C_W), 0.1),
        "lb_bwd": nrm(ks[9], (DEPTH, C_W), 0.1),
        "gn_c": 1.0 + nrm(ks[10], (N_C, C_VW), 0.01),
        "ln_mix_g": 1.0 + nrm(ks[11], (DEPTH, D_MODEL), 0.01),
        "ln_mix_b": nrm(ks[12], (DEPTH, D_MODEL), 0.01),
        "ln_ffn_g": 1.0 + nrm(ks[13], (DEPTH, D_MODEL), 0.01),
        "ln_ffn_b": nrm(ks[14], (DEPTH, D_MODEL), 0.01),
        "ffn_w_up": nrm(ks[15], (DEPTH, D_MODEL, 2 * D_FF), D_MODEL ** -0.5),
        "ffn_conv_w": nrm(ks[16], (DEPTH, CONV_W, D_FF), CONV_W ** -0.5),
        "ffn_conv_b": nrm(ks[17], (DEPTH, D_FF), 0.01),
        "ffn_w_down": nrm(ks[18], (DEPTH, D_FF, D_MODEL), BETA * D_FF ** -0.5),
    }


def reference(x_prompt, x_sample, w_in_ab, w_out_ab, qn_ab, kn_ab, w_in_c, w_out_c, lb_fwd, lb_bwd,
              gn_c, ln_mix_g, ln_mix_b, ln_ffn_g, ln_ffn_b, ffn_w_up, ffn_conv_w, ffn_conv_b, ffn_w_down):
    y_prompt = _trunk(x_prompt, w_in_ab, w_out_ab, qn_ab, kn_ab, w_in_c, w_out_c, lb_fwd, lb_bwd, gn_c,
                      ln_mix_g, ln_mix_b, ln_ffn_g, ln_ffn_b, ffn_w_up, ffn_conv_w, ffn_conv_b, ffn_w_down)
    y_sample = _trunk(x_sample, w_in_ab, w_out_ab, qn_ab, kn_ab, w_in_c, w_out_c, lb_fwd, lb_bwd, gn_c,
                      ln_mix_g, ln_mix_b, ln_ffn_g, ln_ffn_b, ffn_w_up, ffn_conv_w, ffn_conv_b, ffn_w_down)
    return (y_prompt, y_sample)
```

```python
import functools

import jax
import jax.numpy as jnp
from jax import lax
from jax.experimental import pallas as pl
from jax.experimental.pallas import tpu as pltpu

F32 = jnp.float32
BF16 = jnp.bfloat16

D_MODEL = 1024
DEPTH = 2
HEAD_DIM = 64
A_HEADS = 8
A_PATTERNS = ((128, 1), (512, 4), (2048, 16))
B_HEADS = 8
B_KV_HEADS = 2
GRID_W = 64
ROPE_THETA = 500000.0
ROPE_DIM = HEAD_DIM // 4
AXIAL_THETA = 10000.0
C_HEADS = 8
C_KEY = 128
C_VAL = 128
D_FF = 2816
ALPHA = (2 * DEPTH) ** 0.25
LN_EPS = 1e-5
RMS_EPS = 1e-6
A_W = A_HEADS * HEAD_DIM
B_QW = B_HEADS * HEAD_DIM
B_KVW = B_KV_HEADS * HEAD_DIM
C_W = C_HEADS * C_KEY
C_VW = C_HEADS * C_VAL

LANES = 128
HALF_WINDOW = 64
TM = 512
BAND_L = 128
GQA_TQ = 256
GQA_TK = 512
FF_CHUNK = 1408
HG_CHUNK = 64
HG_SUB = 16
HG_TB = 256
NEG = -1e30
VMEM_LIMIT = 56 * 1024 * 1024

_NT = (((1,), (1,)), ((), ()))
_TN = (((0,), (0,)), ((), ()))


def _params(sem, vmem=None):
    return pltpu.CompilerParams(dimension_semantics=sem, vmem_limit_bytes=vmem)


def _const_spec(shape, single=False):
    nd = len(shape)
    if single:
        return pl.BlockSpec(shape, lambda *_: (0,) * nd, pipeline_mode=pl.Buffered(1))
    return pl.BlockSpec(shape, lambda *_: (0,) * nd)


def _layer_norm(z, g, b):
    mu = jnp.mean(z, -1, keepdims=True)
    d = z - mu
    var = jnp.mean(d * d, -1, keepdims=True)
    return d * lax.rsqrt(var + LN_EPS) * g + b


def _sigmoid(z):
    return 1.0 / (1.0 + jnp.exp(-z))


def _rope_tables(seq):
    pos = jnp.arange(seq, dtype=F32)
    d = jnp.arange(LANES) % HEAD_DIM

    def angles(p, dim, theta):
        freqs = theta ** (-(jnp.arange(0, dim, 2, dtype=F32) / dim))
        return p[:, None] * freqs[None, :]

    h = ROPE_DIM // 2
    ang = angles(pos, ROPE_DIM, ROPE_THETA)
    a = ang[:, d % h]
    lo, hi = d < h, (d >= h) & (d < ROPE_DIM)
    pc = jnp.where(lo | hi, jnp.cos(a), 1.0)
    psa = jnp.where(lo, -jnp.sin(a), 0.0)
    psb = jnp.where(hi, jnp.sin(a), 0.0)

    q = HEAD_DIM // 4
    row = jnp.floor(pos / GRID_W)
    col = pos - row * GRID_W
    ar = angles(row, HEAD_DIM // 2, AXIAL_THETA)[:, d % q]
    ac_ = angles(col, HEAD_DIM // 2, AXIAL_THETA)[:, d % q]
    a2 = jnp.where(d < HEAD_DIM // 2, ar, ac_)
    first = (d % (HEAD_DIM // 2)) < q
    ac = jnp.cos(a2)
    asa = jnp.where(first, -jnp.sin(a2), 0.0)
    asb = jnp.where(first, 0.0, jnp.sin(a2))
    return [t.astype(F32) for t in (pc, psa, psb, ac, asa, asb)]


def _rope(seg, c, sa, sb, shift):
    return seg * c + pltpu.roll(seg, LANES - shift, 1) * sa + pltpu.roll(seg, shift, 1) * sb


def _ab_in_kernel(x_ref, w_ref, pc_ref, psa_ref, psb_ref, ac_ref, asa_ref, asb_ref, qg_ref, kg_ref,
                  ones_ref, qa_ref, ka_ref, va_ref, qb_ref, kd_ref, vd_ref):
    xb = x_ref[...].astype(BF16)

    def proj(a, n):
        return jnp.dot(xb, w_ref[:, a:a + n], preferred_element_type=F32)

    pc, psa, psb = pc_ref[...], psa_ref[...], psb_ref[...]
    ac, asa, asb = ac_ref[...], asa_ref[...], asb_ref[...]

    def partial_rope(h, out_ref):
        for g in range(h.shape[1] // LANES):
            sl = slice(g * LANES, (g + 1) * LANES)
            out_ref[:, sl] = _rope(h[:, sl], pc, psa, psb, ROPE_DIM // 2).astype(BF16)

    def norm_rope(h, gain, out_ref):
        for g in range(h.shape[1] // LANES):
            sl = slice(g * LANES, (g + 1) * LANES)
            seg = h[:, sl]
            sq = seg * seg
            hi = sq.astype(BF16)
            lo = (sq - hi.astype(F32)).astype(BF16)
            ss = (jnp.dot(hi, ones_ref[...], preferred_element_type=F32)
                  + jnp.dot(lo, ones_ref[...], preferred_element_type=F32))
            y = seg * lax.rsqrt(ss * (1.0 / HEAD_DIM) + RMS_EPS) * gain
            out_ref[:, sl] = _rope(y, ac, asa, asb, HEAD_DIM // 4).astype(BF16)

    partial_rope(proj(0, A_W), qa_ref)
    partial_rope(proj(A_W, A_W), ka_ref)
    va_ref[...] = proj(2 * A_W, A_W).astype(BF16)
    norm_rope(proj(3 * A_W, B_QW), qg_ref[...], qb_ref)
    norm_rope(proj(3 * A_W + B_QW, 2 * B_KVW), kg_ref[...], kd_ref)
    vd_ref[...] = proj(3 * A_W + B_QW + 2 * B_KVW, 2 * B_KVW).astype(BF16)


def _ab_in(x2d, w_ext, tabs, qg, kg, ones2, seq):
    t = x2d.shape[0]
    nt = seq // TM
    tab = pl.BlockSpec((TM, LANES), lambda i: (i % nt, 0))
    row = lambda n: pl.BlockSpec((TM, n), lambda i: (i, 0))
    wn = w_ext.shape[1]
    return pl.pallas_call(
        _ab_in_kernel,
        grid=(t // TM,),
        in_specs=[row(D_MODEL), _const_spec((D_MODEL, wn))] + [tab] * 6
        + [_const_spec((1, LANES)), _const_spec((1, LANES)), _const_spec((LANES, LANES))],
        out_specs=[row(A_W)] * 3 + [row(B_QW), row(2 * B_KVW), row(2 * B_KVW)],
        out_shape=[jax.ShapeDtypeStruct((t, A_W), BF16)] * 3
        + [jax.ShapeDtypeStruct((t, B_QW), BF16)]
        + [jax.ShapeDtypeStruct((t, 2 * B_KVW), BF16)] * 2,
        compiler_params=_params(("arbitrary",), VMEM_LIMIT),
        name="ab_in",
    )(x2d, w_ext, *tabs, qg, kg, ones2)


def _band_kernel(q_ref, kp_ref, kc_ref, kn_ref, vp_ref, vc_ref, vn_ref, o_ref, lse_ref, *, m):
    nk = BAND_L + 2 * HALF_WINDOW
    base = pl.program_id(2) * BAND_L
    r = lax.broadcasted_iota(jnp.int32, (BAND_L, nk), 0)
    j = lax.broadcasted_iota(jnp.int32, (BAND_L, nk), 1)
    rel = j - HALF_WINDOW - r
    kpos = base - HALF_WINDOW + j
    bias = jnp.where(jnp.abs(rel) <= HALF_WINDOW, 0.0, NEG)
    bias = jnp.where(kpos >= 0, bias, NEG)
    bias = jnp.where(kpos < m, bias, NEG)
    left = lax.broadcasted_iota(jnp.int32, (BAND_L, LANES), 1) < HEAD_DIM
    for p in range(A_W // LANES):
        sl = slice(p * LANES, (p + 1) * LANES)
        q = q_ref[0, :, sl]
        k = jnp.concatenate([kp_ref[0, :, sl], kc_ref[0, :, sl], kn_ref[0, :, sl]], axis=0)
        v = jnp.concatenate([vp_ref[0, :, sl], vc_ref[0, :, sl], vn_ref[0, :, sl]], axis=0)
        outs, lses = [], []
        for sel in (left, jnp.logical_not(left)):
            qm = jnp.where(sel, q, jnp.zeros_like(q))
            s = lax.dot_general(qm, k, _NT, preferred_element_type=F32) + bias
            mx = jnp.max(s, axis=1, keepdims=True)
            pe = jnp.exp(s - mx)
            l = jnp.sum(pe, axis=1, keepdims=True)
            o = jnp.dot(pe.astype(BF16), v, preferred_element_type=F32)
            outs.append(o * (1.0 / l))
            lses.append(mx + jnp.log(l))
        o_ref[0, :, sl] = jnp.where(left, outs[0], outs[1]).astype(BF16)
        lse_ref[0, :, sl] = jnp.where(left, lses[0], lses[1])


def _band(q, k, v, batch, seq, dil):
    m = seq // dil
    nh = m // HALF_WINDOW
    ratio = BAND_L // HALF_WINDOW
    view = lambda t: t.reshape(batch, m, dil * A_W)
    cur = pl.BlockSpec((1, BAND_L, A_W), lambda b, c, i: (b, i, c))
    prev = pl.BlockSpec((1, HALF_WINDOW, A_W), lambda b, c, i: (b, jnp.maximum(i * ratio - 1, 0), c))
    nxt = pl.BlockSpec((1, HALF_WINDOW, A_W), lambda b, c, i: (b, jnp.minimum((i + 1) * ratio, nh - 1), c))
    o, lse = pl.pallas_call(
        functools.partial(_band_kernel, m=m),
        grid=(batch, dil, m // BAND_L),
        in_specs=[cur, prev, cur, nxt, prev, cur, nxt],
        out_specs=[cur, cur],
        out_shape=[jax.ShapeDtypeStruct((batch, m, dil * A_W), BF16),
                   jax.ShapeDtypeStruct((batch, m, dil * A_W), F32)],
        compiler_params=_params(("arbitrary",) * 3),
        name=f"band_d{dil}",
    )(view(q), view(k), view(k), view(k), view(v), view(v), view(v))
    return o.reshape(batch * seq, A_W), lse.reshape(batch * seq, A_W)


def _gqa_kernel(q_ref, k_ref, v_ref, o_ref, qs_ref, m_ref, l_ref, acc_ref):
    ki = pl.program_id(3)
    left = lax.broadcasted_iota(jnp.int32, (GQA_TQ, LANES), 1) < HEAD_DIM

    @pl.when(ki == 0)
    def _():
        for pr in range(2):
            qp = q_ref[0, :, pr * LANES:(pr + 1) * LANES]
            zero = jnp.zeros_like(qp)
            qs_ref[(2 * pr) * GQA_TQ:(2 * pr + 1) * GQA_TQ, :] = jnp.where(left, qp, zero)
            qs_ref[(2 * pr + 1) * GQA_TQ:(2 * pr + 2) * GQA_TQ, :] = jnp.where(left, zero, qp)
        m_ref[...] = jnp.full(m_ref.shape, NEG, F32)
        l_ref[...] = jnp.zeros(l_ref.shape, F32)
        acc_ref[...] = jnp.zeros(acc_ref.shape, F32)

    s = lax.dot_general(qs_ref[...], k_ref[0], _NT, preferred_element_type=F32)
    m_prev = m_ref[...]
    m_new = jnp.maximum(m_prev, jnp.max(s, axis=1, keepdims=True))
    alpha = jnp.exp(m_prev - m_new)
    p = jnp.exp(s - m_new)
    l_ref[...] = alpha * l_ref[...] + jnp.sum(p, axis=1, keepdims=True)
    acc_ref[...] = alpha * acc_ref[...] + jnp.dot(p.astype(BF16), v_ref[0], preferred_element_type=F32)
    m_ref[...] = m_new

    @pl.when(ki == pl.num_programs(3) - 1)
    def _():
        a = acc_ref[...] * (1.0 / l_ref[...])
        for pr in range(2):
            lo = a[(2 * pr) * GQA_TQ:(2 * pr + 1) * GQA_TQ]
            hi = a[(2 * pr + 1) * GQA_TQ:(2 * pr + 2) * GQA_TQ]
            o_ref[0, :, pr * LANES:(pr + 1) * LANES] = jnp.where(left, lo, hi).astype(BF16)


def _gqa(qb, kd, vd, batch, seq):
    gw = B_QW // B_KV_HEADS
    rows = 4 * GQA_TQ
    out = pl.pallas_call(
        _gqa_kernel,
        grid=(batch, B_KV_HEADS, seq // GQA_TQ, seq // GQA_TK),
        in_specs=[pl.BlockSpec((1, GQA_TQ, gw), lambda b, g, qi, ki: (b, qi, g)),
                  pl.BlockSpec((1, GQA_TK, LANES), lambda b, g, qi, ki: (b, ki, g)),
                  pl.BlockSpec((1, GQA_TK, LANES), lambda b, g, qi, ki: (b, ki, g))],
        out_specs=pl.BlockSpec((1, GQA_TQ, gw), lambda b, g, qi, ki: (b, qi, g)),
        out_shape=jax.ShapeDtypeStruct((batch, seq, B_QW), BF16),
        scratch_shapes=[pltpu.VMEM((rows, LANES), BF16), pltpu.VMEM((rows, 1), F32),
                        pltpu.VMEM((rows, 1), F32), pltpu.VMEM((rows, LANES), F32)],
        compiler_params=_params(("arbitrary",) * 4),
        name="gqa",
    )(qb.reshape(batch, seq, B_QW), kd.reshape(batch, seq, 2 * B_KVW), vd.reshape(batch, seq, 2 * B_KVW))
    return out.reshape(batch * seq, B_QW)


def _ab_out_kernel(o1_ref, o2_ref, o3_ref, l1_ref, l2_ref, l3_ref, yb_ref, x_ref, w_ref, g_ref, b_ref, out_ref):
    l1, l2, l3 = l1_ref[...], l2_ref[...], l3_ref[...]
    mx = jnp.maximum(jnp.maximum(l1, l2), l3)
    e1, e2, e3 = jnp.exp(l1 - mx), jnp.exp(l2 - mx), jnp.exp(l3 - mx)
    ya = (e1 * o1_ref[...].astype(F32) + e2 * o2_ref[...].astype(F32) + e3 * o3_ref[...].astype(F32))
    ya = ya * (1.0 / (e1 + e2 + e3))
    y = jnp.dot(ya.astype(BF16), w_ref[0:A_W, :], preferred_element_type=F32)
    y = y + jnp.dot(yb_ref[...], w_ref[A_W:A_W + B_QW, :], preferred_element_type=F32)
    out_ref[...] = _layer_norm(ALPHA * x_ref[...] + y, g_ref[...], b_ref[...])


def _ab_out(os_, lses, yb, x2d, w, g, b):
    t = x2d.shape[0]
    row = lambda n: pl.BlockSpec((TM, n), lambda i: (i, 0))
    return pl.pallas_call(
        _ab_out_kernel,
        grid=(t // TM,),
        in_specs=[row(A_W)] * 7 + [row(D_MODEL), _const_spec((A_W + B_QW, D_MODEL)),
                                   _const_spec((1, D_MODEL)), _const_spec((1, D_MODEL))],
        out_specs=row(D_MODEL),
        out_shape=jax.ShapeDtypeStruct((t, D_MODEL), F32),
        compiler_params=_params(("arbitrary",), VMEM_LIMIT),
        name="ab_out",
    )(*os_, *lses, yb, x2d, w, g, b)


def _gelu(x):
    return 0.5 * x * (1.0 + jnp.tanh(0.7978845608028654 * (x + 0.044715 * (x * x * x))))


def _ffn_kernel(x_ref, xp_ref, xn_ref, wu_ref, cw_ref, cb_ref, wd_ref, g_ref, b_ref, out_ref, *, nt):
    i = pl.program_id(0)
    keep_prev = (i % nt != 0).astype(F32)
    keep_next = (i % nt != nt - 1).astype(F32)
    x = x_ref[...]
    xb = x.astype(BF16)
    halo = jnp.concatenate([xp_ref[...], xn_ref[...]], axis=0).astype(BF16)
    xe = jnp.concatenate([xb, halo], axis=0)
    rows = lax.broadcasted_iota(jnp.int32, (TM, 1), 0)
    acc = None
    for c in range(D_FF // FF_CHUNK):
        a = c * FF_CHUNK
        u = jnp.dot(xb, wu_ref[:, a:a + FF_CHUNK], preferred_element_type=F32)
        ge = jnp.dot(xe, wu_ref[:, D_FF + a:D_FF + a + FF_CHUNK], preferred_element_type=F32)
        gm = ge[0:TM]
        g_before = ge[TM + 7:TM + 8] * keep_prev
        g_after = ge[TM + 8:TM + 9] * keep_next
        gp = jnp.where(rows == 0, g_before, pltpu.roll(gm, 1, 0))
        gn = jnp.where(rows == TM - 1, g_after, pltpu.roll(gm, TM - 1, 0))
        cw = cw_ref[:, a:a + FF_CHUNK]
        gc = gp * cw[0:1] + gm * cw[1:2] + gn * cw[2:3] + cb_ref[:, a:a + FF_CHUNK]
        act = (_gelu(gc) * u).astype(BF16)
        part = jnp.dot(act, wd_ref[a:a + FF_CHUNK, :], preferred_element_type=F32)
        acc = part if acc is None else acc + part
    out_ref[...] = _layer_norm(ALPHA * x + acc, g_ref[...], b_ref[...])


def _ffn(x2d, wu, cw, cb, wd, g, b, seq):
    t = x2d.shape[0]
    nt = seq // TM
    r8 = TM // 8
    return pl.pallas_call(
        functools.partial(_ffn_kernel, nt=nt),
        grid=(t // TM,),
        in_specs=[pl.BlockSpec((TM, D_MODEL), lambda i: (i, 0)),
                  pl.BlockSpec((8, D_MODEL), lambda i: (jnp.maximum(i * r8 - 1, 0), 0)),
                  pl.BlockSpec((8, D_MODEL), lambda i: (jnp.minimum((i + 1) * r8, t // 8 - 1), 0)),
                  _const_spec((D_MODEL, 2 * D_FF), single=True),
                  _const_spec((3, D_FF)), _const_spec((1, D_FF)),
                  _const_spec((D_FF, D_MODEL), single=True),
                  _const_spec((1, D_MODEL)), _const_spec((1, D_MODEL))],
        out_specs=pl.BlockSpec((TM, D_MODEL), lambda i: (i, 0)),
        out_shape=jax.ShapeDtypeStruct((t, D_MODEL), F32),
        compiler_params=_params(("arbitrary",), VMEM_LIMIT),
        name="ffn",
    )(x2d, x2d, x2d, wu, cw, cb, wd, g, b)


def _lower_bound(tbl, layer):
    rows = [tbl[r:r + 1] for r in range(DEPTH)]
    mx = functools.reduce(jnp.maximum, rows)
    es = [jnp.exp(r - mx) for r in rows]
    inv = 1.0 / functools.reduce(lambda a, b: a + b, es)
    ps = [e * inv for e in es]
    return functools.reduce(lambda a, b: a + b, ps[:layer + 1]) - ps[0]


def _c_in_kernel(x_ref, w_ref, lbf_ref, lbb_ref, q_ref, lf_ref, lb_ref, v_ref, g_ref, *, layer):
    xb = x_ref[...].astype(BF16)
    half = C_W // 2

    def proj(a, n):
        return jnp.dot(xb, w_ref[:, a:a + n], preferred_element_type=F32)

    def silu(z):
        return z * _sigmoid(z)

    lbf = _lower_bound(lbf_ref[...], layer)
    lbb = _lower_bound(lbb_ref[...], layer)
    for c in range(2):
        a = c * half
        sl = slice(a, a + half)
        q_ref[:, sl] = proj(a, half).astype(BF16)
        for tbl, off, out_ref in ((lbf, C_W, lf_ref), (lbb, 2 * C_W, lb_ref)):
            lb = tbl[:, sl]
            out_ref[:, sl] = jnp.log(lb + (1.0 - lb) * _sigmoid(proj(off + a, half)))
        v_ref[:, sl] = silu(proj(3 * C_W + a, half)).astype(BF16)
        g_ref[:, sl] = silu(proj(3 * C_W + C_VW + a, half)).astype(BF16)


def _c_in(x2d, w, lbf, lbb, layer):
    t = x2d.shape[0]
    row = pl.BlockSpec((TM, C_W), lambda i: (i, 0))
    return pl.pallas_call(
        functools.partial(_c_in_kernel, layer=layer),
        grid=(t // TM,),
        in_specs=[row, _const_spec(w.shape), _const_spec((DEPTH, C_W)), _const_spec((DEPTH, C_W))],
        out_specs=[row] * 5,
        out_shape=[jax.ShapeDtypeStruct((t, C_W), d) for d in (BF16, F32, F32, BF16, BF16)],
        compiler_params=_params(("arbitrary",), VMEM_LIMIT),
        name="c_in",
    )(x2d, w, lbf, lbb)


def _hgrn_chunk(qf, lf, vb, st_t, tri, trib, rev):
    hi = lf.astype(BF16)
    r1 = lf - hi.astype(F32)
    mid = r1.astype(BF16)
    lo = (r1 - mid.astype(F32)).astype(BF16)
    dotf = lambda a, b: jnp.dot(a, b, preferred_element_type=F32)
    b = dotf(trib, hi) + dotf(trib, mid) + dotf(trib, lo)
    bex = b - lf
    kk = 1.0 - jnp.exp(lf)
    o = lax.dot_general((qf * jnp.exp(b)).astype(BF16), st_t.astype(BF16), _NT, preferred_element_type=F32)
    btot = b[0:1] if rev else b[HG_CHUNK - 1:HG_CHUNK]
    kdec = (kk * jnp.exp(btot - b)).astype(BF16)
    a_rows = []
    for blk in range(HG_CHUNK // HG_SUB):
        r0 = blk * HG_SUB
        ref = bex[r0 + HG_SUB - 1:r0 + HG_SUB] if rev else bex[r0:r0 + 1]
        qt = (qf[r0:r0 + HG_SUB] * jnp.exp(b[r0:r0 + HG_SUB] - ref)).astype(BF16)
        lo_r, hi_r = (r0, HG_CHUNK) if rev else (0, r0 + HG_SUB)
        ks = (kk[lo_r:hi_r] * jnp.exp(ref - b[lo_r:hi_r])).astype(BF16)
        pieces = []
        if lo_r > 0:
            pieces.append(jnp.zeros((lo_r, C_KEY), BF16))
        pieces.append(ks)
        if hi_r < HG_CHUNK:
            pieces.append(jnp.zeros((HG_CHUNK - hi_r, C_KEY), BF16))
        kfull = jnp.concatenate(pieces, axis=0) if len(pieces) > 1 else ks
        a_rows.append(lax.dot_general(qt, kfull, _NT, preferred_element_type=F32))
    attn = jnp.where(tri, jnp.concatenate(a_rows, axis=0), 0.0)
    o = o + dotf(attn.astype(BF16), vb)
    st_new = st_t * jnp.exp(btot) + lax.dot_general(vb, kdec, _TN, preferred_element_type=F32)
    return o, st_new


def _hgrn_kernel(*refs, rev, fused):
    if fused:
        q_ref, lf_ref, v_ref, of_ref, gate_ref, gn_ref, y_ref, st_ref = refs
    else:
        q_ref, lf_ref, v_ref, y_ref, st_ref = refs

    @pl.when(pl.program_id(2) == 0)
    def _():
        st_ref[...] = jnp.zeros(st_ref.shape, F32)

    ri = lax.broadcasted_iota(jnp.int32, (HG_CHUNK, HG_CHUNK), 0)
    ci = lax.broadcasted_iota(jnp.int32, (HG_CHUNK, HG_CHUNK), 1)
    tri = (ci >= ri) if rev else (ci <= ri)
    trib = jnp.where(tri, 1.0, 0.0).astype(BF16)
    n_chunks = HG_TB // HG_CHUNK
    order = range(n_chunks - 1, -1, -1) if rev else range(n_chunks)
    st_t = st_ref[...]
    for c in order:
        rows = slice(c * HG_CHUNK, (c + 1) * HG_CHUNK)
        o, st_t = _hgrn_chunk(q_ref[0, rows, :].astype(F32), lf_ref[0, rows, :], v_ref[0, rows, :],
                              st_t, tri, trib, rev)
        if fused:
            tot = of_ref[0, rows, :] + o
            inv = lax.rsqrt(jnp.mean(tot * tot, -1, keepdims=True) + RMS_EPS)
            y_ref[0, rows, :] = (tot * inv * gn_ref[...] * gate_ref[0, rows, :].astype(F32)).astype(BF16)
        else:
            y_ref[0, rows, :] = o
    st_ref[...] = st_t


def _hgrn(q, lf, v, batch, seq, rev, fused_args=None):
    nb = seq // HG_TB
    blk = (lambda b, h, i: (b, nb - 1 - i, h)) if rev else (lambda b, h, i: (b, i, h))
    tile = pl.BlockSpec((1, HG_TB, C_KEY), blk)
    view = lambda t: t.reshape(batch, seq, C_W)
    ins, specs = [view(q), view(lf), view(v)], [tile, tile, tile]
    fused = fused_args is not None
    if fused:
        o_f, gate, gn = fused_args
        ins += [view(o_f), view(gate), gn]
        specs += [tile, tile, pl.BlockSpec((1, C_VAL), lambda b, h, i: (0, h))]
    out = pl.pallas_call(
        functools.partial(_hgrn_kernel, rev=rev, fused=fused),
        grid=(batch, C_HEADS, nb),
        in_specs=specs,
        out_specs=tile,
        out_shape=jax.ShapeDtypeStruct((batch, seq, C_VW), BF16 if fused else F32),
        scratch_shapes=[pltpu.VMEM((C_VAL, C_KEY), F32)],
        compiler_params=_params(("arbitrary",) * 3),
        name="hgrn_bwd" if rev else "hgrn_fwd",
    )(*ins)
    return out.reshape(batch * seq, C_VW)


def _proj_ln_kernel(a_ref, x_ref, w_ref, g_ref, b_ref, out_ref):
    y = jnp.dot(a_ref[...], w_ref[...], preferred_element_type=F32)
    out_ref[...] = _layer_norm(ALPHA * x_ref[...] + y, g_ref[...], b_ref[...])


def _proj_ln(a, x2d, w, g, b):
    t = x2d.shape[0]
    kdim = a.shape[1]
    return pl.pallas_call(
        _proj_ln_kernel,
        grid=(t // TM,),
        in_specs=[pl.BlockSpec((TM, kdim), lambda i: (i, 0)), pl.BlockSpec((TM, D_MODEL), lambda i: (i, 0)),
                  _const_spec((kdim, D_MODEL)), _const_spec((1, D_MODEL)), _const_spec((1, D_MODEL))],
        out_specs=pl.BlockSpec((TM, D_MODEL), lambda i: (i, 0)),
        out_shape=jax.ShapeDtypeStruct((t, D_MODEL), F32),
        compiler_params=_params(("arbitrary",), VMEM_LIMIT),
        name="c_out",
    )(a, x2d, w, g, b)


def _prep_weights(w_in_ab, w_out_ab, qn_ab, kn_ab, w_in_c, w_out_c, gn_c, ffn_w_up, ffn_w_down):
    scale = HEAD_DIM ** -0.5
    ab = []
    for j in range(w_in_ab.shape[0]):
        w = w_in_ab[j]
        o3 = 3 * A_W
        kb = w[:, o3 + B_QW:o3 + B_QW + B_KVW]
        vb = w[:, o3 + B_QW + B_KVW:]
        dup = lambda t: jnp.concatenate([t[:, :HEAD_DIM], t[:, :HEAD_DIM], t[:, HEAD_DIM:], t[:, HEAD_DIM:]], 1)
        w_ext = jnp.concatenate([w[:, :A_W] * scale, w[:, A_W:o3 + B_QW], dup(kb), dup(vb)], 1).astype(BF16)
        qg = (jnp.concatenate([qn_ab[j], qn_ab[j]]) * scale).reshape(1, LANES)
        kg = jnp.concatenate([kn_ab[j], kn_ab[j]]).reshape(1, LANES)
        ab.append((w_ext, qg, kg, w_out_ab[j].astype(BF16)))
    cc = [(w_in_c[j].astype(BF16), w_out_c[j].astype(BF16), gn_c[j].reshape(1, C_VW)) for j in range(w_in_c.shape[0])]
    return ab, cc, ffn_w_up.astype(BF16), ffn_w_down.astype(BF16)


def _trunk(x, prep, lb_fwd, lb_bwd, ln_mix_g, ln_mix_b, ln_ffn_g, ln_ffn_b, ffn_conv_w, ffn_conv_b):
    ab, cc, wu, wd = prep
    batch, seq, _ = x.shape
    x2d = x.reshape(batch * seq, D_MODEL)
    tabs = _rope_tables(seq)
    blk = jnp.arange(LANES) // HEAD_DIM
    ones2 = (blk[:, None] == blk[None, :]).astype(BF16)
    vec = lambda t: t.reshape(1, -1)
    for l in range(DEPTH):
        j = l // 2
        if l % 2 == 0:
            w_ext, qg, kg, w_out = ab[j]
            qa, ka, va, qb, kd, vd = _ab_in(x2d, w_ext, tabs, qg, kg, ones2, seq)
            os_, lses = [], []
            for window, dil in A_PATTERNS:
                assert window // (2 * dil) == HALF_WINDOW
                o, lse = _band(qa, ka, va, batch, seq, dil)
                os_.append(o)
                lses.append(lse)
            yb = _gqa(qb, kd, vd, batch, seq)
            x2d = _ab_out(os_, lses, yb, x2d, w_out, vec(ln_mix_g[l]), vec(ln_mix_b[l]))
        else:
            w_in, w_out, gn = cc[j]
            q, lf, lb, v, gate = _c_in(x2d, w_in, lb_fwd, lb_bwd, l)
            o_f = _hgrn(q, lf, v, batch, seq, rev=False)
            y = _hgrn(q, lb, v, batch, seq, rev=True, fused_args=(o_f, gate, gn))
            x2d = _proj_ln(y, x2d, w_out, vec(ln_mix_g[l]), vec(ln_mix_b[l]))
        x2d = _ffn(x2d, wu[l], ffn_conv_w[l], vec(ffn_conv_b[l]), wd[l], vec(ln_ffn_g[l]), vec(ln_ffn_b[l]), seq)
    return x2d.reshape(batch, seq, D_MODEL)


def kernel(x_prompt, x_sample, w_in_ab, w_out_ab, qn_ab, kn_ab, w_in_c, w_out_c, lb_fwd, lb_bwd, gn_c, ln_mix_g, ln_mix_b, ln_ffn_g, ln_ffn_b, ffn_w_up, ffn_conv_w, ffn_conv_b, ffn_w_down):
    prep = _prep_weights(w_in_ab, w_out_ab, qn_ab, kn_ab, w_in_c, w_out_c, gn_c, ffn_w_up, ffn_w_down)
    rest = (lb_fwd, lb_bwd, ln_mix_g, ln_mix_b, ln_ffn_g, ln_ffn_b, ffn_conv_w, ffn_conv_b)
    return (_trunk(x_prompt, prep, *rest), _trunk(x_sample, prep, *rest))
```

```python
import functools

import jax
import jax.numpy as jnp
from jax import lax
from jax.experimental import pallas as pl
from jax.experimental.pallas import tpu as pltpu

F32 = jnp.float32
BF16 = jnp.bfloat16

D_MODEL = 1024
DEPTH = 2
HEAD_DIM = 64
A_HEADS = 8
A_PATTERNS = ((128, 1), (512, 4), (2048, 16))
B_HEADS = 8
B_KV_HEADS = 2
GRID_W = 64
ROPE_THETA = 500000.0
ROPE_DIM = HEAD_DIM // 4
AXIAL_THETA = 10000.0
C_HEADS = 8
C_KEY = 128
C_VAL = 128
D_FF = 2816
ALPHA = (2 * DEPTH) ** 0.25
LN_EPS = 1e-5
RMS_EPS = 1e-6
A_W = A_HEADS * HEAD_DIM
B_QW = B_HEADS * HEAD_DIM
B_KVW = B_KV_HEADS * HEAD_DIM
C_W = C_HEADS * C_KEY
C_VW = C_HEADS * C_VAL

LANES = 128
HALF_WINDOW = 64
TM = 512
BAND_L = 128
GQA_TQ = 256
GQA_TK = 512
FF_CHUNK = 1408
HG_CHUNK = 64
HG_SUB = 16
HG_TB = 256
HG_HP = 4
NEG = -1e30
LOG2E = 1.4426950408889634
VMEM_LIMIT = 56 * 1024 * 1024

_NT = (((1,), (1,)), ((), ()))
_TN = (((0,), (0,)), ((), ()))


def _params(sem, vmem=None):
    return pltpu.CompilerParams(dimension_semantics=sem, vmem_limit_bytes=vmem)


def _const_spec(shape, single=False):
    nd = len(shape)
    if single:
        return pl.BlockSpec(shape, lambda *_: (0,) * nd, pipeline_mode=pl.Buffered(1))
    return pl.BlockSpec(shape, lambda *_: (0,) * nd)


def _layer_norm(z, g, b):
    mu = jnp.mean(z, -1, keepdims=True)
    d = z - mu
    var = jnp.mean(d * d, -1, keepdims=True)
    return d * lax.rsqrt(var + LN_EPS) * g + b


def _sigmoid(z):
    return 1.0 / (1.0 + jnp.exp(-z))


def _rope_tables(seq):
    pos = jnp.arange(seq, dtype=F32)
    d = jnp.arange(LANES) % HEAD_DIM

    def angles(p, dim, theta):
        freqs = theta ** (-(jnp.arange(0, dim, 2, dtype=F32) / dim))
        return p[:, None] * freqs[None, :]

    h = ROPE_DIM // 2
    ang = angles(pos, ROPE_DIM, ROPE_THETA)
    a = ang[:, d % h]
    lo, hi = d < h, (d >= h) & (d < ROPE_DIM)
    pc = jnp.where(lo | hi, jnp.cos(a), 1.0)
    psa = jnp.where(lo, -jnp.sin(a), 0.0)
    psb = jnp.where(hi, jnp.sin(a), 0.0)

    q = HEAD_DIM // 4
    row = jnp.floor(pos / GRID_W)
    col = pos - row * GRID_W
    ar = angles(row, HEAD_DIM // 2, AXIAL_THETA)[:, d % q]
    ac_ = angles(col, HEAD_DIM // 2, AXIAL_THETA)[:, d % q]
    a2 = jnp.where(d < HEAD_DIM // 2, ar, ac_)
    first = (d % (HEAD_DIM // 2)) < q
    ac = jnp.cos(a2)
    asa = jnp.where(first, -jnp.sin(a2), 0.0)
    asb = jnp.where(first, 0.0, jnp.sin(a2))
    return [t.astype(F32) for t in (pc, psa, psb, ac, asa, asb)]


def _rope(seg, c, sa, sb, shift):
    return seg * c + pltpu.roll(seg, LANES - shift, 1) * sa + pltpu.roll(seg, shift, 1) * sb


def _ab_in_kernel(x_ref, w_ref, pc_ref, psa_ref, psb_ref, ac_ref, asa_ref, asb_ref, qg_ref, kg_ref,
                  ones_ref, qa_ref, ka_ref, va_ref, qb_ref, kd_ref, vd_ref):
    xb = x_ref[...].astype(BF16)

    def proj(a, n):
        return jnp.dot(xb, w_ref[:, a:a + n], preferred_element_type=F32)

    pc, psa, psb = pc_ref[...], psa_ref[...], psb_ref[...]
    ac, asa, asb = ac_ref[...], asa_ref[...], asb_ref[...]

    def partial_rope(h, out_ref):
        for g in range(h.shape[1] // LANES):
            sl = slice(g * LANES, (g + 1) * LANES)
            out_ref[:, sl] = _rope(h[:, sl], pc, psa, psb, ROPE_DIM // 2).astype(BF16)

    def norm_rope(h, gain, out_ref):
        for g in range(h.shape[1] // LANES):
            sl = slice(g * LANES, (g + 1) * LANES)
            seg = h[:, sl]
            sq = seg * seg
            hi = sq.astype(BF16)
            lo = (sq - hi.astype(F32)).astype(BF16)
            ss = (jnp.dot(hi, ones_ref[...], preferred_element_type=F32)
                  + jnp.dot(lo, ones_ref[...], preferred_element_type=F32))
            y = seg * lax.rsqrt(ss * (1.0 / HEAD_DIM) + RMS_EPS) * gain
            out_ref[:, sl] = _rope(y, ac, asa, asb, HEAD_DIM // 4).astype(BF16)

    partial_rope(proj(0, A_W), qa_ref)
    partial_rope(proj(A_W, A_W), ka_ref)
    va_ref[...] = proj(2 * A_W, A_W).astype(BF16)
    norm_rope(proj(3 * A_W, B_QW), qg_ref[...], qb_ref)
    norm_rope(proj(3 * A_W + B_QW, 2 * B_KVW), kg_ref[...], kd_ref)
    vd_ref[...] = proj(3 * A_W + B_QW + 2 * B_KVW, 2 * B_KVW).astype(BF16)


def _ab_in(x2d, w_ext, tabs, qg, kg, ones2, seq):
    t = x2d.shape[0]
    nt = seq // TM
    tab = pl.BlockSpec((TM, LANES), lambda i: (i % nt, 0))
    row = lambda n: pl.BlockSpec((TM, n), lambda i: (i, 0))
    wn = w_ext.shape[1]
    return pl.pallas_call(
        _ab_in_kernel,
        grid=(t // TM,),
        in_specs=[row(D_MODEL), _const_spec((D_MODEL, wn))] + [tab] * 6
        + [_const_spec((1, LANES)), _const_spec((1, LANES)), _const_spec((LANES, LANES))],
        out_specs=[row(A_W)] * 3 + [row(B_QW), row(2 * B_KVW), row(2 * B_KVW)],
        out_shape=[jax.ShapeDtypeStruct((t, A_W), BF16)] * 3
        + [jax.ShapeDtypeStruct((t, B_QW), BF16)]
        + [jax.ShapeDtypeStruct((t, 2 * B_KVW), BF16)] * 2,
        compiler_params=_params(("arbitrary",), VMEM_LIMIT),
        name="ab_in",
    )(x2d, w_ext, *tabs, qg, kg, ones2)


def _band_kernel(q_ref, kp_ref, kc_ref, kn_ref, vp_ref, vc_ref, vn_ref, o_ref, lse_ref, *, m):
    nk = BAND_L + 2 * HALF_WINDOW
    base = pl.program_id(2) * BAND_L
    r = lax.broadcasted_iota(jnp.int32, (BAND_L, nk), 0)
    j = lax.broadcasted_iota(jnp.int32, (BAND_L, nk), 1)
    rel = j - HALF_WINDOW - r
    kpos = base - HALF_WINDOW + j
    bias = jnp.where(jnp.abs(rel) <= HALF_WINDOW, 0.0, NEG)
    bias = jnp.where(kpos >= 0, bias, NEG)
    bias = jnp.where(kpos < m, bias, NEG)
    left = lax.broadcasted_iota(jnp.int32, (BAND_L, LANES), 1) < HEAD_DIM
    for p in range(A_W // LANES):
        sl = slice(p * LANES, (p + 1) * LANES)
        q = q_ref[0, :, sl]
        k = jnp.concatenate([kp_ref[0, :, sl], kc_ref[0, :, sl], kn_ref[0, :, sl]], axis=0)
        v = jnp.concatenate([vp_ref[0, :, sl], vc_ref[0, :, sl], vn_ref[0, :, sl]], axis=0)
        outs, lses = [], []
        for sel in (left, jnp.logical_not(left)):
            qm = jnp.where(sel, q, jnp.zeros_like(q))
            s = lax.dot_general(qm, k, _NT, preferred_element_type=F32) + bias
            mx = jnp.max(s, axis=1, keepdims=True)
            pe = jnp.exp(s - mx)
            l = jnp.sum(pe, axis=1, keepdims=True)
            o = jnp.dot(pe.astype(BF16), v, preferred_element_type=F32)
            outs.append(o * (1.0 / l))
            lses.append(mx + jnp.log(l))
        o_ref[0, :, sl] = jnp.where(left, outs[0], outs[1]).astype(BF16)
        lse_ref[0, :, sl] = jnp.where(left, lses[0], lses[1])


def _band(q, k, v, batch, seq, dil):
    m = seq // dil
    nh = m // HALF_WINDOW
    ratio = BAND_L // HALF_WINDOW
    view = lambda t: t.reshape(batch, m, dil * A_W)
    cur = pl.BlockSpec((1, BAND_L, A_W), lambda b, c, i: (b, i, c))
    prev = pl.BlockSpec((1, HALF_WINDOW, A_W), lambda b, c, i: (b, jnp.maximum(i * ratio - 1, 0), c))
    nxt = pl.BlockSpec((1, HALF_WINDOW, A_W), lambda b, c, i: (b, jnp.minimum((i + 1) * ratio, nh - 1), c))
    o, lse = pl.pallas_call(
        functools.partial(_band_kernel, m=m),
        grid=(batch, dil, m // BAND_L),
        in_specs=[cur, prev, cur, nxt, prev, cur, nxt],
        out_specs=[cur, cur],
        out_shape=[jax.ShapeDtypeStruct((batch, m, dil * A_W), BF16),
                   jax.ShapeDtypeStruct((batch, m, dil * A_W), F32)],
        compiler_params=_params(("arbitrary",) * 3),
        name=f"band_d{dil}",
    )(view(q), view(k), view(k), view(k), view(v), view(v), view(v))
    return o.reshape(batch * seq, A_W), lse.reshape(batch * seq, A_W)


def _gqa_kernel(q_ref, k_ref, v_ref, o_ref, qs_ref, m_ref, acc_ref, s_ref, *, seq):
    left = lax.broadcasted_iota(jnp.int32, (GQA_TQ, LANES), 1) < HEAD_DIM
    for pr in range(2):
        qp = q_ref[0, :, pr * LANES:(pr + 1) * LANES]
        zero = jnp.zeros_like(qp)
        qs_ref[(2 * pr) * GQA_TQ:(2 * pr + 1) * GQA_TQ, :] = jnp.where(left, qp, zero)
        qs_ref[(2 * pr + 1) * GQA_TQ:(2 * pr + 2) * GQA_TQ, :] = jnp.where(left, zero, qp)
    m_ref[...] = jnp.full(m_ref.shape, NEG, F32)
    acc_ref[...] = jnp.zeros(acc_ref.shape, F32)
    ones_lane = lax.broadcasted_iota(jnp.int32, (GQA_TK, LANES), 1) == HEAD_DIM

    halves = [slice(h * 2 * GQA_TQ, (h + 1) * 2 * GQA_TQ) for h in range(2)]

    def scores(tile, buf, cols):
        off = pl.multiple_of(tile * GQA_TK, GQA_TK)
        k = k_ref[0, pl.ds(off, GQA_TK), :]
        s_ref[buf, :, cols] = lax.dot_general(k, qs_ref[cols, :], _NT, preferred_element_type=F32)

    def softmax_pv(tile, buf, cols):
        off = pl.multiple_of(tile * GQA_TK, GQA_TK)
        v = jnp.where(ones_lane, jnp.ones((), BF16), v_ref[0, pl.ds(off, GQA_TK), :])
        s = s_ref[buf, :, cols]
        m_prev = m_ref[:, cols]
        m_new = jnp.maximum(m_prev, jnp.max(s, axis=0, keepdims=True))
        alpha = jnp.exp2(m_prev - m_new)
        p = jnp.exp2((s - m_new).astype(BF16))
        pv = lax.dot_general(v, p, _TN, preferred_element_type=F32)
        acc_ref[:, cols] = alpha * acc_ref[:, cols] + pv
        m_ref[:, cols] = m_new

    def step(tile, buf, prefetch):
        for cols in halves:
            if prefetch:
                scores(tile + 1, 1 - buf, cols)
            softmax_pv(tile, buf, cols)

    def pair(j, carry):
        step(2 * j, 0, True)
        step(2 * j + 1, 1, True)
        return carry

    n_tiles = seq // GQA_TK
    for cols in halves:
        scores(0, 0, cols)
    lax.fori_loop(0, n_tiles // 2 - 1, pair, 0)
    step(n_tiles - 2, 0, True)
    step(n_tiles - 1, 1, False)
    acc = acc_ref[...]
    a = (acc * (1.0 / acc[HEAD_DIM:HEAD_DIM + 1, :])).T
    for pr in range(2):
        lo = a[(2 * pr) * GQA_TQ:(2 * pr + 1) * GQA_TQ]
        hi = pltpu.roll(a[(2 * pr + 1) * GQA_TQ:(2 * pr + 2) * GQA_TQ], HEAD_DIM, 1)
        o_ref[0, :, pr * LANES:(pr + 1) * LANES] = jnp.where(left, lo, hi).astype(BF16)


def _gqa(qb, kd, vd, batch, seq):
    gw = B_QW // B_KV_HEADS
    cols = 4 * GQA_TQ
    out = pl.pallas_call(
        functools.partial(_gqa_kernel, seq=seq),
        grid=(batch, B_KV_HEADS, seq // GQA_TQ),
        in_specs=[pl.BlockSpec((1, GQA_TQ, gw), lambda b, g, qi: (b, qi, g)),
                  pl.BlockSpec((1, seq, LANES), lambda b, g, qi: (b, 0, g)),
                  pl.BlockSpec((1, seq, LANES), lambda b, g, qi: (b, 0, g))],
        out_specs=pl.BlockSpec((1, GQA_TQ, gw), lambda b, g, qi: (b, qi, g)),
        out_shape=jax.ShapeDtypeStruct((batch, seq, B_QW), BF16),
        scratch_shapes=[pltpu.VMEM((cols, LANES), BF16), pltpu.VMEM((1, cols), F32),
                        pltpu.VMEM((LANES, cols), F32), pltpu.VMEM((2, GQA_TK, cols), F32)],
        compiler_params=_params(("arbitrary",) * 3, VMEM_LIMIT),
        name="gqa",
    )(qb.reshape(batch, seq, B_QW), kd.reshape(batch, seq, 2 * B_KVW), vd.reshape(batch, seq, 2 * B_KVW))
    return out.reshape(batch * seq, B_QW)


def _ab_out_kernel(o1_ref, o2_ref, o3_ref, l1_ref, l2_ref, l3_ref, yb_ref, x_ref, w_ref, g_ref, b_ref, out_ref):
    l1, l2, l3 = l1_ref[...], l2_ref[...], l3_ref[...]
    mx = jnp.maximum(jnp.maximum(l1, l2), l3)
    e1, e2, e3 = jnp.exp(l1 - mx), jnp.exp(l2 - mx), jnp.exp(l3 - mx)
    ya = (e1 * o1_ref[...].astype(F32) + e2 * o2_ref[...].astype(F32) + e3 * o3_ref[...].astype(F32))
    ya = ya * (1.0 / (e1 + e2 + e3))
    y = jnp.dot(ya.astype(BF16), w_ref[0:A_W, :], preferred_element_type=F32)
    y = y + jnp.dot(yb_ref[...], w_ref[A_W:A_W + B_QW, :], preferred_element_type=F32)
    out_ref[...] = _layer_norm(ALPHA * x_ref[...] + y, g_ref[...], b_ref[...])


def _ab_out(os_, lses, yb, x2d, w, g, b):
    t = x2d.shape[0]
    row = lambda n: pl.BlockSpec((TM, n), lambda i: (i, 0))
    return pl.pallas_call(
        _ab_out_kernel,
        grid=(t // TM,),
        in_specs=[row(A_W)] * 7 + [row(D_MODEL), _const_spec((A_W + B_QW, D_MODEL)),
                                   _const_spec((1, D_MODEL)), _const_spec((1, D_MODEL))],
        out_specs=row(D_MODEL),
        out_shape=jax.ShapeDtypeStruct((t, D_MODEL), F32),
        compiler_params=_params(("arbitrary",), VMEM_LIMIT),
        name="ab_out",
    )(*os_, *lses, yb, x2d, w, g, b)


def _gelu(x):
    return 0.5 * x * (1.0 + jnp.tanh(0.7978845608028654 * (x + 0.044715 * (x * x * x))))


def _ffn_kernel(x_ref, xp_ref, xn_ref, wu_ref, cw_ref, cb_ref, wd_ref, g_ref, b_ref, out_ref, *, nt):
    i = pl.program_id(0)
    keep_prev = (i % nt != 0).astype(F32)
    keep_next = (i % nt != nt - 1).astype(F32)
    x = x_ref[...]
    xb = x.astype(BF16)
    halo = jnp.concatenate([xp_ref[...], xn_ref[...]], axis=0).astype(BF16)
    xe = jnp.concatenate([xb, halo], axis=0)
    rows = lax.broadcasted_iota(jnp.int32, (TM, 1), 0)
    acc = None
    for c in range(D_FF // FF_CHUNK):
        a = c * FF_CHUNK
        u = jnp.dot(xb, wu_ref[:, a:a + FF_CHUNK], preferred_element_type=F32)
        ge = jnp.dot(xe, wu_ref[:, D_FF + a:D_FF + a + FF_CHUNK], preferred_element_type=F32)
        gm = ge[0:TM]
        g_before = ge[TM + 7:TM + 8] * keep_prev
        g_after = ge[TM + 8:TM + 9] * keep_next
        gp = jnp.where(rows == 0, g_before, pltpu.roll(gm, 1, 0))
        gn = jnp.where(rows == TM - 1, g_after, pltpu.roll(gm, TM - 1, 0))
        cw = cw_ref[:, a:a + FF_CHUNK]
        gc = gp * cw[0:1] + gm * cw[1:2] + gn * cw[2:3] + cb_ref[:, a:a + FF_CHUNK]
        act = (_gelu(gc) * u).astype(BF16)
        part = jnp.dot(act, wd_ref[a:a + FF_CHUNK, :], preferred_element_type=F32)
        acc = part if acc is None else acc + part
    out_ref[...] = _layer_norm(ALPHA * x + acc, g_ref[...], b_ref[...])


def _ffn(x2d, wu, cw, cb, wd, g, b, seq):
    t = x2d.shape[0]
    nt = seq // TM
    r8 = TM // 8
    return pl.pallas_call(
        functools.partial(_ffn_kernel, nt=nt),
        grid=(t // TM,),
        in_specs=[pl.BlockSpec((TM, D_MODEL), lambda i: (i, 0)),
                  pl.BlockSpec((8, D_MODEL), lambda i: (jnp.maximum(i * r8 - 1, 0), 0)),
                  pl.BlockSpec((8, D_MODEL), lambda i: (jnp.minimum((i + 1) * r8, t // 8 - 1), 0)),
                  _const_spec((D_MODEL, 2 * D_FF), single=True),
                  _const_spec((3, D_FF)), _const_spec((1, D_FF)),
                  _const_spec((D_FF, D_MODEL), single=True),
                  _const_spec((1, D_MODEL)), _const_spec((1, D_MODEL))],
        out_specs=pl.BlockSpec((TM, D_MODEL), lambda i: (i, 0)),
        out_shape=jax.ShapeDtypeStruct((t, D_MODEL), F32),
        compiler_params=_params(("arbitrary",), VMEM_LIMIT),
        name="ffn",
    )(x2d, x2d, x2d, wu, cw, cb, wd, g, b)


def _lower_bound(tbl, layer):
    rows = [tbl[r:r + 1] for r in range(DEPTH)]
    mx = functools.reduce(jnp.maximum, rows)
    es = [jnp.exp(r - mx) for r in rows]
    inv = 1.0 / functools.reduce(lambda a, b: a + b, es)
    ps = [e * inv for e in es]
    return functools.reduce(lambda a, b: a + b, ps[:layer + 1]) - ps[0]


def _c_in_kernel(x_ref, w_ref, lbf_ref, lbb_ref, q_ref, lf_ref, lb_ref, v_ref, g_ref, *, layer):
    xb = x_ref[...].astype(BF16)
    half = C_W // 2

    def proj(a, n):
        return jnp.dot(xb, w_ref[:, a:a + n], preferred_element_type=F32)

    def silu(z):
        return z * _sigmoid(z)

    lbf = _lower_bound(lbf_ref[...], layer)
    lbb = _lower_bound(lbb_ref[...], layer)
    for c in range(2):
        a = c * half
        sl = slice(a, a + half)
        q_ref[:, sl] = proj(a, half).astype(BF16)
        for tbl, off, out_ref in ((lbf, C_W, lf_ref), (lbb, 2 * C_W, lb_ref)):
            lb = tbl[:, sl]
            out_ref[:, sl] = jnp.log(lb + (1.0 - lb) * _sigmoid(proj(off + a, half)))
        v_ref[:, sl] = silu(proj(3 * C_W + a, half)).astype(BF16)
        g_ref[:, sl] = silu(proj(3 * C_W + C_VW + a, half)).astype(BF16)


def _c_in(x2d, w, lbf, lbb, layer):
    t = x2d.shape[0]
    row = pl.BlockSpec((TM, C_W), lambda i: (i, 0))
    return pl.pallas_call(
        functools.partial(_c_in_kernel, layer=layer),
        grid=(t // TM,),
        in_specs=[row, _const_spec(w.shape), _const_spec((DEPTH, C_W)), _const_spec((DEPTH, C_W))],
        out_specs=[row] * 5,
        out_shape=[jax.ShapeDtypeStruct((t, C_W), d) for d in (BF16, F32, F32, BF16, BF16)],
        compiler_params=_params(("arbitrary",), VMEM_LIMIT),
        name="c_in",
    )(x2d, w, lbf, lbb)


def _hgrn_kernel(*refs, rev, fused):
    if fused:
        q_ref, lf_ref, v_ref, of_ref, gate_ref, gn_ref, y_ref, st_ref = refs
    else:
        q_ref, lf_ref, v_ref, y_ref, st_ref = refs

    @pl.when(pl.program_id(2) == 0)
    def _():
        st_ref[...] = jnp.zeros(st_ref.shape, F32)

    ri = lax.broadcasted_iota(jnp.int32, (HG_CHUNK, HG_CHUNK), 0)
    ci = lax.broadcasted_iota(jnp.int32, (HG_CHUNK, HG_CHUNK), 1)
    tri = (ci >= ri) if rev else (ci <= ri)
    trib = jnp.where(tri, 1.0, 0.0).astype(BF16)
    dotf = lambda a, b: jnp.dot(a, b, preferred_element_type=F32)
    n_chunks = HG_TB // HG_CHUNK
    order = range(n_chunks - 1, -1, -1) if rev else range(n_chunks)
    units = [(c, h) for c in order for h in range(HG_HP)]
    window = lambda u: (0, slice(u[0] * HG_CHUNK, (u[0] + 1) * HG_CHUNK), slice(u[1] * C_KEY, (u[1] + 1) * C_KEY))

    b_all = {}
    for u in units:
        lf = lf_ref[window(u)]
        hi = lf.astype(BF16)
        r1 = lf - hi.astype(F32)
        mid = r1.astype(BF16)
        lo = (r1 - mid.astype(F32)).astype(BF16)
        b_all[u] = dotf(trib, hi) + dotf(trib, mid) + dotf(trib, lo)

    qdec, kdec, etot, a_rows = {}, {}, {}, {}
    for u in units:
        lf, b = lf_ref[window(u)], b_all[u]
        qf = q_ref[window(u)].astype(F32)
        bex = b - lf
        kk = 1.0 - jnp.exp(lf)
        btot = b[0:1] if rev else b[HG_CHUNK - 1:HG_CHUNK]
        qdec[u] = (qf * jnp.exp(b)).astype(BF16)
        kdec[u] = (kk * jnp.exp(btot - b)).astype(BF16)
        etot[u] = jnp.exp(btot)
        rows = []
        for blk in range(HG_CHUNK // HG_SUB):
            r0 = blk * HG_SUB
            ref = bex[r0 + HG_SUB - 1:r0 + HG_SUB] if rev else bex[r0:r0 + 1]
            qt = (qf[r0:r0 + HG_SUB] * jnp.exp(b[r0:r0 + HG_SUB] - ref)).astype(BF16)
            lo_r, hi_r = (r0, HG_CHUNK) if rev else (0, r0 + HG_SUB)
            ks = (kk[lo_r:hi_r] * jnp.exp(ref - b[lo_r:hi_r])).astype(BF16)
            pieces = []
            if lo_r > 0:
                pieces.append(jnp.zeros((lo_r, C_KEY), BF16))
            pieces.append(ks)
            if hi_r < HG_CHUNK:
                pieces.append(jnp.zeros((HG_CHUNK - hi_r, C_KEY), BF16))
            kfull = jnp.concatenate(pieces, axis=0) if len(pieces) > 1 else ks
            rows.append(lax.dot_general(qt, kfull, _NT, preferred_element_type=F32))
        a_rows[u] = rows
    attn = {u: jnp.where(tri, jnp.concatenate(a_rows[u], axis=0), 0.0).astype(BF16) for u in units}

    states = [st_ref[h] for h in range(HG_HP)]
    for u in units:
        h = u[1]
        vb = v_ref[window(u)]
        o = lax.dot_general(qdec[u], states[h].astype(BF16), _NT, preferred_element_type=F32)
        o = o + dotf(attn[u], vb)
        states[h] = states[h] * etot[u] + lax.dot_general(vb, kdec[u], _TN, preferred_element_type=F32)
        if fused:
            tot = of_ref[window(u)] + o
            inv = lax.rsqrt(jnp.mean(tot * tot, -1, keepdims=True) + RMS_EPS)
            y = tot * inv * gn_ref[:, window(u)[2]] * gate_ref[window(u)].astype(F32)
            y_ref[window(u)] = y.astype(BF16)
        else:
            y_ref[window(u)] = o
    for h in range(HG_HP):
        st_ref[h] = states[h]


def _hgrn(q, lf, v, batch, seq, rev, fused_args=None):
    nb = seq // HG_TB
    width = HG_HP * C_KEY
    blk = (lambda b, h, i: (b, nb - 1 - i, h)) if rev else (lambda b, h, i: (b, i, h))
    tile = pl.BlockSpec((1, HG_TB, width), blk)
    view = lambda t: t.reshape(batch, seq, C_W)
    ins, specs = [view(q), view(lf), view(v)], [tile, tile, tile]
    fused = fused_args is not None
    if fused:
        o_f, gate, gn = fused_args
        ins += [view(o_f), view(gate), gn]
        specs += [tile, tile, pl.BlockSpec((1, width), lambda b, h, i: (0, h))]
    out = pl.pallas_call(
        functools.partial(_hgrn_kernel, rev=rev, fused=fused),
        grid=(batch, C_HEADS // HG_HP, nb),
        in_specs=specs,
        out_specs=tile,
        out_shape=jax.ShapeDtypeStruct((batch, seq, C_VW), BF16 if fused else F32),
        scratch_shapes=[pltpu.VMEM((HG_HP, C_VAL, C_KEY), F32)],
        compiler_params=_params(("arbitrary",) * 3),
        name="hgrn_bwd" if rev else "hgrn_fwd",
    )(*ins)
    return out.reshape(batch * seq, C_VW)


def _proj_ln_kernel(a_ref, x_ref, w_ref, g_ref, b_ref, out_ref):
    y = jnp.dot(a_ref[...], w_ref[...], preferred_element_type=F32)
    out_ref[...] = _layer_norm(ALPHA * x_ref[...] + y, g_ref[...], b_ref[...])


def _proj_ln(a, x2d, w, g, b):
    t = x2d.shape[0]
    kdim = a.shape[1]
    return pl.pallas_call(
        _proj_ln_kernel,
        grid=(t // TM,),
        in_specs=[pl.BlockSpec((TM, kdim), lambda i: (i, 0)), pl.BlockSpec((TM, D_MODEL), lambda i: (i, 0)),
                  _const_spec((kdim, D_MODEL)), _const_spec((1, D_MODEL)), _const_spec((1, D_MODEL))],
        out_specs=pl.BlockSpec((TM, D_MODEL), lambda i: (i, 0)),
        out_shape=jax.ShapeDtypeStruct((t, D_MODEL), F32),
        compiler_params=_params(("arbitrary",), VMEM_LIMIT),
        name="c_out",
    )(a, x2d, w, g, b)


def _prep_weights(w_in_ab, w_out_ab, qn_ab, kn_ab, w_in_c, w_out_c, gn_c, ffn_w_up, ffn_w_down):
    scale = HEAD_DIM ** -0.5
    ab = []
    for j in range(w_in_ab.shape[0]):
        w = w_in_ab[j]
        o3 = 3 * A_W
        kb = w[:, o3 + B_QW:o3 + B_QW + B_KVW]
        vb = w[:, o3 + B_QW + B_KVW:]
        dup = lambda t: jnp.concatenate([t[:, :HEAD_DIM], t[:, :HEAD_DIM], t[:, HEAD_DIM:], t[:, HEAD_DIM:]], 1)
        w_ext = jnp.concatenate([w[:, :A_W] * scale, w[:, A_W:o3 + B_QW], dup(kb), dup(vb)], 1).astype(BF16)
        qg = (jnp.concatenate([qn_ab[j], qn_ab[j]]) * (scale * LOG2E)).reshape(1, LANES)
        kg = jnp.concatenate([kn_ab[j], kn_ab[j]]).reshape(1, LANES)
        ab.append((w_ext, qg, kg, w_out_ab[j].astype(BF16)))
    cc = [(w_in_c[j].astype(BF16), w_out_c[j].astype(BF16), gn_c[j].reshape(1, C_VW)) for j in range(w_in_c.shape[0])]
    return ab, cc, ffn_w_up.astype(BF16), ffn_w_down.astype(BF16)


def _trunk(x, prep, lb_fwd, lb_bwd, ln_mix_g, ln_mix_b, ln_ffn_g, ln_ffn_b, ffn_conv_w, ffn_conv_b):
    ab, cc, wu, wd = prep
    batch, seq, _ = x.shape
    x2d = x.reshape(batch * seq, D_MODEL)
    tabs = _rope_tables(seq)
    blk = jnp.arange(LANES) // HEAD_DIM
    ones2 = (blk[:, None] == blk[None, :]).astype(BF16)
    vec = lambda t: t.reshape(1, -1)
    for l in range(DEPTH):
        j = l // 2
        if l % 2 == 0:
            w_ext, qg, kg, w_out = ab[j]
            qa, ka, va, qb, kd, vd = _ab_in(x2d, w_ext, tabs, qg, kg, ones2, seq)
            os_, lses = [], []
            for window, dil in A_PATTERNS:
                assert window // (2 * dil) == HALF_WINDOW
                o, lse = _band(qa, ka, va, batch, seq, dil)
                os_.append(o)
                lses.append(lse)
            yb = _gqa(qb, kd, vd, batch, seq)
            x2d = _ab_out(os_, lses, yb, x2d, w_out, vec(ln_mix_g[l]), vec(ln_mix_b[l]))
        else:
            w_in, w_out, gn = cc[j]
            q, lf, lb, v, gate = _c_in(x2d, w_in, lb_fwd, lb_bwd, l)
            o_f = _hgrn(q, lf, v, batch, seq, rev=False)
            y = _hgrn(q, lb, v, batch, seq, rev=True, fused_args=(o_f, gate, gn))
            x2d = _proj_ln(y, x2d, w_out, vec(ln_mix_g[l]), vec(ln_mix_b[l]))
        x2d = _ffn(x2d, wu[l], ffn_conv_w[l], vec(ffn_conv_b[l]), wd[l], vec(ln_ffn_g[l]), vec(ln_ffn_b[l]), seq)
    return x2d.reshape(batch, seq, D_MODEL)


def kernel(x_prompt, x_sample, w_in_ab, w_out_ab, qn_ab, kn_ab, w_in_c, w_out_c, lb_fwd, lb_bwd, gn_c, ln_mix_g, ln_mix_b, ln_ffn_g, ln_ffn_b, ffn_w_up, ffn_conv_w, ffn_conv_b, ffn_w_down):
    prep = _prep_weights(w_in_ab, w_out_ab, qn_ab, kn_ab, w_in_c, w_out_c, gn_c, ffn_w_up, ffn_w_down)
    rest = (lb_fwd, lb_bwd, ln_mix_g, ln_mix_b, ln_ffn_g, ln_ffn_b, ffn_conv_w, ffn_conv_b)
    return (_trunk(x_prompt, prep, *rest), _trunk(x_sample, prep, *rest))
```

```python
import functools

import jax
import jax.numpy as jnp
from jax import lax
from jax.experimental import pallas as pl
from jax.experimental.pallas import tpu as pltpu

F32 = jnp.float32
BF16 = jnp.bfloat16

D_MODEL = 1024
DEPTH = 2
HEAD_DIM = 64
A_HEADS = 8
A_PATTERNS = ((128, 1), (512, 4), (2048, 16))
B_HEADS = 8
B_KV_HEADS = 2
GRID_W = 64
ROPE_THETA = 500000.0
ROPE_DIM = HEAD_DIM // 4
AXIAL_THETA = 10000.0
C_HEADS = 8
C_KEY = 128
C_VAL = 128
D_FF = 2816
ALPHA = (2 * DEPTH) ** 0.25
LN_EPS = 1e-5
RMS_EPS = 1e-6
A_W = A_HEADS * HEAD_DIM
B_QW = B_HEADS * HEAD_DIM
B_KVW = B_KV_HEADS * HEAD_DIM
C_W = C_HEADS * C_KEY
C_VW = C_HEADS * C_VAL

LANES = 128
HALF_WINDOW = 64
TM = 512
BAND_L = 128
DIL_CLASSES = 16
BAND4_ROWS = 32
BAND4_HALO = 16
GQA_TQ = 512
GQA_TK = 512
GQA_CHAIN = 512
FF_CHUNKS = (768, 768, 640, 640)
HG_CHUNK = 64
HG_SUB = 16
HG_TB = 512
HG_HP = 4
NEG = -1e30
LOG2E = 1.4426950408889634
VMEM_LIMIT = 56 * 1024 * 1024

_NT = (((1,), (1,)), ((), ()))
_TN = (((0,), (0,)), ((), ()))


def _params(sem, vmem=None):
    return pltpu.CompilerParams(dimension_semantics=sem, vmem_limit_bytes=vmem)


def _const_spec(shape, single=False):
    nd = len(shape)
    if single:
        return pl.BlockSpec(shape, lambda *_: (0,) * nd, pipeline_mode=pl.Buffered(1))
    return pl.BlockSpec(shape, lambda *_: (0,) * nd)


def _layer_norm(z, g, b):
    mu = jnp.mean(z, -1, keepdims=True)
    d = z - mu
    var = jnp.mean(d * d, -1, keepdims=True)
    return d * lax.rsqrt(var + LN_EPS) * g + b


def _sigmoid(z):
    return 1.0 / (1.0 + jnp.exp(-z))


def _pipelined(sections):
    pending = sections[0][0]()
    for i, (_, epilogue) in enumerate(sections):
        upcoming = sections[i + 1][0]() if i + 1 < len(sections) else None
        epilogue(pending)
        pending = upcoming


def _rope_tables(seq):
    pos = jnp.arange(seq, dtype=F32)
    d = jnp.arange(LANES) % HEAD_DIM

    def angles(p, dim, theta):
        freqs = theta ** (-(jnp.arange(0, dim, 2, dtype=F32) / dim))
        return p[:, None] * freqs[None, :]

    h = ROPE_DIM // 2
    ang = angles(pos, ROPE_DIM, ROPE_THETA)
    a = ang[:, d % h]
    lo, hi = d < h, (d >= h) & (d < ROPE_DIM)
    pc = jnp.where(lo | hi, jnp.cos(a), 1.0)
    psa = jnp.where(lo, -jnp.sin(a), 0.0)
    psb = jnp.where(hi, jnp.sin(a), 0.0)

    q = HEAD_DIM // 4
    row = jnp.floor(pos / GRID_W)
    col = pos - row * GRID_W
    ar = angles(row, HEAD_DIM // 2, AXIAL_THETA)[:, d % q]
    ac_ = angles(col, HEAD_DIM // 2, AXIAL_THETA)[:, d % q]
    a2 = jnp.where(d < HEAD_DIM // 2, ar, ac_)
    first = (d % (HEAD_DIM // 2)) < q
    ac = jnp.cos(a2)
    asa = jnp.where(first, -jnp.sin(a2), 0.0)
    asb = jnp.where(first, 0.0, jnp.sin(a2))
    return [t.astype(F32) for t in (pc, psa, psb, ac, asa, asb)]


def _rope(seg, c, sa, sb, shift):
    return seg * c + pltpu.roll(seg, LANES - shift, 1) * sa + pltpu.roll(seg, shift, 1) * sb


def _ab_in_kernel(x_ref, w_ref, pc_ref, psa_ref, psb_ref, ac_ref, asa_ref, asb_ref, qg_ref, kg_ref,
                  ones_ref, qa_ref, ka_ref, va_ref, qb_ref, kd_ref, vd_ref, qa16_ref, ka16_ref, va16_ref,
                  qs_ref, ks_ref, vs_ref):
    xb = x_ref[...].astype(BF16)

    def proj(a, n):
        return jnp.dot(xb, w_ref[:, a:a + n], preferred_element_type=F32)

    pc, psa, psb = pc_ref[...], psa_ref[...], psb_ref[...]
    ac, asa, asb = ac_ref[...], asa_ref[...], asb_ref[...]

    def dilated_a(h, rope, out_ref, out16_ref, slab_ref):
        for g in range(A_W // LANES):
            sl = slice(g * LANES, (g + 1) * LANES)
            y = _rope(h[:, sl], pc, psa, psb, ROPE_DIM // 2) if rope else h[:, sl]
            out_ref[:, sl] = y.astype(BF16)
            slab_ref[g] = y
        for c in range(DIL_CLASSES):
            for g in range(A_W // LANES):
                rows = slab_ref[g, pl.ds(c, TM // DIL_CLASSES, stride=DIL_CLASSES), :]
                out16_ref[0, c, :, g * LANES:(g + 1) * LANES] = rows.astype(BF16)

    def norm_rope(h, gain, out_ref):
        for g in range(h.shape[1] // LANES):
            sl = slice(g * LANES, (g + 1) * LANES)
            seg = h[:, sl]
            sq = seg * seg
            hi = sq.astype(BF16)
            lo = (sq - hi.astype(F32)).astype(BF16)
            ss = (jnp.dot(hi, ones_ref[...], preferred_element_type=F32)
                  + jnp.dot(lo, ones_ref[...], preferred_element_type=F32))
            y = seg * lax.rsqrt(ss * (1.0 / HEAD_DIM) + RMS_EPS) * gain
            out_ref[:, sl] = _rope(y, ac, asa, asb, HEAD_DIM // 4).astype(BF16)

    def cast_to(out_ref):
        def fn(h):
            out_ref[...] = h.astype(BF16)
        return fn

    o_qb = 3 * A_W
    o_kd = o_qb + B_QW
    o_vd = o_kd + 2 * B_KVW
    _pipelined([
        (functools.partial(proj, 0, A_W),
         functools.partial(dilated_a, rope=True, out_ref=qa_ref, out16_ref=qa16_ref, slab_ref=qs_ref)),
        (functools.partial(proj, A_W, A_W),
         functools.partial(dilated_a, rope=True, out_ref=ka_ref, out16_ref=ka16_ref, slab_ref=ks_ref)),
        (functools.partial(proj, 2 * A_W, A_W),
         functools.partial(dilated_a, rope=False, out_ref=va_ref, out16_ref=va16_ref, slab_ref=vs_ref)),
        (functools.partial(proj, o_qb, B_QW), functools.partial(norm_rope, gain=qg_ref[...], out_ref=qb_ref)),
        (functools.partial(proj, o_kd, 2 * B_KVW), functools.partial(norm_rope, gain=kg_ref[...], out_ref=kd_ref)),
        (functools.partial(proj, o_vd, 2 * B_KVW), cast_to(vd_ref)),
    ])


def _ab_in(x2d, w_ext, tabs, qg, kg, ones2, batch, seq):
    t = x2d.shape[0]
    nt = seq // TM
    tab = pl.BlockSpec((TM, LANES), lambda i: (i % nt, 0))
    row = lambda n: pl.BlockSpec((TM, n), lambda i: (i, 0))
    cls = pl.BlockSpec((1, DIL_CLASSES, TM // DIL_CLASSES, A_W), lambda i: (i // nt, 0, i % nt, 0))
    cls_shape = jax.ShapeDtypeStruct((batch, DIL_CLASSES, seq // DIL_CLASSES, A_W), BF16)
    wn = w_ext.shape[1]
    return pl.pallas_call(
        _ab_in_kernel,
        grid=(t // TM,),
        in_specs=[row(D_MODEL), _const_spec((D_MODEL, wn))] + [tab] * 6
        + [_const_spec((1, LANES)), _const_spec((1, LANES)), _const_spec((LANES, LANES))],
        out_specs=[row(A_W)] * 3 + [row(B_QW), row(2 * B_KVW), row(2 * B_KVW)] + [cls] * 3,
        out_shape=[jax.ShapeDtypeStruct((t, A_W), BF16)] * 3
        + [jax.ShapeDtypeStruct((t, B_QW), BF16)]
        + [jax.ShapeDtypeStruct((t, 2 * B_KVW), BF16)] * 2 + [cls_shape] * 3,
        scratch_shapes=[pltpu.VMEM((A_W // LANES, TM, LANES), F32)] * 3,
        compiler_params=_params(("arbitrary",), VMEM_LIMIT),
        name="ab_in",
    )(x2d, w_ext, *tabs, qg, kg, ones2)


def _band_pair(q, k, v, bias, left):
    outs, lses = [], []
    for sel in (left, jnp.logical_not(left)):
        qm = jnp.where(sel, q, jnp.zeros_like(q))
        s = lax.dot_general(qm, k, _NT, preferred_element_type=F32) + bias
        mx = jnp.max(s, axis=1, keepdims=True)
        pe = jnp.exp(s - mx)
        l = jnp.sum(pe, axis=1, keepdims=True)
        o = jnp.dot(pe.astype(BF16), v, preferred_element_type=F32)
        outs.append(o * (1.0 / l))
        lses.append(mx + jnp.log(l))
    return jnp.where(left, outs[0], outs[1]), jnp.where(left, lses[0], lses[1])


def _band_kernel(q_ref, kp_ref, kc_ref, kn_ref, vp_ref, vc_ref, vn_ref, o_ref, lse_ref, *, m):
    nk = BAND_L + 2 * HALF_WINDOW
    base = pl.program_id(len(q_ref.shape) - 2) * BAND_L
    r = lax.broadcasted_iota(jnp.int32, (BAND_L, nk), 0)
    j = lax.broadcasted_iota(jnp.int32, (BAND_L, nk), 1)
    rel = j - HALF_WINDOW - r
    kpos = base - HALF_WINDOW + j
    bias = jnp.where(jnp.abs(rel) <= HALF_WINDOW, 0.0, NEG)
    bias = jnp.where(kpos >= 0, bias, NEG)
    bias = jnp.where(kpos < m, bias, NEG)
    left = lax.broadcasted_iota(jnp.int32, (BAND_L, LANES), 1) < HEAD_DIM
    pre = (0,) * (len(q_ref.shape) - 2)
    for p in range(A_W // LANES):
        at = pre + (slice(None), slice(p * LANES, (p + 1) * LANES))
        k = jnp.concatenate([kp_ref[at], kc_ref[at], kn_ref[at]], axis=0)
        v = jnp.concatenate([vp_ref[at], vc_ref[at], vn_ref[at]], axis=0)
        o, lse = _band_pair(q_ref[at], k, v, bias, left)
        o_ref[at] = o.astype(BF16)
        lse_ref[at] = lse


def _band_specs(lead, rows):
    nh = rows // HALF_WINDOW
    ratio = BAND_L // HALF_WINDOW
    ones = (1,) * lead
    cur = pl.BlockSpec(ones + (BAND_L, A_W), lambda *g: g[:lead] + (g[-1], 0))
    prev = pl.BlockSpec(ones + (HALF_WINDOW, A_W), lambda *g: g[:lead] + (jnp.maximum(g[-1] * ratio - 1, 0), 0))
    nxt = pl.BlockSpec(ones + (HALF_WINDOW, A_W), lambda *g: g[:lead] + (jnp.minimum((g[-1] + 1) * ratio, nh - 1), 0))
    return cur, prev, nxt


def _band1(q, k, v, batch, seq):
    cur, prev, nxt = _band_specs(1, seq)
    view = lambda t: t.reshape(batch, seq, A_W)
    o, lse = pl.pallas_call(
        functools.partial(_band_kernel, m=seq),
        grid=(batch, seq // BAND_L),
        in_specs=[cur, prev, cur, nxt, prev, cur, nxt],
        out_specs=[cur, cur],
        out_shape=[jax.ShapeDtypeStruct((batch, seq, A_W), BF16), jax.ShapeDtypeStruct((batch, seq, A_W), F32)],
        compiler_params=_params(("arbitrary",) * 2),
        name="band_d1",
    )(view(q), view(k), view(k), view(k), view(v), view(v), view(v))
    return o.reshape(batch * seq, A_W), lse.reshape(batch * seq, A_W)


def _band16(q16, k16, v16, batch, seq):
    m = seq // DIL_CLASSES
    cur, prev, nxt = _band_specs(2, m)
    shape = (batch, DIL_CLASSES, m, A_W)
    return pl.pallas_call(
        functools.partial(_band_kernel, m=m),
        grid=(batch, DIL_CLASSES, m // BAND_L),
        in_specs=[cur, prev, cur, nxt, prev, cur, nxt],
        out_specs=[cur, cur],
        out_shape=[jax.ShapeDtypeStruct(shape, BF16), jax.ShapeDtypeStruct(shape, F32)],
        compiler_params=_params(("arbitrary",) * 3),
        name="band_d16",
    )(q16, k16, k16, k16, v16, v16, v16)


def _band4_kernel(q_ref, kp_ref, kc_ref, kn_ref, vp_ref, vc_ref, vn_ref, o16_ref, l16_ref, o_ref, lse_ref, *, m16):
    qr = BAND4_ROWS
    kr = qr + 2 * BAND4_HALO
    i = pl.program_id(2)
    shift_q, shift_k = qr.bit_length() - 1, kr.bit_length() - 1
    row = lax.broadcasted_iota(jnp.int32, (4 * qr, 4 * kr), 0)
    col = lax.broadcasted_iota(jnp.int32, (4 * qr, 4 * kr), 1)
    uq, rq = row >> shift_q, row & (qr - 1)
    uk, jk = col >> shift_k, col & (kr - 1)
    rel = 4 * (jk - BAND4_HALO - rq) + (uk - uq)
    kpos = i * qr - BAND4_HALO + jk
    bias = jnp.where(jnp.abs(rel) <= HALF_WINDOW, 0.0, NEG)
    bias = jnp.where(kpos >= 0, bias, NEG)
    bias = jnp.where(kpos < m16, bias, NEG)
    left = lax.broadcasted_iota(jnp.int32, (4 * qr, LANES), 1) < HEAD_DIM
    for p in range(A_W // LANES):
        sl = slice(p * LANES, (p + 1) * LANES)
        gather = lambda refs: jnp.concatenate([r[0, u, 0, :, sl] for u in range(4) for r in refs], axis=0)
        o4, l4 = _band_pair(gather([q_ref]), gather([kp_ref, kc_ref, kn_ref]), gather([vp_ref, vc_ref, vn_ref]),
                            bias, left)
        o16 = gather([o16_ref]).astype(F32)
        l16 = gather([l16_ref])
        mx = jnp.maximum(l4, l16)
        w4, w16 = jnp.exp(l4 - mx), jnp.exp(l16 - mx)
        tot = w4 + w16
        om = (w4 * o4 + w16 * o16) * (1.0 / tot)
        lm = mx + jnp.log(tot)
        for u in range(4):
            o_ref[0, u, 0, :, sl] = om[u * qr:(u + 1) * qr].astype(BF16)
            lse_ref[0, u, 0, :, sl] = lm[u * qr:(u + 1) * qr]


def _band4(q16, k16, v16, o16, l16, batch, seq):
    m16 = seq // DIL_CLASSES
    ratio = BAND4_ROWS // BAND4_HALO
    nh = m16 // BAND4_HALO
    view = lambda t: t.reshape(batch, 4, 4, m16, A_W)
    cur = pl.BlockSpec((1, 4, 1, BAND4_ROWS, A_W), lambda b, c, i: (b, 0, c, i, 0))
    prev = pl.BlockSpec((1, 4, 1, BAND4_HALO, A_W), lambda b, c, i: (b, 0, c, jnp.maximum(i * ratio - 1, 0), 0))
    nxt = pl.BlockSpec((1, 4, 1, BAND4_HALO, A_W), lambda b, c, i: (b, 0, c, jnp.minimum((i + 1) * ratio, nh - 1), 0))
    shape = (batch, 4, 4, m16, A_W)
    o, lse = pl.pallas_call(
        functools.partial(_band4_kernel, m16=m16),
        grid=(batch, 4, m16 // BAND4_ROWS),
        in_specs=[cur, prev, cur, nxt, prev, cur, nxt, cur, cur],
        out_specs=[cur, cur],
        out_shape=[jax.ShapeDtypeStruct(shape, BF16), jax.ShapeDtypeStruct(shape, F32)],
        compiler_params=_params(("arbitrary",) * 3),
        name="band_d4",
    )(view(q16), view(k16), view(k16), view(k16), view(v16), view(v16), view(v16), view(o16), view(l16))
    return o.reshape(batch, DIL_CLASSES, m16, A_W), lse.reshape(batch, DIL_CLASSES, m16, A_W)


def _gqa_kernel(q_ref, k_ref, v_ref, o_ref, qs_ref, m_ref, acc_ref, s_ref, *, seq):
    left = lax.broadcasted_iota(jnp.int32, (GQA_TQ, LANES), 1) < HEAD_DIM
    for pr in range(2):
        qp = q_ref[0, :, pr * LANES:(pr + 1) * LANES]
        zero = jnp.zeros_like(qp)
        qs_ref[(2 * pr) * GQA_TQ:(2 * pr + 1) * GQA_TQ, :] = jnp.where(left, qp, zero)
        qs_ref[(2 * pr + 1) * GQA_TQ:(2 * pr + 2) * GQA_TQ, :] = jnp.where(left, zero, qp)
    m_ref[...] = jnp.full(m_ref.shape, NEG, F32)
    acc_ref[...] = jnp.zeros(acc_ref.shape, F32)
    ones_lane = lax.broadcasted_iota(jnp.int32, (GQA_TK, LANES), 1) == HEAD_DIM

    chains = [slice(c * GQA_CHAIN, (c + 1) * GQA_CHAIN) for c in range(4 * GQA_TQ // GQA_CHAIN)]

    def scores(tile, buf, cols):
        off = pl.multiple_of(tile * GQA_TK, GQA_TK)
        k = k_ref[0, pl.ds(off, GQA_TK), :]
        s_ref[buf, :, cols] = lax.dot_general(k, qs_ref[cols, :], _NT, preferred_element_type=F32)

    def softmax_pv(tile, buf, cols):
        off = pl.multiple_of(tile * GQA_TK, GQA_TK)
        v = jnp.where(ones_lane, jnp.ones((), BF16), v_ref[0, pl.ds(off, GQA_TK), :])
        s = s_ref[buf, :, cols]
        m_prev = m_ref[:, cols]
        m_new = jnp.maximum(m_prev, jnp.max(s, axis=0, keepdims=True))
        alpha = jnp.exp2(m_prev - m_new)
        p = jnp.exp2((s - m_new).astype(BF16))
        pv = lax.dot_general(v, p, _TN, preferred_element_type=F32)
        acc_ref[:, cols] = alpha * acc_ref[:, cols] + pv
        m_ref[:, cols] = m_new

    def step(tile, buf, prefetch):
        for cols in chains:
            if prefetch:
                scores(tile + 1, 1 - buf, cols)
            softmax_pv(tile, buf, cols)

    def pair(j, carry):
        step(2 * j, 0, True)
        step(2 * j + 1, 1, True)
        return carry

    n_tiles = seq // GQA_TK
    for cols in chains:
        scores(0, 0, cols)
    lax.fori_loop(0, n_tiles // 2 - 1, pair, 0)
    step(n_tiles - 2, 0, True)
    step(n_tiles - 1, 1, False)
    acc = acc_ref[...]
    a = (acc * (1.0 / acc[HEAD_DIM:HEAD_DIM + 1, :])).T
    for pr in range(2):
        lo = a[(2 * pr) * GQA_TQ:(2 * pr + 1) * GQA_TQ]
        hi = pltpu.roll(a[(2 * pr + 1) * GQA_TQ:(2 * pr + 2) * GQA_TQ], HEAD_DIM, 1)
        o_ref[0, :, pr * LANES:(pr + 1) * LANES] = jnp.where(left, lo, hi).astype(BF16)


def _gqa(qb, kd, vd, batch, seq):
    gw = B_QW // B_KV_HEADS
    cols = 4 * GQA_TQ
    out = pl.pallas_call(
        functools.partial(_gqa_kernel, seq=seq),
        grid=(batch, B_KV_HEADS, seq // GQA_TQ),
        in_specs=[pl.BlockSpec((1, GQA_TQ, gw), lambda b, g, qi: (b, qi, g)),
                  pl.BlockSpec((1, seq, LANES), lambda b, g, qi: (b, 0, g)),
                  pl.BlockSpec((1, seq, LANES), lambda b, g, qi: (b, 0, g))],
        out_specs=pl.BlockSpec((1, GQA_TQ, gw), lambda b, g, qi: (b, qi, g)),
        out_shape=jax.ShapeDtypeStruct((batch, seq, B_QW), BF16),
        scratch_shapes=[pltpu.VMEM((cols, LANES), BF16), pltpu.VMEM((1, cols), F32),
                        pltpu.VMEM((LANES, cols), F32), pltpu.VMEM((2, GQA_TK, cols), F32)],
        compiler_params=_params(("arbitrary",) * 3, VMEM_LIMIT),
        name="gqa",
    )(qb.reshape(batch, seq, B_QW), kd.reshape(batch, seq, 2 * B_KVW), vd.reshape(batch, seq, 2 * B_KVW))
    return out.reshape(batch * seq, B_QW)


def _ab_out_kernel(o1_ref, l1_ref, om_ref, lm_ref, yb_ref, x_ref, w_ref, g_ref, b_ref, out_ref, os_ref, ls_ref):
    for c in range(DIL_CLASSES):
        for g in range(A_W // LANES):
            sl = slice(g * LANES, (g + 1) * LANES)
            rows = pl.ds(c, TM // DIL_CLASSES, stride=DIL_CLASSES)
            os_ref[g, rows, :] = om_ref[0, c, :, sl].astype(F32)
            ls_ref[g, rows, :] = lm_ref[0, c, :, sl]
    ya = []
    for g in range(A_W // LANES):
        sl = slice(g * LANES, (g + 1) * LANES)
        l1, lm = l1_ref[:, sl], ls_ref[g]
        mx = jnp.maximum(l1, lm)
        e1, em = jnp.exp(l1 - mx), jnp.exp(lm - mx)
        ya.append(((e1 * o1_ref[:, sl].astype(F32) + em * os_ref[g]) * (1.0 / (e1 + em))).astype(BF16))
    y = jnp.dot(jnp.concatenate(ya, axis=1), w_ref[0:A_W, :], preferred_element_type=F32)
    y = y + jnp.dot(yb_ref[...], w_ref[A_W:A_W + B_QW, :], preferred_element_type=F32)
    out_ref[...] = _layer_norm(ALPHA * x_ref[...] + y, g_ref[...], b_ref[...])


def _ab_out(o1, l1, om, lm, yb, x2d, w, g, b, seq):
    t = x2d.shape[0]
    nt = seq // TM
    row = lambda n: pl.BlockSpec((TM, n), lambda i: (i, 0))
    cls = pl.BlockSpec((1, DIL_CLASSES, TM // DIL_CLASSES, A_W), lambda i: (i // nt, 0, i % nt, 0))
    return pl.pallas_call(
        _ab_out_kernel,
        grid=(t // TM,),
        in_specs=[row(A_W), row(A_W), cls, cls, row(B_QW), row(D_MODEL), _const_spec((A_W + B_QW, D_MODEL)),
                  _const_spec((1, D_MODEL)), _const_spec((1, D_MODEL))],
        out_specs=row(D_MODEL),
        out_shape=jax.ShapeDtypeStruct((t, D_MODEL), F32),
        scratch_shapes=[pltpu.VMEM((A_W // LANES, TM, LANES), F32)] * 2,
        compiler_params=_params(("arbitrary",), VMEM_LIMIT),
        name="ab_out",
    )(o1, l1, om, lm, yb, x2d, w, g, b)


def _gelu(x):
    return 0.5 * x * (1.0 + jnp.tanh(0.7978845608028654 * (x + 0.044715 * (x * x * x))))


def _ffn_kernel(x_ref, xp_ref, xn_ref, wu_ref, cw_ref, cb_ref, wd_ref, g_ref, b_ref, out_ref, *, nt):
    i = pl.program_id(0)
    keep_prev = (i % nt != 0).astype(F32)
    keep_next = (i % nt != nt - 1).astype(F32)
    x = x_ref[...]
    xb = x.astype(BF16)
    halo = jnp.concatenate([xp_ref[...], xn_ref[...]], axis=0).astype(BF16)
    xe = jnp.concatenate([xb, halo], axis=0)
    rows = lax.broadcasted_iota(jnp.int32, (TM, 1), 0)
    offsets = [sum(FF_CHUNKS[:c]) for c in range(len(FF_CHUNKS))]

    def up(c):
        a, n = offsets[c], FF_CHUNKS[c]
        u = jnp.dot(xb, wu_ref[:, a:a + n], preferred_element_type=F32)
        ge = jnp.dot(xe, wu_ref[:, D_FF + a:D_FF + a + n], preferred_element_type=F32)
        return u, ge

    def gate(c, u, ge):
        a, n = offsets[c], FF_CHUNKS[c]
        gm = ge[0:TM]
        g_before = ge[TM + 7:TM + 8] * keep_prev
        g_after = ge[TM + 8:TM + 9] * keep_next
        gp = jnp.where(rows == 0, g_before, pltpu.roll(gm, 1, 0))
        gn = jnp.where(rows == TM - 1, g_after, pltpu.roll(gm, TM - 1, 0))
        cw = cw_ref[:, a:a + n]
        gc = gp * cw[0:1] + gm * cw[1:2] + gn * cw[2:3] + cb_ref[:, a:a + n]
        return (_gelu(gc) * u).astype(BF16)

    acc = None
    pending = up(0)
    for c in range(len(FF_CHUNKS)):
        upcoming = up(c + 1) if c + 1 < len(FF_CHUNKS) else None
        act = gate(c, *pending)
        part = jnp.dot(act, wd_ref[offsets[c]:offsets[c] + FF_CHUNKS[c], :], preferred_element_type=F32)
        acc = part if acc is None else acc + part
        pending = upcoming
    out_ref[...] = _layer_norm(ALPHA * x + acc, g_ref[...], b_ref[...])


def _ffn(x2d, wu, cw, cb, wd, g, b, seq):
    t = x2d.shape[0]
    nt = seq // TM
    r8 = TM // 8
    return pl.pallas_call(
        functools.partial(_ffn_kernel, nt=nt),
        grid=(t // TM,),
        in_specs=[pl.BlockSpec((TM, D_MODEL), lambda i: (i, 0)),
                  pl.BlockSpec((8, D_MODEL), lambda i: (jnp.maximum(i * r8 - 1, 0), 0)),
                  pl.BlockSpec((8, D_MODEL), lambda i: (jnp.minimum((i + 1) * r8, t // 8 - 1), 0)),
                  _const_spec((D_MODEL, 2 * D_FF), single=True),
                  _const_spec((3, D_FF)), _const_spec((1, D_FF)),
                  _const_spec((D_FF, D_MODEL), single=True),
                  _const_spec((1, D_MODEL)), _const_spec((1, D_MODEL))],
        out_specs=pl.BlockSpec((TM, D_MODEL), lambda i: (i, 0)),
        out_shape=jax.ShapeDtypeStruct((t, D_MODEL), F32),
        compiler_params=_params(("arbitrary",), VMEM_LIMIT),
        name="ffn",
    )(x2d, x2d, x2d, wu, cw, cb, wd, g, b)


def _lower_bound(tbl, layer):
    rows = [tbl[r:r + 1] for r in range(DEPTH)]
    mx = functools.reduce(jnp.maximum, rows)
    es = [jnp.exp(r - mx) for r in rows]
    inv = 1.0 / functools.reduce(lambda a, b: a + b, es)
    ps = [e * inv for e in es]
    return functools.reduce(lambda a, b: a + b, ps[:layer + 1]) - ps[0]


def _c_in_kernel(x_ref, w_ref, lbf_ref, lbb_ref, q_ref, lf_ref, lb_ref, v_ref, g_ref, *, layer):
    xb = x_ref[...].astype(BF16)
    half = C_W // 2

    def proj(a, n):
        return jnp.dot(xb, w_ref[:, a:a + n], preferred_element_type=F32)

    def silu(z):
        return z * _sigmoid(z)

    lbf = _lower_bound(lbf_ref[...], layer)
    lbb = _lower_bound(lbb_ref[...], layer)

    def store_q(z, sl):
        q_ref[:, sl] = z.astype(BF16)

    def store_logf(tbl, out_ref):
        def fn(z, sl):
            lb = tbl[:, sl]
            out_ref[:, sl] = jnp.log(lb + (1.0 - lb) * _sigmoid(z))
        return fn

    def store_silu(out_ref):
        def fn(z, sl):
            out_ref[:, sl] = silu(z).astype(BF16)
        return fn

    sections = []
    for c in range(2):
        a = c * half
        sl = slice(a, a + half)
        for off, fn in ((0, store_q), (C_W, store_logf(lbf, lf_ref)), (2 * C_W, store_logf(lbb, lb_ref)),
                        (3 * C_W, store_silu(v_ref)), (3 * C_W + C_VW, store_silu(g_ref))):
            sections.append((functools.partial(proj, off + a, half), functools.partial(fn, sl=sl)))
    _pipelined(sections)


def _c_in(x2d, w, lbf, lbb, layer):
    t = x2d.shape[0]
    row = pl.BlockSpec((TM, C_W), lambda i: (i, 0))
    return pl.pallas_call(
        functools.partial(_c_in_kernel, layer=layer),
        grid=(t // TM,),
        in_specs=[row, _const_spec(w.shape), _const_spec((DEPTH, C_W)), _const_spec((DEPTH, C_W))],
        out_specs=[row] * 5,
        out_shape=[jax.ShapeDtypeStruct((t, C_W), d) for d in (BF16, F32, F32, BF16, BF16)],
        compiler_params=_params(("arbitrary",), VMEM_LIMIT),
        name="c_in",
    )(x2d, w, lbf, lbb)


def _hgrn_kernel(*refs, rev, fused):
    if fused:
        q_ref, lf_ref, v_ref, of_ref, gate_ref, gn_ref, y_ref, st_ref = refs
    else:
        q_ref, lf_ref, v_ref, y_ref, st_ref = refs

    @pl.when(pl.program_id(2) == 0)
    def _():
        st_ref[...] = jnp.zeros(st_ref.shape, F32)

    ri = lax.broadcasted_iota(jnp.int32, (HG_CHUNK, HG_CHUNK), 0)
    ci = lax.broadcasted_iota(jnp.int32, (HG_CHUNK, HG_CHUNK), 1)
    tri = (ci >= ri) if rev else (ci <= ri)
    trib = jnp.where(tri, 1.0, 0.0).astype(BF16)
    dotf = lambda a, b: jnp.dot(a, b, preferred_element_type=F32)
    n_chunks = HG_TB // HG_CHUNK
    order = range(n_chunks - 1, -1, -1) if rev else range(n_chunks)
    units = [(c, h) for c in order for h in range(HG_HP)]
    window = lambda u: (0, slice(u[0] * HG_CHUNK, (u[0] + 1) * HG_CHUNK), slice(u[1] * C_KEY, (u[1] + 1) * C_KEY))

    b_all = {}
    for u in units:
        lf = lf_ref[window(u)]
        hi = lf.astype(BF16)
        r1 = lf - hi.astype(F32)
        mid = r1.astype(BF16)
        lo = (r1 - mid.astype(F32)).astype(BF16)
        b_all[u] = dotf(trib, hi) + dotf(trib, mid) + dotf(trib, lo)

    qdec, kdec, etot, a_rows = {}, {}, {}, {}
    for u in units:
        lf, b = lf_ref[window(u)], b_all[u]
        qf = q_ref[window(u)].astype(F32)
        bex = b - lf
        kk = 1.0 - jnp.exp(lf)
        btot = b[0:1] if rev else b[HG_CHUNK - 1:HG_CHUNK]
        qdec[u] = (qf * jnp.exp(b)).astype(BF16)
        kdec[u] = (kk * jnp.exp(btot - b)).astype(BF16)
        etot[u] = jnp.exp(btot)
        rows = []
        for blk in range(HG_CHUNK // HG_SUB):
            r0 = blk * HG_SUB
            ref = bex[r0 + HG_SUB - 1:r0 + HG_SUB] if rev else bex[r0:r0 + 1]
            qt = (qf[r0:r0 + HG_SUB] * jnp.exp(b[r0:r0 + HG_SUB] - ref)).astype(BF16)
            lo_r, hi_r = (r0, HG_CHUNK) if rev else (0, r0 + HG_SUB)
            ks = (kk[lo_r:hi_r] * jnp.exp(ref - b[lo_r:hi_r])).astype(BF16)
            pieces = []
            if lo_r > 0:
                pieces.append(jnp.zeros((lo_r, C_KEY), BF16))
            pieces.append(ks)
            if hi_r < HG_CHUNK:
                pieces.append(jnp.zeros((HG_CHUNK - hi_r, C_KEY), BF16))
            kfull = jnp.concatenate(pieces, axis=0) if len(pieces) > 1 else ks
            rows.append(lax.dot_general(qt, kfull, _NT, preferred_element_type=F32))
        a_rows[u] = rows
    attn = {u: jnp.where(tri, jnp.concatenate(a_rows[u], axis=0), 0.0).astype(BF16) for u in units}

    states = [st_ref[h] for h in range(HG_HP)]
    for u in units:
        h = u[1]
        vb = v_ref[window(u)]
        o = lax.dot_general(qdec[u], states[h].astype(BF16), _NT, preferred_element_type=F32)
        o = o + dotf(attn[u], vb)
        states[h] = states[h] * etot[u] + lax.dot_general(vb, kdec[u], _TN, preferred_element_type=F32)
        if fused:
            tot = of_ref[window(u)] + o
            inv = lax.rsqrt(jnp.mean(tot * tot, -1, keepdims=True) + RMS_EPS)
            y = tot * inv * gn_ref[:, window(u)[2]] * gate_ref[window(u)].astype(F32)
            y_ref[window(u)] = y.astype(BF16)
        else:
            y_ref[window(u)] = o
    for h in range(HG_HP):
        st_ref[h] = states[h]


def _hgrn(q, lf, v, batch, seq, rev, fused_args=None):
    nb = seq // HG_TB
    width = HG_HP * C_KEY
    blk = (lambda b, h, i: (b, nb - 1 - i, h)) if rev else (lambda b, h, i: (b, i, h))
    tile = pl.BlockSpec((1, HG_TB, width), blk)
    view = lambda t: t.reshape(batch, seq, C_W)
    ins, specs = [view(q), view(lf), view(v)], [tile, tile, tile]
    fused = fused_args is not None
    if fused:
        o_f, gate, gn = fused_args
        ins += [view(o_f), view(gate), gn]
        specs += [tile, tile, pl.BlockSpec((1, width), lambda b, h, i: (0, h))]
    out = pl.pallas_call(
        functools.partial(_hgrn_kernel, rev=rev, fused=fused),
        grid=(batch, C_HEADS // HG_HP, nb),
        in_specs=specs,
        out_specs=tile,
        out_shape=jax.ShapeDtypeStruct((batch, seq, C_VW), BF16 if fused else F32),
        scratch_shapes=[pltpu.VMEM((HG_HP, C_VAL, C_KEY), F32)],
        compiler_params=_params(("arbitrary",) * 3),
        name="hgrn_bwd" if rev else "hgrn_fwd",
    )(*ins)
    return out.reshape(batch * seq, C_VW)


def _proj_ln_kernel(a_ref, x_ref, w_ref, g_ref, b_ref, out_ref):
    y = jnp.dot(a_ref[...], w_ref[...], preferred_element_type=F32)
    out_ref[...] = _layer_norm(ALPHA * x_ref[...] + y, g_ref[...], b_ref[...])


def _proj_ln(a, x2d, w, g, b):
    t = x2d.shape[0]
    kdim = a.shape[1]
    return pl.pallas_call(
        _proj_ln_kernel,
        grid=(t // TM,),
        in_specs=[pl.BlockSpec((TM, kdim), lambda i: (i, 0)), pl.BlockSpec((TM, D_MODEL), lambda i: (i, 0)),
                  _const_spec((kdim, D_MODEL)), _const_spec((1, D_MODEL)), _const_spec((1, D_MODEL))],
        out_specs=pl.BlockSpec((TM, D_MODEL), lambda i: (i, 0)),
        out_shape=jax.ShapeDtypeStruct((t, D_MODEL), F32),
        compiler_params=_params(("arbitrary",), VMEM_LIMIT),
        name="c_out",
    )(a, x2d, w, g, b)


def _prep_weights(w_in_ab, w_out_ab, qn_ab, kn_ab, w_in_c, w_out_c, gn_c, ffn_w_up, ffn_w_down):
    scale = HEAD_DIM ** -0.5
    ab = []
    for j in range(w_in_ab.shape[0]):
        w = w_in_ab[j]
        o3 = 3 * A_W
        kb = w[:, o3 + B_QW:o3 + B_QW + B_KVW]
        vb = w[:, o3 + B_QW + B_KVW:]
        dup = lambda t: jnp.concatenate([t[:, :HEAD_DIM], t[:, :HEAD_DIM], t[:, HEAD_DIM:], t[:, HEAD_DIM:]], 1)
        w_ext = jnp.concatenate([w[:, :A_W] * scale, w[:, A_W:o3 + B_QW], dup(kb), dup(vb)], 1).astype(BF16)
        qg = (jnp.concatenate([qn_ab[j], qn_ab[j]]) * (scale * LOG2E)).reshape(1, LANES)
        kg = jnp.concatenate([kn_ab[j], kn_ab[j]]).reshape(1, LANES)
        ab.append((w_ext, qg, kg, w_out_ab[j].astype(BF16)))
    cc = [(w_in_c[j].astype(BF16), w_out_c[j].astype(BF16), gn_c[j].reshape(1, C_VW)) for j in range(w_in_c.shape[0])]
    return ab, cc, ffn_w_up.astype(BF16), ffn_w_down.astype(BF16)


def _trunk(x, prep, lb_fwd, lb_bwd, ln_mix_g, ln_mix_b, ln_ffn_g, ln_ffn_b, ffn_conv_w, ffn_conv_b):
    ab, cc, wu, wd = prep
    batch, seq, _ = x.shape
    x2d = x.reshape(batch * seq, D_MODEL)
    tabs = _rope_tables(seq)
    blk = jnp.arange(LANES) // HEAD_DIM
    ones2 = (blk[:, None] == blk[None, :]).astype(BF16)
    vec = lambda t: t.reshape(1, -1)
    for l in range(DEPTH):
        j = l // 2
        if l % 2 == 0:
            w_ext, qg, kg, w_out = ab[j]
            assert [d for _, d in A_PATTERNS] == [1, 4, DIL_CLASSES]
            assert all(window // (2 * d) == HALF_WINDOW for window, d in A_PATTERNS)
            qa, ka, va, qb, kd, vd, qa16, ka16, va16 = _ab_in(x2d, w_ext, tabs, qg, kg, ones2, batch, seq)
            o1, l1 = _band1(qa, ka, va, batch, seq)
            o16, l16 = _band16(qa16, ka16, va16, batch, seq)
            om, lm = _band4(qa16, ka16, va16, o16, l16, batch, seq)
            yb = _gqa(qb, kd, vd, batch, seq)
            x2d = _ab_out(o1, l1, om, lm, yb, x2d, w_out, vec(ln_mix_g[l]), vec(ln_mix_b[l]), seq)
        else:
            w_in, w_out, gn = cc[j]
            q, lf, lb, v, gate = _c_in(x2d, w_in, lb_fwd, lb_bwd, l)
            o_f = _hgrn(q, lf, v, batch, seq, rev=False)
            y = _hgrn(q, lb, v, batch, seq, rev=True, fused_args=(o_f, gate, gn))
            x2d = _proj_ln(y, x2d, w_out, vec(ln_mix_g[l]), vec(ln_mix_b[l]))
        x2d = _ffn(x2d, wu[l], ffn_conv_w[l], vec(ffn_conv_b[l]), wd[l], vec(ln_ffn_g[l]), vec(ln_ffn_b[l]), seq)
    return x2d.reshape(batch, seq, D_MODEL)


def kernel(x_prompt, x_sample, w_in_ab, w_out_ab, qn_ab, kn_ab, w_in_c, w_out_c, lb_fwd, lb_bwd, gn_c, ln_mix_g, ln_mix_b, ln_ffn_g, ln_ffn_b, ffn_w_up, ffn_conv_w, ffn_conv_b, ffn_w_down):
    prep = _prep_weights(w_in_ab, w_out_ab, qn_ab, kn_ab, w_in_c, w_out_c, gn_c, ffn_w_up, ffn_w_down)
    rest = (lb_fwd, lb_bwd, ln_mix_g, ln_mix_b, ln_ffn_g, ln_ffn_b, ffn_conv_w, ffn_conv_b)
    return (_trunk(x_prompt, prep, *rest), _trunk(x_sample, prep, *rest))
```

```python
import functools

import jax
import jax.numpy as jnp
from jax import lax
from jax.experimental import pallas as pl
from jax.experimental.pallas import tpu as pltpu

F32 = jnp.float32
BF16 = jnp.bfloat16

D_MODEL = 1024
DEPTH = 2
HEAD_DIM = 64
A_HEADS = 8
A_PATTERNS = ((128, 1), (512, 4), (2048, 16))
B_HEADS = 8
B_KV_HEADS = 2
GRID_W = 64
ROPE_THETA = 500000.0
ROPE_DIM = HEAD_DIM // 4
AXIAL_THETA = 10000.0
C_HEADS = 8
C_KEY = 128
C_VAL = 128
D_FF = 2816
ALPHA = (2 * DEPTH) ** 0.25
LN_EPS = 1e-5
RMS_EPS = 1e-6
A_W = A_HEADS * HEAD_DIM
B_QW = B_HEADS * HEAD_DIM
B_KVW = B_KV_HEADS * HEAD_DIM
C_W = C_HEADS * C_KEY
C_VW = C_HEADS * C_VAL

LANES = 128
HALF_WINDOW = 64
TM = 512
BAND_L = 128
DIL_CLASSES = 16
BAND4_ROWS = 32
BAND4_HALO = 16
GQA_TQ = 512
GQA_TK = 512
GQA_CHAIN = 512
FF_CHUNKS = (768, 768, 640, 640)
HG_CHUNK = 64
HG_SUB = 16
HG_TB = 512
HG_HP = 4
NEG = -1e30
LOG2E = 1.4426950408889634
VMEM_LIMIT = 56 * 1024 * 1024

_NT = (((1,), (1,)), ((), ()))
_TN = (((0,), (0,)), ((), ()))


def _params(sem, vmem=None):
    return pltpu.CompilerParams(dimension_semantics=sem, vmem_limit_bytes=vmem)


def _const_spec(shape, single=False):
    nd = len(shape)
    if single:
        return pl.BlockSpec(shape, lambda *_: (0,) * nd, pipeline_mode=pl.Buffered(1))
    return pl.BlockSpec(shape, lambda *_: (0,) * nd)


def _layer_norm(z, g, b):
    mu = jnp.mean(z, -1, keepdims=True)
    d = z - mu
    var = jnp.mean(d * d, -1, keepdims=True)
    return d * lax.rsqrt(var + LN_EPS) * g + b


def _sigmoid(z):
    return 1.0 / (1.0 + jnp.exp(-z))


def _pipelined(sections):
    pending = sections[0][0]()
    for i, (_, epilogue) in enumerate(sections):
        upcoming = sections[i + 1][0]() if i + 1 < len(sections) else None
        epilogue(pending)
        pending = upcoming


def _rope_tables(seq):
    pos = jnp.arange(seq, dtype=F32)
    d = jnp.arange(LANES) % HEAD_DIM

    def angles(p, dim, theta):
        freqs = theta ** (-(jnp.arange(0, dim, 2, dtype=F32) / dim))
        return p[:, None] * freqs[None, :]

    h = ROPE_DIM // 2
    ang = angles(pos, ROPE_DIM, ROPE_THETA)
    a = ang[:, d % h]
    lo, hi = d < h, (d >= h) & (d < ROPE_DIM)
    pc = jnp.where(lo | hi, jnp.cos(a), 1.0)
    psa = jnp.where(lo, -jnp.sin(a), 0.0)
    psb = jnp.where(hi, jnp.sin(a), 0.0)

    q = HEAD_DIM // 4
    row = jnp.floor(pos / GRID_W)
    col = pos - row * GRID_W
    ar = angles(row, HEAD_DIM // 2, AXIAL_THETA)[:, d % q]
    ac_ = angles(col, HEAD_DIM // 2, AXIAL_THETA)[:, d % q]
    a2 = jnp.where(d < HEAD_DIM // 2, ar, ac_)
    first = (d % (HEAD_DIM // 2)) < q
    ac = jnp.cos(a2)
    asa = jnp.where(first, -jnp.sin(a2), 0.0)
    asb = jnp.where(first, 0.0, jnp.sin(a2))
    return [t.astype(F32) for t in (pc, psa, psb, ac, asa, asb)]


def _rope(seg, c, sa, sb, shift):
    return seg * c + pltpu.roll(seg, LANES - shift, 1) * sa + pltpu.roll(seg, shift, 1) * sb


def _ab_in_kernel(x_ref, w_ref, pc_ref, psa_ref, psb_ref, ac_ref, asa_ref, asb_ref, qg_ref, kg_ref,
                  ones_ref, qa_ref, ka_ref, va_ref, qb_ref, kd_ref, vd_ref, qa16_ref, ka16_ref, va16_ref,
                  qs_ref, ks_ref, vs_ref):
    xb = x_ref[...].astype(BF16)

    def proj(a, n):
        return jnp.dot(xb, w_ref[:, a:a + n], preferred_element_type=F32)

    pc, psa, psb = pc_ref[...], psa_ref[...], psb_ref[...]
    ac, asa, asb = ac_ref[...], asa_ref[...], asb_ref[...]

    def dilated_a(h, rope, out_ref, out16_ref, slab_ref):
        for g in range(A_W // LANES):
            sl = slice(g * LANES, (g + 1) * LANES)
            y = _rope(h[:, sl], pc, psa, psb, ROPE_DIM // 2) if rope else h[:, sl]
            out_ref[:, sl] = y.astype(BF16)
            slab_ref[g] = y
        for c in range(DIL_CLASSES):
            for g in range(A_W // LANES):
                rows = slab_ref[g, pl.ds(c, TM // DIL_CLASSES, stride=DIL_CLASSES), :]
                out16_ref[0, c, :, g * LANES:(g + 1) * LANES] = rows.astype(BF16)

    def norm_rope(h, gain, out_ref):
        for g in range(h.shape[1] // LANES):
            sl = slice(g * LANES, (g + 1) * LANES)
            seg = h[:, sl]
            sq = seg * seg
            hi = sq.astype(BF16)
            lo = (sq - hi.astype(F32)).astype(BF16)
            ss = (jnp.dot(hi, ones_ref[...], preferred_element_type=F32)
                  + jnp.dot(lo, ones_ref[...], preferred_element_type=F32))
            y = seg * lax.rsqrt(ss * (1.0 / HEAD_DIM) + RMS_EPS) * gain
            out_ref[:, sl] = _rope(y, ac, asa, asb, HEAD_DIM // 4).astype(BF16)

    def cast_to(out_ref):
        def fn(h):
            out_ref[...] = h.astype(BF16)
        return fn

    o_qb = 3 * A_W
    o_kd = o_qb + B_QW
    o_vd = o_kd + 2 * B_KVW
    _pipelined([
        (functools.partial(proj, 0, A_W),
         functools.partial(dilated_a, rope=True, out_ref=qa_ref, out16_ref=qa16_ref, slab_ref=qs_ref)),
        (functools.partial(proj, A_W, A_W),
         functools.partial(dilated_a, rope=True, out_ref=ka_ref, out16_ref=ka16_ref, slab_ref=ks_ref)),
        (functools.partial(proj, 2 * A_W, A_W),
         functools.partial(dilated_a, rope=False, out_ref=va_ref, out16_ref=va16_ref, slab_ref=vs_ref)),
        (functools.partial(proj, o_qb, B_QW), functools.partial(norm_rope, gain=qg_ref[...], out_ref=qb_ref)),
        (functools.partial(proj, o_kd, 2 * B_KVW), functools.partial(norm_rope, gain=kg_ref[...], out_ref=kd_ref)),
        (functools.partial(proj, o_vd, 2 * B_KVW), cast_to(vd_ref)),
    ])


def _ab_in(x2d, w_ext, tabs, qg, kg, ones2, batch, seq):
    t = x2d.shape[0]
    nt = seq // TM
    tab = pl.BlockSpec((TM, LANES), lambda i: (i % nt, 0))
    row = lambda n: pl.BlockSpec((TM, n), lambda i: (i, 0))
    cls = pl.BlockSpec((1, DIL_CLASSES, TM // DIL_CLASSES, A_W), lambda i: (i // nt, 0, i % nt, 0))
    cls_shape = jax.ShapeDtypeStruct((batch, DIL_CLASSES, seq // DIL_CLASSES, A_W), BF16)
    wn = w_ext.shape[1]
    return pl.pallas_call(
        _ab_in_kernel,
        grid=(t // TM,),
        in_specs=[row(D_MODEL), _const_spec((D_MODEL, wn))] + [tab] * 6
        + [_const_spec((1, LANES)), _const_spec((1, LANES)), _const_spec((LANES, LANES))],
        out_specs=[row(A_W)] * 3 + [row(B_QW), row(2 * B_KVW), row(2 * B_KVW)] + [cls] * 3,
        out_shape=[jax.ShapeDtypeStruct((t, A_W), BF16)] * 3
        + [jax.ShapeDtypeStruct((t, B_QW), BF16)]
        + [jax.ShapeDtypeStruct((t, 2 * B_KVW), BF16)] * 2 + [cls_shape] * 3,
        scratch_shapes=[pltpu.VMEM((A_W // LANES, TM, LANES), F32)] * 3,
        compiler_params=_params(("arbitrary",), VMEM_LIMIT),
        name="ab_in",
    )(x2d, w_ext, *tabs, qg, kg, ones2)


def _band_groups(qs, ks, vs, bias, left):
    chains = [(g, sel) for g in range(len(qs)) for sel in (left, jnp.logical_not(left))]
    scores = []
    for g, sel in chains:
        qm = jnp.where(sel, qs[g], jnp.zeros_like(qs[g]))
        scores.append(lax.dot_general(qm, ks[g], _NT, preferred_element_type=F32) + bias)
    probs = []
    for s in scores:
        mx = jnp.max(s, axis=1, keepdims=True)
        pe = jnp.exp(s - mx)
        probs.append((mx, jnp.sum(pe, axis=1, keepdims=True), pe.astype(BF16)))
    outs = []
    for (g, _), (mx, l, pe) in zip(chains, probs):
        o = jnp.dot(pe, vs[g], preferred_element_type=F32)
        outs.append((o * (1.0 / l), mx + jnp.log(l)))
    return [(jnp.where(left, outs[2 * g][0], outs[2 * g + 1][0]), jnp.where(left, outs[2 * g][1], outs[2 * g + 1][1]))
            for g in range(len(qs))]


def _band_kernel(q_ref, kp_ref, kc_ref, kn_ref, vp_ref, vc_ref, vn_ref, o_ref, lse_ref, *, m):
    nk = BAND_L + 2 * HALF_WINDOW
    base = pl.program_id(len(q_ref.shape) - 2) * BAND_L
    r = lax.broadcasted_iota(jnp.int32, (BAND_L, nk), 0)
    j = lax.broadcasted_iota(jnp.int32, (BAND_L, nk), 1)
    rel = j - HALF_WINDOW - r
    kpos = base - HALF_WINDOW + j
    bias = jnp.where(jnp.abs(rel) <= HALF_WINDOW, 0.0, NEG)
    bias = jnp.where(kpos >= 0, bias, NEG)
    bias = jnp.where(kpos < m, bias, NEG)
    left = lax.broadcasted_iota(jnp.int32, (BAND_L, LANES), 1) < HEAD_DIM
    pre = (0,) * (len(q_ref.shape) - 2)
    ats = [pre + (slice(None), slice(p * LANES, (p + 1) * LANES)) for p in range(A_W // LANES)]
    ks = [jnp.concatenate([kp_ref[at], kc_ref[at], kn_ref[at]], axis=0) for at in ats]
    vs = [jnp.concatenate([vp_ref[at], vc_ref[at], vn_ref[at]], axis=0) for at in ats]
    for at, (o, lse) in zip(ats, _band_groups([q_ref[at] for at in ats], ks, vs, bias, left)):
        o_ref[at] = o.astype(BF16)
        lse_ref[at] = lse


def _band_specs(lead, rows):
    nh = rows // HALF_WINDOW
    ratio = BAND_L // HALF_WINDOW
    ones = (1,) * lead
    cur = pl.BlockSpec(ones + (BAND_L, A_W), lambda *g: g[:lead] + (g[-1], 0))
    prev = pl.BlockSpec(ones + (HALF_WINDOW, A_W), lambda *g: g[:lead] + (jnp.maximum(g[-1] * ratio - 1, 0), 0))
    nxt = pl.BlockSpec(ones + (HALF_WINDOW, A_W), lambda *g: g[:lead] + (jnp.minimum((g[-1] + 1) * ratio, nh - 1), 0))
    return cur, prev, nxt


def _band1(q, k, v, batch, seq):
    cur, prev, nxt = _band_specs(1, seq)
    view = lambda t: t.reshape(batch, seq, A_W)
    o, lse = pl.pallas_call(
        functools.partial(_band_kernel, m=seq),
        grid=(batch, seq // BAND_L),
        in_specs=[cur, prev, cur, nxt, prev, cur, nxt],
        out_specs=[cur, cur],
        out_shape=[jax.ShapeDtypeStruct((batch, seq, A_W), BF16), jax.ShapeDtypeStruct((batch, seq, A_W), F32)],
        compiler_params=_params(("arbitrary",) * 2),
        name="band_d1",
    )(view(q), view(k), view(k), view(k), view(v), view(v), view(v))
    return o.reshape(batch * seq, A_W), lse.reshape(batch * seq, A_W)


def _band16(q16, k16, v16, batch, seq):
    m = seq // DIL_CLASSES
    cur, prev, nxt = _band_specs(2, m)
    shape = (batch, DIL_CLASSES, m, A_W)
    return pl.pallas_call(
        functools.partial(_band_kernel, m=m),
        grid=(batch, DIL_CLASSES, m // BAND_L),
        in_specs=[cur, prev, cur, nxt, prev, cur, nxt],
        out_specs=[cur, cur],
        out_shape=[jax.ShapeDtypeStruct(shape, BF16), jax.ShapeDtypeStruct(shape, F32)],
        compiler_params=_params(("arbitrary",) * 3),
        name="band_d16",
    )(q16, k16, k16, k16, v16, v16, v16)


def _band4_kernel(q_ref, kp_ref, kc_ref, kn_ref, vp_ref, vc_ref, vn_ref, o16_ref, l16_ref, o_ref, lse_ref, *, m16):
    qr = BAND4_ROWS
    kr = qr + 2 * BAND4_HALO
    i = pl.program_id(2)
    shift_q, shift_k = qr.bit_length() - 1, kr.bit_length() - 1
    row = lax.broadcasted_iota(jnp.int32, (4 * qr, 4 * kr), 0)
    col = lax.broadcasted_iota(jnp.int32, (4 * qr, 4 * kr), 1)
    uq, rq = row >> shift_q, row & (qr - 1)
    uk, jk = col >> shift_k, col & (kr - 1)
    rel = 4 * (jk - BAND4_HALO - rq) + (uk - uq)
    kpos = i * qr - BAND4_HALO + jk
    bias = jnp.where(jnp.abs(rel) <= HALF_WINDOW, 0.0, NEG)
    bias = jnp.where(kpos >= 0, bias, NEG)
    bias = jnp.where(kpos < m16, bias, NEG)
    left = lax.broadcasted_iota(jnp.int32, (4 * qr, LANES), 1) < HEAD_DIM
    slabs = [slice(p * LANES, (p + 1) * LANES) for p in range(A_W // LANES)]

    def gather(refs, sl):
        return jnp.concatenate([r[0, u, 0, :, sl] for u in range(4) for r in refs], axis=0)

    branch4 = _band_groups([gather([q_ref], sl) for sl in slabs],
                           [gather([kp_ref, kc_ref, kn_ref], sl) for sl in slabs],
                           [gather([vp_ref, vc_ref, vn_ref], sl) for sl in slabs], bias, left)
    for sl, (o4, l4) in zip(slabs, branch4):
        o16 = gather([o16_ref], sl).astype(F32)
        l16 = gather([l16_ref], sl)
        mx = jnp.maximum(l4, l16)
        w4, w16 = jnp.exp(l4 - mx), jnp.exp(l16 - mx)
        tot = w4 + w16
        om = (w4 * o4 + w16 * o16) * (1.0 / tot)
        lm = mx + jnp.log(tot)
        for u in range(4):
            o_ref[0, u, 0, :, sl] = om[u * qr:(u + 1) * qr].astype(BF16)
            lse_ref[0, u, 0, :, sl] = lm[u * qr:(u + 1) * qr]


def _band4(q16, k16, v16, o16, l16, batch, seq):
    m16 = seq // DIL_CLASSES
    ratio = BAND4_ROWS // BAND4_HALO
    nh = m16 // BAND4_HALO
    view = lambda t: t.reshape(batch, 4, 4, m16, A_W)
    cur = pl.BlockSpec((1, 4, 1, BAND4_ROWS, A_W), lambda b, c, i: (b, 0, c, i, 0))
    prev = pl.BlockSpec((1, 4, 1, BAND4_HALO, A_W), lambda b, c, i: (b, 0, c, jnp.maximum(i * ratio - 1, 0), 0))
    nxt = pl.BlockSpec((1, 4, 1, BAND4_HALO, A_W), lambda b, c, i: (b, 0, c, jnp.minimum((i + 1) * ratio, nh - 1), 0))
    shape = (batch, 4, 4, m16, A_W)
    o, lse = pl.pallas_call(
        functools.partial(_band4_kernel, m16=m16),
        grid=(batch, 4, m16 // BAND4_ROWS),
        in_specs=[cur, prev, cur, nxt, prev, cur, nxt, cur, cur],
        out_specs=[cur, cur],
        out_shape=[jax.ShapeDtypeStruct(shape, BF16), jax.ShapeDtypeStruct(shape, F32)],
        compiler_params=_params(("arbitrary",) * 3),
        name="band_d4",
    )(view(q16), view(k16), view(k16), view(k16), view(v16), view(v16), view(v16), view(o16), view(l16))
    return o.reshape(batch, DIL_CLASSES, m16, A_W), lse.reshape(batch, DIL_CLASSES, m16, A_W)


def _gqa_kernel(q_ref, k_ref, v_ref, o_ref, qs_ref, m_ref, acc_ref, s_ref, *, seq):
    left = lax.broadcasted_iota(jnp.int32, (GQA_TQ, LANES), 1) < HEAD_DIM
    for pr in range(2):
        qp = q_ref[0, :, pr * LANES:(pr + 1) * LANES]
        zero = jnp.zeros_like(qp)
        qs_ref[(2 * pr) * GQA_TQ:(2 * pr + 1) * GQA_TQ, :] = jnp.where(left, qp, zero)
        qs_ref[(2 * pr + 1) * GQA_TQ:(2 * pr + 2) * GQA_TQ, :] = jnp.where(left, zero, qp)
    m_ref[...] = jnp.full(m_ref.shape, NEG, F32)
    acc_ref[...] = jnp.zeros(acc_ref.shape, F32)
    ones_lane = lax.broadcasted_iota(jnp.int32, (GQA_TK, LANES), 1) == HEAD_DIM

    chains = [slice(c * GQA_CHAIN, (c + 1) * GQA_CHAIN) for c in range(4 * GQA_TQ // GQA_CHAIN)]

    def scores(tile, buf, cols):
        off = pl.multiple_of(tile * GQA_TK, GQA_TK)
        k = k_ref[0, pl.ds(off, GQA_TK), :]
        s_ref[buf, :, cols] = lax.dot_general(k, qs_ref[cols, :], _NT, preferred_element_type=F32)

    def softmax_pv(tile, buf, cols):
        off = pl.multiple_of(tile * GQA_TK, GQA_TK)
        v = jnp.where(ones_lane, jnp.ones((), BF16), v_ref[0, pl.ds(off, GQA_TK), :])
        s = s_ref[buf, :, cols]
        m_prev = m_ref[:, cols]
        m_new = jnp.maximum(m_prev, jnp.max(s, axis=0, keepdims=True))
        alpha = jnp.exp2(m_prev - m_new)
        p = jnp.exp2((s - m_new).astype(BF16))
        pv = lax.dot_general(v, p, _TN, preferred_element_type=F32)
        acc_ref[:, cols] = alpha * acc_ref[:, cols] + pv
        m_ref[:, cols] = m_new

    def step(tile, buf, prefetch):
        for cols in chains:
            if prefetch:
                scores(tile + 1, 1 - buf, cols)
            softmax_pv(tile, buf, cols)

    def pair(j, carry):
        step(2 * j, 0, True)
        step(2 * j + 1, 1, True)
        return carry

    n_tiles = seq // GQA_TK
    for cols in chains:
        scores(0, 0, cols)
    lax.fori_loop(0, n_tiles // 2 - 1, pair, 0)
    step(n_tiles - 2, 0, True)
    step(n_tiles - 1, 1, False)
    acc = acc_ref[...]
    a = (acc * (1.0 / acc[HEAD_DIM:HEAD_DIM + 1, :])).T
    for pr in range(2):
        lo = a[(2 * pr) * GQA_TQ:(2 * pr + 1) * GQA_TQ]
        hi = pltpu.roll(a[(2 * pr + 1) * GQA_TQ:(2 * pr + 2) * GQA_TQ], HEAD_DIM, 1)
        o_ref[0, :, pr * LANES:(pr + 1) * LANES] = jnp.where(left, lo, hi).astype(BF16)


def _gqa(qb, kd, vd, batch, seq):
    gw = B_QW // B_KV_HEADS
    cols = 4 * GQA_TQ
    out = pl.pallas_call(
        functools.partial(_gqa_kernel, seq=seq),
        grid=(batch, B_KV_HEADS, seq // GQA_TQ),
        in_specs=[pl.BlockSpec((1, GQA_TQ, gw), lambda b, g, qi: (b, qi, g)),
                  pl.BlockSpec((1, seq, LANES), lambda b, g, qi: (b, 0, g)),
                  pl.BlockSpec((1, seq, LANES), lambda b, g, qi: (b, 0, g))],
        out_specs=pl.BlockSpec((1, GQA_TQ, gw), lambda b, g, qi: (b, qi, g)),
        out_shape=jax.ShapeDtypeStruct((batch, seq, B_QW), BF16),
        scratch_shapes=[pltpu.VMEM((cols, LANES), BF16), pltpu.VMEM((1, cols), F32),
                        pltpu.VMEM((LANES, cols), F32), pltpu.VMEM((2, GQA_TK, cols), F32)],
        compiler_params=_params(("arbitrary",) * 3, VMEM_LIMIT),
        name="gqa",
    )(qb.reshape(batch, seq, B_QW), kd.reshape(batch, seq, 2 * B_KVW), vd.reshape(batch, seq, 2 * B_KVW))
    return out.reshape(batch * seq, B_QW)


def _ab_out_kernel(o1_ref, l1_ref, om_ref, lm_ref, yb_ref, x_ref, w_ref, g_ref, b_ref, out_ref, os_ref, ls_ref):
    for c in range(DIL_CLASSES):
        for g in range(A_W // LANES):
            sl = slice(g * LANES, (g + 1) * LANES)
            rows = pl.ds(c, TM // DIL_CLASSES, stride=DIL_CLASSES)
            os_ref[g, rows, :] = om_ref[0, c, :, sl].astype(F32)
            ls_ref[g, rows, :] = lm_ref[0, c, :, sl]
    ya = []
    for g in range(A_W // LANES):
        sl = slice(g * LANES, (g + 1) * LANES)
        l1, lm = l1_ref[:, sl], ls_ref[g]
        mx = jnp.maximum(l1, lm)
        e1, em = jnp.exp(l1 - mx), jnp.exp(lm - mx)
        ya.append(((e1 * o1_ref[:, sl].astype(F32) + em * os_ref[g]) * (1.0 / (e1 + em))).astype(BF16))
    y = jnp.dot(jnp.concatenate(ya, axis=1), w_ref[0:A_W, :], preferred_element_type=F32)
    y = y + jnp.dot(yb_ref[...], w_ref[A_W:A_W + B_QW, :], preferred_element_type=F32)
    out_ref[...] = _layer_norm(ALPHA * x_ref[...] + y, g_ref[...], b_ref[...])


def _ab_out(o1, l1, om, lm, yb, x2d, w, g, b, seq):
    t = x2d.shape[0]
    nt = seq // TM
    row = lambda n: pl.BlockSpec((TM, n), lambda i: (i, 0))
    cls = pl.BlockSpec((1, DIL_CLASSES, TM // DIL_CLASSES, A_W), lambda i: (i // nt, 0, i % nt, 0))
    return pl.pallas_call(
        _ab_out_kernel,
        grid=(t // TM,),
        in_specs=[row(A_W), row(A_W), cls, cls, row(B_QW), row(D_MODEL), _const_spec((A_W + B_QW, D_MODEL)),
                  _const_spec((1, D_MODEL)), _const_spec((1, D_MODEL))],
        out_specs=row(D_MODEL),
        out_shape=jax.ShapeDtypeStruct((t, D_MODEL), F32),
        scratch_shapes=[pltpu.VMEM((A_W // LANES, TM, LANES), F32)] * 2,
        compiler_params=_params(("arbitrary",), VMEM_LIMIT),
        name="ab_out",
    )(o1, l1, om, lm, yb, x2d, w, g, b)


def _gelu(x):
    return 0.5 * x * (1.0 + jnp.tanh(0.7978845608028654 * (x + 0.044715 * (x * x * x))))


def _ffn_kernel(x_ref, xp_ref, xn_ref, wu_ref, cw_ref, cb_ref, wd_ref, g_ref, b_ref, out_ref, *, nt):
    i = pl.program_id(0)
    keep_prev = (i % nt != 0).astype(F32)
    keep_next = (i % nt != nt - 1).astype(F32)
    x = x_ref[...]
    xb = x.astype(BF16)
    halo = jnp.concatenate([xp_ref[...], xn_ref[...]], axis=0).astype(BF16)
    xe = jnp.concatenate([xb, halo], axis=0)
    rows = lax.broadcasted_iota(jnp.int32, (TM, 1), 0)
    offsets = [sum(FF_CHUNKS[:c]) for c in range(len(FF_CHUNKS))]

    def up(c):
        a, n = offsets[c], FF_CHUNKS[c]
        u = jnp.dot(xb, wu_ref[:, a:a + n], preferred_element_type=F32)
        ge = jnp.dot(xe, wu_ref[:, D_FF + a:D_FF + a + n], preferred_element_type=F32)
        return u, ge

    def gate(c, u, ge):
        a, n = offsets[c], FF_CHUNKS[c]
        gm = ge[0:TM]
        g_before = ge[TM + 7:TM + 8] * keep_prev
        g_after = ge[TM + 8:TM + 9] * keep_next
        gp = jnp.where(rows == 0, g_before, pltpu.roll(gm, 1, 0))
        gn = jnp.where(rows == TM - 1, g_after, pltpu.roll(gm, TM - 1, 0))
        cw = cw_ref[:, a:a + n]
        gc = gp * cw[0:1] + gm * cw[1:2] + gn * cw[2:3] + cb_ref[:, a:a + n]
        return (_gelu(gc) * u).astype(BF16)

    acc = None
    pending = up(0)
    for c in range(len(FF_CHUNKS)):
        upcoming = up(c + 1) if c + 1 < len(FF_CHUNKS) else None
        act = gate(c, *pending)
        part = jnp.dot(act, wd_ref[offsets[c]:offsets[c] + FF_CHUNKS[c], :], preferred_element_type=F32)
        acc = part if acc is None else acc + part
        pending = upcoming
    out_ref[...] = _layer_norm(ALPHA * x + acc, g_ref[...], b_ref[...])


def _ffn(x2d, wu, cw, cb, wd, g, b, seq):
    t = x2d.shape[0]
    nt = seq // TM
    r8 = TM // 8
    return pl.pallas_call(
        functools.partial(_ffn_kernel, nt=nt),
        grid=(t // TM,),
        in_specs=[pl.BlockSpec((TM, D_MODEL), lambda i: (i, 0)),
                  pl.BlockSpec((8, D_MODEL), lambda i: (jnp.maximum(i * r8 - 1, 0), 0)),
                  pl.BlockSpec((8, D_MODEL), lambda i: (jnp.minimum((i + 1) * r8, t // 8 - 1), 0)),
                  _const_spec((D_MODEL, 2 * D_FF), single=True),
                  _const_spec((3, D_FF)), _const_spec((1, D_FF)),
                  _const_spec((D_FF, D_MODEL), single=True),
                  _const_spec((1, D_MODEL)), _const_spec((1, D_MODEL))],
        out_specs=pl.BlockSpec((TM, D_MODEL), lambda i: (i, 0)),
        out_shape=jax.ShapeDtypeStruct((t, D_MODEL), F32),
        compiler_params=_params(("arbitrary",), VMEM_LIMIT),
        name="ffn",
    )(x2d, x2d, x2d, wu, cw, cb, wd, g, b)


def _lower_bound(tbl, layer):
    rows = [tbl[r:r + 1] for r in range(DEPTH)]
    mx = functools.reduce(jnp.maximum, rows)
    es = [jnp.exp(r - mx) for r in rows]
    inv = 1.0 / functools.reduce(lambda a, b: a + b, es)
    ps = [e * inv for e in es]
    return functools.reduce(lambda a, b: a + b, ps[:layer + 1]) - ps[0]


def _c_in_kernel(x_ref, w_ref, lbf_ref, lbb_ref, q_ref, lf_ref, lb_ref, v_ref, g_ref, *, layer):
    xb = x_ref[...].astype(BF16)
    half = C_W // 2

    def proj(a, n):
        return jnp.dot(xb, w_ref[:, a:a + n], preferred_element_type=F32)

    def silu(z):
        return z * _sigmoid(z)

    lbf = _lower_bound(lbf_ref[...], layer)
    lbb = _lower_bound(lbb_ref[...], layer)

    def store_q(z, sl):
        q_ref[:, sl] = z.astype(BF16)

    def store_logf(tbl, out_ref):
        def fn(z, sl):
            lb = tbl[:, sl]
            out_ref[:, sl] = jnp.log(lb + (1.0 - lb) * _sigmoid(z))
        return fn

    def store_silu(out_ref):
        def fn(z, sl):
            out_ref[:, sl] = silu(z).astype(BF16)
        return fn

    sections = []
    for c in range(2):
        a = c * half
        sl = slice(a, a + half)
        for off, fn in ((0, store_q), (C_W, store_logf(lbf, lf_ref)), (2 * C_W, store_logf(lbb, lb_ref)),
                        (3 * C_W, store_silu(v_ref)), (3 * C_W + C_VW, store_silu(g_ref))):
            sections.append((functools.partial(proj, off + a, half), functools.partial(fn, sl=sl)))
    _pipelined(sections)


def _c_in(x2d, w, lbf, lbb, layer):
    t = x2d.shape[0]
    row = pl.BlockSpec((TM, C_W), lambda i: (i, 0))
    return pl.pallas_call(
        functools.partial(_c_in_kernel, layer=layer),
        grid=(t // TM,),
        in_specs=[row, _const_spec(w.shape), _const_spec((DEPTH, C_W)), _const_spec((DEPTH, C_W))],
        out_specs=[row] * 5,
        out_shape=[jax.ShapeDtypeStruct((t, C_W), d) for d in (BF16, F32, F32, BF16, BF16)],
        compiler_params=_params(("arbitrary",), VMEM_LIMIT),
        name="c_in",
    )(x2d, w, lbf, lbb)


def _hgrn_kernel(*refs, rev, fused):
    if fused:
        q_ref, lf_ref, v_ref, of_ref, gate_ref, gn_ref, y_ref, st_ref = refs
    else:
        q_ref, lf_ref, v_ref, y_ref, st_ref = refs

    @pl.when(pl.program_id(2) == 0)
    def _():
        st_ref[...] = jnp.zeros(st_ref.shape, F32)

    ri = lax.broadcasted_iota(jnp.int32, (HG_CHUNK, HG_CHUNK), 0)
    ci = lax.broadcasted_iota(jnp.int32, (HG_CHUNK, HG_CHUNK), 1)
    tri = (ci >= ri) if rev else (ci <= ri)
    trib = jnp.where(tri, 1.0, 0.0).astype(BF16)
    dotf = lambda a, b: jnp.dot(a, b, preferred_element_type=F32)
    n_chunks = HG_TB // HG_CHUNK
    order = range(n_chunks - 1, -1, -1) if rev else range(n_chunks)
    units = [(c, h) for c in order for h in range(HG_HP)]
    window = lambda u: (0, slice(u[0] * HG_CHUNK, (u[0] + 1) * HG_CHUNK), slice(u[1] * C_KEY, (u[1] + 1) * C_KEY))

    b_all = {}
    for u in units:
        lf = lf_ref[window(u)]
        hi = lf.astype(BF16)
        r1 = lf - hi.astype(F32)
        mid = r1.astype(BF16)
        lo = (r1 - mid.astype(F32)).astype(BF16)
        b_all[u] = dotf(trib, hi) + dotf(trib, mid) + dotf(trib, lo)

    qdec, kdec, etot, a_rows = {}, {}, {}, {}
    for u in units:
        lf, b = lf_ref[window(u)], b_all[u]
        qf = q_ref[window(u)].astype(F32)
        bex = b - lf
        kk = 1.0 - jnp.exp(lf)
        btot = b[0:1] if rev else b[HG_CHUNK - 1:HG_CHUNK]
        qdec[u] = (qf * jnp.exp(b)).astype(BF16)
        kdec[u] = (kk * jnp.exp(btot - b)).astype(BF16)
        etot[u] = jnp.exp(btot)
        rows = []
        for blk in range(HG_CHUNK // HG_SUB):
            r0 = blk * HG_SUB
            ref = bex[r0 + HG_SUB - 1:r0 + HG_SUB] if rev else bex[r0:r0 + 1]
            qt = (qf[r0:r0 + HG_SUB] * jnp.exp(b[r0:r0 + HG_SUB] - ref)).astype(BF16)
            lo_r, hi_r = (r0, HG_CHUNK) if rev else (0, r0 + HG_SUB)
            ks = (kk[lo_r:hi_r] * jnp.exp(ref - b[lo_r:hi_r])).astype(BF16)
            pieces = []
            if lo_r > 0:
                pieces.append(jnp.zeros((lo_r, C_KEY), BF16))
            pieces.append(ks)
            if hi_r < HG_CHUNK:
                pieces.append(jnp.zeros((HG_CHUNK - hi_r, C_KEY), BF16))
            kfull = jnp.concatenate(pieces, axis=0) if len(pieces) > 1 else ks
            rows.append(lax.dot_general(qt, kfull, _NT, preferred_element_type=F32))
        a_rows[u] = rows
    attn = {u: jnp.where(tri, jnp.concatenate(a_rows[u], axis=0), 0.0).astype(BF16) for u in units}

    states = [st_ref[h] for h in range(HG_HP)]
    for u in units:
        h = u[1]
        vb = v_ref[window(u)]
        o = lax.dot_general(qdec[u], states[h].astype(BF16), _NT, preferred_element_type=F32)
        o = o + dotf(attn[u], vb)
        states[h] = states[h] * etot[u] + lax.dot_general(vb, kdec[u], _TN, preferred_element_type=F32)
        if fused:
            tot = of_ref[window(u)] + o
            inv = lax.rsqrt(jnp.mean(tot * tot, -1, keepdims=True) + RMS_EPS)
            y = tot * inv * gn_ref[:, window(u)[2]] * gate_ref[window(u)].astype(F32)
            y_ref[window(u)] = y.astype(BF16)
        else:
            y_ref[window(u)] = o
    for h in range(HG_HP):
        st_ref[h] = states[h]


def _hgrn(q, lf, v, batch, seq, rev, fused_args=None):
    nb = seq // HG_TB
    width = HG_HP * C_KEY
    blk = (lambda b, h, i: (b, nb - 1 - i, h)) if rev else (lambda b, h, i: (b, i, h))
    tile = pl.BlockSpec((1, HG_TB, width), blk)
    view = lambda t: t.reshape(batch, seq, C_W)
    ins, specs = [view(q), view(lf), view(v)], [tile, tile, tile]
    fused = fused_args is not None
    if fused:
        o_f, gate, gn = fused_args
        ins += [view(o_f), view(gate), gn]
        specs += [tile, tile, pl.BlockSpec((1, width), lambda b, h, i: (0, h))]
    out = pl.pallas_call(
        functools.partial(_hgrn_kernel, rev=rev, fused=fused),
        grid=(batch, C_HEADS // HG_HP, nb),
        in_specs=specs,
        out_specs=tile,
        out_shape=jax.ShapeDtypeStruct((batch, seq, C_VW), BF16 if fused else F32),
        scratch_shapes=[pltpu.VMEM((HG_HP, C_VAL, C_KEY), F32)],
        compiler_params=_params(("arbitrary",) * 3),
        name="hgrn_bwd" if rev else "hgrn_fwd",
    )(*ins)
    return out.reshape(batch * seq, C_VW)


def _proj_ln_kernel(a_ref, x_ref, w_ref, g_ref, b_ref, out_ref):
    y = jnp.dot(a_ref[...], w_ref[...], preferred_element_type=F32)
    out_ref[...] = _layer_norm(ALPHA * x_ref[...] + y, g_ref[...], b_ref[...])


def _proj_ln(a, x2d, w, g, b):
    t = x2d.shape[0]
    kdim = a.shape[1]
    return pl.pallas_call(
        _proj_ln_kernel,
        grid=(t // TM,),
        in_specs=[pl.BlockSpec((TM, kdim), lambda i: (i, 0)), pl.BlockSpec((TM, D_MODEL), lambda i: (i, 0)),
                  _const_spec((kdim, D_MODEL)), _const_spec((1, D_MODEL)), _const_spec((1, D_MODEL))],
        out_specs=pl.BlockSpec((TM, D_MODEL), lambda i: (i, 0)),
        out_shape=jax.ShapeDtypeStruct((t, D_MODEL), F32),
        compiler_params=_params(("arbitrary",), VMEM_LIMIT),
        name="c_out",
    )(a, x2d, w, g, b)


def _prep_weights(w_in_ab, w_out_ab, qn_ab, kn_ab, w_in_c, w_out_c, gn_c, ffn_w_up, ffn_w_down):
    scale = HEAD_DIM ** -0.5
    ab = []
    for j in range(w_in_ab.shape[0]):
        w = w_in_ab[j]
        o3 = 3 * A_W
        kb = w[:, o3 + B_QW:o3 + B_QW + B_KVW]
        vb = w[:, o3 + B_QW + B_KVW:]
        dup = lambda t: jnp.concatenate([t[:, :HEAD_DIM], t[:, :HEAD_DIM], t[:, HEAD_DIM:], t[:, HEAD_DIM:]], 1)
        w_ext = jnp.concatenate([w[:, :A_W] * scale, w[:, A_W:o3 + B_QW], dup(kb), dup(vb)], 1).astype(BF16)
        qg = (jnp.concatenate([qn_ab[j], qn_ab[j]]) * (scale * LOG2E)).reshape(1, LANES)
        kg = jnp.concatenate([kn_ab[j], kn_ab[j]]).reshape(1, LANES)
        ab.append((w_ext, qg, kg, w_out_ab[j].astype(BF16)))
    cc = [(w_in_c[j].astype(BF16), w_out_c[j].astype(BF16), gn_c[j].reshape(1, C_VW)) for j in range(w_in_c.shape[0])]
    return ab, cc, ffn_w_up.astype(BF16), ffn_w_down.astype(BF16)


def _trunk(x, prep, lb_fwd, lb_bwd, ln_mix_g, ln_mix_b, ln_ffn_g, ln_ffn_b, ffn_conv_w, ffn_conv_b):
    ab, cc, wu, wd = prep
    batch, seq, _ = x.shape
    x2d = x.reshape(batch * seq, D_MODEL)
    tabs = _rope_tables(seq)
    blk = jnp.arange(LANES) // HEAD_DIM
    ones2 = (blk[:, None] == blk[None, :]).astype(BF16)
    vec = lambda t: t.reshape(1, -1)
    for l in range(DEPTH):
        j = l // 2
        if l % 2 == 0:
            w_ext, qg, kg, w_out = ab[j]
            assert [d for _, d in A_PATTERNS] == [1, 4, DIL_CLASSES]
            assert all(window // (2 * d) == HALF_WINDOW for window, d in A_PATTERNS)
            qa, ka, va, qb, kd, vd, qa16, ka16, va16 = _ab_in(x2d, w_ext, tabs, qg, kg, ones2, batch, seq)
            o1, l1 = _band1(qa, ka, va, batch, seq)
            o16, l16 = _band16(qa16, ka16, va16, batch, seq)
            om, lm = _band4(qa16, ka16, va16, o16, l16, batch, seq)
            yb = _gqa(qb, kd, vd, batch, seq)
            x2d = _ab_out(o1, l1, om, lm, yb, x2d, w_out, vec(ln_mix_g[l]), vec(ln_mix_b[l]), seq)
        else:
            w_in, w_out, gn = cc[j]
            q, lf, lb, v, gate = _c_in(x2d, w_in, lb_fwd, lb_bwd, l)
            o_f = _hgrn(q, lf, v, batch, seq, rev=False)
            y = _hgrn(q, lb, v, batch, seq, rev=True, fused_args=(o_f, gate, gn))
            x2d = _proj_ln(y, x2d, w_out, vec(ln_mix_g[l]), vec(ln_mix_b[l]))
        x2d = _ffn(x2d, wu[l], ffn_conv_w[l], vec(ffn_conv_b[l]), wd[l], vec(ln_ffn_g[l]), vec(ln_ffn_b[l]), seq)
    return x2d.reshape(batch, seq, D_MODEL)


def kernel(x_prompt, x_sample, w_in_ab, w_out_ab, qn_ab, kn_ab, w_in_c, w_out_c, lb_fwd, lb_bwd, gn_c, ln_mix_g, ln_mix_b, ln_ffn_g, ln_ffn_b, ffn_w_up, ffn_conv_w, ffn_conv_b, ffn_w_down):
    prep = _prep_weights(w_in_ab, w_out_ab, qn_ab, kn_ab, w_in_c, w_out_c, gn_c, ffn_w_up, ffn_w_down)
    rest = (lb_fwd, lb_bwd, ln_mix_g, ln_mix_b, ln_ffn_g, ln_ffn_b, ffn_conv_w, ffn_conv_b)
    return (_trunk(x_prompt, prep, *rest), _trunk(x_sample, prep, *rest))
```

```python
import functools

import jax
import jax.numpy as jnp
from jax import lax
from jax.experimental import pallas as pl
from jax.experimental.pallas import tpu as pltpu

F32 = jnp.float32
BF16 = jnp.bfloat16

D_MODEL = 1024
DEPTH = 2
HEAD_DIM = 64
A_HEADS = 8
A_PATTERNS = ((128, 1), (512, 4), (2048, 16))
B_HEADS = 8
B_KV_HEADS = 2
GRID_W = 64
ROPE_THETA = 500000.0
ROPE_DIM = HEAD_DIM // 4
AXIAL_THETA = 10000.0
C_HEADS = 8
C_KEY = 128
C_VAL = 128
D_FF = 2816
ALPHA = (2 * DEPTH) ** 0.25
LN_EPS = 1e-5
RMS_EPS = 1e-6
A_W = A_HEADS * HEAD_DIM
B_QW = B_HEADS * HEAD_DIM
B_KVW = B_KV_HEADS * HEAD_DIM
C_W = C_HEADS * C_KEY
C_VW = C_HEADS * C_VAL

LANES = 128
HALF_WINDOW = 64
TM = 512
BAND_L = 128
BAND_STEP = 512
BAND4_STEP = 128
DIL_CLASSES = 16
BAND4_ROWS = 32
BAND4_HALO = 16
GQA_TQ = 512
GQA_TK = 512
GQA_CHAIN = 512
FF_CHUNKS = (768, 768, 640, 640)
HG_CHUNK = 64
HG_SUB = 16
HG_TB = 512
HG_HP = 4
NEG = -1e30
LOG2E = 1.4426950408889634
VMEM_LIMIT = 56 * 1024 * 1024

_NT = (((1,), (1,)), ((), ()))
_TN = (((0,), (0,)), ((), ()))


def _params(sem, vmem=None):
    return pltpu.CompilerParams(dimension_semantics=sem, vmem_limit_bytes=vmem)


def _const_spec(shape, single=False):
    nd = len(shape)
    if single:
        return pl.BlockSpec(shape, lambda *_: (0,) * nd, pipeline_mode=pl.Buffered(1))
    return pl.BlockSpec(shape, lambda *_: (0,) * nd)


def _layer_norm(z, g, b):
    mu = jnp.mean(z, -1, keepdims=True)
    d = z - mu
    var = jnp.mean(d * d, -1, keepdims=True)
    return d * lax.rsqrt(var + LN_EPS) * g + b


def _sigmoid(z):
    return 1.0 / (1.0 + jnp.exp(-z))


def _pipelined(sections):
    pending = sections[0][0]()
    for i, (_, epilogue) in enumerate(sections):
        upcoming = sections[i + 1][0]() if i + 1 < len(sections) else None
        epilogue(pending)
        pending = upcoming


def _rope_tables(seq):
    pos = jnp.arange(seq, dtype=F32)
    d = jnp.arange(LANES) % HEAD_DIM

    def angles(p, dim, theta):
        freqs = theta ** (-(jnp.arange(0, dim, 2, dtype=F32) / dim))
        return p[:, None] * freqs[None, :]

    h = ROPE_DIM // 2
    ang = angles(pos, ROPE_DIM, ROPE_THETA)
    a = ang[:, d % h]
    lo, hi = d < h, (d >= h) & (d < ROPE_DIM)
    pc = jnp.where(lo | hi, jnp.cos(a), 1.0)
    psa = jnp.where(lo, -jnp.sin(a), 0.0)
    psb = jnp.where(hi, jnp.sin(a), 0.0)

    q = HEAD_DIM // 4
    row = jnp.floor(pos / GRID_W)
    col = pos - row * GRID_W
    ar = angles(row, HEAD_DIM // 2, AXIAL_THETA)[:, d % q]
    ac_ = angles(col, HEAD_DIM // 2, AXIAL_THETA)[:, d % q]
    a2 = jnp.where(d < HEAD_DIM // 2, ar, ac_)
    first = (d % (HEAD_DIM // 2)) < q
    ac = jnp.cos(a2)
    asa = jnp.where(first, -jnp.sin(a2), 0.0)
    asb = jnp.where(first, 0.0, jnp.sin(a2))
    return [t.astype(F32) for t in (pc, psa, psb, ac, asa, asb)]


def _rope(seg, c, sa, sb, shift):
    return seg * c + pltpu.roll(seg, LANES - shift, 1) * sa + pltpu.roll(seg, shift, 1) * sb


def _ab_in_kernel(x_ref, w_ref, pc_ref, psa_ref, psb_ref, ac_ref, asa_ref, asb_ref, qg_ref, kg_ref,
                  ones_ref, qa_ref, ka_ref, va_ref, qb_ref, kd_ref, vd_ref, qa16_ref, ka16_ref, va16_ref,
                  qs_ref, ks_ref, vs_ref):
    xb = x_ref[...].astype(BF16)

    def proj(a, n):
        return jnp.dot(xb, w_ref[:, a:a + n], preferred_element_type=F32)

    pc, psa, psb = pc_ref[...], psa_ref[...], psb_ref[...]
    ac, asa, asb = ac_ref[...], asa_ref[...], asb_ref[...]

    def dilated_a(h, rope, out_ref, out16_ref, slab_ref):
        for g in range(A_W // LANES):
            sl = slice(g * LANES, (g + 1) * LANES)
            y = _rope(h[:, sl], pc, psa, psb, ROPE_DIM // 2) if rope else h[:, sl]
            out_ref[:, sl] = y.astype(BF16)
            slab_ref[g] = y
        for c in range(DIL_CLASSES):
            for g in range(A_W // LANES):
                rows = slab_ref[g, pl.ds(c, TM // DIL_CLASSES, stride=DIL_CLASSES), :]
                out16_ref[0, c, :, g * LANES:(g + 1) * LANES] = rows.astype(BF16)

    def norm_rope(h, gain, out_ref):
        for g in range(h.shape[1] // LANES):
            sl = slice(g * LANES, (g + 1) * LANES)
            seg = h[:, sl]
            sq = seg * seg
            hi = sq.astype(BF16)
            lo = (sq - hi.astype(F32)).astype(BF16)
            ss = (jnp.dot(hi, ones_ref[...], preferred_element_type=F32)
                  + jnp.dot(lo, ones_ref[...], preferred_element_type=F32))
            y = seg * lax.rsqrt(ss * (1.0 / HEAD_DIM) + RMS_EPS) * gain
            out_ref[:, sl] = _rope(y, ac, asa, asb, HEAD_DIM // 4).astype(BF16)

    def cast_to(out_ref):
        def fn(h):
            out_ref[...] = h.astype(BF16)
        return fn

    o_qb = 3 * A_W
    o_kd = o_qb + B_QW
    o_vd = o_kd + 2 * B_KVW
    _pipelined([
        (functools.partial(proj, 0, A_W),
         functools.partial(dilated_a, rope=True, out_ref=qa_ref, out16_ref=qa16_ref, slab_ref=qs_ref)),
        (functools.partial(proj, A_W, A_W),
         functools.partial(dilated_a, rope=True, out_ref=ka_ref, out16_ref=ka16_ref, slab_ref=ks_ref)),
        (functools.partial(proj, 2 * A_W, A_W),
         functools.partial(dilated_a, rope=False, out_ref=va_ref, out16_ref=va16_ref, slab_ref=vs_ref)),
        (functools.partial(proj, o_qb, B_QW), functools.partial(norm_rope, gain=qg_ref[...], out_ref=qb_ref)),
        (functools.partial(proj, o_kd, 2 * B_KVW), functools.partial(norm_rope, gain=kg_ref[...], out_ref=kd_ref)),
        (functools.partial(proj, o_vd, 2 * B_KVW), cast_to(vd_ref)),
    ])


def _ab_in(x2d, w_ext, tabs, qg, kg, ones2, batch, seq):
    t = x2d.shape[0]
    nt = seq // TM
    tab = pl.BlockSpec((TM, LANES), lambda i: (i % nt, 0))
    row = lambda n: pl.BlockSpec((TM, n), lambda i: (i, 0))
    cls = pl.BlockSpec((1, DIL_CLASSES, TM // DIL_CLASSES, A_W), lambda i: (i // nt, 0, i % nt, 0))
    cls_shape = jax.ShapeDtypeStruct((batch, DIL_CLASSES, seq // DIL_CLASSES, A_W), BF16)
    wn = w_ext.shape[1]
    return pl.pallas_call(
        _ab_in_kernel,
        grid=(t // TM,),
        in_specs=[row(D_MODEL), _const_spec((D_MODEL, wn))] + [tab] * 6
        + [_const_spec((1, LANES)), _const_spec((1, LANES)), _const_spec((LANES, LANES))],
        out_specs=[row(A_W)] * 3 + [row(B_QW), row(2 * B_KVW), row(2 * B_KVW)] + [cls] * 3,
        out_shape=[jax.ShapeDtypeStruct((t, A_W), BF16)] * 3
        + [jax.ShapeDtypeStruct((t, B_QW), BF16)]
        + [jax.ShapeDtypeStruct((t, 2 * B_KVW), BF16)] * 2 + [cls_shape] * 3,
        scratch_shapes=[pltpu.VMEM((A_W // LANES, TM, LANES), F32)] * 3,
        compiler_params=_params(("arbitrary",), VMEM_LIMIT),
        name="ab_in",
    )(x2d, w_ext, *tabs, qg, kg, ones2)


def _band_groups(qs, ks, vs, bias, left):
    chains = [(g, sel) for g in range(len(qs)) for sel in (left, jnp.logical_not(left))]
    scores = []
    for g, sel in chains:
        qm = jnp.where(sel, qs[g], jnp.zeros_like(qs[g]))
        scores.append(lax.dot_general(qm, ks[g], _NT, preferred_element_type=F32) + bias)
    probs = []
    for s in scores:
        mx = jnp.max(s, axis=1, keepdims=True)
        pe = jnp.exp(s - mx)
        probs.append((mx, jnp.sum(pe, axis=1, keepdims=True), pe.astype(BF16)))
    outs = []
    for (g, _), (mx, l, pe) in zip(chains, probs):
        o = jnp.dot(pe, vs[g], preferred_element_type=F32)
        outs.append((o * (1.0 / l), mx + jnp.log(l)))
    return [(jnp.where(left, outs[2 * g][0], outs[2 * g + 1][0]), jnp.where(left, outs[2 * g][1], outs[2 * g + 1][1]))
            for g in range(len(qs))]


def _band_kernel(q_ref, kp_ref, kc_ref, kn_ref, vp_ref, vc_ref, vn_ref, o_ref, lse_ref, *, m, step_rows):
    nk = BAND_L + 2 * HALF_WINDOW
    base = pl.program_id(len(q_ref.shape) - 2) * step_rows
    r = lax.broadcasted_iota(jnp.int32, (BAND_L, nk), 0)
    j = lax.broadcasted_iota(jnp.int32, (BAND_L, nk), 1)
    band = jnp.where(jnp.abs(j - HALF_WINDOW - r) <= HALF_WINDOW, 0.0, NEG)
    left = lax.broadcasted_iota(jnp.int32, (BAND_L, LANES), 1) < HEAD_DIM
    pre = (0,) * (len(q_ref.shape) - 2)
    lanes = [slice(p * LANES, (p + 1) * LANES) for p in range(A_W // LANES)]
    at = lambda rows, sl: pre + (rows, sl)
    whole = slice(None)
    ks = [jnp.concatenate([kp_ref[at(whole, sl)], kc_ref[at(whole, sl)], kn_ref[at(whole, sl)]], axis=0) for sl in lanes]
    vs = [jnp.concatenate([vp_ref[at(whole, sl)], vc_ref[at(whole, sl)], vn_ref[at(whole, sl)]], axis=0) for sl in lanes]
    for blk in range(step_rows // BAND_L):
        r0 = blk * BAND_L
        rows = slice(r0, r0 + BAND_L)
        kpos = base + r0 - HALF_WINDOW + j
        bias = jnp.where(kpos >= 0, band, NEG)
        bias = jnp.where(kpos < m, bias, NEG)
        res = _band_groups([q_ref[at(rows, sl)] for sl in lanes], [kk[r0:r0 + nk] for kk in ks],
                           [vv[r0:r0 + nk] for vv in vs], bias, left)
        for sl, (o, lse) in zip(lanes, res):
            o_ref[at(rows, sl)] = o.astype(BF16)
            lse_ref[at(rows, sl)] = lse


def _band_specs(lead, rows, step_rows):
    nh = rows // HALF_WINDOW
    ratio = step_rows // HALF_WINDOW
    ones = (1,) * lead
    cur = pl.BlockSpec(ones + (step_rows, A_W), lambda *g: g[:lead] + (g[-1], 0))
    prev = pl.BlockSpec(ones + (HALF_WINDOW, A_W), lambda *g: g[:lead] + (jnp.maximum(g[-1] * ratio - 1, 0), 0))
    nxt = pl.BlockSpec(ones + (HALF_WINDOW, A_W), lambda *g: g[:lead] + (jnp.minimum((g[-1] + 1) * ratio, nh - 1), 0))
    return cur, prev, nxt


def _band1(q, k, v, batch, seq):
    step_rows = min(BAND_STEP, seq)
    cur, prev, nxt = _band_specs(1, seq, step_rows)
    view = lambda t: t.reshape(batch, seq, A_W)
    o, lse = pl.pallas_call(
        functools.partial(_band_kernel, m=seq, step_rows=step_rows),
        grid=(batch, seq // step_rows),
        in_specs=[cur, prev, cur, nxt, prev, cur, nxt],
        out_specs=[cur, cur],
        out_shape=[jax.ShapeDtypeStruct((batch, seq, A_W), BF16), jax.ShapeDtypeStruct((batch, seq, A_W), F32)],
        compiler_params=_params(("arbitrary",) * 2),
        name="band_d1",
    )(view(q), view(k), view(k), view(k), view(v), view(v), view(v))
    return o.reshape(batch * seq, A_W), lse.reshape(batch * seq, A_W)


def _band16(q16, k16, v16, batch, seq):
    m = seq // DIL_CLASSES
    step_rows = min(BAND_STEP, m)
    cur, prev, nxt = _band_specs(2, m, step_rows)
    shape = (batch, DIL_CLASSES, m, A_W)
    return pl.pallas_call(
        functools.partial(_band_kernel, m=m, step_rows=step_rows),
        grid=(batch, DIL_CLASSES, m // step_rows),
        in_specs=[cur, prev, cur, nxt, prev, cur, nxt],
        out_specs=[cur, cur],
        out_shape=[jax.ShapeDtypeStruct(shape, BF16), jax.ShapeDtypeStruct(shape, F32)],
        compiler_params=_params(("arbitrary",) * 3),
        name="band_d16",
    )(q16, k16, k16, k16, v16, v16, v16)


def _band4_kernel(q_ref, kp_ref, kc_ref, kn_ref, vp_ref, vc_ref, vn_ref, o16_ref, l16_ref, o_ref, lse_ref, *,
                  m16, step_rows):
    qr = BAND4_ROWS
    kr = qr + 2 * BAND4_HALO
    i = pl.program_id(2)
    shift_q, shift_k = qr.bit_length() - 1, kr.bit_length() - 1
    row = lax.broadcasted_iota(jnp.int32, (4 * qr, 4 * kr), 0)
    col = lax.broadcasted_iota(jnp.int32, (4 * qr, 4 * kr), 1)
    uq, rq = row >> shift_q, row & (qr - 1)
    uk, jk = col >> shift_k, col & (kr - 1)
    band = jnp.where(jnp.abs(4 * (jk - BAND4_HALO - rq) + (uk - uq)) <= HALF_WINDOW, 0.0, NEG)
    left = lax.broadcasted_iota(jnp.int32, (4 * qr, LANES), 1) < HEAD_DIM
    slabs = [slice(p * LANES, (p + 1) * LANES) for p in range(A_W // LANES)]
    window = lambda refs, u, sl: jnp.concatenate([r[0, u, 0, :, sl] for r in refs], axis=0)
    kwin = [[window([kp_ref, kc_ref, kn_ref], u, sl) for u in range(4)] for sl in slabs]
    vwin = [[window([vp_ref, vc_ref, vn_ref], u, sl) for u in range(4)] for sl in slabs]
    for blk in range(step_rows // qr):
        r0 = blk * qr
        rows = slice(r0, r0 + qr)
        kpos = i * step_rows + r0 - BAND4_HALO + jk
        bias = jnp.where(kpos >= 0, band, NEG)
        bias = jnp.where(kpos < m16, bias, NEG)
        classes = lambda ref, sl: jnp.concatenate([ref[0, u, 0, rows, sl] for u in range(4)], axis=0)
        branch4 = _band_groups([classes(q_ref, sl) for sl in slabs],
                               [jnp.concatenate([w[r0:r0 + kr] for w in kw], axis=0) for kw in kwin],
                               [jnp.concatenate([w[r0:r0 + kr] for w in vw], axis=0) for vw in vwin], bias, left)
        for sl, (o4, l4) in zip(slabs, branch4):
            o16 = classes(o16_ref, sl).astype(F32)
            l16 = classes(l16_ref, sl)
            mx = jnp.maximum(l4, l16)
            w4, w16 = jnp.exp(l4 - mx), jnp.exp(l16 - mx)
            tot = w4 + w16
            om = (w4 * o4 + w16 * o16) * (1.0 / tot)
            lm = mx + jnp.log(tot)
            for u in range(4):
                o_ref[0, u, 0, rows, sl] = om[u * qr:(u + 1) * qr].astype(BF16)
                lse_ref[0, u, 0, rows, sl] = lm[u * qr:(u + 1) * qr]


def _band4(q16, k16, v16, o16, l16, batch, seq):
    m16 = seq // DIL_CLASSES
    step_rows = min(BAND4_STEP, m16)
    ratio = step_rows // BAND4_HALO
    nh = m16 // BAND4_HALO
    view = lambda t: t.reshape(batch, 4, 4, m16, A_W)
    cur = pl.BlockSpec((1, 4, 1, step_rows, A_W), lambda b, c, i: (b, 0, c, i, 0))
    prev = pl.BlockSpec((1, 4, 1, BAND4_HALO, A_W), lambda b, c, i: (b, 0, c, jnp.maximum(i * ratio - 1, 0), 0))
    nxt = pl.BlockSpec((1, 4, 1, BAND4_HALO, A_W), lambda b, c, i: (b, 0, c, jnp.minimum((i + 1) * ratio, nh - 1), 0))
    shape = (batch, 4, 4, m16, A_W)
    o, lse = pl.pallas_call(
        functools.partial(_band4_kernel, m16=m16, step_rows=step_rows),
        grid=(batch, 4, m16 // step_rows),
        in_specs=[cur, prev, cur, nxt, prev, cur, nxt, cur, cur],
        out_specs=[cur, cur],
        out_shape=[jax.ShapeDtypeStruct(shape, BF16), jax.ShapeDtypeStruct(shape, F32)],
        compiler_params=_params(("arbitrary",) * 3),
        name="band_d4",
    )(view(q16), view(k16), view(k16), view(k16), view(v16), view(v16), view(v16), view(o16), view(l16))
    return o.reshape(batch, DIL_CLASSES, m16, A_W), lse.reshape(batch, DIL_CLASSES, m16, A_W)


def _gqa_kernel(q_ref, k_ref, v_ref, o_ref, qs_ref, m_ref, acc_ref, s_ref, *, seq):
    left = lax.broadcasted_iota(jnp.int32, (GQA_TQ, LANES), 1) < HEAD_DIM
    for pr in range(2):
        qp = q_ref[0, :, pr * LANES:(pr + 1) * LANES]
        zero = jnp.zeros_like(qp)
        qs_ref[(2 * pr) * GQA_TQ:(2 * pr + 1) * GQA_TQ, :] = jnp.where(left, qp, zero)
        qs_ref[(2 * pr + 1) * GQA_TQ:(2 * pr + 2) * GQA_TQ, :] = jnp.where(left, zero, qp)
    m_ref[...] = jnp.full(m_ref.shape, NEG, F32)
    acc_ref[...] = jnp.zeros(acc_ref.shape, F32)
    ones_lane = lax.broadcasted_iota(jnp.int32, (GQA_TK, LANES), 1) == HEAD_DIM

    chains = [slice(c * GQA_CHAIN, (c + 1) * GQA_CHAIN) for c in range(4 * GQA_TQ // GQA_CHAIN)]

    def scores(tile, buf, cols):
        off = pl.multiple_of(tile * GQA_TK, GQA_TK)
        k = k_ref[0, pl.ds(off, GQA_TK), :]
        s_ref[buf, :, cols] = lax.dot_general(k, qs_ref[cols, :], _NT, preferred_element_type=F32)

    def softmax_pv(tile, buf, cols):
        off = pl.multiple_of(tile * GQA_TK, GQA_TK)
        v = jnp.where(ones_lane, jnp.ones((), BF16), v_ref[0, pl.ds(off, GQA_TK), :])
        s = s_ref[buf, :, cols]
        m_prev = m_ref[:, cols]
        m_new = jnp.maximum(m_prev, jnp.max(s, axis=0, keepdims=True))
        alpha = jnp.exp2(m_prev - m_new)
        p = jnp.exp2((s - m_new).astype(BF16))
        pv = lax.dot_general(v, p, _TN, preferred_element_type=F32)
        acc_ref[:, cols] = alpha * acc_ref[:, cols] + pv
        m_ref[:, cols] = m_new

    def step(tile, buf, prefetch):
        for cols in chains:
            if prefetch:
                scores(tile + 1, 1 - buf, cols)
            softmax_pv(tile, buf, cols)

    def pair(j, carry):
        step(2 * j, 0, True)
        step(2 * j + 1, 1, True)
        return carry

    n_tiles = seq // GQA_TK
    for cols in chains:
        scores(0, 0, cols)
    lax.fori_loop(0, n_tiles // 2 - 1, pair, 0)
    step(n_tiles - 2, 0, True)
    step(n_tiles - 1, 1, False)
    acc = acc_ref[...]
    a = (acc * (1.0 / acc[HEAD_DIM:HEAD_DIM + 1, :])).T
    for pr in range(2):
        lo = a[(2 * pr) * GQA_TQ:(2 * pr + 1) * GQA_TQ]
        hi = pltpu.roll(a[(2 * pr + 1) * GQA_TQ:(2 * pr + 2) * GQA_TQ], HEAD_DIM, 1)
        o_ref[0, :, pr * LANES:(pr + 1) * LANES] = jnp.where(left, lo, hi).astype(BF16)


def _gqa(qb, kd, vd, batch, seq):
    gw = B_QW // B_KV_HEADS
    cols = 4 * GQA_TQ
    out = pl.pallas_call(
        functools.partial(_gqa_kernel, seq=seq),
        grid=(batch, B_KV_HEADS, seq // GQA_TQ),
        in_specs=[pl.BlockSpec((1, GQA_TQ, gw), lambda b, g, qi: (b, qi, g)),
                  pl.BlockSpec((1, seq, LANES), lambda b, g, qi: (b, 0, g)),
                  pl.BlockSpec((1, seq, LANES), lambda b, g, qi: (b, 0, g))],
        out_specs=pl.BlockSpec((1, GQA_TQ, gw), lambda b, g, qi: (b, qi, g)),
        out_shape=jax.ShapeDtypeStruct((batch, seq, B_QW), BF16),
        scratch_shapes=[pltpu.VMEM((cols, LANES), BF16), pltpu.VMEM((1, cols), F32),
                        pltpu.VMEM((LANES, cols), F32), pltpu.VMEM((2, GQA_TK, cols), F32)],
        compiler_params=_params(("arbitrary",) * 3, VMEM_LIMIT),
        name="gqa",
    )(qb.reshape(batch, seq, B_QW), kd.reshape(batch, seq, 2 * B_KVW), vd.reshape(batch, seq, 2 * B_KVW))
    return out.reshape(batch * seq, B_QW)


def _ab_out_kernel(o1_ref, l1_ref, om_ref, lm_ref, yb_ref, x_ref, w_ref, g_ref, b_ref, out_ref, os_ref, ls_ref):
    for c in range(DIL_CLASSES):
        for g in range(A_W // LANES):
            sl = slice(g * LANES, (g + 1) * LANES)
            rows = pl.ds(c, TM // DIL_CLASSES, stride=DIL_CLASSES)
            os_ref[g, rows, :] = om_ref[0, c, :, sl].astype(F32)
            ls_ref[g, rows, :] = lm_ref[0, c, :, sl]
    ya = []
    for g in range(A_W // LANES):
        sl = slice(g * LANES, (g + 1) * LANES)
        l1, lm = l1_ref[:, sl], ls_ref[g]
        mx = jnp.maximum(l1, lm)
        e1, em = jnp.exp(l1 - mx), jnp.exp(lm - mx)
        ya.append(((e1 * o1_ref[:, sl].astype(F32) + em * os_ref[g]) * (1.0 / (e1 + em))).astype(BF16))
    y = jnp.dot(jnp.concatenate(ya, axis=1), w_ref[0:A_W, :], preferred_element_type=F32)
    y = y + jnp.dot(yb_ref[...], w_ref[A_W:A_W + B_QW, :], preferred_element_type=F32)
    out_ref[...] = _layer_norm(ALPHA * x_ref[...] + y, g_ref[...], b_ref[...])


def _ab_out(o1, l1, om, lm, yb, x2d, w, g, b, seq):
    t = x2d.shape[0]
    nt = seq // TM
    row = lambda n: pl.BlockSpec((TM, n), lambda i: (i, 0))
    cls = pl.BlockSpec((1, DIL_CLASSES, TM // DIL_CLASSES, A_W), lambda i: (i // nt, 0, i % nt, 0))
    return pl.pallas_call(
        _ab_out_kernel,
        grid=(t // TM,),
        in_specs=[row(A_W), row(A_W), cls, cls, row(B_QW), row(D_MODEL), _const_spec((A_W + B_QW, D_MODEL)),
                  _const_spec((1, D_MODEL)), _const_spec((1, D_MODEL))],
        out_specs=row(D_MODEL),
        out_shape=jax.ShapeDtypeStruct((t, D_MODEL), F32),
        scratch_shapes=[pltpu.VMEM((A_W // LANES, TM, LANES), F32)] * 2,
        compiler_params=_params(("arbitrary",), VMEM_LIMIT),
        name="ab_out",
    )(o1, l1, om, lm, yb, x2d, w, g, b)


def _gelu(x):
    return 0.5 * x * (1.0 + jnp.tanh(0.7978845608028654 * (x + 0.044715 * (x * x * x))))


def _ffn_kernel(x_ref, xp_ref, xn_ref, wu_ref, cw_ref, cb_ref, wd_ref, g_ref, b_ref, out_ref, *, nt):
    i = pl.program_id(0)
    keep_prev = (i % nt != 0).astype(F32)
    keep_next = (i % nt != nt - 1).astype(F32)
    x = x_ref[...]
    xb = x.astype(BF16)
    halo = jnp.concatenate([xp_ref[...], xn_ref[...]], axis=0).astype(BF16)
    xe = jnp.concatenate([xb, halo], axis=0)
    rows = lax.broadcasted_iota(jnp.int32, (TM, 1), 0)
    offsets = [sum(FF_CHUNKS[:c]) for c in range(len(FF_CHUNKS))]

    def up(c):
        a, n = offsets[c], FF_CHUNKS[c]
        u = jnp.dot(xb, wu_ref[:, a:a + n], preferred_element_type=F32)
        ge = jnp.dot(xe, wu_ref[:, D_FF + a:D_FF + a + n], preferred_element_type=F32)
        return u, ge

    def gate(c, u, ge):
        a, n = offsets[c], FF_CHUNKS[c]
        gm = ge[0:TM]
        g_before = ge[TM + 7:TM + 8] * keep_prev
        g_after = ge[TM + 8:TM + 9] * keep_next
        gp = jnp.where(rows == 0, g_before, pltpu.roll(gm, 1, 0))
        gn = jnp.where(rows == TM - 1, g_after, pltpu.roll(gm, TM - 1, 0))
        cw = cw_ref[:, a:a + n]
        gc = gp * cw[0:1] + gm * cw[1:2] + gn * cw[2:3] + cb_ref[:, a:a + n]
        return (_gelu(gc) * u).astype(BF16)

    acc = None
    pending = up(0)
    for c in range(len(FF_CHUNKS)):
        upcoming = up(c + 1) if c + 1 < len(FF_CHUNKS) else None
        act = gate(c, *pending)
        part = jnp.dot(act, wd_ref[offsets[c]:offsets[c] + FF_CHUNKS[c], :], preferred_element_type=F32)
        acc = part if acc is None else acc + part
        pending = upcoming
    out_ref[...] = _layer_norm(ALPHA * x + acc, g_ref[...], b_ref[...])


def _ffn(x2d, wu, cw, cb, wd, g, b, seq):
    t = x2d.shape[0]
    nt = seq // TM
    r8 = TM // 8
    return pl.pallas_call(
        functools.partial(_ffn_kernel, nt=nt),
        grid=(t // TM,),
        in_specs=[pl.BlockSpec((TM, D_MODEL), lambda i: (i, 0)),
                  pl.BlockSpec((8, D_MODEL), lambda i: (jnp.maximum(i * r8 - 1, 0), 0)),
                  pl.BlockSpec((8, D_MODEL), lambda i: (jnp.minimum((i + 1) * r8, t // 8 - 1), 0)),
                  _const_spec((D_MODEL, 2 * D_FF), single=True),
                  _const_spec((3, D_FF)), _const_spec((1, D_FF)),
                  _const_spec((D_FF, D_MODEL), single=True),
                  _const_spec((1, D_MODEL)), _const_spec((1, D_MODEL))],
        out_specs=pl.BlockSpec((TM, D_MODEL), lambda i: (i, 0)),
        out_shape=jax.ShapeDtypeStruct((t, D_MODEL), F32),
        compiler_params=_params(("arbitrary",), VMEM_LIMIT),
        name="ffn",
    )(x2d, x2d, x2d, wu, cw, cb, wd, g, b)


def _lower_bound(tbl, layer):
    rows = [tbl[r:r + 1] for r in range(DEPTH)]
    mx = functools.reduce(jnp.maximum, rows)
    es = [jnp.exp(r - mx) for r in rows]
    inv = 1.0 / functools.reduce(lambda a, b: a + b, es)
    ps = [e * inv for e in es]
    return functools.reduce(lambda a, b: a + b, ps[:layer + 1]) - ps[0]


def _c_in_kernel(x_ref, w_ref, lbf_ref, lbb_ref, q_ref, lf_ref, lb_ref, v_ref, g_ref, *, layer):
    xb = x_ref[...].astype(BF16)
    half = C_W // 2

    def proj(a, n):
        return jnp.dot(xb, w_ref[:, a:a + n], preferred_element_type=F32)

    def silu(z):
        return z * _sigmoid(z)

    lbf = _lower_bound(lbf_ref[...], layer)
    lbb = _lower_bound(lbb_ref[...], layer)

    def store_q(z, sl):
        q_ref[:, sl] = z.astype(BF16)

    def store_logf(tbl, out_ref):
        def fn(z, sl):
            lb = tbl[:, sl]
            out_ref[:, sl] = jnp.log(lb + (1.0 - lb) * _sigmoid(z))
        return fn

    def store_silu(out_ref):
        def fn(z, sl):
            out_ref[:, sl] = silu(z).astype(BF16)
        return fn

    sections = []
    for c in range(2):
        a = c * half
        sl = slice(a, a + half)
        for off, fn in ((0, store_q), (C_W, store_logf(lbf, lf_ref)), (2 * C_W, store_logf(lbb, lb_ref)),
                        (3 * C_W, store_silu(v_ref)), (3 * C_W + C_VW, store_silu(g_ref))):
            sections.append((functools.partial(proj, off + a, half), functools.partial(fn, sl=sl)))
    _pipelined(sections)


def _c_in(x2d, w, lbf, lbb, layer):
    t = x2d.shape[0]
    row = pl.BlockSpec((TM, C_W), lambda i: (i, 0))
    return pl.pallas_call(
        functools.partial(_c_in_kernel, layer=layer),
        grid=(t // TM,),
        in_specs=[row, _const_spec(w.shape), _const_spec((DEPTH, C_W)), _const_spec((DEPTH, C_W))],
        out_specs=[row] * 5,
        out_shape=[jax.ShapeDtypeStruct((t, C_W), d) for d in (BF16, F32, F32, BF16, BF16)],
        compiler_params=_params(("arbitrary",), VMEM_LIMIT),
        name="c_in",
    )(x2d, w, lbf, lbb)


def _hgrn_kernel(*refs, rev, fused):
    if fused:
        q_ref, lf_ref, v_ref, of_ref, gate_ref, gn_ref, y_ref, st_ref = refs
    else:
        q_ref, lf_ref, v_ref, y_ref, st_ref = refs

    @pl.when(pl.program_id(2) == 0)
    def _():
        st_ref[...] = jnp.zeros(st_ref.shape, F32)

    ri = lax.broadcasted_iota(jnp.int32, (HG_CHUNK, HG_CHUNK), 0)
    ci = lax.broadcasted_iota(jnp.int32, (HG_CHUNK, HG_CHUNK), 1)
    tri = (ci >= ri) if rev else (ci <= ri)
    trib = jnp.where(tri, 1.0, 0.0).astype(BF16)
    dotf = lambda a, b: jnp.dot(a, b, preferred_element_type=F32)
    n_chunks = HG_TB // HG_CHUNK
    order = range(n_chunks - 1, -1, -1) if rev else range(n_chunks)
    units = [(c, h) for c in order for h in range(HG_HP)]
    window = lambda u: (0, slice(u[0] * HG_CHUNK, (u[0] + 1) * HG_CHUNK), slice(u[1] * C_KEY, (u[1] + 1) * C_KEY))

    b_all = {}
    for u in units:
        lf = lf_ref[window(u)]
        hi = lf.astype(BF16)
        lo = (lf - hi.astype(F32)).astype(BF16)
        b_all[u] = dotf(trib, hi) + dotf(trib, lo)

    qdec, kdec, etot, a_rows = {}, {}, {}, {}
    for u in units:
        lf, b = lf_ref[window(u)], b_all[u]
        qf = q_ref[window(u)].astype(F32)
        bex = b - lf
        kk = 1.0 - jnp.exp(lf)
        btot = b[0:1] if rev else b[HG_CHUNK - 1:HG_CHUNK]
        qdec[u] = (qf * jnp.exp(b)).astype(BF16)
        kdec[u] = (kk * jnp.exp(btot - b)).astype(BF16)
        etot[u] = jnp.exp(btot)
        rows = []
        for blk in range(HG_CHUNK // HG_SUB):
            r0 = blk * HG_SUB
            ref = bex[r0 + HG_SUB - 1:r0 + HG_SUB] if rev else bex[r0:r0 + 1]
            qt = (qf[r0:r0 + HG_SUB] * jnp.exp(b[r0:r0 + HG_SUB] - ref)).astype(BF16)
            lo_r, hi_r = (r0, HG_CHUNK) if rev else (0, r0 + HG_SUB)
            ks = (kk[lo_r:hi_r] * jnp.exp(ref - b[lo_r:hi_r])).astype(BF16)
            pieces = []
            if lo_r > 0:
                pieces.append(jnp.zeros((lo_r, C_KEY), BF16))
            pieces.append(ks)
            if hi_r < HG_CHUNK:
                pieces.append(jnp.zeros((HG_CHUNK - hi_r, C_KEY), BF16))
            kfull = jnp.concatenate(pieces, axis=0) if len(pieces) > 1 else ks
            rows.append(lax.dot_general(qt, kfull, _NT, preferred_element_type=F32))
        a_rows[u] = rows
    attn = {u: jnp.where(tri, jnp.concatenate(a_rows[u], axis=0), 0.0).astype(BF16) for u in units}

    states = [st_ref[h] for h in range(HG_HP)]
    for u in units:
        h = u[1]
        vb = v_ref[window(u)]
        o = lax.dot_general(qdec[u], states[h].astype(BF16), _NT, preferred_element_type=F32)
        o = o + dotf(attn[u], vb)
        states[h] = states[h] * etot[u] + lax.dot_general(vb, kdec[u], _TN, preferred_element_type=F32)
        if fused:
            tot = of_ref[window(u)] + o
            inv = lax.rsqrt(jnp.mean(tot * tot, -1, keepdims=True) + RMS_EPS)
            y = tot * inv * gn_ref[:, window(u)[2]] * gate_ref[window(u)].astype(F32)
            y_ref[window(u)] = y.astype(BF16)
        else:
            y_ref[window(u)] = o
    for h in range(HG_HP):
        st_ref[h] = states[h]


def _hgrn(q, lf, v, batch, seq, rev, fused_args=None):
    nb = seq // HG_TB
    width = HG_HP * C_KEY
    blk = (lambda b, h, i: (b, nb - 1 - i, h)) if rev else (lambda b, h, i: (b, i, h))
    tile = pl.BlockSpec((1, HG_TB, width), blk)
    view = lambda t: t.reshape(batch, seq, C_W)
    ins, specs = [view(q), view(lf), view(v)], [tile, tile, tile]
    fused = fused_args is not None
    if fused:
        o_f, gate, gn = fused_args
        ins += [view(o_f), view(gate), gn]
        specs += [tile, tile, pl.BlockSpec((1, width), lambda b, h, i: (0, h))]
    out = pl.pallas_call(
        functools.partial(_hgrn_kernel, rev=rev, fused=fused),
        grid=(batch, C_HEADS // HG_HP, nb),
        in_specs=specs,
        out_specs=tile,
        out_shape=jax.ShapeDtypeStruct((batch, seq, C_VW), BF16 if fused else F32),
        scratch_shapes=[pltpu.VMEM((HG_HP, C_VAL, C_KEY), F32)],
        compiler_params=_params(("arbitrary",) * 3),
        name="hgrn_bwd" if rev else "hgrn_fwd",
    )(*ins)
    return out.reshape(batch * seq, C_VW)


def _proj_ln_kernel(a_ref, x_ref, w_ref, g_ref, b_ref, out_ref):
    y = jnp.dot(a_ref[...], w_ref[...], preferred_element_type=F32)
    out_ref[...] = _layer_norm(ALPHA * x_ref[...] + y, g_ref[...], b_ref[...])


def _proj_ln(a, x2d, w, g, b):
    t = x2d.shape[0]
    kdim = a.shape[1]
    return pl.pallas_call(
        _proj_ln_kernel,
        grid=(t // TM,),
        in_specs=[pl.BlockSpec((TM, kdim), lambda i: (i, 0)), pl.BlockSpec((TM, D_MODEL), lambda i: (i, 0)),
                  _const_spec((kdim, D_MODEL)), _const_spec((1, D_MODEL)), _const_spec((1, D_MODEL))],
        out_specs=pl.BlockSpec((TM, D_MODEL), lambda i: (i, 0)),
        out_shape=jax.ShapeDtypeStruct((t, D_MODEL), F32),
        compiler_params=_params(("arbitrary",), VMEM_LIMIT),
        name="c_out",
    )(a, x2d, w, g, b)


def _prep_weights(w_in_ab, w_out_ab, qn_ab, kn_ab, w_in_c, w_out_c, gn_c, ffn_w_up, ffn_w_down):
    scale = HEAD_DIM ** -0.5
    ab = []
    for j in range(w_in_ab.shape[0]):
        w = w_in_ab[j]
        o3 = 3 * A_W
        kb = w[:, o3 + B_QW:o3 + B_QW + B_KVW]
        vb = w[:, o3 + B_QW + B_KVW:]
        dup = lambda t: jnp.concatenate([t[:, :HEAD_DIM], t[:, :HEAD_DIM], t[:, HEAD_DIM:], t[:, HEAD_DIM:]], 1)
        w_ext = jnp.concatenate([w[:, :A_W] * scale, w[:, A_W:o3 + B_QW], dup(kb), dup(vb)], 1).astype(BF16)
        qg = (jnp.concatenate([qn_ab[j], qn_ab[j]]) * (scale * LOG2E)).reshape(1, LANES)
        kg = jnp.concatenate([kn_ab[j], kn_ab[j]]).reshape(1, LANES)
        ab.append((w_ext, qg, kg, w_out_ab[j].astype(BF16)))
    cc = [(w_in_c[j].astype(BF16), w_out_c[j].astype(BF16), gn_c[j].reshape(1, C_VW)) for j in range(w_in_c.shape[0])]
    return ab, cc, ffn_w_up.astype(BF16), ffn_w_down.astype(BF16)


def _trunk(x, prep, lb_fwd, lb_bwd, ln_mix_g, ln_mix_b, ln_ffn_g, ln_ffn_b, ffn_conv_w, ffn_conv_b):
    ab, cc, wu, wd = prep
    batch, seq, _ = x.shape
    x2d = x.reshape(batch * seq, D_MODEL)
    tabs = _rope_tables(seq)
    blk = jnp.arange(LANES) // HEAD_DIM
    ones2 = (blk[:, None] == blk[None, :]).astype(BF16)
    vec = lambda t: t.reshape(1, -1)
    for l in range(DEPTH):
        j = l // 2
        if l % 2 == 0:
            w_ext, qg, kg, w_out = ab[j]
            assert [d for _, d in A_PATTERNS] == [1, 4, DIL_CLASSES]
            assert all(window // (2 * d) == HALF_WINDOW for window, d in A_PATTERNS)
            qa, ka, va, qb, kd, vd, qa16, ka16, va16 = _ab_in(x2d, w_ext, tabs, qg, kg, ones2, batch, seq)
            o1, l1 = _band1(qa, ka, va, batch, seq)
            o16, l16 = _band16(qa16, ka16, va16, batch, seq)
            om, lm = _band4(qa16, ka16, va16, o16, l16, batch, seq)
            yb = _gqa(qb, kd, vd, batch, seq)
            x2d = _ab_out(o1, l1, om, lm, yb, x2d, w_out, vec(ln_mix_g[l]), vec(ln_mix_b[l]), seq)
        else:
            w_in, w_out, gn = cc[j]
            q, lf, lb, v, gate = _c_in(x2d, w_in, lb_fwd, lb_bwd, l)
            o_f = _hgrn(q, lf, v, batch, seq, rev=False)
            y = _hgrn(q, lb, v, batch, seq, rev=True, fused_args=(o_f, gate, gn))
            x2d = _proj_ln(y, x2d, w_out, vec(ln_mix_g[l]), vec(ln_mix_b[l]))
        x2d = _ffn(x2d, wu[l], ffn_conv_w[l], vec(ffn_conv_b[l]), wd[l], vec(ln_ffn_g[l]), vec(ln_ffn_b[l]), seq)
    return x2d.reshape(batch, seq, D_MODEL)


def kernel(x_prompt, x_sample, w_in_ab, w_out_ab, qn_ab, kn_ab, w_in_c, w_out_c, lb_fwd, lb_bwd, gn_c, ln_mix_g, ln_mix_b, ln_ffn_g, ln_ffn_b, ffn_w_up, ffn_conv_w, ffn_conv_b, ffn_w_down):
    prep = _prep_weights(w_in_ab, w_out_ab, qn_ab, kn_ab, w_in_c, w_out_c, gn_c, ffn_w_up, ffn_w_down)
    rest = (lb_fwd, lb_bwd, ln_mix_g, ln_mix_b, ln_ffn_g, ln_ffn_b, ffn_conv_w, ffn_conv_b)
    return (_trunk(x_prompt, prep, *rest), _trunk(x_sample, prep, *rest))
```

```python
import functools

import jax
import jax.numpy as jnp
from jax import lax
from jax.experimental import pallas as pl
from jax.experimental.pallas import tpu as pltpu

F32 = jnp.float32
BF16 = jnp.bfloat16

D_MODEL = 1024
DEPTH = 2
HEAD_DIM = 64
A_HEADS = 8
A_PATTERNS = ((128, 1), (512, 4), (2048, 16))
B_HEADS = 8
B_KV_HEADS = 2
GRID_W = 64
ROPE_THETA = 500000.0
ROPE_DIM = HEAD_DIM // 4
AXIAL_THETA = 10000.0
C_HEADS = 8
C_KEY = 128
C_VAL = 128
D_FF = 2816
ALPHA = (2 * DEPTH) ** 0.25
LN_EPS = 1e-5
RMS_EPS = 1e-6
A_W = A_HEADS * HEAD_DIM
B_QW = B_HEADS * HEAD_DIM
B_KVW = B_KV_HEADS * HEAD_DIM
C_W = C_HEADS * C_KEY
C_VW = C_HEADS * C_VAL

LANES = 128
HALF_WINDOW = 64
TM = 512
PROJ_ROWS = 128
FFN_ROWS = 256
BAND_L = 128
BAND_STEP = 512
BAND4_STEP = 128
DIL_CLASSES = 16
BAND4_ROWS = 32
BAND4_HALO = 16
GQA_TQ = 512
GQA_TK = 512
GQA_CHAIN = 512
FF_CHUNKS = (768, 768, 768, 512)
FFN_SUB = 1
HG_CHUNK = 64
HG_SUB = 16
HG_TB = 512
HG_HP = 4
NEG = -1e30
LOG2E = 1.4426950408889634
VMEM_LIMIT = 56 * 1024 * 1024

_NT = (((1,), (1,)), ((), ()))
_TN = (((0,), (0,)), ((), ()))


def _params(sem, vmem=None):
    return pltpu.CompilerParams(dimension_semantics=sem, vmem_limit_bytes=vmem)


def _const_spec(shape, single=False):
    nd = len(shape)
    if single:
        return pl.BlockSpec(shape, lambda *_: (0,) * nd, pipeline_mode=pl.Buffered(1))
    return pl.BlockSpec(shape, lambda *_: (0,) * nd)


def _layer_norm(z, g, b):
    mu = jnp.mean(z, -1, keepdims=True)
    d = z - mu
    var = jnp.mean(d * d, -1, keepdims=True)
    return d * lax.rsqrt(var + LN_EPS) * g + b


def _sigmoid(z):
    return 1.0 / (1.0 + jnp.exp(-z))


def _row_block_dot(x, w, rows=PROJ_ROWS):
    m = x.shape[0]
    starts = list(range(0, m - m % rows, rows)) or [0]
    ends = starts[1:] + [m]
    parts = [jnp.dot(x[a:b], w, preferred_element_type=F32) for a, b in zip(starts, ends)]
    return parts[0] if len(parts) == 1 else jnp.concatenate(parts, axis=0)


def _pipelined(sections):
    pending = sections[0][0]()
    for i, (_, epilogue) in enumerate(sections):
        upcoming = sections[i + 1][0]() if i + 1 < len(sections) else None
        epilogue(pending)
        pending = upcoming


def _rope_tables(seq):
    pos = jnp.arange(seq, dtype=F32)
    d = jnp.arange(LANES) % HEAD_DIM

    def angles(p, dim, theta):
        freqs = theta ** (-(jnp.arange(0, dim, 2, dtype=F32) / dim))
        return p[:, None] * freqs[None, :]

    h = ROPE_DIM // 2
    ang = angles(pos, ROPE_DIM, ROPE_THETA)
    a = ang[:, d % h]
    lo, hi = d < h, (d >= h) & (d < ROPE_DIM)
    pc = jnp.where(lo | hi, jnp.cos(a), 1.0)
    psa = jnp.where(lo, -jnp.sin(a), 0.0)
    psb = jnp.where(hi, jnp.sin(a), 0.0)

    q = HEAD_DIM // 4
    row = jnp.floor(pos / GRID_W)
    col = pos - row * GRID_W
    ar = angles(row, HEAD_DIM // 2, AXIAL_THETA)[:, d % q]
    ac_ = angles(col, HEAD_DIM // 2, AXIAL_THETA)[:, d % q]
    a2 = jnp.where(d < HEAD_DIM // 2, ar, ac_)
    first = (d % (HEAD_DIM // 2)) < q
    ac = jnp.cos(a2)
    asa = jnp.where(first, -jnp.sin(a2), 0.0)
    asb = jnp.where(first, 0.0, jnp.sin(a2))
    return [t.astype(F32) for t in (pc, psa, psb, ac, asa, asb)]


def _rope(seg, c, sa, sb, shift):
    return seg * c + pltpu.roll(seg, LANES - shift, 1) * sa + pltpu.roll(seg, shift, 1) * sb


def _ab_in_kernel(x_ref, w_ref, pc_ref, psa_ref, psb_ref, ac_ref, asa_ref, asb_ref, qg_ref, kg_ref,
                  ones_ref, qa_ref, ka_ref, va_ref, qb_ref, kd_ref, vd_ref, qa16_ref, ka16_ref, va16_ref,
                  qs_ref, ks_ref, vs_ref):
    xb = x_ref[...].astype(BF16)

    def proj(a, n):
        return jnp.dot(xb, w_ref[:, a:a + n], preferred_element_type=F32)

    pc, psa, psb = pc_ref[...], psa_ref[...], psb_ref[...]
    ac, asa, asb = ac_ref[...], asa_ref[...], asb_ref[...]

    def dilated_a(h, rope, out_ref, out16_ref, slab_ref):
        for g in range(A_W // LANES):
            sl = slice(g * LANES, (g + 1) * LANES)
            y = _rope(h[:, sl], pc, psa, psb, ROPE_DIM // 2) if rope else h[:, sl]
            out_ref[:, sl] = y.astype(BF16)
            slab_ref[g] = y
        for c in range(DIL_CLASSES):
            for g in range(A_W // LANES):
                rows = slab_ref[g, pl.ds(c, TM // DIL_CLASSES, stride=DIL_CLASSES), :]
                out16_ref[0, c, :, g * LANES:(g + 1) * LANES] = rows.astype(BF16)

    def norm_rope(h, gain, out_ref):
        for g in range(h.shape[1] // LANES):
            sl = slice(g * LANES, (g + 1) * LANES)
            seg = h[:, sl]
            sq = seg * seg
            hi = sq.astype(BF16)
            lo = (sq - hi.astype(F32)).astype(BF16)
            ss = (jnp.dot(hi, ones_ref[...], preferred_element_type=F32)
                  + jnp.dot(lo, ones_ref[...], preferred_element_type=F32))
            y = seg * lax.rsqrt(ss * (1.0 / HEAD_DIM) + RMS_EPS) * gain
            out_ref[:, sl] = _rope(y, ac, asa, asb, HEAD_DIM // 4).astype(BF16)

    def cast_to(out_ref):
        def fn(h):
            out_ref[...] = h.astype(BF16)
        return fn

    o_qb = 3 * A_W
    o_kd = o_qb + B_QW
    o_vd = o_kd + 2 * B_KVW
    _pipelined([
        (functools.partial(proj, 0, A_W),
         functools.partial(dilated_a, rope=True, out_ref=qa_ref, out16_ref=qa16_ref, slab_ref=qs_ref)),
        (functools.partial(proj, A_W, A_W),
         functools.partial(dilated_a, rope=True, out_ref=ka_ref, out16_ref=ka16_ref, slab_ref=ks_ref)),
        (functools.partial(proj, 2 * A_W, A_W),
         functools.partial(dilated_a, rope=False, out_ref=va_ref, out16_ref=va16_ref, slab_ref=vs_ref)),
        (functools.partial(proj, o_qb, B_QW), functools.partial(norm_rope, gain=qg_ref[...], out_ref=qb_ref)),
        (functools.partial(proj, o_kd, 2 * B_KVW), functools.partial(norm_rope, gain=kg_ref[...], out_ref=kd_ref)),
        (functools.partial(proj, o_vd, 2 * B_KVW), cast_to(vd_ref)),
    ])


def _ab_in(x2d, w_ext, tabs, qg, kg, ones2, batch, seq):
    t = x2d.shape[0]
    nt = seq // TM
    tab = pl.BlockSpec((TM, LANES), lambda i: (i % nt, 0))
    row = lambda n: pl.BlockSpec((TM, n), lambda i: (i, 0))
    cls = pl.BlockSpec((1, DIL_CLASSES, TM // DIL_CLASSES, A_W), lambda i: (i // nt, 0, i % nt, 0))
    cls_shape = jax.ShapeDtypeStruct((batch, DIL_CLASSES, seq // DIL_CLASSES, A_W), BF16)
    wn = w_ext.shape[1]
    return pl.pallas_call(
        _ab_in_kernel,
        grid=(t // TM,),
        in_specs=[row(D_MODEL), _const_spec((D_MODEL, wn))] + [tab] * 6
        + [_const_spec((1, LANES)), _const_spec((1, LANES)), _const_spec((LANES, LANES))],
        out_specs=[row(A_W)] * 3 + [row(B_QW), row(2 * B_KVW), row(2 * B_KVW)] + [cls] * 3,
        out_shape=[jax.ShapeDtypeStruct((t, A_W), BF16)] * 3
        + [jax.ShapeDtypeStruct((t, B_QW), BF16)]
        + [jax.ShapeDtypeStruct((t, 2 * B_KVW), BF16)] * 2 + [cls_shape] * 3,
        scratch_shapes=[pltpu.VMEM((A_W // LANES, TM, LANES), F32)] * 3,
        compiler_params=_params(("arbitrary",), VMEM_LIMIT),
        name="ab_in",
    )(x2d, w_ext, *tabs, qg, kg, ones2)


def _band_groups(qs, ks, vs, bias, left):
    chains = [(g, sel) for g in range(len(qs)) for sel in (left, jnp.logical_not(left))]
    scores = []
    for g, sel in chains:
        qm = jnp.where(sel, qs[g], jnp.zeros_like(qs[g]))
        scores.append(lax.dot_general(qm, ks[g], _NT, preferred_element_type=F32) + bias)
    probs = []
    for s in scores:
        mx = jnp.max(s, axis=1, keepdims=True)
        pe = jnp.exp(s - mx)
        probs.append((mx, jnp.sum(pe, axis=1, keepdims=True), pe.astype(BF16)))
    outs = []
    for (g, _), (mx, l, pe) in zip(chains, probs):
        o = jnp.dot(pe, vs[g], preferred_element_type=F32)
        outs.append((o * (1.0 / l), mx + jnp.log(l)))
    return [(jnp.where(left, outs[2 * g][0], outs[2 * g + 1][0]), jnp.where(left, outs[2 * g][1], outs[2 * g + 1][1]))
            for g in range(len(qs))]


def _band_kernel(q_ref, kp_ref, kc_ref, kn_ref, vp_ref, vc_ref, vn_ref, o_ref, lse_ref, *, m, step_rows):
    nk = BAND_L + 2 * HALF_WINDOW
    base = pl.program_id(len(q_ref.shape) - 2) * step_rows
    r = lax.broadcasted_iota(jnp.int32, (BAND_L, nk), 0)
    j = lax.broadcasted_iota(jnp.int32, (BAND_L, nk), 1)
    band = jnp.where(jnp.abs(j - HALF_WINDOW - r) <= HALF_WINDOW, 0.0, NEG)
    left = lax.broadcasted_iota(jnp.int32, (BAND_L, LANES), 1) < HEAD_DIM
    pre = (0,) * (len(q_ref.shape) - 2)
    lanes = [slice(p * LANES, (p + 1) * LANES) for p in range(A_W // LANES)]
    at = lambda rows, sl: pre + (rows, sl)
    whole = slice(None)
    ks = [jnp.concatenate([kp_ref[at(whole, sl)], kc_ref[at(whole, sl)], kn_ref[at(whole, sl)]], axis=0) for sl in lanes]
    vs = [jnp.concatenate([vp_ref[at(whole, sl)], vc_ref[at(whole, sl)], vn_ref[at(whole, sl)]], axis=0) for sl in lanes]
    for blk in range(step_rows // BAND_L):
        r0 = blk * BAND_L
        rows = slice(r0, r0 + BAND_L)
        kpos = base + r0 - HALF_WINDOW + j
        bias = jnp.where(kpos >= 0, band, NEG)
        bias = jnp.where(kpos < m, bias, NEG)
        res = _band_groups([q_ref[at(rows, sl)] for sl in lanes], [kk[r0:r0 + nk] for kk in ks],
                           [vv[r0:r0 + nk] for vv in vs], bias, left)
        for sl, (o, lse) in zip(lanes, res):
            o_ref[at(rows, sl)] = o.astype(BF16)
            lse_ref[at(rows, sl)] = lse


def _band_specs(lead, rows, step_rows):
    nh = rows // HALF_WINDOW
    ratio = step_rows // HALF_WINDOW
    ones = (1,) * lead
    cur = pl.BlockSpec(ones + (step_rows, A_W), lambda *g: g[:lead] + (g[-1], 0))
    prev = pl.BlockSpec(ones + (HALF_WINDOW, A_W), lambda *g: g[:lead] + (jnp.maximum(g[-1] * ratio - 1, 0), 0))
    nxt = pl.BlockSpec(ones + (HALF_WINDOW, A_W), lambda *g: g[:lead] + (jnp.minimum((g[-1] + 1) * ratio, nh - 1), 0))
    return cur, prev, nxt


def _band1(q, k, v, batch, seq):
    step_rows = min(BAND_STEP, seq)
    cur, prev, nxt = _band_specs(1, seq, step_rows)
    view = lambda t: t.reshape(batch, seq, A_W)
    o, lse = pl.pallas_call(
        functools.partial(_band_kernel, m=seq, step_rows=step_rows),
        grid=(batch, seq // step_rows),
        in_specs=[cur, prev, cur, nxt, prev, cur, nxt],
        out_specs=[cur, cur],
        out_shape=[jax.ShapeDtypeStruct((batch, seq, A_W), BF16), jax.ShapeDtypeStruct((batch, seq, A_W), F32)],
        compiler_params=_params(("arbitrary",) * 2),
        name="band_d1",
    )(view(q), view(k), view(k), view(k), view(v), view(v), view(v))
    return o.reshape(batch * seq, A_W), lse.reshape(batch * seq, A_W)


def _band16(q16, k16, v16, batch, seq):
    m = seq // DIL_CLASSES
    step_rows = min(BAND_STEP, m)
    cur, prev, nxt = _band_specs(2, m, step_rows)
    shape = (batch, DIL_CLASSES, m, A_W)
    return pl.pallas_call(
        functools.partial(_band_kernel, m=m, step_rows=step_rows),
        grid=(batch, DIL_CLASSES, m // step_rows),
        in_specs=[cur, prev, cur, nxt, prev, cur, nxt],
        out_specs=[cur, cur],
        out_shape=[jax.ShapeDtypeStruct(shape, BF16), jax.ShapeDtypeStruct(shape, F32)],
        compiler_params=_params(("arbitrary",) * 3),
        name="band_d16",
    )(q16, k16, k16, k16, v16, v16, v16)


def _band4_kernel(q_ref, kp_ref, kc_ref, kn_ref, vp_ref, vc_ref, vn_ref, o16_ref, l16_ref, o_ref, lse_ref, *,
                  m16, step_rows):
    qr = BAND4_ROWS
    kr = qr + 2 * BAND4_HALO
    i = pl.program_id(2)
    shift_q, shift_k = qr.bit_length() - 1, kr.bit_length() - 1
    row = lax.broadcasted_iota(jnp.int32, (4 * qr, 4 * kr), 0)
    col = lax.broadcasted_iota(jnp.int32, (4 * qr, 4 * kr), 1)
    uq, rq = row >> shift_q, row & (qr - 1)
    uk, jk = col >> shift_k, col & (kr - 1)
    band = jnp.where(jnp.abs(4 * (jk - BAND4_HALO - rq) + (uk - uq)) <= HALF_WINDOW, 0.0, NEG)
    left = lax.broadcasted_iota(jnp.int32, (4 * qr, LANES), 1) < HEAD_DIM
    slabs = [slice(p * LANES, (p + 1) * LANES) for p in range(A_W // LANES)]
    window = lambda refs, u, sl: jnp.concatenate([r[0, u, 0, :, sl] for r in refs], axis=0)
    kwin = [[window([kp_ref, kc_ref, kn_ref], u, sl) for u in range(4)] for sl in slabs]
    vwin = [[window([vp_ref, vc_ref, vn_ref], u, sl) for u in range(4)] for sl in slabs]
    for blk in range(step_rows // qr):
        r0 = blk * qr
        rows = slice(r0, r0 + qr)
        kpos = i * step_rows + r0 - BAND4_HALO + jk
        bias = jnp.where(kpos >= 0, band, NEG)
        bias = jnp.where(kpos < m16, bias, NEG)
        classes = lambda ref, sl: jnp.concatenate([ref[0, u, 0, rows, sl] for u in range(4)], axis=0)
        branch4 = _band_groups([classes(q_ref, sl) for sl in slabs],
                               [jnp.concatenate([w[r0:r0 + kr] for w in kw], axis=0) for kw in kwin],
                               [jnp.concatenate([w[r0:r0 + kr] for w in vw], axis=0) for vw in vwin], bias, left)
        for sl, (o4, l4) in zip(slabs, branch4):
            o16 = classes(o16_ref, sl).astype(F32)
            l16 = classes(l16_ref, sl)
            mx = jnp.maximum(l4, l16)
            w4, w16 = jnp.exp(l4 - mx), jnp.exp(l16 - mx)
            tot = w4 + w16
            om = (w4 * o4 + w16 * o16) * (1.0 / tot)
            lm = mx + jnp.log(tot)
            for u in range(4):
                o_ref[0, u, 0, rows, sl] = om[u * qr:(u + 1) * qr].astype(BF16)
                lse_ref[0, u, 0, rows, sl] = lm[u * qr:(u + 1) * qr]


def _band4(q16, k16, v16, o16, l16, batch, seq):
    m16 = seq // DIL_CLASSES
    step_rows = min(BAND4_STEP, m16)
    ratio = step_rows // BAND4_HALO
    nh = m16 // BAND4_HALO
    view = lambda t: t.reshape(batch, 4, 4, m16, A_W)
    cur = pl.BlockSpec((1, 4, 1, step_rows, A_W), lambda b, c, i: (b, 0, c, i, 0))
    prev = pl.BlockSpec((1, 4, 1, BAND4_HALO, A_W), lambda b, c, i: (b, 0, c, jnp.maximum(i * ratio - 1, 0), 0))
    nxt = pl.BlockSpec((1, 4, 1, BAND4_HALO, A_W), lambda b, c, i: (b, 0, c, jnp.minimum((i + 1) * ratio, nh - 1), 0))
    shape = (batch, 4, 4, m16, A_W)
    o, lse = pl.pallas_call(
        functools.partial(_band4_kernel, m16=m16, step_rows=step_rows),
        grid=(batch, 4, m16 // step_rows),
        in_specs=[cur, prev, cur, nxt, prev, cur, nxt, cur, cur],
        out_specs=[cur, cur],
        out_shape=[jax.ShapeDtypeStruct(shape, BF16), jax.ShapeDtypeStruct(shape, F32)],
        compiler_params=_params(("arbitrary",) * 3),
        name="band_d4",
    )(view(q16), view(k16), view(k16), view(k16), view(v16), view(v16), view(v16), view(o16), view(l16))
    return o.reshape(batch, DIL_CLASSES, m16, A_W), lse.reshape(batch, DIL_CLASSES, m16, A_W)


def _gqa_kernel(q_ref, k_ref, v_ref, o_ref, qs_ref, m_ref, acc_ref, s_ref, *, seq):
    left = lax.broadcasted_iota(jnp.int32, (GQA_TQ, LANES), 1) < HEAD_DIM
    for pr in range(2):
        qp = q_ref[0, :, pr * LANES:(pr + 1) * LANES]
        zero = jnp.zeros_like(qp)
        qs_ref[(2 * pr) * GQA_TQ:(2 * pr + 1) * GQA_TQ, :] = jnp.where(left, qp, zero)
        qs_ref[(2 * pr + 1) * GQA_TQ:(2 * pr + 2) * GQA_TQ, :] = jnp.where(left, zero, qp)
    m_ref[...] = jnp.full(m_ref.shape, NEG, F32)
    acc_ref[...] = jnp.zeros(acc_ref.shape, F32)
    n_tiles = seq // GQA_TK
    ones_lane = lax.broadcasted_iota(jnp.int32, (GQA_TK, LANES), 1) == HEAD_DIM

    chains = [slice(c * GQA_CHAIN, (c + 1) * GQA_CHAIN) for c in range(4 * GQA_TQ // GQA_CHAIN)]

    def scores(tile, buf, cols):
        off = pl.multiple_of(tile * GQA_TK, GQA_TK)
        k = k_ref[0, pl.ds(off, GQA_TK), :]
        s_ref[buf, :, cols] = lax.dot_general(k, qs_ref[cols, :], _NT, preferred_element_type=F32)

    def softmax_pv(tile, buf, cols):
        off = pl.multiple_of(tile * GQA_TK, GQA_TK)
        v = jnp.where(ones_lane, jnp.ones((), BF16), v_ref[0, pl.ds(off, GQA_TK), :])
        s = s_ref[buf, :, cols]
        m_prev = m_ref[:, cols]
        m_new = jnp.maximum(m_prev, jnp.max(s, axis=0, keepdims=True))
        alpha = jnp.exp2(m_prev - m_new)
        p = jnp.exp2((s - m_new).astype(BF16))
        pv = lax.dot_general(v, p, _TN, preferred_element_type=F32)
        acc_ref[:, cols] = alpha * acc_ref[:, cols] + pv
        m_ref[:, cols] = m_new

    def step(tile, buf, prefetch):
        for cols in chains:
            if prefetch:
                scores(tile + 1, 1 - buf, cols)
            softmax_pv(tile, buf, cols)

    def pair(j, carry):
        step(2 * j, 0, True)
        step(2 * j + 1, 1, True)
        return carry

    for cols in chains:
        scores(0, 0, cols)
    lax.fori_loop(0, n_tiles // 2 - 1, pair, 0)
    step(n_tiles - 2, 0, True)
    step(n_tiles - 1, 1, False)
    acc = acc_ref[...]
    a = (acc * (1.0 / acc[HEAD_DIM:HEAD_DIM + 1, :])).T
    for pr in range(2):
        lo = a[(2 * pr) * GQA_TQ:(2 * pr + 1) * GQA_TQ]
        hi = pltpu.roll(a[(2 * pr + 1) * GQA_TQ:(2 * pr + 2) * GQA_TQ], HEAD_DIM, 1)
        o_ref[0, :, pr * LANES:(pr + 1) * LANES] = jnp.where(left, lo, hi).astype(BF16)


def _gqa(qb, kd, vd, batch, seq):
    gw = B_QW // B_KV_HEADS
    cols = 4 * GQA_TQ
    out = pl.pallas_call(
        functools.partial(_gqa_kernel, seq=seq),
        grid=(batch, B_KV_HEADS, seq // GQA_TQ),
        in_specs=[pl.BlockSpec((1, GQA_TQ, gw), lambda b, g, qi: (b, qi, g)),
                  pl.BlockSpec((1, seq, LANES), lambda b, g, qi: (b, 0, g)),
                  pl.BlockSpec((1, seq, LANES), lambda b, g, qi: (b, 0, g))],
        out_specs=pl.BlockSpec((1, GQA_TQ, gw), lambda b, g, qi: (b, qi, g)),
        out_shape=jax.ShapeDtypeStruct((batch, seq, B_QW), BF16),
        scratch_shapes=[pltpu.VMEM((cols, LANES), BF16), pltpu.VMEM((1, cols), F32),
                        pltpu.VMEM((LANES, cols), F32), pltpu.VMEM((2, GQA_TK, cols), F32)],
        compiler_params=_params(("arbitrary",) * 3, VMEM_LIMIT),
        name="gqa",
    )(qb.reshape(batch, seq, B_QW), kd.reshape(batch, seq, 2 * B_KVW), vd.reshape(batch, seq, 2 * B_KVW))
    return out.reshape(batch * seq, B_QW)


def _ab_out_kernel(o1_ref, l1_ref, om_ref, lm_ref, yb_ref, x_ref, w_ref, g_ref, b_ref, out_ref, os_ref, ls_ref):
    for c in range(DIL_CLASSES):
        for g in range(A_W // LANES):
            sl = slice(g * LANES, (g + 1) * LANES)
            rows = pl.ds(c, TM // DIL_CLASSES, stride=DIL_CLASSES)
            os_ref[g, rows, :] = om_ref[0, c, :, sl].astype(F32)
            ls_ref[g, rows, :] = lm_ref[0, c, :, sl]
    ya = []
    for g in range(A_W // LANES):
        sl = slice(g * LANES, (g + 1) * LANES)
        l1, lm = l1_ref[:, sl], ls_ref[g]
        mx = jnp.maximum(l1, lm)
        e1, em = jnp.exp(l1 - mx), jnp.exp(lm - mx)
        ya.append(((e1 * o1_ref[:, sl].astype(F32) + em * os_ref[g]) * (1.0 / (e1 + em))).astype(BF16))
    y = _row_block_dot(jnp.concatenate(ya + [yb_ref[...]], axis=1), w_ref[...])
    out_ref[...] = _layer_norm(ALPHA * x_ref[...] + y, g_ref[...], b_ref[...])


def _ab_out(o1, l1, om, lm, yb, x2d, w, g, b, seq):
    t = x2d.shape[0]
    nt = seq // TM
    row = lambda n: pl.BlockSpec((TM, n), lambda i: (i, 0))
    cls = pl.BlockSpec((1, DIL_CLASSES, TM // DIL_CLASSES, A_W), lambda i: (i // nt, 0, i % nt, 0))
    return pl.pallas_call(
        _ab_out_kernel,
        grid=(t // TM,),
        in_specs=[row(A_W), row(A_W), cls, cls, row(B_QW), row(D_MODEL), _const_spec((A_W + B_QW, D_MODEL)),
                  _const_spec((1, D_MODEL)), _const_spec((1, D_MODEL))],
        out_specs=row(D_MODEL),
        out_shape=jax.ShapeDtypeStruct((t, D_MODEL), F32),
        scratch_shapes=[pltpu.VMEM((A_W // LANES, TM, LANES), F32)] * 2,
        compiler_params=_params(("arbitrary",), VMEM_LIMIT),
        name="ab_out",
    )(o1, l1, om, lm, yb, x2d, w, g, b)


def _gelu(x):
    return 0.5 * x * (1.0 + jnp.tanh(0.7978845608028654 * (x + 0.044715 * (x * x * x))))


def _ffn_kernel(x_ref, xp_ref, xn_ref, wu_ref, cw_ref, cb_ref, wd_ref, g_ref, b_ref, out_ref, *, nt):
    i = pl.program_id(0)
    keep_prev = (i % nt != 0).astype(F32)
    keep_next = (i % nt != nt - 1).astype(F32)
    rows = lax.broadcasted_iota(jnp.int32, (TM, 1), 0)
    offsets = [sum(FF_CHUNKS[:c]) for c in range(len(FF_CHUNKS))]
    operands, acc = {}, {}

    def tile_operands(j):
        if j not in operands:
            r0 = j * TM
            xb = x_ref[r0:r0 + TM, :].astype(BF16)
            before = xp_ref[...] if j == 0 else x_ref[r0 - 8:r0, :]
            after = xn_ref[...] if j == FFN_SUB - 1 else x_ref[r0 + TM:r0 + TM + 8, :]
            halo = jnp.concatenate([before, after], axis=0).astype(BF16)
            operands[j] = (xb, jnp.concatenate([xb, halo], axis=0))
        return operands[j]

    def up(j, c):
        a, n = offsets[c], FF_CHUNKS[c]
        xb, xe = tile_operands(j)
        u = _row_block_dot(xb, wu_ref[:, a:a + n], FFN_ROWS)
        ge = _row_block_dot(xe, wu_ref[:, D_FF + a:D_FF + a + n], FFN_ROWS)
        return u, ge

    def finish(j, c, pending):
        u, ge = pending
        a, n = offsets[c], FF_CHUNKS[c]
        gm = ge[0:TM]
        g_before = ge[TM + 7:TM + 8]
        g_after = ge[TM + 8:TM + 9]
        if j == 0:
            g_before = g_before * keep_prev
        if j == FFN_SUB - 1:
            g_after = g_after * keep_next
        gp = jnp.where(rows == 0, g_before, pltpu.roll(gm, 1, 0))
        gn = jnp.where(rows == TM - 1, g_after, pltpu.roll(gm, TM - 1, 0))
        cw = cw_ref[:, a:a + n]
        gc = gp * cw[0:1] + gm * cw[1:2] + gn * cw[2:3] + cb_ref[:, a:a + n]
        act = (_gelu(gc) * u).astype(BF16)
        part = _row_block_dot(act, wd_ref[a:a + n, :], FFN_ROWS)
        acc[j] = part if c == 0 else acc[j] + part
        if c == len(FF_CHUNKS) - 1:
            sl = slice(j * TM, (j + 1) * TM)
            out_ref[sl, :] = _layer_norm(ALPHA * x_ref[sl, :] + acc[j], g_ref[...], b_ref[...])

    _pipelined([(functools.partial(up, j, c), functools.partial(finish, j, c))
                for j in range(FFN_SUB) for c in range(len(FF_CHUNKS))])


def _ffn(x2d, wu, cw, cb, wd, g, b, seq):
    t = x2d.shape[0]
    step = FFN_SUB * TM
    nt = seq // step
    r8 = step // 8
    return pl.pallas_call(
        functools.partial(_ffn_kernel, nt=nt),
        grid=(t // step,),
        in_specs=[pl.BlockSpec((step, D_MODEL), lambda i: (i, 0)),
                  pl.BlockSpec((8, D_MODEL), lambda i: (jnp.maximum(i * r8 - 1, 0), 0)),
                  pl.BlockSpec((8, D_MODEL), lambda i: (jnp.minimum((i + 1) * r8, t // 8 - 1), 0)),
                  _const_spec((D_MODEL, 2 * D_FF), single=True),
                  _const_spec((3, D_FF)), _const_spec((1, D_FF)),
                  _const_spec((D_FF, D_MODEL), single=True),
                  _const_spec((1, D_MODEL)), _const_spec((1, D_MODEL))],
        out_specs=pl.BlockSpec((step, D_MODEL), lambda i: (i, 0)),
        out_shape=jax.ShapeDtypeStruct((t, D_MODEL), F32),
        compiler_params=_params(("arbitrary",), VMEM_LIMIT),
        name="ffn",
    )(x2d, x2d, x2d, wu, cw, cb, wd, g, b)


def _lower_bound(tbl, layer):
    rows = [tbl[r:r + 1] for r in range(DEPTH)]
    mx = functools.reduce(jnp.maximum, rows)
    es = [jnp.exp(r - mx) for r in rows]
    inv = 1.0 / functools.reduce(lambda a, b: a + b, es)
    ps = [e * inv for e in es]
    return functools.reduce(lambda a, b: a + b, ps[:layer + 1]) - ps[0]


def _c_in_kernel(x_ref, w_ref, lbf_ref, lbb_ref, q_ref, lf_ref, lb_ref, v_ref, g_ref, *, layer):
    half = C_W // 2
    blocks = [slice(r * PROJ_ROWS, (r + 1) * PROJ_ROWS) for r in range(TM // PROJ_ROWS)]
    xbs = [x_ref[rows, :].astype(BF16) for rows in blocks]

    def proj(r, a, n):
        return jnp.dot(xbs[r], w_ref[:, a:a + n], preferred_element_type=F32)

    def silu(z):
        return z * _sigmoid(z)

    lbf = _lower_bound(lbf_ref[...], layer)
    lbb = _lower_bound(lbb_ref[...], layer)

    def store_q(z, rows, sl):
        q_ref[rows, sl] = z.astype(BF16)

    def store_logf(tbl, out_ref):
        def fn(z, rows, sl):
            lb = tbl[:, sl]
            out_ref[rows, sl] = jnp.log(lb + (1.0 - lb) * _sigmoid(z))
        return fn

    def store_silu(out_ref):
        def fn(z, rows, sl):
            out_ref[rows, sl] = silu(z).astype(BF16)
        return fn

    sections = []
    for r, rows in enumerate(blocks):
        for c in range(2):
            a = c * half
            sl = slice(a, a + half)
            for off, fn in ((0, store_q), (C_W, store_logf(lbf, lf_ref)), (2 * C_W, store_logf(lbb, lb_ref)),
                            (3 * C_W, store_silu(v_ref)), (3 * C_W + C_VW, store_silu(g_ref))):
                sections.append((functools.partial(proj, r, off + a, half), functools.partial(fn, rows=rows, sl=sl)))
    _pipelined(sections)


def _c_in(x2d, w, lbf, lbb, layer):
    t = x2d.shape[0]
    row = pl.BlockSpec((TM, C_W), lambda i: (i, 0))
    return pl.pallas_call(
        functools.partial(_c_in_kernel, layer=layer),
        grid=(t // TM,),
        in_specs=[row, _const_spec(w.shape), _const_spec((DEPTH, C_W)), _const_spec((DEPTH, C_W))],
        out_specs=[row] * 5,
        out_shape=[jax.ShapeDtypeStruct((t, C_W), d) for d in (BF16, F32, F32, BF16, BF16)],
        compiler_params=_params(("arbitrary",), VMEM_LIMIT),
        name="c_in",
    )(x2d, w, lbf, lbb)


def _hgrn_kernel(*refs, rev, fused):
    if fused:
        q_ref, lf_ref, v_ref, of_ref, gate_ref, gn_ref, y_ref, st_ref = refs
    else:
        q_ref, lf_ref, v_ref, y_ref, st_ref = refs

    @pl.when(pl.program_id(2) == 0)
    def _():
        st_ref[...] = jnp.zeros(st_ref.shape, F32)

    ri = lax.broadcasted_iota(jnp.int32, (HG_CHUNK, HG_CHUNK), 0)
    ci = lax.broadcasted_iota(jnp.int32, (HG_CHUNK, HG_CHUNK), 1)
    tri = (ci >= ri) if rev else (ci <= ri)
    trib = jnp.where(tri, 1.0, 0.0).astype(BF16)
    dotf = lambda a, b: jnp.dot(a, b, preferred_element_type=F32)
    n_chunks = HG_TB // HG_CHUNK
    order = range(n_chunks - 1, -1, -1) if rev else range(n_chunks)
    units = [(c, h) for c in order for h in range(HG_HP)]
    window = lambda u: (0, slice(u[0] * HG_CHUNK, (u[0] + 1) * HG_CHUNK), slice(u[1] * C_KEY, (u[1] + 1) * C_KEY))

    b_all = {}
    for u in units:
        lf = lf_ref[window(u)]
        hi = lf.astype(BF16)
        lo = (lf - hi.astype(F32)).astype(BF16)
        b_all[u] = dotf(trib, hi) + dotf(trib, lo)

    qdec, kdec, etot, a_rows = {}, {}, {}, {}
    for u in units:
        lf, b = lf_ref[window(u)], b_all[u]
        qf = q_ref[window(u)].astype(F32)
        bex = b - lf
        kk = 1.0 - jnp.exp(lf)
        btot = b[0:1] if rev else b[HG_CHUNK - 1:HG_CHUNK]
        qdec[u] = (qf * jnp.exp(b)).astype(BF16)
        kdec[u] = (kk * jnp.exp(btot - b)).astype(BF16)
        etot[u] = jnp.exp(btot)
        rows = []
        for blk in range(HG_CHUNK // HG_SUB):
            r0 = blk * HG_SUB
            ref = bex[r0 + HG_SUB - 1:r0 + HG_SUB] if rev else bex[r0:r0 + 1]
            qt = (qf[r0:r0 + HG_SUB] * jnp.exp(b[r0:r0 + HG_SUB] - ref)).astype(BF16)
            lo_r, hi_r = (r0, HG_CHUNK) if rev else (0, r0 + HG_SUB)
            ks = (kk[lo_r:hi_r] * jnp.exp(ref - b[lo_r:hi_r])).astype(BF16)
            pieces = []
            if lo_r > 0:
                pieces.append(jnp.zeros((lo_r, C_KEY), BF16))
            pieces.append(ks)
            if hi_r < HG_CHUNK:
                pieces.append(jnp.zeros((HG_CHUNK - hi_r, C_KEY), BF16))
            kfull = jnp.concatenate(pieces, axis=0) if len(pieces) > 1 else ks
            rows.append(lax.dot_general(qt, kfull, _NT, preferred_element_type=F32))
        a_rows[u] = rows
    attn = {u: jnp.where(tri, jnp.concatenate(a_rows[u], axis=0), 0.0).astype(BF16) for u in units}

    states = [st_ref[h] for h in range(HG_HP)]
    for u in units:
        h = u[1]
        vb = v_ref[window(u)]
        o = lax.dot_general(qdec[u], states[h].astype(BF16), _NT, preferred_element_type=F32)
        o = o + dotf(attn[u], vb)
        states[h] = states[h] * etot[u] + lax.dot_general(vb, kdec[u], _TN, preferred_element_type=F32)
        if fused:
            tot = of_ref[window(u)] + o
            inv = lax.rsqrt(jnp.mean(tot * tot, -1, keepdims=True) + RMS_EPS)
            y = tot * inv * gn_ref[:, window(u)[2]] * gate_ref[window(u)].astype(F32)
            y_ref[window(u)] = y.astype(BF16)
        else:
            y_ref[window(u)] = o
    for h in range(HG_HP):
        st_ref[h] = states[h]


def _hgrn(q, lf, v, batch, seq, rev, fused_args=None):
    nb = seq // HG_TB
    width = HG_HP * C_KEY
    blk = (lambda b, h, i: (b, nb - 1 - i, h)) if rev else (lambda b, h, i: (b, i, h))
    tile = pl.BlockSpec((1, HG_TB, width), blk)
    view = lambda t: t.reshape(batch, seq, C_W)
    ins, specs = [view(q), view(lf), view(v)], [tile, tile, tile]
    fused = fused_args is not None
    if fused:
        o_f, gate, gn = fused_args
        ins += [view(o_f), view(gate), gn]
        specs += [tile, tile, pl.BlockSpec((1, width), lambda b, h, i: (0, h))]
    out = pl.pallas_call(
        functools.partial(_hgrn_kernel, rev=rev, fused=fused),
        grid=(batch, C_HEADS // HG_HP, nb),
        in_specs=specs,
        out_specs=tile,
        out_shape=jax.ShapeDtypeStruct((batch, seq, C_VW), BF16 if fused else F32),
        scratch_shapes=[pltpu.VMEM((HG_HP, C_VAL, C_KEY), F32)],
        compiler_params=_params(("arbitrary",) * 3),
        name="hgrn_bwd" if rev else "hgrn_fwd",
    )(*ins)
    return out.reshape(batch * seq, C_VW)


def _proj_ln_kernel(a_ref, x_ref, w_ref, g_ref, b_ref, out_ref):
    y = _row_block_dot(a_ref[...], w_ref[...])
    out_ref[...] = _layer_norm(ALPHA * x_ref[...] + y, g_ref[...], b_ref[...])


def _proj_ln(a, x2d, w, g, b):
    t = x2d.shape[0]
    kdim = a.shape[1]
    return pl.pallas_call(
        _proj_ln_kernel,
        grid=(t // TM,),
        in_specs=[pl.BlockSpec((TM, kdim), lambda i: (i, 0)), pl.BlockSpec((TM, D_MODEL), lambda i: (i, 0)),
                  _const_spec((kdim, D_MODEL)), _const_spec((1, D_MODEL)), _const_spec((1, D_MODEL))],
        out_specs=pl.BlockSpec((TM, D_MODEL), lambda i: (i, 0)),
        out_shape=jax.ShapeDtypeStruct((t, D_MODEL), F32),
        compiler_params=_params(("arbitrary",), VMEM_LIMIT),
        name="c_out",
    )(a, x2d, w, g, b)


def _prep_weights(w_in_ab, w_out_ab, qn_ab, kn_ab, w_in_c, w_out_c, gn_c, ffn_w_up, ffn_w_down):
    scale = HEAD_DIM ** -0.5
    ab = []
    for j in range(w_in_ab.shape[0]):
        w = w_in_ab[j]
        o3 = 3 * A_W
        kb = w[:, o3 + B_QW:o3 + B_QW + B_KVW]
        vb = w[:, o3 + B_QW + B_KVW:]
        dup = lambda t: jnp.concatenate([t[:, :HEAD_DIM], t[:, :HEAD_DIM], t[:, HEAD_DIM:], t[:, HEAD_DIM:]], 1)
        w_ext = jnp.concatenate([w[:, :A_W] * scale, w[:, A_W:o3 + B_QW], dup(kb), dup(vb)], 1).astype(BF16)
        qg = (jnp.concatenate([qn_ab[j], qn_ab[j]]) * (scale * LOG2E)).reshape(1, LANES)
        kg = jnp.concatenate([kn_ab[j], kn_ab[j]]).reshape(1, LANES)
        ab.append((w_ext, qg, kg, w_out_ab[j].astype(BF16)))
    cc = [(w_in_c[j].astype(BF16), w_out_c[j].astype(BF16), gn_c[j].reshape(1, C_VW)) for j in range(w_in_c.shape[0])]
    return ab, cc, ffn_w_up.astype(BF16), ffn_w_down.astype(BF16)


def _trunk(x, prep, lb_fwd, lb_bwd, ln_mix_g, ln_mix_b, ln_ffn_g, ln_ffn_b, ffn_conv_w, ffn_conv_b):
    ab, cc, wu, wd = prep
    batch, seq, _ = x.shape
    x2d = x.reshape(batch * seq, D_MODEL)
    tabs = _rope_tables(seq)
    blk = jnp.arange(LANES) // HEAD_DIM
    ones2 = (blk[:, None] == blk[None, :]).astype(BF16)
    vec = lambda t: t.reshape(1, -1)
    for l in range(DEPTH):
        j = l // 2
        if l % 2 == 0:
            w_ext, qg, kg, w_out = ab[j]
            assert [d for _, d in A_PATTERNS] == [1, 4, DIL_CLASSES]
            assert all(window // (2 * d) == HALF_WINDOW for window, d in A_PATTERNS)
            qa, ka, va, qb, kd, vd, qa16, ka16, va16 = _ab_in(x2d, w_ext, tabs, qg, kg, ones2, batch, seq)
            o1, l1 = _band1(qa, ka, va, batch, seq)
            o16, l16 = _band16(qa16, ka16, va16, batch, seq)
            om, lm = _band4(qa16, ka16, va16, o16, l16, batch, seq)
            yb = _gqa(qb, kd, vd, batch, seq)
            x2d = _ab_out(o1, l1, om, lm, yb, x2d, w_out, vec(ln_mix_g[l]), vec(ln_mix_b[l]), seq)
        else:
            w_in, w_out, gn = cc[j]
            q, lf, lb, v, gate = _c_in(x2d, w_in, lb_fwd, lb_bwd, l)
            o_f = _hgrn(q, lf, v, batch, seq, rev=False)
            y = _hgrn(q, lb, v, batch, seq, rev=True, fused_args=(o_f, gate, gn))
            x2d = _proj_ln(y, x2d, w_out, vec(ln_mix_g[l]), vec(ln_mix_b[l]))
        x2d = _ffn(x2d, wu[l], ffn_conv_w[l], vec(ffn_conv_b[l]), wd[l], vec(ln_ffn_g[l]), vec(ln_ffn_b[l]), seq)
    return x2d.reshape(batch, seq, D_MODEL)


def kernel(x_prompt, x_sample, w_in_ab, w_out_ab, qn_ab, kn_ab, w_in_c, w_out_c, lb_fwd, lb_bwd, gn_c, ln_mix_g, ln_mix_b, ln_ffn_g, ln_ffn_b, ffn_w_up, ffn_conv_w, ffn_conv_b, ffn_w_down):
    prep = _prep_weights(w_in_ab, w_out_ab, qn_ab, kn_ab, w_in_c, w_out_c, gn_c, ffn_w_up, ffn_w_down)
    rest = (lb_fwd, lb_bwd, ln_mix_g, ln_mix_b, ln_ffn_g, ln_ffn_b, ffn_conv_w, ffn_conv_b)
    return (_trunk(x_prompt, prep, *rest), _trunk(x_sample, prep, *rest))
```

```python
import functools

import jax
import jax.numpy as jnp
from jax import lax
from jax.experimental import pallas as pl
from jax.experimental.pallas import tpu as pltpu

F32 = jnp.float32
BF16 = jnp.bfloat16

D_MODEL = 1024
DEPTH = 2
HEAD_DIM = 64
A_HEADS = 8
A_PATTERNS = ((128, 1), (512, 4), (2048, 16))
B_HEADS = 8
B_KV_HEADS = 2
GRID_W = 64
ROPE_THETA = 500000.0
ROPE_DIM = HEAD_DIM // 4
AXIAL_THETA = 10000.0
C_HEADS = 8
C_KEY = 128
C_VAL = 128
D_FF = 2816
ALPHA = (2 * DEPTH) ** 0.25
LN_EPS = 1e-5
RMS_EPS = 1e-6
A_W = A_HEADS * HEAD_DIM
B_QW = B_HEADS * HEAD_DIM
B_KVW = B_KV_HEADS * HEAD_DIM
C_W = C_HEADS * C_KEY
C_VW = C_HEADS * C_VAL

LANES = 128
HALF_WINDOW = 64
TM = 512
PROJ_ROWS = 128
FFN_ROWS = 256
BAND_L = 128
BAND_STEP = 512
BAND4_STEP = 128
DIL_CLASSES = 16
BAND4_ROWS = 32
BAND4_HALO = 16
GQA_TQ = 512
GQA_TK = 512
GQA_CHAIN = 512
FF_CHUNKS = (768, 768, 768, 512)
FFN_SUB = 1
HG_CHUNK = 64
HG_SUB = 16
HG_TB = 512
HG_HP = 4
HG_TB_OUT = 256
NEG = -1e30
LOG2E = 1.4426950408889634
VMEM_LIMIT = 56 * 1024 * 1024

_NT = (((1,), (1,)), ((), ()))
_TN = (((0,), (0,)), ((), ()))


def _params(sem, vmem=None):
    return pltpu.CompilerParams(dimension_semantics=sem, vmem_limit_bytes=vmem)


def _const_spec(shape, single=False):
    nd = len(shape)
    if single:
        return pl.BlockSpec(shape, lambda *_: (0,) * nd, pipeline_mode=pl.Buffered(1))
    return pl.BlockSpec(shape, lambda *_: (0,) * nd)


def _layer_norm(z, g, b):
    mu = jnp.mean(z, -1, keepdims=True)
    d = z - mu
    var = jnp.mean(d * d, -1, keepdims=True)
    return d * lax.rsqrt(var + LN_EPS) * g + b


def _sigmoid(z):
    return 1.0 / (1.0 + jnp.exp(-z))


def _row_block_dot(x, w, rows=PROJ_ROWS):
    m = x.shape[0]
    starts = list(range(0, m - m % rows, rows)) or [0]
    ends = starts[1:] + [m]
    parts = [jnp.dot(x[a:b], w, preferred_element_type=F32) for a, b in zip(starts, ends)]
    return parts[0] if len(parts) == 1 else jnp.concatenate(parts, axis=0)


def _pipelined(sections):
    pending = sections[0][0]()
    for i, (_, epilogue) in enumerate(sections):
        upcoming = sections[i + 1][0]() if i + 1 < len(sections) else None
        epilogue(pending)
        pending = upcoming


def _rope_tables(seq):
    pos = jnp.arange(seq, dtype=F32)
    d = jnp.arange(LANES) % HEAD_DIM

    def angles(p, dim, theta):
        freqs = theta ** (-(jnp.arange(0, dim, 2, dtype=F32) / dim))
        return p[:, None] * freqs[None, :]

    h = ROPE_DIM // 2
    ang = angles(pos, ROPE_DIM, ROPE_THETA)
    a = ang[:, d % h]
    lo, hi = d < h, (d >= h) & (d < ROPE_DIM)
    pc = jnp.where(lo | hi, jnp.cos(a), 1.0)
    psa = jnp.where(lo, -jnp.sin(a), 0.0)
    psb = jnp.where(hi, jnp.sin(a), 0.0)

    q = HEAD_DIM // 4
    row = jnp.floor(pos / GRID_W)
    col = pos - row * GRID_W
    ar = angles(row, HEAD_DIM // 2, AXIAL_THETA)[:, d % q]
    ac_ = angles(col, HEAD_DIM // 2, AXIAL_THETA)[:, d % q]
    a2 = jnp.where(d < HEAD_DIM // 2, ar, ac_)
    first = (d % (HEAD_DIM // 2)) < q
    ac = jnp.cos(a2)
    asa = jnp.where(first, -jnp.sin(a2), 0.0)
    asb = jnp.where(first, 0.0, jnp.sin(a2))
    return [t.astype(F32) for t in (pc, psa, psb, ac, asa, asb)]


def _rope(seg, c, sa, sb, shift):
    return seg * c + pltpu.roll(seg, LANES - shift, 1) * sa + pltpu.roll(seg, shift, 1) * sb


def _ab_in_kernel(x_ref, w_ref, pc_ref, psa_ref, psb_ref, ac_ref, asa_ref, asb_ref, qg_ref, kg_ref,
                  ones_ref, qa_ref, ka_ref, va_ref, qb_ref, kd_ref, vd_ref, qa16_ref, ka16_ref, va16_ref,
                  qs_ref, ks_ref, vs_ref):
    xb = x_ref[...].astype(BF16)

    def proj(a, n):
        return jnp.dot(xb, w_ref[:, a:a + n], preferred_element_type=F32)

    pc, psa, psb = pc_ref[...], psa_ref[...], psb_ref[...]
    ac, asa, asb = ac_ref[...], asa_ref[...], asb_ref[...]

    def dilated_a(h, rope, out_ref, out16_ref, slab_ref):
        for g in range(A_W // LANES):
            sl = slice(g * LANES, (g + 1) * LANES)
            y = _rope(h[:, sl], pc, psa, psb, ROPE_DIM // 2) if rope else h[:, sl]
            out_ref[:, sl] = y.astype(BF16)
            slab_ref[g] = y
        for c in range(DIL_CLASSES):
            for g in range(A_W // LANES):
                rows = slab_ref[g, pl.ds(c, TM // DIL_CLASSES, stride=DIL_CLASSES), :]
                out16_ref[0, c, :, g * LANES:(g + 1) * LANES] = rows.astype(BF16)

    def norm_rope(h, gain, out_ref):
        for g in range(h.shape[1] // LANES):
            sl = slice(g * LANES, (g + 1) * LANES)
            seg = h[:, sl]
            sq = seg * seg
            hi = sq.astype(BF16)
            lo = (sq - hi.astype(F32)).astype(BF16)
            ss = (jnp.dot(hi, ones_ref[...], preferred_element_type=F32)
                  + jnp.dot(lo, ones_ref[...], preferred_element_type=F32))
            y = seg * lax.rsqrt(ss * (1.0 / HEAD_DIM) + RMS_EPS) * gain
            out_ref[:, sl] = _rope(y, ac, asa, asb, HEAD_DIM // 4).astype(BF16)

    def cast_to(out_ref):
        def fn(h):
            out_ref[...] = h.astype(BF16)
        return fn

    o_qb = 3 * A_W
    o_kd = o_qb + B_QW
    o_vd = o_kd + 2 * B_KVW
    _pipelined([
        (functools.partial(proj, 0, A_W),
         functools.partial(dilated_a, rope=True, out_ref=qa_ref, out16_ref=qa16_ref, slab_ref=qs_ref)),
        (functools.partial(proj, A_W, A_W),
         functools.partial(dilated_a, rope=True, out_ref=ka_ref, out16_ref=ka16_ref, slab_ref=ks_ref)),
        (functools.partial(proj, 2 * A_W, A_W),
         functools.partial(dilated_a, rope=False, out_ref=va_ref, out16_ref=va16_ref, slab_ref=vs_ref)),
        (functools.partial(proj, o_qb, B_QW), functools.partial(norm_rope, gain=qg_ref[...], out_ref=qb_ref)),
        (functools.partial(proj, o_kd, 2 * B_KVW), functools.partial(norm_rope, gain=kg_ref[...], out_ref=kd_ref)),
        (functools.partial(proj, o_vd, 2 * B_KVW), cast_to(vd_ref)),
    ])


def _ab_in(x2d, w_ext, tabs, qg, kg, ones2, batch, seq):
    t = x2d.shape[0]
    nt = seq // TM
    tab = pl.BlockSpec((TM, LANES), lambda i: (i % nt, 0))
    row = lambda n: pl.BlockSpec((TM, n), lambda i: (i, 0))
    cls = pl.BlockSpec((1, DIL_CLASSES, TM // DIL_CLASSES, A_W), lambda i: (i // nt, 0, i % nt, 0))
    cls_shape = jax.ShapeDtypeStruct((batch, DIL_CLASSES, seq // DIL_CLASSES, A_W), BF16)
    wn = w_ext.shape[1]
    return pl.pallas_call(
        _ab_in_kernel,
        grid=(t // TM,),
        in_specs=[row(D_MODEL), _const_spec((D_MODEL, wn))] + [tab] * 6
        + [_const_spec((1, LANES)), _const_spec((1, LANES)), _const_spec((LANES, LANES))],
        out_specs=[row(A_W)] * 3 + [row(B_QW), row(2 * B_KVW), row(2 * B_KVW)] + [cls] * 3,
        out_shape=[jax.ShapeDtypeStruct((t, A_W), BF16)] * 3
        + [jax.ShapeDtypeStruct((t, B_QW), BF16)]
        + [jax.ShapeDtypeStruct((t, 2 * B_KVW), BF16)] * 2 + [cls_shape] * 3,
        scratch_shapes=[pltpu.VMEM((A_W // LANES, TM, LANES), F32)] * 3,
        compiler_params=_params(("arbitrary",), VMEM_LIMIT),
        name="ab_in",
    )(x2d, w_ext, *tabs, qg, kg, ones2)


def _band_groups(qs, ks, vs, bias, left):
    chains = [(g, sel) for g in range(len(qs)) for sel in (left, jnp.logical_not(left))]
    scores = []
    for g, sel in chains:
        qm = jnp.where(sel, qs[g], jnp.zeros_like(qs[g]))
        scores.append(lax.dot_general(qm, ks[g], _NT, preferred_element_type=F32) + bias)
    probs = []
    for s in scores:
        mx = jnp.max(s, axis=1, keepdims=True)
        pe = jnp.exp(s - mx)
        probs.append((mx, jnp.sum(pe, axis=1, keepdims=True), pe.astype(BF16)))
    outs = []
    for (g, _), (mx, l, pe) in zip(chains, probs):
        o = jnp.dot(pe, vs[g], preferred_element_type=F32)
        outs.append((o * (1.0 / l), mx + jnp.log(l)))
    return [(jnp.where(left, outs[2 * g][0], outs[2 * g + 1][0]), jnp.where(left, outs[2 * g][1], outs[2 * g + 1][1]))
            for g in range(len(qs))]


def _band_kernel(q_ref, kp_ref, kc_ref, kn_ref, vp_ref, vc_ref, vn_ref, o_ref, lse_ref, *, m, step_rows):
    nk = BAND_L + 2 * HALF_WINDOW
    base = pl.program_id(len(q_ref.shape) - 2) * step_rows
    r = lax.broadcasted_iota(jnp.int32, (BAND_L, nk), 0)
    j = lax.broadcasted_iota(jnp.int32, (BAND_L, nk), 1)
    band = jnp.where(jnp.abs(j - HALF_WINDOW - r) <= HALF_WINDOW, 0.0, NEG)
    left = lax.broadcasted_iota(jnp.int32, (BAND_L, LANES), 1) < HEAD_DIM
    pre = (0,) * (len(q_ref.shape) - 2)
    lanes = [slice(p * LANES, (p + 1) * LANES) for p in range(A_W // LANES)]
    at = lambda rows, sl: pre + (rows, sl)
    whole = slice(None)
    ks = [jnp.concatenate([kp_ref[at(whole, sl)], kc_ref[at(whole, sl)], kn_ref[at(whole, sl)]], axis=0) for sl in lanes]
    vs = [jnp.concatenate([vp_ref[at(whole, sl)], vc_ref[at(whole, sl)], vn_ref[at(whole, sl)]], axis=0) for sl in lanes]
    for blk in range(step_rows // BAND_L):
        r0 = blk * BAND_L
        rows = slice(r0, r0 + BAND_L)
        kpos = base + r0 - HALF_WINDOW + j
        bias = jnp.where(kpos >= 0, band, NEG)
        bias = jnp.where(kpos < m, bias, NEG)
        res = _band_groups([q_ref[at(rows, sl)] for sl in lanes], [kk[r0:r0 + nk] for kk in ks],
                           [vv[r0:r0 + nk] for vv in vs], bias, left)
        for sl, (o, lse) in zip(lanes, res):
            o_ref[at(rows, sl)] = o.astype(BF16)
            lse_ref[at(rows, sl)] = lse


def _band_specs(lead, rows, step_rows):
    nh = rows // HALF_WINDOW
    ratio = step_rows // HALF_WINDOW
    ones = (1,) * lead
    cur = pl.BlockSpec(ones + (step_rows, A_W), lambda *g: g[:lead] + (g[-1], 0))
    prev = pl.BlockSpec(ones + (HALF_WINDOW, A_W), lambda *g: g[:lead] + (jnp.maximum(g[-1] * ratio - 1, 0), 0))
    nxt = pl.BlockSpec(ones + (HALF_WINDOW, A_W), lambda *g: g[:lead] + (jnp.minimum((g[-1] + 1) * ratio, nh - 1), 0))
    return cur, prev, nxt


def _band1(q, k, v, batch, seq):
    step_rows = min(BAND_STEP, seq)
    cur, prev, nxt = _band_specs(1, seq, step_rows)
    view = lambda t: t.reshape(batch, seq, A_W)
    o, lse = pl.pallas_call(
        functools.partial(_band_kernel, m=seq, step_rows=step_rows),
        grid=(batch, seq // step_rows),
        in_specs=[cur, prev, cur, nxt, prev, cur, nxt],
        out_specs=[cur, cur],
        out_shape=[jax.ShapeDtypeStruct((batch, seq, A_W), BF16), jax.ShapeDtypeStruct((batch, seq, A_W), F32)],
        compiler_params=_params(("arbitrary",) * 2),
        name="band_d1",
    )(view(q), view(k), view(k), view(k), view(v), view(v), view(v))
    return o.reshape(batch * seq, A_W), lse.reshape(batch * seq, A_W)


def _band16(q16, k16, v16, batch, seq):
    m = seq // DIL_CLASSES
    step_rows = min(BAND_STEP, m)
    cur, prev, nxt = _band_specs(2, m, step_rows)
    shape = (batch, DIL_CLASSES, m, A_W)
    return pl.pallas_call(
        functools.partial(_band_kernel, m=m, step_rows=step_rows),
        grid=(batch, DIL_CLASSES, m // step_rows),
        in_specs=[cur, prev, cur, nxt, prev, cur, nxt],
        out_specs=[cur, cur],
        out_shape=[jax.ShapeDtypeStruct(shape, BF16), jax.ShapeDtypeStruct(shape, F32)],
        compiler_params=_params(("arbitrary",) * 3),
        name="band_d16",
    )(q16, k16, k16, k16, v16, v16, v16)


def _band4_kernel(q_ref, kp_ref, kc_ref, kn_ref, vp_ref, vc_ref, vn_ref, o16_ref, l16_ref, o_ref, lse_ref, *,
                  m16, step_rows):
    qr = BAND4_ROWS
    kr = qr + 2 * BAND4_HALO
    i = pl.program_id(2)
    shift_q, shift_k = qr.bit_length() - 1, kr.bit_length() - 1
    row = lax.broadcasted_iota(jnp.int32, (4 * qr, 4 * kr), 0)
    col = lax.broadcasted_iota(jnp.int32, (4 * qr, 4 * kr), 1)
    uq, rq = row >> shift_q, row & (qr - 1)
    uk, jk = col >> shift_k, col & (kr - 1)
    band = jnp.where(jnp.abs(4 * (jk - BAND4_HALO - rq) + (uk - uq)) <= HALF_WINDOW, 0.0, NEG)
    left = lax.broadcasted_iota(jnp.int32, (4 * qr, LANES), 1) < HEAD_DIM
    slabs = [slice(p * LANES, (p + 1) * LANES) for p in range(A_W // LANES)]
    window = lambda refs, u, sl: jnp.concatenate([r[0, u, 0, :, sl] for r in refs], axis=0)
    kwin = [[window([kp_ref, kc_ref, kn_ref], u, sl) for u in range(4)] for sl in slabs]
    vwin = [[window([vp_ref, vc_ref, vn_ref], u, sl) for u in range(4)] for sl in slabs]
    for blk in range(step_rows // qr):
        r0 = blk * qr
        rows = slice(r0, r0 + qr)
        kpos = i * step_rows + r0 - BAND4_HALO + jk
        bias = jnp.where(kpos >= 0, band, NEG)
        bias = jnp.where(kpos < m16, bias, NEG)
        classes = lambda ref, sl: jnp.concatenate([ref[0, u, 0, rows, sl] for u in range(4)], axis=0)
        branch4 = _band_groups([classes(q_ref, sl) for sl in slabs],
                               [jnp.concatenate([w[r0:r0 + kr] for w in kw], axis=0) for kw in kwin],
                               [jnp.concatenate([w[r0:r0 + kr] for w in vw], axis=0) for vw in vwin], bias, left)
        for sl, (o4, l4) in zip(slabs, branch4):
            o16 = classes(o16_ref, sl).astype(F32)
            l16 = classes(l16_ref, sl)
            mx = jnp.maximum(l4, l16)
            w4, w16 = jnp.exp(l4 - mx), jnp.exp(l16 - mx)
            tot = w4 + w16
            om = (w4 * o4 + w16 * o16) * (1.0 / tot)
            lm = mx + jnp.log(tot)
            for u in range(4):
                o_ref[0, u, 0, rows, sl] = om[u * qr:(u + 1) * qr].astype(BF16)
                lse_ref[0, u, 0, rows, sl] = lm[u * qr:(u + 1) * qr]


def _band4(q16, k16, v16, o16, l16, batch, seq):
    m16 = seq // DIL_CLASSES
    step_rows = min(BAND4_STEP, m16)
    ratio = step_rows // BAND4_HALO
    nh = m16 // BAND4_HALO
    view = lambda t: t.reshape(batch, 4, 4, m16, A_W)
    cur = pl.BlockSpec((1, 4, 1, step_rows, A_W), lambda b, c, i: (b, 0, c, i, 0))
    prev = pl.BlockSpec((1, 4, 1, BAND4_HALO, A_W), lambda b, c, i: (b, 0, c, jnp.maximum(i * ratio - 1, 0), 0))
    nxt = pl.BlockSpec((1, 4, 1, BAND4_HALO, A_W), lambda b, c, i: (b, 0, c, jnp.minimum((i + 1) * ratio, nh - 1), 0))
    shape = (batch, 4, 4, m16, A_W)
    o, lse = pl.pallas_call(
        functools.partial(_band4_kernel, m16=m16, step_rows=step_rows),
        grid=(batch, 4, m16 // step_rows),
        in_specs=[cur, prev, cur, nxt, prev, cur, nxt, cur, cur],
        out_specs=[cur, cur],
        out_shape=[jax.ShapeDtypeStruct(shape, BF16), jax.ShapeDtypeStruct(shape, F32)],
        compiler_params=_params(("arbitrary",) * 3),
        name="band_d4",
    )(view(q16), view(k16), view(k16), view(k16), view(v16), view(v16), view(v16), view(o16), view(l16))
    return o.reshape(batch, DIL_CLASSES, m16, A_W), lse.reshape(batch, DIL_CLASSES, m16, A_W)


def _gqa_kernel(q_ref, k_ref, v_ref, o_ref, qs_ref, m_ref, acc_ref, s_ref, *, seq):
    left = lax.broadcasted_iota(jnp.int32, (GQA_TQ, LANES), 1) < HEAD_DIM
    for pr in range(2):
        qp = q_ref[0, :, pr * LANES:(pr + 1) * LANES]
        zero = jnp.zeros_like(qp)
        qs_ref[(2 * pr) * GQA_TQ:(2 * pr + 1) * GQA_TQ, :] = jnp.where(left, qp, zero)
        qs_ref[(2 * pr + 1) * GQA_TQ:(2 * pr + 2) * GQA_TQ, :] = jnp.where(left, zero, qp)
    m_ref[...] = jnp.full(m_ref.shape, NEG, F32)
    acc_ref[...] = jnp.zeros(acc_ref.shape, F32)
    n_tiles = seq // GQA_TK
    ones_lane = lax.broadcasted_iota(jnp.int32, (GQA_TK, LANES), 1) == HEAD_DIM

    chains = [slice(c * GQA_CHAIN, (c + 1) * GQA_CHAIN) for c in range(4 * GQA_TQ // GQA_CHAIN)]

    def scores(tile, buf, cols):
        off = pl.multiple_of(tile * GQA_TK, GQA_TK)
        k = k_ref[0, pl.ds(off, GQA_TK), :]
        s_ref[buf, :, cols] = lax.dot_general(k, qs_ref[cols, :], _NT, preferred_element_type=F32)

    def softmax_pv(tile, buf, cols):
        off = pl.multiple_of(tile * GQA_TK, GQA_TK)
        v = jnp.where(ones_lane, jnp.ones((), BF16), v_ref[0, pl.ds(off, GQA_TK), :])
        s = s_ref[buf, :, cols]
        m_prev = m_ref[:, cols]
        m_new = jnp.maximum(m_prev, jnp.max(s, axis=0, keepdims=True))
        alpha = jnp.exp2(m_prev - m_new)
        p = jnp.exp2((s - m_new).astype(BF16))
        pv = lax.dot_general(v, p, _TN, preferred_element_type=F32)
        acc_ref[:, cols] = alpha * acc_ref[:, cols] + pv
        m_ref[:, cols] = m_new

    def step(tile, buf, prefetch):
        for cols in chains:
            if prefetch:
                scores(tile + 1, 1 - buf, cols)
            softmax_pv(tile, buf, cols)

    def pair(j, carry):
        step(2 * j, 0, True)
        step(2 * j + 1, 1, True)
        return carry

    for cols in chains:
        scores(0, 0, cols)
    lax.fori_loop(0, n_tiles // 2 - 1, pair, 0)
    step(n_tiles - 2, 0, True)
    step(n_tiles - 1, 1, False)
    acc = acc_ref[...]
    a = (acc * (1.0 / acc[HEAD_DIM:HEAD_DIM + 1, :])).T
    for pr in range(2):
        lo = a[(2 * pr) * GQA_TQ:(2 * pr + 1) * GQA_TQ]
        hi = pltpu.roll(a[(2 * pr + 1) * GQA_TQ:(2 * pr + 2) * GQA_TQ], HEAD_DIM, 1)
        o_ref[0, :, pr * LANES:(pr + 1) * LANES] = jnp.where(left, lo, hi).astype(BF16)


def _gqa(qb, kd, vd, batch, seq):
    gw = B_QW // B_KV_HEADS
    cols = 4 * GQA_TQ
    out = pl.pallas_call(
        functools.partial(_gqa_kernel, seq=seq),
        grid=(batch, B_KV_HEADS, seq // GQA_TQ),
        in_specs=[pl.BlockSpec((1, GQA_TQ, gw), lambda b, g, qi: (b, qi, g)),
                  pl.BlockSpec((1, seq, LANES), lambda b, g, qi: (b, 0, g)),
                  pl.BlockSpec((1, seq, LANES), lambda b, g, qi: (b, 0, g))],
        out_specs=pl.BlockSpec((1, GQA_TQ, gw), lambda b, g, qi: (b, qi, g)),
        out_shape=jax.ShapeDtypeStruct((batch, seq, B_QW), BF16),
        scratch_shapes=[pltpu.VMEM((cols, LANES), BF16), pltpu.VMEM((1, cols), F32),
                        pltpu.VMEM((LANES, cols), F32), pltpu.VMEM((2, GQA_TK, cols), F32)],
        compiler_params=_params(("arbitrary",) * 3, VMEM_LIMIT),
        name="gqa",
    )(qb.reshape(batch, seq, B_QW), kd.reshape(batch, seq, 2 * B_KVW), vd.reshape(batch, seq, 2 * B_KVW))
    return out.reshape(batch * seq, B_QW)


def _ab_out_kernel(o1_ref, l1_ref, om_ref, lm_ref, yb_ref, x_ref, w_ref, g_ref, b_ref, out_ref, os_ref, ls_ref):
    for c in range(DIL_CLASSES):
        for g in range(A_W // LANES):
            sl = slice(g * LANES, (g + 1) * LANES)
            rows = pl.ds(c, TM // DIL_CLASSES, stride=DIL_CLASSES)
            os_ref[g, rows, :] = om_ref[0, c, :, sl].astype(F32)
            ls_ref[g, rows, :] = lm_ref[0, c, :, sl]
    ya = []
    for g in range(A_W // LANES):
        sl = slice(g * LANES, (g + 1) * LANES)
        l1, lm = l1_ref[:, sl], ls_ref[g]
        mx = jnp.maximum(l1, lm)
        e1, em = jnp.exp(l1 - mx), jnp.exp(lm - mx)
        ya.append(((e1 * o1_ref[:, sl].astype(F32) + em * os_ref[g]) * (1.0 / (e1 + em))).astype(BF16))
    y = _row_block_dot(jnp.concatenate(ya + [yb_ref[...]], axis=1), w_ref[...])
    out_ref[...] = _layer_norm(ALPHA * x_ref[...] + y, g_ref[...], b_ref[...])


def _ab_out(o1, l1, om, lm, yb, x2d, w, g, b, seq):
    t = x2d.shape[0]
    nt = seq // TM
    row = lambda n: pl.BlockSpec((TM, n), lambda i: (i, 0))
    cls = pl.BlockSpec((1, DIL_CLASSES, TM // DIL_CLASSES, A_W), lambda i: (i // nt, 0, i % nt, 0))
    return pl.pallas_call(
        _ab_out_kernel,
        grid=(t // TM,),
        in_specs=[row(A_W), row(A_W), cls, cls, row(B_QW), row(D_MODEL), _const_spec((A_W + B_QW, D_MODEL)),
                  _const_spec((1, D_MODEL)), _const_spec((1, D_MODEL))],
        out_specs=row(D_MODEL),
        out_shape=jax.ShapeDtypeStruct((t, D_MODEL), F32),
        scratch_shapes=[pltpu.VMEM((A_W // LANES, TM, LANES), F32)] * 2,
        compiler_params=_params(("arbitrary",), VMEM_LIMIT),
        name="ab_out",
    )(o1, l1, om, lm, yb, x2d, w, g, b)


def _gelu(x):
    return 0.5 * x * (1.0 + jnp.tanh(0.7978845608028654 * (x + 0.044715 * (x * x * x))))


def _ffn_kernel(x_ref, xp_ref, xn_ref, wu_ref, cw_ref, cb_ref, wd_ref, g_ref, b_ref, out_ref, *, nt):
    i = pl.program_id(0)
    keep_prev = (i % nt != 0).astype(F32)
    keep_next = (i % nt != nt - 1).astype(F32)
    rows = lax.broadcasted_iota(jnp.int32, (TM, 1), 0)
    offsets = [sum(FF_CHUNKS[:c]) for c in range(len(FF_CHUNKS))]
    operands, acc = {}, {}

    def tile_operands(j):
        if j not in operands:
            r0 = j * TM
            xb = x_ref[r0:r0 + TM, :].astype(BF16)
            before = xp_ref[...] if j == 0 else x_ref[r0 - 8:r0, :]
            after = xn_ref[...] if j == FFN_SUB - 1 else x_ref[r0 + TM:r0 + TM + 8, :]
            halo = jnp.concatenate([before, after], axis=0).astype(BF16)
            operands[j] = (xb, jnp.concatenate([xb, halo], axis=0))
        return operands[j]

    def up(j, c):
        a, n = offsets[c], FF_CHUNKS[c]
        xb, xe = tile_operands(j)
        u = _row_block_dot(xb, wu_ref[:, a:a + n], FFN_ROWS)
        ge = _row_block_dot(xe, wu_ref[:, D_FF + a:D_FF + a + n], FFN_ROWS)
        return u, ge

    def finish(j, c, pending):
        u, ge = pending
        a, n = offsets[c], FF_CHUNKS[c]
        gm = ge[0:TM]
        g_before = ge[TM + 7:TM + 8]
        g_after = ge[TM + 8:TM + 9]
        if j == 0:
            g_before = g_before * keep_prev
        if j == FFN_SUB - 1:
            g_after = g_after * keep_next
        gp = jnp.where(rows == 0, g_before, pltpu.roll(gm, 1, 0))
        gn = jnp.where(rows == TM - 1, g_after, pltpu.roll(gm, TM - 1, 0))
        cw = cw_ref[:, a:a + n]
        gc = gp * cw[0:1] + gm * cw[1:2] + gn * cw[2:3] + cb_ref[:, a:a + n]
        act = (_gelu(gc) * u).astype(BF16)
        part = _row_block_dot(act, wd_ref[a:a + n, :], FFN_ROWS)
        acc[j] = part if c == 0 else acc[j] + part
        if c == len(FF_CHUNKS) - 1:
            sl = slice(j * TM, (j + 1) * TM)
            out_ref[sl, :] = _layer_norm(ALPHA * x_ref[sl, :] + acc[j], g_ref[...], b_ref[...])

    _pipelined([(functools.partial(up, j, c), functools.partial(finish, j, c))
                for j in range(FFN_SUB) for c in range(len(FF_CHUNKS))])


def _ffn(x2d, wu, cw, cb, wd, g, b, seq):
    t = x2d.shape[0]
    step = FFN_SUB * TM
    nt = seq // step
    r8 = step // 8
    return pl.pallas_call(
        functools.partial(_ffn_kernel, nt=nt),
        grid=(t // step,),
        in_specs=[pl.BlockSpec((step, D_MODEL), lambda i: (i, 0)),
                  pl.BlockSpec((8, D_MODEL), lambda i: (jnp.maximum(i * r8 - 1, 0), 0)),
                  pl.BlockSpec((8, D_MODEL), lambda i: (jnp.minimum((i + 1) * r8, t // 8 - 1), 0)),
                  _const_spec((D_MODEL, 2 * D_FF), single=True),
                  _const_spec((3, D_FF)), _const_spec((1, D_FF)),
                  _const_spec((D_FF, D_MODEL), single=True),
                  _const_spec((1, D_MODEL)), _const_spec((1, D_MODEL))],
        out_specs=pl.BlockSpec((step, D_MODEL), lambda i: (i, 0)),
        out_shape=jax.ShapeDtypeStruct((t, D_MODEL), F32),
        compiler_params=_params(("arbitrary",), VMEM_LIMIT),
        name="ffn",
    )(x2d, x2d, x2d, wu, cw, cb, wd, g, b)


def _lower_bound(tbl, layer):
    rows = [tbl[r:r + 1] for r in range(DEPTH)]
    mx = functools.reduce(jnp.maximum, rows)
    es = [jnp.exp(r - mx) for r in rows]
    inv = 1.0 / functools.reduce(lambda a, b: a + b, es)
    ps = [e * inv for e in es]
    return functools.reduce(lambda a, b: a + b, ps[:layer + 1]) - ps[0]


def _c_in_kernel(x_ref, w_ref, lbf_ref, lbb_ref, q_ref, lf_ref, lb_ref, v_ref, g_ref, *, layer):
    half = C_W // 2
    blocks = [slice(r * PROJ_ROWS, (r + 1) * PROJ_ROWS) for r in range(TM // PROJ_ROWS)]
    xbs = [x_ref[rows, :].astype(BF16) for rows in blocks]

    def proj(r, a, n):
        return jnp.dot(xbs[r], w_ref[:, a:a + n], preferred_element_type=F32)

    def silu(z):
        return z * _sigmoid(z)

    lbf = _lower_bound(lbf_ref[...], layer)
    lbb = _lower_bound(lbb_ref[...], layer)

    def store_q(z, rows, sl):
        q_ref[rows, sl] = z.astype(BF16)

    def store_logf(tbl, out_ref):
        def fn(z, rows, sl):
            lb = tbl[:, sl]
            out_ref[rows, sl] = jnp.log(lb + (1.0 - lb) * _sigmoid(z))
        return fn

    def store_silu(out_ref):
        def fn(z, rows, sl):
            out_ref[rows, sl] = silu(z).astype(BF16)
        return fn

    sections = []
    for r, rows in enumerate(blocks):
        for c in range(2):
            a = c * half
            sl = slice(a, a + half)
            for off, fn in ((0, store_q), (C_W, store_logf(lbf, lf_ref)), (2 * C_W, store_logf(lbb, lb_ref)),
                            (3 * C_W, store_silu(v_ref)), (3 * C_W + C_VW, store_silu(g_ref))):
                sections.append((functools.partial(proj, r, off + a, half), functools.partial(fn, rows=rows, sl=sl)))
    _pipelined(sections)


def _c_in(x2d, w, lbf, lbb, layer):
    t = x2d.shape[0]
    row = pl.BlockSpec((TM, C_W), lambda i: (i, 0))
    return pl.pallas_call(
        functools.partial(_c_in_kernel, layer=layer),
        grid=(t // TM,),
        in_specs=[row, _const_spec(w.shape), _const_spec((DEPTH, C_W)), _const_spec((DEPTH, C_W))],
        out_specs=[row] * 5,
        out_shape=[jax.ShapeDtypeStruct((t, C_W), d) for d in (BF16, F32, F32, BF16, BF16)],
        compiler_params=_params(("arbitrary",), VMEM_LIMIT),
        name="c_in",
    )(x2d, w, lbf, lbb)


def _hgrn_kernel(*refs, rev, fused, heads, step_rows):
    if fused:
        (q_ref, lf_ref, v_ref, of_ref, gate_ref, gn_ref, x_ref, w_ref, lg_ref, lb_ref,
         out_ref, st_ref, y_ref) = refs
    else:
        q_ref, lf_ref, v_ref, y_ref, st_ref = refs

    @pl.when(pl.program_id(2) == 0)
    def _():
        st_ref[...] = jnp.zeros(st_ref.shape, F32)

    ri = lax.broadcasted_iota(jnp.int32, (HG_CHUNK, HG_CHUNK), 0)
    ci = lax.broadcasted_iota(jnp.int32, (HG_CHUNK, HG_CHUNK), 1)
    tri = (ci >= ri) if rev else (ci <= ri)
    trib = jnp.where(tri, 1.0, 0.0).astype(BF16)
    dotf = lambda a, b: jnp.dot(a, b, preferred_element_type=F32)
    n_chunks = step_rows // HG_CHUNK
    order = range(n_chunks - 1, -1, -1) if rev else range(n_chunks)
    units = [(c, h) for c in order for h in range(heads)]
    window = lambda u: (0, slice(u[0] * HG_CHUNK, (u[0] + 1) * HG_CHUNK), slice(u[1] * C_KEY, (u[1] + 1) * C_KEY))

    b_all = {}
    for u in units:
        lf = lf_ref[window(u)]
        hi = lf.astype(BF16)
        lo = (lf - hi.astype(F32)).astype(BF16)
        b_all[u] = dotf(trib, hi) + dotf(trib, lo)

    qdec, kdec, etot, a_rows = {}, {}, {}, {}
    for u in units:
        lf, b = lf_ref[window(u)], b_all[u]
        qf = q_ref[window(u)].astype(F32)
        bex = b - lf
        kk = 1.0 - jnp.exp(lf)
        btot = b[0:1] if rev else b[HG_CHUNK - 1:HG_CHUNK]
        qdec[u] = (qf * jnp.exp(b)).astype(BF16)
        kdec[u] = (kk * jnp.exp(btot - b)).astype(BF16)
        etot[u] = jnp.exp(btot)
        rows = []
        for blk in range(HG_CHUNK // HG_SUB):
            r0 = blk * HG_SUB
            ref = bex[r0 + HG_SUB - 1:r0 + HG_SUB] if rev else bex[r0:r0 + 1]
            qt = (qf[r0:r0 + HG_SUB] * jnp.exp(b[r0:r0 + HG_SUB] - ref)).astype(BF16)
            lo_r, hi_r = (r0, HG_CHUNK) if rev else (0, r0 + HG_SUB)
            ks = (kk[lo_r:hi_r] * jnp.exp(ref - b[lo_r:hi_r])).astype(BF16)
            pieces = []
            if lo_r > 0:
                pieces.append(jnp.zeros((lo_r, C_KEY), BF16))
            pieces.append(ks)
            if hi_r < HG_CHUNK:
                pieces.append(jnp.zeros((HG_CHUNK - hi_r, C_KEY), BF16))
            kfull = jnp.concatenate(pieces, axis=0) if len(pieces) > 1 else ks
            rows.append(lax.dot_general(qt, kfull, _NT, preferred_element_type=F32))
        a_rows[u] = rows
    attn = {u: jnp.where(tri, jnp.concatenate(a_rows[u], axis=0), 0.0).astype(BF16) for u in units}

    states = [st_ref[h] for h in range(heads)]
    for u in units:
        h = u[1]
        vb = v_ref[window(u)]
        o = lax.dot_general(qdec[u], states[h].astype(BF16), _NT, preferred_element_type=F32)
        o = o + dotf(attn[u], vb)
        states[h] = states[h] * etot[u] + lax.dot_general(vb, kdec[u], _TN, preferred_element_type=F32)
        if fused:
            tot = of_ref[window(u)] + o
            inv = lax.rsqrt(jnp.mean(tot * tot, -1, keepdims=True) + RMS_EPS)
            y = tot * inv * gn_ref[:, window(u)[2]] * gate_ref[window(u)].astype(F32)
            y_ref[window(u)[1:]] = y.astype(BF16)
        else:
            y_ref[window(u)] = o
    for h in range(heads):
        st_ref[h] = states[h]
    if fused:
        proj = _row_block_dot(y_ref[...], w_ref[...])
        out_ref[0] = _layer_norm(ALPHA * x_ref[0] + proj, lg_ref[...], lb_ref[...])


def _hgrn_specs(batch, seq, rev, heads, step_rows):
    nb = seq // step_rows
    blk = (lambda b, h, i: (b, nb - 1 - i, h)) if rev else (lambda b, h, i: (b, i, h))
    return pl.BlockSpec((1, step_rows, heads * C_KEY), blk), (batch, C_HEADS // heads, nb)


def _hgrn_fwd(q, lf, v, batch, seq):
    tile, grid = _hgrn_specs(batch, seq, False, HG_HP, HG_TB)
    view = lambda t: t.reshape(batch, seq, C_W)
    out = pl.pallas_call(
        functools.partial(_hgrn_kernel, rev=False, fused=False, heads=HG_HP, step_rows=HG_TB),
        grid=grid,
        in_specs=[tile, tile, tile],
        out_specs=tile,
        out_shape=jax.ShapeDtypeStruct((batch, seq, C_VW), F32),
        scratch_shapes=[pltpu.VMEM((HG_HP, C_VAL, C_KEY), F32)],
        compiler_params=_params(("arbitrary",) * 3),
        name="hgrn_fwd",
    )(view(q), view(lf), view(v))
    return out.reshape(batch * seq, C_VW)


def _hgrn_bwd_out(q, lf, v, o_f, gate, gn, x2d, w, g, b, batch, seq):
    tile, grid = _hgrn_specs(batch, seq, True, C_HEADS, HG_TB_OUT)
    view = lambda t: t.reshape(batch, seq, C_W)
    out = pl.pallas_call(
        functools.partial(_hgrn_kernel, rev=True, fused=True, heads=C_HEADS, step_rows=HG_TB_OUT),
        grid=grid,
        in_specs=[tile] * 5 + [_const_spec((1, C_VW)), tile, _const_spec((C_VW, D_MODEL), single=True),
                               _const_spec((1, D_MODEL)), _const_spec((1, D_MODEL))],
        out_specs=tile,
        out_shape=jax.ShapeDtypeStruct((batch, seq, D_MODEL), F32),
        scratch_shapes=[pltpu.VMEM((C_HEADS, C_VAL, C_KEY), F32), pltpu.VMEM((HG_TB_OUT, C_VW), BF16)],
        compiler_params=_params(("arbitrary",) * 3, VMEM_LIMIT),
        name="hgrn_bwd_out",
    )(view(q), view(lf), view(v), view(o_f), view(gate), gn, x2d.reshape(batch, seq, D_MODEL), w, g, b)
    return out.reshape(batch * seq, D_MODEL)


def _prep_weights(w_in_ab, w_out_ab, qn_ab, kn_ab, w_in_c, w_out_c, gn_c, ffn_w_up, ffn_w_down):
    scale = HEAD_DIM ** -0.5
    ab = []
    for j in range(w_in_ab.shape[0]):
        w = w_in_ab[j]
        o3 = 3 * A_W
        kb = w[:, o3 + B_QW:o3 + B_QW + B_KVW]
        vb = w[:, o3 + B_QW + B_KVW:]
        dup = lambda t: jnp.concatenate([t[:, :HEAD_DIM], t[:, :HEAD_DIM], t[:, HEAD_DIM:], t[:, HEAD_DIM:]], 1)
        w_ext = jnp.concatenate([w[:, :A_W] * scale, w[:, A_W:o3 + B_QW], dup(kb), dup(vb)], 1).astype(BF16)
        qg = (jnp.concatenate([qn_ab[j], qn_ab[j]]) * (scale * LOG2E)).reshape(1, LANES)
        kg = jnp.concatenate([kn_ab[j], kn_ab[j]]).reshape(1, LANES)
        ab.append((w_ext, qg, kg, w_out_ab[j].astype(BF16)))
    cc = [(w_in_c[j].astype(BF16), w_out_c[j].astype(BF16), gn_c[j].reshape(1, C_VW)) for j in range(w_in_c.shape[0])]
    return ab, cc, ffn_w_up.astype(BF16), ffn_w_down.astype(BF16)


def _trunk(x, prep, lb_fwd, lb_bwd, ln_mix_g, ln_mix_b, ln_ffn_g, ln_ffn_b, ffn_conv_w, ffn_conv_b):
    ab, cc, wu, wd = prep
    batch, seq, _ = x.shape
    x2d = x.reshape(batch * seq, D_MODEL)
    tabs = _rope_tables(seq)
    blk = jnp.arange(LANES) // HEAD_DIM
    ones2 = (blk[:, None] == blk[None, :]).astype(BF16)
    vec = lambda t: t.reshape(1, -1)
    for l in range(DEPTH):
        j = l // 2
        if l % 2 == 0:
            w_ext, qg, kg, w_out = ab[j]
            assert [d for _, d in A_PATTERNS] == [1, 4, DIL_CLASSES]
            assert all(window // (2 * d) == HALF_WINDOW for window, d in A_PATTERNS)
            qa, ka, va, qb, kd, vd, qa16, ka16, va16 = _ab_in(x2d, w_ext, tabs, qg, kg, ones2, batch, seq)
            o1, l1 = _band1(qa, ka, va, batch, seq)
            o16, l16 = _band16(qa16, ka16, va16, batch, seq)
            om, lm = _band4(qa16, ka16, va16, o16, l16, batch, seq)
            yb = _gqa(qb, kd, vd, batch, seq)
            x2d = _ab_out(o1, l1, om, lm, yb, x2d, w_out, vec(ln_mix_g[l]), vec(ln_mix_b[l]), seq)
        else:
            w_in, w_out, gn = cc[j]
            q, lf, lb, v, gate = _c_in(x2d, w_in, lb_fwd, lb_bwd, l)
            o_f = _hgrn_fwd(q, lf, v, batch, seq)
            x2d = _hgrn_bwd_out(q, lb, v, o_f, gate, gn, x2d, w_out, vec(ln_mix_g[l]), vec(ln_mix_b[l]), batch, seq)
        x2d = _ffn(x2d, wu[l], ffn_conv_w[l], vec(ffn_conv_b[l]), wd[l], vec(ln_ffn_g[l]), vec(ln_ffn_b[l]), seq)
    return x2d.reshape(batch, seq, D_MODEL)


def kernel(x_prompt, x_sample, w_in_ab, w_out_ab, qn_ab, kn_ab, w_in_c, w_out_c, lb_fwd, lb_bwd, gn_c, ln_mix_g, ln_mix_b, ln_ffn_g, ln_ffn_b, ffn_w_up, ffn_conv_w, ffn_conv_b, ffn_w_down):
    prep = _prep_weights(w_in_ab, w_out_ab, qn_ab, kn_ab, w_in_c, w_out_c, gn_c, ffn_w_up, ffn_w_down)
    rest = (lb_fwd, lb_bwd, ln_mix_g, ln_mix_b, ln_ffn_g, ln_ffn_b, ffn_conv_w, ffn_conv_b)
    return (_trunk(x_prompt, prep, *rest), _trunk(x_sample, prep, *rest))
```

```python
import functools

import jax
import jax.numpy as jnp
from jax import lax
from jax.experimental import pallas as pl
from jax.experimental.pallas import tpu as pltpu

F32 = jnp.float32
BF16 = jnp.bfloat16

D_MODEL = 1024
DEPTH = 2
HEAD_DIM = 64
A_HEADS = 8
A_PATTERNS = ((128, 1), (512, 4), (2048, 16))
B_HEADS = 8
B_KV_HEADS = 2
GRID_W = 64
ROPE_THETA = 500000.0
ROPE_DIM = HEAD_DIM // 4
AXIAL_THETA = 10000.0
C_HEADS = 8
C_KEY = 128
C_VAL = 128
D_FF = 2816
ALPHA = (2 * DEPTH) ** 0.25
LN_EPS = 1e-5
RMS_EPS = 1e-6
A_W = A_HEADS * HEAD_DIM
B_QW = B_HEADS * HEAD_DIM
B_KVW = B_KV_HEADS * HEAD_DIM
C_W = C_HEADS * C_KEY
C_VW = C_HEADS * C_VAL

LANES = 128
HALF_WINDOW = 64
TM = 512
PROJ_ROWS = 128
FFN_ROWS = 256
BAND_L = 128
BAND_STEP = 512
BAND4_STEP = 128
DIL_CLASSES = 16
BAND4_ROWS = 32
BAND4_HALO = 16
GQA_TQ = 512
GQA_TK = 512
GQA_CHAIN = 512
GQA_PV_ROWS = 80
GQA_PV_KEYS = 256
FF_CHUNKS = (768, 768, 768, 512)
FFN_SUB = 1
HG_CHUNK = 64
HG_SUB = 16
HG_TB = 512
HG_HP = 4
HG_TB_OUT = 256
NEG = -1e30
LOG2E = 1.4426950408889634
VMEM_LIMIT = 56 * 1024 * 1024

_NT = (((1,), (1,)), ((), ()))
_TN = (((0,), (0,)), ((), ()))


def _params(sem, vmem=None):
    return pltpu.CompilerParams(dimension_semantics=sem, vmem_limit_bytes=vmem)


def _const_spec(shape, single=False):
    nd = len(shape)
    if single:
        return pl.BlockSpec(shape, lambda *_: (0,) * nd, pipeline_mode=pl.Buffered(1))
    return pl.BlockSpec(shape, lambda *_: (0,) * nd)


def _layer_norm(z, g, b):
    mu = jnp.mean(z, -1, keepdims=True)
    d = z - mu
    var = jnp.mean(d * d, -1, keepdims=True)
    return d * lax.rsqrt(var + LN_EPS) * g + b


def _sigmoid(z):
    return 1.0 / (1.0 + jnp.exp(-z))


def _row_block_dot(x, w, rows=PROJ_ROWS):
    m = x.shape[0]
    starts = list(range(0, m - m % rows, rows)) or [0]
    ends = starts[1:] + [m]
    parts = [jnp.dot(x[a:b], w, preferred_element_type=F32) for a, b in zip(starts, ends)]
    return parts[0] if len(parts) == 1 else jnp.concatenate(parts, axis=0)


def _pipelined(sections):
    pending = sections[0][0]()
    for i, (_, epilogue) in enumerate(sections):
        upcoming = sections[i + 1][0]() if i + 1 < len(sections) else None
        epilogue(pending)
        pending = upcoming


def _rope_tables(seq):
    pos = jnp.arange(seq, dtype=F32)
    d = jnp.arange(LANES) % HEAD_DIM

    def angles(p, dim, theta):
        freqs = theta ** (-(jnp.arange(0, dim, 2, dtype=F32) / dim))
        return p[:, None] * freqs[None, :]

    h = ROPE_DIM // 2
    ang = angles(pos, ROPE_DIM, ROPE_THETA)
    a = ang[:, d % h]
    lo, hi = d < h, (d >= h) & (d < ROPE_DIM)
    pc = jnp.where(lo | hi, jnp.cos(a), 1.0)
    psa = jnp.where(lo, -jnp.sin(a), 0.0)
    psb = jnp.where(hi, jnp.sin(a), 0.0)

    q = HEAD_DIM // 4
    row = jnp.floor(pos / GRID_W)
    col = pos - row * GRID_W
    ar = angles(row, HEAD_DIM // 2, AXIAL_THETA)[:, d % q]
    ac_ = angles(col, HEAD_DIM // 2, AXIAL_THETA)[:, d % q]
    a2 = jnp.where(d < HEAD_DIM // 2, ar, ac_)
    first = (d % (HEAD_DIM // 2)) < q
    ac = jnp.cos(a2)
    asa = jnp.where(first, -jnp.sin(a2), 0.0)
    asb = jnp.where(first, 0.0, jnp.sin(a2))
    return [t.astype(F32) for t in (pc, psa, psb, ac, asa, asb)]


def _rope(seg, c, sa, sb, shift):
    return seg * c + pltpu.roll(seg, LANES - shift, 1) * sa + pltpu.roll(seg, shift, 1) * sb


def _ab_in_kernel(x_ref, w_ref, pc_ref, psa_ref, psb_ref, ac_ref, asa_ref, asb_ref, qg_ref, kg_ref,
                  ones_ref, qa_ref, ka_ref, va_ref, qb_ref, kd_ref, vd_ref, qa16_ref, ka16_ref, va16_ref,
                  qs_ref, ks_ref, vs_ref):
    xb = x_ref[...].astype(BF16)

    def proj(a, n):
        return jnp.dot(xb, w_ref[:, a:a + n], preferred_element_type=F32)

    pc, psa, psb = pc_ref[...], psa_ref[...], psb_ref[...]
    ac, asa, asb = ac_ref[...], asa_ref[...], asb_ref[...]

    def dilated_a(h, rope, out_ref, out16_ref, slab_ref):
        for g in range(A_W // LANES):
            sl = slice(g * LANES, (g + 1) * LANES)
            y = _rope(h[:, sl], pc, psa, psb, ROPE_DIM // 2) if rope else h[:, sl]
            out_ref[:, sl] = y.astype(BF16)
            slab_ref[g] = y
        for c in range(DIL_CLASSES):
            for g in range(A_W // LANES):
                rows = slab_ref[g, pl.ds(c, TM // DIL_CLASSES, stride=DIL_CLASSES), :]
                out16_ref[0, c, :, g * LANES:(g + 1) * LANES] = rows.astype(BF16)

    def norm_rope(h, gain, out_ref):
        for g in range(h.shape[1] // LANES):
            sl = slice(g * LANES, (g + 1) * LANES)
            seg = h[:, sl]
            sq = seg * seg
            hi = sq.astype(BF16)
            lo = (sq - hi.astype(F32)).astype(BF16)
            ss = (jnp.dot(hi, ones_ref[...], preferred_element_type=F32)
                  + jnp.dot(lo, ones_ref[...], preferred_element_type=F32))
            y = seg * lax.rsqrt(ss * (1.0 / HEAD_DIM) + RMS_EPS) * gain
            out_ref[:, sl] = _rope(y, ac, asa, asb, HEAD_DIM // 4).astype(BF16)

    def cast_to(out_ref):
        def fn(h):
            out_ref[...] = h.astype(BF16)
        return fn

    o_qb = 3 * A_W
    o_kd = o_qb + B_QW
    o_vd = o_kd + 2 * B_KVW
    _pipelined([
        (functools.partial(proj, 0, A_W),
         functools.partial(dilated_a, rope=True, out_ref=qa_ref, out16_ref=qa16_ref, slab_ref=qs_ref)),
        (functools.partial(proj, A_W, A_W),
         functools.partial(dilated_a, rope=True, out_ref=ka_ref, out16_ref=ka16_ref, slab_ref=ks_ref)),
        (functools.partial(proj, 2 * A_W, A_W),
         functools.partial(dilated_a, rope=False, out_ref=va_ref, out16_ref=va16_ref, slab_ref=vs_ref)),
        (functools.partial(proj, o_qb, B_QW), functools.partial(norm_rope, gain=qg_ref[...], out_ref=qb_ref)),
        (functools.partial(proj, o_kd, 2 * B_KVW), functools.partial(norm_rope, gain=kg_ref[...], out_ref=kd_ref)),
        (functools.partial(proj, o_vd, 2 * B_KVW), cast_to(vd_ref)),
    ])


def _ab_in(x2d, w_ext, tabs, qg, kg, ones2, batch, seq):
    t = x2d.shape[0]
    nt = seq // TM
    tab = pl.BlockSpec((TM, LANES), lambda i: (i % nt, 0))
    row = lambda n: pl.BlockSpec((TM, n), lambda i: (i, 0))
    cls = pl.BlockSpec((1, DIL_CLASSES, TM // DIL_CLASSES, A_W), lambda i: (i // nt, 0, i % nt, 0))
    cls_shape = jax.ShapeDtypeStruct((batch, DIL_CLASSES, seq // DIL_CLASSES, A_W), BF16)
    wn = w_ext.shape[1]
    return pl.pallas_call(
        _ab_in_kernel,
        grid=(t // TM,),
        in_specs=[row(D_MODEL), _const_spec((D_MODEL, wn))] + [tab] * 6
        + [_const_spec((1, LANES)), _const_spec((1, LANES)), _const_spec((LANES, LANES))],
        out_specs=[row(A_W)] * 3 + [row(B_QW), row(2 * B_KVW), row(2 * B_KVW)] + [cls] * 3,
        out_shape=[jax.ShapeDtypeStruct((t, A_W), BF16)] * 3
        + [jax.ShapeDtypeStruct((t, B_QW), BF16)]
        + [jax.ShapeDtypeStruct((t, 2 * B_KVW), BF16)] * 2 + [cls_shape] * 3,
        scratch_shapes=[pltpu.VMEM((A_W // LANES, TM, LANES), F32)] * 3,
        compiler_params=_params(("arbitrary",), VMEM_LIMIT),
        name="ab_in",
    )(x2d, w_ext, *tabs, qg, kg, ones2)


def _band_groups(qs, ks, vs, bias, left):
    chains = [(g, sel) for g in range(len(qs)) for sel in (left, jnp.logical_not(left))]
    scores = []
    for g, sel in chains:
        qm = jnp.where(sel, qs[g], jnp.zeros_like(qs[g]))
        scores.append(lax.dot_general(qm, ks[g], _NT, preferred_element_type=F32) + bias)
    probs = []
    for s in scores:
        mx = jnp.max(s, axis=1, keepdims=True)
        pe = jnp.exp(s - mx)
        probs.append((mx, jnp.sum(pe, axis=1, keepdims=True), pe.astype(BF16)))
    outs = []
    for (g, _), (mx, l, pe) in zip(chains, probs):
        o = jnp.dot(pe, vs[g], preferred_element_type=F32)
        outs.append((o * (1.0 / l), mx + jnp.log(l)))
    return [(jnp.where(left, outs[2 * g][0], outs[2 * g + 1][0]), jnp.where(left, outs[2 * g][1], outs[2 * g + 1][1]))
            for g in range(len(qs))]


def _band_kernel(q_ref, kp_ref, kc_ref, kn_ref, vp_ref, vc_ref, vn_ref, o_ref, lse_ref, *, m, step_rows):
    nk = BAND_L + 2 * HALF_WINDOW
    base = pl.program_id(len(q_ref.shape) - 2) * step_rows
    r = lax.broadcasted_iota(jnp.int32, (BAND_L, nk), 0)
    j = lax.broadcasted_iota(jnp.int32, (BAND_L, nk), 1)
    band = jnp.where(jnp.abs(j - HALF_WINDOW - r) <= HALF_WINDOW, 0.0, NEG)
    left = lax.broadcasted_iota(jnp.int32, (BAND_L, LANES), 1) < HEAD_DIM
    pre = (0,) * (len(q_ref.shape) - 2)
    lanes = [slice(p * LANES, (p + 1) * LANES) for p in range(A_W // LANES)]
    at = lambda rows, sl: pre + (rows, sl)
    whole = slice(None)
    ks = [jnp.concatenate([kp_ref[at(whole, sl)], kc_ref[at(whole, sl)], kn_ref[at(whole, sl)]], axis=0) for sl in lanes]
    vs = [jnp.concatenate([vp_ref[at(whole, sl)], vc_ref[at(whole, sl)], vn_ref[at(whole, sl)]], axis=0) for sl in lanes]
    for blk in range(step_rows // BAND_L):
        r0 = blk * BAND_L
        rows = slice(r0, r0 + BAND_L)
        kpos = base + r0 - HALF_WINDOW + j
        bias = jnp.where(kpos >= 0, band, NEG)
        bias = jnp.where(kpos < m, bias, NEG)
        res = _band_groups([q_ref[at(rows, sl)] for sl in lanes], [kk[r0:r0 + nk] for kk in ks],
                           [vv[r0:r0 + nk] for vv in vs], bias, left)
        for sl, (o, lse) in zip(lanes, res):
            o_ref[at(rows, sl)] = o.astype(BF16)
            lse_ref[at(rows, sl)] = lse


def _band_specs(lead, rows, step_rows):
    nh = rows // HALF_WINDOW
    ratio = step_rows // HALF_WINDOW
    ones = (1,) * lead
    cur = pl.BlockSpec(ones + (step_rows, A_W), lambda *g: g[:lead] + (g[-1], 0))
    prev = pl.BlockSpec(ones + (HALF_WINDOW, A_W), lambda *g: g[:lead] + (jnp.maximum(g[-1] * ratio - 1, 0), 0))
    nxt = pl.BlockSpec(ones + (HALF_WINDOW, A_W), lambda *g: g[:lead] + (jnp.minimum((g[-1] + 1) * ratio, nh - 1), 0))
    return cur, prev, nxt


def _band1(q, k, v, batch, seq):
    step_rows = min(BAND_STEP, seq)
    cur, prev, nxt = _band_specs(1, seq, step_rows)
    view = lambda t: t.reshape(batch, seq, A_W)
    o, lse = pl.pallas_call(
        functools.partial(_band_kernel, m=seq, step_rows=step_rows),
        grid=(batch, seq // step_rows),
        in_specs=[cur, prev, cur, nxt, prev, cur, nxt],
        out_specs=[cur, cur],
        out_shape=[jax.ShapeDtypeStruct((batch, seq, A_W), BF16), jax.ShapeDtypeStruct((batch, seq, A_W), F32)],
        compiler_params=_params(("arbitrary",) * 2),
        name="band_d1",
    )(view(q), view(k), view(k), view(k), view(v), view(v), view(v))
    return o.reshape(batch * seq, A_W), lse.reshape(batch * seq, A_W)


def _band16(q16, k16, v16, batch, seq):
    m = seq // DIL_CLASSES
    step_rows = min(BAND_STEP, m)
    cur, prev, nxt = _band_specs(2, m, step_rows)
    shape = (batch, DIL_CLASSES, m, A_W)
    return pl.pallas_call(
        functools.partial(_band_kernel, m=m, step_rows=step_rows),
        grid=(batch, DIL_CLASSES, m // step_rows),
        in_specs=[cur, prev, cur, nxt, prev, cur, nxt],
        out_specs=[cur, cur],
        out_shape=[jax.ShapeDtypeStruct(shape, BF16), jax.ShapeDtypeStruct(shape, F32)],
        compiler_params=_params(("arbitrary",) * 3),
        name="band_d16",
    )(q16, k16, k16, k16, v16, v16, v16)


def _band4_kernel(q_ref, kp_ref, kc_ref, kn_ref, vp_ref, vc_ref, vn_ref, o16_ref, l16_ref, o_ref, lse_ref, *,
                  m16, step_rows):
    qr = BAND4_ROWS
    kr = qr + 2 * BAND4_HALO
    i = pl.program_id(2)
    shift_q, shift_k = qr.bit_length() - 1, kr.bit_length() - 1
    row = lax.broadcasted_iota(jnp.int32, (4 * qr, 4 * kr), 0)
    col = lax.broadcasted_iota(jnp.int32, (4 * qr, 4 * kr), 1)
    uq, rq = row >> shift_q, row & (qr - 1)
    uk, jk = col >> shift_k, col & (kr - 1)
    band = jnp.where(jnp.abs(4 * (jk - BAND4_HALO - rq) + (uk - uq)) <= HALF_WINDOW, 0.0, NEG)
    left = lax.broadcasted_iota(jnp.int32, (4 * qr, LANES), 1) < HEAD_DIM
    slabs = [slice(p * LANES, (p + 1) * LANES) for p in range(A_W // LANES)]
    window = lambda refs, u, sl: jnp.concatenate([r[0, u, 0, :, sl] for r in refs], axis=0)
    kwin = [[window([kp_ref, kc_ref, kn_ref], u, sl) for u in range(4)] for sl in slabs]
    vwin = [[window([vp_ref, vc_ref, vn_ref], u, sl) for u in range(4)] for sl in slabs]
    for blk in range(step_rows // qr):
        r0 = blk * qr
        rows = slice(r0, r0 + qr)
        kpos = i * step_rows + r0 - BAND4_HALO + jk
        bias = jnp.where(kpos >= 0, band, NEG)
        bias = jnp.where(kpos < m16, bias, NEG)
        classes = lambda ref, sl: jnp.concatenate([ref[0, u, 0, rows, sl] for u in range(4)], axis=0)
        branch4 = _band_groups([classes(q_ref, sl) for sl in slabs],
                               [jnp.concatenate([w[r0:r0 + kr] for w in kw], axis=0) for kw in kwin],
                               [jnp.concatenate([w[r0:r0 + kr] for w in vw], axis=0) for vw in vwin], bias, left)
        for sl, (o4, l4) in zip(slabs, branch4):
            o16 = classes(o16_ref, sl).astype(F32)
            l16 = classes(l16_ref, sl)
            mx = jnp.maximum(l4, l16)
            w4, w16 = jnp.exp(l4 - mx), jnp.exp(l16 - mx)
            tot = w4 + w16
            om = (w4 * o4 + w16 * o16) * (1.0 / tot)
            lm = mx + jnp.log(tot)
            for u in range(4):
                o_ref[0, u, 0, rows, sl] = om[u * qr:(u + 1) * qr].astype(BF16)
                lse_ref[0, u, 0, rows, sl] = lm[u * qr:(u + 1) * qr]


def _band4(q16, k16, v16, o16, l16, batch, seq):
    m16 = seq // DIL_CLASSES
    step_rows = min(BAND4_STEP, m16)
    ratio = step_rows // BAND4_HALO
    nh = m16 // BAND4_HALO
    view = lambda t: t.reshape(batch, 4, 4, m16, A_W)
    cur = pl.BlockSpec((1, 4, 1, step_rows, A_W), lambda b, c, i: (b, 0, c, i, 0))
    prev = pl.BlockSpec((1, 4, 1, BAND4_HALO, A_W), lambda b, c, i: (b, 0, c, jnp.maximum(i * ratio - 1, 0), 0))
    nxt = pl.BlockSpec((1, 4, 1, BAND4_HALO, A_W), lambda b, c, i: (b, 0, c, jnp.minimum((i + 1) * ratio, nh - 1), 0))
    shape = (batch, 4, 4, m16, A_W)
    o, lse = pl.pallas_call(
        functools.partial(_band4_kernel, m16=m16, step_rows=step_rows),
        grid=(batch, 4, m16 // step_rows),
        in_specs=[cur, prev, cur, nxt, prev, cur, nxt, cur, cur],
        out_specs=[cur, cur],
        out_shape=[jax.ShapeDtypeStruct(shape, BF16), jax.ShapeDtypeStruct(shape, F32)],
        compiler_params=_params(("arbitrary",) * 3),
        name="band_d4",
    )(view(q16), view(k16), view(k16), view(k16), view(v16), view(v16), view(v16), view(o16), view(l16))
    return o.reshape(batch, DIL_CLASSES, m16, A_W), lse.reshape(batch, DIL_CLASSES, m16, A_W)


def _gqa_kernel(q_ref, k_ref, v_ref, o_ref, qs_ref, m_ref, acc_ref, s_ref, *, seq):
    left = lax.broadcasted_iota(jnp.int32, (GQA_TQ, LANES), 1) < HEAD_DIM
    for pr in range(2):
        qp = q_ref[0, :, pr * LANES:(pr + 1) * LANES]
        zero = jnp.zeros_like(qp)
        qs_ref[(2 * pr) * GQA_TQ:(2 * pr + 1) * GQA_TQ, :] = jnp.where(left, qp, zero)
        qs_ref[(2 * pr + 1) * GQA_TQ:(2 * pr + 2) * GQA_TQ, :] = jnp.where(left, zero, qp)
    m_ref[...] = jnp.full(m_ref.shape, NEG, F32)
    acc_ref[...] = jnp.zeros(acc_ref.shape, F32)
    n_tiles = seq // GQA_TK
    ones_lane = lax.broadcasted_iota(jnp.int32, (GQA_TK, LANES), 1) == HEAD_DIM

    chains = [slice(c * GQA_CHAIN, (c + 1) * GQA_CHAIN) for c in range(4 * GQA_TQ // GQA_CHAIN)]

    def scores(tile, buf, cols):
        off = pl.multiple_of(tile * GQA_TK, GQA_TK)
        k = k_ref[0, pl.ds(off, GQA_TK), :]
        s_ref[buf, :, cols] = lax.dot_general(k, qs_ref[cols, :], _NT, preferred_element_type=F32)

    def softmax_pv(tile, buf, cols):
        off = pl.multiple_of(tile * GQA_TK, GQA_TK)
        v = jnp.where(ones_lane, jnp.ones((), BF16), v_ref[0, pl.ds(off, GQA_TK), :])[:, :GQA_PV_ROWS]
        m_prev = m_ref[:, cols]
        m_new = jnp.maximum(m_prev, jnp.max(s_ref[buf, :, cols], axis=0, keepdims=True))
        alpha = jnp.exp2(m_prev - m_new)
        pv = None
        for r in range(0, GQA_TK, GQA_PV_KEYS):
            p = jnp.exp2((s_ref[buf, r:r + GQA_PV_KEYS, cols] - m_new).astype(BF16))
            part = lax.dot_general(v[r:r + GQA_PV_KEYS], p, _TN, preferred_element_type=F32)
            pv = part if pv is None else pv + part
        acc_ref[:, cols] = alpha * acc_ref[:, cols] + pv
        m_ref[:, cols] = m_new

    def step(tile, buf, prefetch):
        for cols in chains:
            if prefetch:
                scores(tile + 1, 1 - buf, cols)
            softmax_pv(tile, buf, cols)

    def pair(j, carry):
        step(2 * j, 0, True)
        step(2 * j + 1, 1, True)
        return carry

    for cols in chains:
        scores(0, 0, cols)
    lax.fori_loop(0, n_tiles // 2 - 1, pair, 0)
    step(n_tiles - 2, 0, True)
    step(n_tiles - 1, 1, False)
    acc = acc_ref[...]
    a = (acc[0:HEAD_DIM] * (1.0 / acc[HEAD_DIM:HEAD_DIM + 1, :])).T
    for pr in range(2):
        lo = a[(2 * pr) * GQA_TQ:(2 * pr + 1) * GQA_TQ]
        hi = a[(2 * pr + 1) * GQA_TQ:(2 * pr + 2) * GQA_TQ]
        o_ref[0, :, pr * LANES:(pr + 1) * LANES] = jnp.concatenate([lo, hi], axis=1).astype(BF16)


def _gqa(qb, kd, vd, batch, seq):
    gw = B_QW // B_KV_HEADS
    cols = 4 * GQA_TQ
    out = pl.pallas_call(
        functools.partial(_gqa_kernel, seq=seq),
        grid=(batch, B_KV_HEADS, seq // GQA_TQ),
        in_specs=[pl.BlockSpec((1, GQA_TQ, gw), lambda b, g, qi: (b, qi, g)),
                  pl.BlockSpec((1, seq, LANES), lambda b, g, qi: (b, 0, g)),
                  pl.BlockSpec((1, seq, LANES), lambda b, g, qi: (b, 0, g))],
        out_specs=pl.BlockSpec((1, GQA_TQ, gw), lambda b, g, qi: (b, qi, g)),
        out_shape=jax.ShapeDtypeStruct((batch, seq, B_QW), BF16),
        scratch_shapes=[pltpu.VMEM((cols, LANES), BF16), pltpu.VMEM((1, cols), F32),
                        pltpu.VMEM((GQA_PV_ROWS, cols), F32), pltpu.VMEM((2, GQA_TK, cols), F32)],
        compiler_params=_params(("arbitrary",) * 3, VMEM_LIMIT),
        name="gqa",
    )(qb.reshape(batch, seq, B_QW), kd.reshape(batch, seq, 2 * B_KVW), vd.reshape(batch, seq, 2 * B_KVW))
    return out.reshape(batch * seq, B_QW)


def _ab_out_kernel(o1_ref, l1_ref, om_ref, lm_ref, yb_ref, x_ref, w_ref, g_ref, b_ref, out_ref, os_ref, ls_ref):
    for c in range(DIL_CLASSES):
        for g in range(A_W // LANES):
            sl = slice(g * LANES, (g + 1) * LANES)
            rows = pl.ds(c, TM // DIL_CLASSES, stride=DIL_CLASSES)
            os_ref[g, rows, :] = om_ref[0, c, :, sl].astype(F32)
            ls_ref[g, rows, :] = lm_ref[0, c, :, sl]
    ya = []
    for g in range(A_W // LANES):
        sl = slice(g * LANES, (g + 1) * LANES)
        l1, lm = l1_ref[:, sl], ls_ref[g]
        mx = jnp.maximum(l1, lm)
        e1, em = jnp.exp(l1 - mx), jnp.exp(lm - mx)
        ya.append(((e1 * o1_ref[:, sl].astype(F32) + em * os_ref[g]) * (1.0 / (e1 + em))).astype(BF16))
    y = _row_block_dot(jnp.concatenate(ya + [yb_ref[...]], axis=1), w_ref[...])
    out_ref[...] = _layer_norm(ALPHA * x_ref[...] + y, g_ref[...], b_ref[...])


def _ab_out(o1, l1, om, lm, yb, x2d, w, g, b, seq):
    t = x2d.shape[0]
    nt = seq // TM
    row = lambda n: pl.BlockSpec((TM, n), lambda i: (i, 0))
    cls = pl.BlockSpec((1, DIL_CLASSES, TM // DIL_CLASSES, A_W), lambda i: (i // nt, 0, i % nt, 0))
    return pl.pallas_call(
        _ab_out_kernel,
        grid=(t // TM,),
        in_specs=[row(A_W), row(A_W), cls, cls, row(B_QW), row(D_MODEL), _const_spec((A_W + B_QW, D_MODEL)),
                  _const_spec((1, D_MODEL)), _const_spec((1, D_MODEL))],
        out_specs=row(D_MODEL),
        out_shape=jax.ShapeDtypeStruct((t, D_MODEL), F32),
        scratch_shapes=[pltpu.VMEM((A_W // LANES, TM, LANES), F32)] * 2,
        compiler_params=_params(("arbitrary",), VMEM_LIMIT),
        name="ab_out",
    )(o1, l1, om, lm, yb, x2d, w, g, b)


def _gelu(x):
    return 0.5 * x * (1.0 + jnp.tanh(0.7978845608028654 * (x + 0.044715 * (x * x * x))))


def _ffn_kernel(x_ref, xp_ref, xn_ref, wu_ref, cw_ref, cb_ref, wd_ref, g_ref, b_ref, out_ref, *, nt):
    i = pl.program_id(0)
    keep_prev = (i % nt != 0).astype(F32)
    keep_next = (i % nt != nt - 1).astype(F32)
    rows = lax.broadcasted_iota(jnp.int32, (TM, 1), 0)
    offsets = [sum(FF_CHUNKS[:c]) for c in range(len(FF_CHUNKS))]
    operands, acc = {}, {}

    def tile_operands(j):
        if j not in operands:
            r0 = j * TM
            xb = x_ref[r0:r0 + TM, :].astype(BF16)
            before = xp_ref[...] if j == 0 else x_ref[r0 - 8:r0, :]
            after = xn_ref[...] if j == FFN_SUB - 1 else x_ref[r0 + TM:r0 + TM + 8, :]
            halo = jnp.concatenate([before, after], axis=0).astype(BF16)
            operands[j] = (xb, jnp.concatenate([xb, halo], axis=0))
        return operands[j]

    def up(j, c):
        a, n = offsets[c], FF_CHUNKS[c]
        xb, xe = tile_operands(j)
        u = _row_block_dot(xb, wu_ref[:, a:a + n], FFN_ROWS)
        ge = _row_block_dot(xe, wu_ref[:, D_FF + a:D_FF + a + n], FFN_ROWS)
        return u, ge

    def finish(j, c, pending):
        u, ge = pending
        a, n = offsets[c], FF_CHUNKS[c]
        gm = ge[0:TM]
        g_before = ge[TM + 7:TM + 8]
        g_after = ge[TM + 8:TM + 9]
        if j == 0:
            g_before = g_before * keep_prev
        if j == FFN_SUB - 1:
            g_after = g_after * keep_next
        gp = jnp.where(rows == 0, g_before, pltpu.roll(gm, 1, 0))
        gn = jnp.where(rows == TM - 1, g_after, pltpu.roll(gm, TM - 1, 0))
        cw = cw_ref[:, a:a + n]
        gc = gp * cw[0:1] + gm * cw[1:2] + gn * cw[2:3] + cb_ref[:, a:a + n]
        act = (_gelu(gc) * u).astype(BF16)
        part = _row_block_dot(act, wd_ref[a:a + n, :], FFN_ROWS)
        acc[j] = part if c == 0 else acc[j] + part
        if c == len(FF_CHUNKS) - 1:
            sl = slice(j * TM, (j + 1) * TM)
            out_ref[sl, :] = _layer_norm(ALPHA * x_ref[sl, :] + acc[j], g_ref[...], b_ref[...])

    _pipelined([(functools.partial(up, j, c), functools.partial(finish, j, c))
                for j in range(FFN_SUB) for c in range(len(FF_CHUNKS))])


def _ffn(x2d, wu, cw, cb, wd, g, b, seq):
    t = x2d.shape[0]
    step = FFN_SUB * TM
    nt = seq // step
    r8 = step // 8
    return pl.pallas_call(
        functools.partial(_ffn_kernel, nt=nt),
        grid=(t // step,),
        in_specs=[pl.BlockSpec((step, D_MODEL), lambda i: (i, 0)),
                  pl.BlockSpec((8, D_MODEL), lambda i: (jnp.maximum(i * r8 - 1, 0), 0)),
                  pl.BlockSpec((8, D_MODEL), lambda i: (jnp.minimum((i + 1) * r8, t // 8 - 1), 0)),
                  _const_spec((D_MODEL, 2 * D_FF), single=True),
                  _const_spec((3, D_FF)), _const_spec((1, D_FF)),
                  _const_spec((D_FF, D_MODEL), single=True),
                  _const_spec((1, D_MODEL)), _const_spec((1, D_MODEL))],
        out_specs=pl.BlockSpec((step, D_MODEL), lambda i: (i, 0)),
        out_shape=jax.ShapeDtypeStruct((t, D_MODEL), F32),
        compiler_params=_params(("arbitrary",), VMEM_LIMIT),
        name="ffn",
    )(x2d, x2d, x2d, wu, cw, cb, wd, g, b)


def _lower_bound(tbl, layer):
    rows = [tbl[r:r + 1] for r in range(DEPTH)]
    mx = functools.reduce(jnp.maximum, rows)
    es = [jnp.exp(r - mx) for r in rows]
    inv = 1.0 / functools.reduce(lambda a, b: a + b, es)
    ps = [e * inv for e in es]
    return functools.reduce(lambda a, b: a + b, ps[:layer + 1]) - ps[0]


def _c_in_kernel(x_ref, w_ref, lbf_ref, lbb_ref, q_ref, lf_ref, lb_ref, v_ref, g_ref, *, layer):
    half = C_W // 2
    blocks = [slice(r * PROJ_ROWS, (r + 1) * PROJ_ROWS) for r in range(TM // PROJ_ROWS)]
    xbs = [x_ref[rows, :].astype(BF16) for rows in blocks]

    def proj(r, a, n):
        return jnp.dot(xbs[r], w_ref[:, a:a + n], preferred_element_type=F32)

    def silu(z):
        return z * _sigmoid(z)

    lbf = _lower_bound(lbf_ref[...], layer)
    lbb = _lower_bound(lbb_ref[...], layer)

    def store_q(z, rows, sl):
        q_ref[rows, sl] = z.astype(BF16)

    def store_logf(tbl, out_ref):
        def fn(z, rows, sl):
            lb = tbl[:, sl]
            out_ref[rows, sl] = jnp.log(lb + (1.0 - lb) * _sigmoid(z))
        return fn

    def store_silu(out_ref):
        def fn(z, rows, sl):
            out_ref[rows, sl] = silu(z).astype(BF16)
        return fn

    sections = []
    for r, rows in enumerate(blocks):
        for c in range(2):
            a = c * half
            sl = slice(a, a + half)
            for off, fn in ((0, store_q), (C_W, store_logf(lbf, lf_ref)), (2 * C_W, store_logf(lbb, lb_ref)),
                            (3 * C_W, store_silu(v_ref)), (3 * C_W + C_VW, store_silu(g_ref))):
                sections.append((functools.partial(proj, r, off + a, half), functools.partial(fn, rows=rows, sl=sl)))
    _pipelined(sections)


def _c_in(x2d, w, lbf, lbb, layer):
    t = x2d.shape[0]
    row = pl.BlockSpec((TM, C_W), lambda i: (i, 0))
    return pl.pallas_call(
        functools.partial(_c_in_kernel, layer=layer),
        grid=(t // TM,),
        in_specs=[row, _const_spec(w.shape), _const_spec((DEPTH, C_W)), _const_spec((DEPTH, C_W))],
        out_specs=[row] * 5,
        out_shape=[jax.ShapeDtypeStruct((t, C_W), d) for d in (BF16, F32, F32, BF16, BF16)],
        compiler_params=_params(("arbitrary",), VMEM_LIMIT),
        name="c_in",
    )(x2d, w, lbf, lbb)


def _hgrn_kernel(*refs, rev, fused, heads, step_rows):
    if fused:
        (q_ref, lf_ref, v_ref, of_ref, gate_ref, gn_ref, x_ref, w_ref, lg_ref, lb_ref,
         out_ref, st_ref, y_ref) = refs
    else:
        q_ref, lf_ref, v_ref, y_ref, st_ref = refs

    @pl.when(pl.program_id(2) == 0)
    def _():
        st_ref[...] = jnp.zeros(st_ref.shape, F32)

    ri = lax.broadcasted_iota(jnp.int32, (HG_CHUNK, HG_CHUNK), 0)
    ci = lax.broadcasted_iota(jnp.int32, (HG_CHUNK, HG_CHUNK), 1)
    tri = (ci >= ri) if rev else (ci <= ri)
    trib = jnp.where(tri, 1.0, 0.0).astype(BF16)
    dotf = lambda a, b: jnp.dot(a, b, preferred_element_type=F32)
    n_chunks = step_rows // HG_CHUNK
    order = range(n_chunks - 1, -1, -1) if rev else range(n_chunks)
    units = [(c, h) for c in order for h in range(heads)]
    window = lambda u: (0, slice(u[0] * HG_CHUNK, (u[0] + 1) * HG_CHUNK), slice(u[1] * C_KEY, (u[1] + 1) * C_KEY))

    b_all = {}
    for u in units:
        lf = lf_ref[window(u)]
        hi = lf.astype(BF16)
        lo = (lf - hi.astype(F32)).astype(BF16)
        b_all[u] = dotf(trib, hi) + dotf(trib, lo)

    qdec, kdec, etot, a_rows = {}, {}, {}, {}
    for u in units:
        lf, b = lf_ref[window(u)], b_all[u]
        qf = q_ref[window(u)].astype(F32)
        bex = b - lf
        kk = 1.0 - jnp.exp(lf)
        btot = b[0:1] if rev else b[HG_CHUNK - 1:HG_CHUNK]
        qdec[u] = (qf * jnp.exp(b)).astype(BF16)
        kdec[u] = (kk * jnp.exp(btot - b)).astype(BF16)
        etot[u] = jnp.exp(btot)
        rows = []
        for blk in range(HG_CHUNK // HG_SUB):
            r0 = blk * HG_SUB
            ref = bex[r0 + HG_SUB - 1:r0 + HG_SUB] if rev else bex[r0:r0 + 1]
            qt = (qf[r0:r0 + HG_SUB] * jnp.exp(b[r0:r0 + HG_SUB] - ref)).astype(BF16)
            lo_r, hi_r = (r0, HG_CHUNK) if rev else (0, r0 + HG_SUB)
            ks = (kk[lo_r:hi_r] * jnp.exp(ref - b[lo_r:hi_r])).astype(BF16)
            pieces = []
            if lo_r > 0:
                pieces.append(jnp.zeros((lo_r, C_KEY), BF16))
            pieces.append(ks)
            if hi_r < HG_CHUNK:
                pieces.append(jnp.zeros((HG_CHUNK - hi_r, C_KEY), BF16))
            kfull = jnp.concatenate(pieces, axis=0) if len(pieces) > 1 else ks
            rows.append(lax.dot_general(qt, kfull, _NT, preferred_element_type=F32))
        a_rows[u] = rows
    attn = {u: jnp.where(tri, jnp.concatenate(a_rows[u], axis=0), 0.0).astype(BF16) for u in units}

    states = [st_ref[h] for h in range(heads)]
    for u in units:
        h = u[1]
        vb = v_ref[window(u)]
        o = lax.dot_general(qdec[u], states[h].astype(BF16), _NT, preferred_element_type=F32)
        o = o + dotf(attn[u], vb)
        states[h] = states[h] * etot[u] + lax.dot_general(vb, kdec[u], _TN, preferred_element_type=F32)
        if fused:
            tot = of_ref[window(u)] + o
            inv = lax.rsqrt(jnp.mean(tot * tot, -1, keepdims=True) + RMS_EPS)
            y = tot * inv * gn_ref[:, window(u)[2]] * gate_ref[window(u)].astype(F32)
            y_ref[window(u)[1:]] = y.astype(BF16)
        else:
            y_ref[window(u)] = o
    for h in range(heads):
        st_ref[h] = states[h]
    if fused:
        proj = _row_block_dot(y_ref[...], w_ref[...])
        out_ref[0] = _layer_norm(ALPHA * x_ref[0] + proj, lg_ref[...], lb_ref[...])


def _hgrn_specs(batch, seq, rev, heads, step_rows):
    nb = seq // step_rows
    blk = (lambda b, h, i: (b, nb - 1 - i, h)) if rev else (lambda b, h, i: (b, i, h))
    return pl.BlockSpec((1, step_rows, heads * C_KEY), blk), (batch, C_HEADS // heads, nb)


def _hgrn_fwd(q, lf, v, batch, seq):
    tile, grid = _hgrn_specs(batch, seq, False, HG_HP, HG_TB)
    view = lambda t: t.reshape(batch, seq, C_W)
    out = pl.pallas_call(
        functools.partial(_hgrn_kernel, rev=False, fused=False, heads=HG_HP, step_rows=HG_TB),
        grid=grid,
        in_specs=[tile, tile, tile],
        out_specs=tile,
        out_shape=jax.ShapeDtypeStruct((batch, seq, C_VW), F32),
        scratch_shapes=[pltpu.VMEM((HG_HP, C_VAL, C_KEY), F32)],
        compiler_params=_params(("arbitrary",) * 3),
        name="hgrn_fwd",
    )(view(q), view(lf), view(v))
    return out.reshape(batch * seq, C_VW)


def _hgrn_bwd_out(q, lf, v, o_f, gate, gn, x2d, w, g, b, batch, seq):
    tile, grid = _hgrn_specs(batch, seq, True, C_HEADS, HG_TB_OUT)
    view = lambda t: t.reshape(batch, seq, C_W)
    out = pl.pallas_call(
        functools.partial(_hgrn_kernel, rev=True, fused=True, heads=C_HEADS, step_rows=HG_TB_OUT),
        grid=grid,
        in_specs=[tile] * 5 + [_const_spec((1, C_VW)), tile, _const_spec((C_VW, D_MODEL), single=True),
                               _const_spec((1, D_MODEL)), _const_spec((1, D_MODEL))],
        out_specs=tile,
        out_shape=jax.ShapeDtypeStruct((batch, seq, D_MODEL), F32),
        scratch_shapes=[pltpu.VMEM((C_HEADS, C_VAL, C_KEY), F32), pltpu.VMEM((HG_TB_OUT, C_VW), BF16)],
        compiler_params=_params(("arbitrary",) * 3, VMEM_LIMIT),
        name="hgrn_bwd_out",
    )(view(q), view(lf), view(v), view(o_f), view(gate), gn, x2d.reshape(batch, seq, D_MODEL), w, g, b)
    return out.reshape(batch * seq, D_MODEL)


def _prep_weights(w_in_ab, w_out_ab, qn_ab, kn_ab, w_in_c, w_out_c, gn_c, ffn_w_up, ffn_w_down):
    scale = HEAD_DIM ** -0.5
    ab = []
    for j in range(w_in_ab.shape[0]):
        w = w_in_ab[j]
        o3 = 3 * A_W
        kb = w[:, o3 + B_QW:o3 + B_QW + B_KVW]
        vb = w[:, o3 + B_QW + B_KVW:]
        dup = lambda t: jnp.concatenate([t[:, :HEAD_DIM], t[:, :HEAD_DIM], t[:, HEAD_DIM:], t[:, HEAD_DIM:]], 1)
        w_ext = jnp.concatenate([w[:, :A_W] * scale, w[:, A_W:o3 + B_QW], dup(kb), dup(vb)], 1).astype(BF16)
        qg = (jnp.concatenate([qn_ab[j], qn_ab[j]]) * (scale * LOG2E)).reshape(1, LANES)
        kg = jnp.concatenate([kn_ab[j], kn_ab[j]]).reshape(1, LANES)
        ab.append((w_ext, qg, kg, w_out_ab[j].astype(BF16)))
    cc = [(w_in_c[j].astype(BF16), w_out_c[j].astype(BF16), gn_c[j].reshape(1, C_VW)) for j in range(w_in_c.shape[0])]
    return ab, cc, ffn_w_up.astype(BF16), ffn_w_down.astype(BF16)


def _trunk(x, prep, lb_fwd, lb_bwd, ln_mix_g, ln_mix_b, ln_ffn_g, ln_ffn_b, ffn_conv_w, ffn_conv_b):
    ab, cc, wu, wd = prep
    batch, seq, _ = x.shape
    x2d = x.reshape(batch * seq, D_MODEL)
    tabs = _rope_tables(seq)
    blk = jnp.arange(LANES) // HEAD_DIM
    ones2 = (blk[:, None] == blk[None, :]).astype(BF16)
    vec = lambda t: t.reshape(1, -1)
    for l in range(DEPTH):
        j = l // 2
        if l % 2 == 0:
            w_ext, qg, kg, w_out = ab[j]
            assert [d for _, d in A_PATTERNS] == [1, 4, DIL_CLASSES]
            assert all(window // (2 * d) == HALF_WINDOW for window, d in A_PATTERNS)
            qa, ka, va, qb, kd, vd, qa16, ka16, va16 = _ab_in(x2d, w_ext, tabs, qg, kg, ones2, batch, seq)
            o1, l1 = _band1(qa, ka, va, batch, seq)
            o16, l16 = _band16(qa16, ka16, va16, batch, seq)
            om, lm = _band4(qa16, ka16, va16, o16, l16, batch, seq)
            yb = _gqa(qb, kd, vd, batch, seq)
            x2d = _ab_out(o1, l1, om, lm, yb, x2d, w_out, vec(ln_mix_g[l]), vec(ln_mix_b[l]), seq)
        else:
            w_in, w_out, gn = cc[j]
            q, lf, lb, v, gate = _c_in(x2d, w_in, lb_fwd, lb_bwd, l)
            o_f = _hgrn_fwd(q, lf, v, batch, seq)
            x2d = _hgrn_bwd_out(q, lb, v, o_f, gate, gn, x2d, w_out, vec(ln_mix_g[l]), vec(ln_mix_b[l]), batch, seq)
        x2d = _ffn(x2d, wu[l], ffn_conv_w[l], vec(ffn_conv_b[l]), wd[l], vec(ln_ffn_g[l]), vec(ln_ffn_b[l]), seq)
    return x2d.reshape(batch, seq, D_MODEL)


def kernel(x_prompt, x_sample, w_in_ab, w_out_ab, qn_ab, kn_ab, w_in_c, w_out_c, lb_fwd, lb_bwd, gn_c, ln_mix_g, ln_mix_b, ln_ffn_g, ln_ffn_b, ffn_w_up, ffn_conv_w, ffn_conv_b, ffn_w_down):
    prep = _prep_weights(w_in_ab, w_out_ab, qn_ab, kn_ab, w_in_c, w_out_c, gn_c, ffn_w_up, ffn_w_down)
    rest = (lb_fwd, lb_bwd, ln_mix_g, ln_mix_b, ln_ffn_g, ln_ffn_b, ffn_conv_w, ffn_conv_b)
    return (_trunk(x_prompt, prep, *rest), _trunk(x_sample, prep, *rest))
```

```python
import functools

import jax
import jax.numpy as jnp
from jax import lax
from jax.experimental import pallas as pl
from jax.experimental.pallas import tpu as pltpu

F32 = jnp.float32
BF16 = jnp.bfloat16

D_MODEL = 1024
DEPTH = 2
HEAD_DIM = 64
A_HEADS = 8
A_PATTERNS = ((128, 1), (512, 4), (2048, 16))
B_HEADS = 8
B_KV_HEADS = 2
GRID_W = 64
ROPE_THETA = 500000.0
ROPE_DIM = HEAD_DIM // 4
AXIAL_THETA = 10000.0
C_HEADS = 8
C_KEY = 128
C_VAL = 128
D_FF = 2816
ALPHA = (2 * DEPTH) ** 0.25
LN_EPS = 1e-5
RMS_EPS = 1e-6
A_W = A_HEADS * HEAD_DIM
B_QW = B_HEADS * HEAD_DIM
B_KVW = B_KV_HEADS * HEAD_DIM
C_W = C_HEADS * C_KEY
C_VW = C_HEADS * C_VAL

LANES = 128
HALF_WINDOW = 64
TM = 512
PROJ_ROWS = 128
FFN_ROWS = 256
BAND_L = 128
BAND_STEP = 512
BAND4_STEP = 128
DIL_CLASSES = 16
BAND4_ROWS = 32
BAND4_HALO = 16
GQA_TQ = 512
GQA_TK = 512
GQA_CHAIN = 512
FF_CHUNKS = (768, 768, 768, 512)
FFN_SUB = 1
HG_CHUNK = 64
HG_SUB = 16
HG_TB = 512
HG_HP = 4
HG_TB_OUT = 256
NEG = -1e30
LOG2E = 1.4426950408889634
VMEM_LIMIT = 56 * 1024 * 1024

_NT = (((1,), (1,)), ((), ()))
_TN = (((0,), (0,)), ((), ()))


def _params(sem, vmem=None):
    return pltpu.CompilerParams(dimension_semantics=sem, vmem_limit_bytes=vmem)


def _const_spec(shape, single=False):
    nd = len(shape)
    if single:
        return pl.BlockSpec(shape, lambda *_: (0,) * nd, pipeline_mode=pl.Buffered(1))
    return pl.BlockSpec(shape, lambda *_: (0,) * nd)


def _layer_norm(z, g, b):
    mu = jnp.mean(z, -1, keepdims=True)
    d = z - mu
    var = jnp.mean(d * d, -1, keepdims=True)
    return d * lax.rsqrt(var + LN_EPS) * g + b


def _sigmoid(z):
    return 1.0 / (1.0 + jnp.exp(-z))


def _row_block_dot(x, w, rows=PROJ_ROWS):
    m = x.shape[0]
    starts = list(range(0, m - m % rows, rows)) or [0]
    ends = starts[1:] + [m]
    parts = [jnp.dot(x[a:b], w, preferred_element_type=F32) for a, b in zip(starts, ends)]
    return parts[0] if len(parts) == 1 else jnp.concatenate(parts, axis=0)


def _pipelined(sections):
    pending = sections[0][0]()
    for i, (_, epilogue) in enumerate(sections):
        upcoming = sections[i + 1][0]() if i + 1 < len(sections) else None
        epilogue(pending)
        pending = upcoming


def _rope_tables(seq):
    pos = jnp.arange(seq, dtype=F32)
    d = jnp.arange(LANES) % HEAD_DIM

    def angles(p, dim, theta):
        freqs = theta ** (-(jnp.arange(0, dim, 2, dtype=F32) / dim))
        return p[:, None] * freqs[None, :]

    h = ROPE_DIM // 2
    ang = angles(pos, ROPE_DIM, ROPE_THETA)
    a = ang[:, d % h]
    lo, hi = d < h, (d >= h) & (d < ROPE_DIM)
    pc = jnp.where(lo | hi, jnp.cos(a), 1.0)
    psa = jnp.where(lo, -jnp.sin(a), 0.0)
    psb = jnp.where(hi, jnp.sin(a), 0.0)

    q = HEAD_DIM // 4
    row = jnp.floor(pos / GRID_W)
    col = pos - row * GRID_W
    ar = angles(row, HEAD_DIM // 2, AXIAL_THETA)[:, d % q]
    ac_ = angles(col, HEAD_DIM // 2, AXIAL_THETA)[:, d % q]
    a2 = jnp.where(d < HEAD_DIM // 2, ar, ac_)
    first = (d % (HEAD_DIM // 2)) < q
    ac = jnp.cos(a2)
    asa = jnp.where(first, -jnp.sin(a2), 0.0)
    asb = jnp.where(first, 0.0, jnp.sin(a2))
    return [t.astype(F32) for t in (pc, psa, psb, ac, asa, asb)]


def _rope(seg, c, sa, sb, shift):
    return seg * c + pltpu.roll(seg, LANES - shift, 1) * sa + pltpu.roll(seg, shift, 1) * sb


def _ab_in_kernel(x_ref, w_ref, pc_ref, psa_ref, psb_ref, ac_ref, asa_ref, asb_ref, qg_ref, kg_ref,
                  ones_ref, qa_ref, ka_ref, va_ref, qb_ref, kd_ref, vd_ref, qa16_ref, ka16_ref, va16_ref,
                  qs_ref, ks_ref, vs_ref):
    xb = x_ref[...].astype(BF16)

    def proj(a, n):
        return jnp.dot(xb, w_ref[:, a:a + n], preferred_element_type=F32)

    pc, psa, psb = pc_ref[...], psa_ref[...], psb_ref[...]
    ac, asa, asb = ac_ref[...], asa_ref[...], asb_ref[...]

    def dilated_a(h, rope, out_ref, out16_ref, slab_ref):
        for g in range(A_W // LANES):
            sl = slice(g * LANES, (g + 1) * LANES)
            y = _rope(h[:, sl], pc, psa, psb, ROPE_DIM // 2) if rope else h[:, sl]
            out_ref[:, sl] = y.astype(BF16)
            slab_ref[g] = y
        for c in range(DIL_CLASSES):
            for g in range(A_W // LANES):
                rows = slab_ref[g, pl.ds(c, TM // DIL_CLASSES, stride=DIL_CLASSES), :]
                out16_ref[0, c, :, g * LANES:(g + 1) * LANES] = rows.astype(BF16)

    def norm_rope(h, gain, out_ref):
        for g in range(h.shape[1] // LANES):
            sl = slice(g * LANES, (g + 1) * LANES)
            seg = h[:, sl]
            sq = seg * seg
            hi = sq.astype(BF16)
            lo = (sq - hi.astype(F32)).astype(BF16)
            ss = (jnp.dot(hi, ones_ref[...], preferred_element_type=F32)
                  + jnp.dot(lo, ones_ref[...], preferred_element_type=F32))
            y = seg * lax.rsqrt(ss * (1.0 / HEAD_DIM) + RMS_EPS) * gain
            out_ref[:, sl] = _rope(y, ac, asa, asb, HEAD_DIM // 4).astype(BF16)

    def cast_to(out_ref):
        def fn(h):
            out_ref[...] = h.astype(BF16)
        return fn

    o_qb = 3 * A_W
    o_kd = o_qb + B_QW
    o_vd = o_kd + 2 * B_KVW
    _pipelined([
        (functools.partial(proj, 0, A_W),
         functools.partial(dilated_a, rope=True, out_ref=qa_ref, out16_ref=qa16_ref, slab_ref=qs_ref)),
        (functools.partial(proj, A_W, A_W),
         functools.partial(dilated_a, rope=True, out_ref=ka_ref, out16_ref=ka16_ref, slab_ref=ks_ref)),
        (functools.partial(proj, 2 * A_W, A_W),
         functools.partial(dilated_a, rope=False, out_ref=va_ref, out16_ref=va16_ref, slab_ref=vs_ref)),
        (functools.partial(proj, o_qb, B_QW), functools.partial(norm_rope, gain=qg_ref[...], out_ref=qb_ref)),
        (functools.partial(proj, o_kd, 2 * B_KVW), functools.partial(norm_rope, gain=kg_ref[...], out_ref=kd_ref)),
        (functools.partial(proj, o_vd, 2 * B_KVW), cast_to(vd_ref)),
    ])


def _ab_in(x2d, w_ext, tabs, qg, kg, ones2, batch, seq):
    t = x2d.shape[0]
    nt = seq // TM
    tab = pl.BlockSpec((TM, LANES), lambda i: (i % nt, 0))
    row = lambda n: pl.BlockSpec((TM, n), lambda i: (i, 0))
    cls = pl.BlockSpec((1, DIL_CLASSES, TM // DIL_CLASSES, A_W), lambda i: (i // nt, 0, i % nt, 0))
    cls_shape = jax.ShapeDtypeStruct((batch, DIL_CLASSES, seq // DIL_CLASSES, A_W), BF16)
    wn = w_ext.shape[1]
    return pl.pallas_call(
        _ab_in_kernel,
        grid=(t // TM,),
        in_specs=[row(D_MODEL), _const_spec((D_MODEL, wn))] + [tab] * 6
        + [_const_spec((1, LANES)), _const_spec((1, LANES)), _const_spec((LANES, LANES))],
        out_specs=[row(A_W)] * 3 + [row(B_QW), row(2 * B_KVW), row(2 * B_KVW)] + [cls] * 3,
        out_shape=[jax.ShapeDtypeStruct((t, A_W), BF16)] * 3
        + [jax.ShapeDtypeStruct((t, B_QW), BF16)]
        + [jax.ShapeDtypeStruct((t, 2 * B_KVW), BF16)] * 2 + [cls_shape] * 3,
        scratch_shapes=[pltpu.VMEM((A_W // LANES, TM, LANES), F32)] * 3,
        compiler_params=_params(("arbitrary",), VMEM_LIMIT),
        name="ab_in",
    )(x2d, w_ext, *tabs, qg, kg, ones2)


def _band_groups(qs, ks, vs, bias, left):
    chains = [(g, sel) for g in range(len(qs)) for sel in (left, jnp.logical_not(left))]
    scores = []
    for g, sel in chains:
        qm = jnp.where(sel, qs[g], jnp.zeros_like(qs[g]))
        scores.append(lax.dot_general(qm, ks[g], _NT, preferred_element_type=F32) + bias)
    probs = []
    for s in scores:
        mx = jnp.max(s, axis=1, keepdims=True)
        pe = jnp.exp(s - mx)
        probs.append((mx, jnp.sum(pe, axis=1, keepdims=True), pe.astype(BF16)))
    outs = []
    for (g, _), (mx, l, pe) in zip(chains, probs):
        o = jnp.dot(pe, vs[g], preferred_element_type=F32)
        outs.append((o * (1.0 / l), mx + jnp.log(l)))
    return [(jnp.where(left, outs[2 * g][0], outs[2 * g + 1][0]), jnp.where(left, outs[2 * g][1], outs[2 * g + 1][1]))
            for g in range(len(qs))]


def _band_kernel(q_ref, kp_ref, kc_ref, kn_ref, vp_ref, vc_ref, vn_ref, o_ref, lse_ref, *, m, step_rows):
    nk = BAND_L + 2 * HALF_WINDOW
    base = pl.program_id(len(q_ref.shape) - 2) * step_rows
    r = lax.broadcasted_iota(jnp.int32, (BAND_L, nk), 0)
    j = lax.broadcasted_iota(jnp.int32, (BAND_L, nk), 1)
    band = jnp.where(jnp.abs(j - HALF_WINDOW - r) <= HALF_WINDOW, 0.0, NEG)
    left = lax.broadcasted_iota(jnp.int32, (BAND_L, LANES), 1) < HEAD_DIM
    pre = (0,) * (len(q_ref.shape) - 2)
    lanes = [slice(p * LANES, (p + 1) * LANES) for p in range(A_W // LANES)]
    at = lambda rows, sl: pre + (rows, sl)
    whole = slice(None)
    ks = [jnp.concatenate([kp_ref[at(whole, sl)], kc_ref[at(whole, sl)], kn_ref[at(whole, sl)]], axis=0) for sl in lanes]
    vs = [jnp.concatenate([vp_ref[at(whole, sl)], vc_ref[at(whole, sl)], vn_ref[at(whole, sl)]], axis=0) for sl in lanes]
    for blk in range(step_rows // BAND_L):
        r0 = blk * BAND_L
        rows = slice(r0, r0 + BAND_L)
        kpos = base + r0 - HALF_WINDOW + j
        bias = jnp.where(kpos >= 0, band, NEG)
        bias = jnp.where(kpos < m, bias, NEG)
        res = _band_groups([q_ref[at(rows, sl)] for sl in lanes], [kk[r0:r0 + nk] for kk in ks],
                           [vv[r0:r0 + nk] for vv in vs], bias, left)
        for sl, (o, lse) in zip(lanes, res):
            o_ref[at(rows, sl)] = o.astype(BF16)
            lse_ref[at(rows, sl)] = lse


def _band_specs(lead, rows, step_rows):
    nh = rows // HALF_WINDOW
    ratio = step_rows // HALF_WINDOW
    ones = (1,) * lead
    cur = pl.BlockSpec(ones + (step_rows, A_W), lambda *g: g[:lead] + (g[-1], 0))
    prev = pl.BlockSpec(ones + (HALF_WINDOW, A_W), lambda *g: g[:lead] + (jnp.maximum(g[-1] * ratio - 1, 0), 0))
    nxt = pl.BlockSpec(ones + (HALF_WINDOW, A_W), lambda *g: g[:lead] + (jnp.minimum((g[-1] + 1) * ratio, nh - 1), 0))
    return cur, prev, nxt


def _band1(q, k, v, batch, seq):
    step_rows = min(BAND_STEP, seq)
    cur, prev, nxt = _band_specs(1, seq, step_rows)
    view = lambda t: t.reshape(batch, seq, A_W)
    o, lse = pl.pallas_call(
        functools.partial(_band_kernel, m=seq, step_rows=step_rows),
        grid=(batch, seq // step_rows),
        in_specs=[cur, prev, cur, nxt, prev, cur, nxt],
        out_specs=[cur, cur],
        out_shape=[jax.ShapeDtypeStruct((batch, seq, A_W), BF16), jax.ShapeDtypeStruct((batch, seq, A_W), F32)],
        compiler_params=_params(("arbitrary",) * 2),
        name="band_d1",
    )(view(q), view(k), view(k), view(k), view(v), view(v), view(v))
    return o.reshape(batch * seq, A_W), lse.reshape(batch * seq, A_W)


def _band16(q16, k16, v16, batch, seq):
    m = seq // DIL_CLASSES
    step_rows = min(BAND_STEP, m)
    cur, prev, nxt = _band_specs(2, m, step_rows)
    shape = (batch, DIL_CLASSES, m, A_W)
    return pl.pallas_call(
        functools.partial(_band_kernel, m=m, step_rows=step_rows),
        grid=(batch, DIL_CLASSES, m // step_rows),
        in_specs=[cur, prev, cur, nxt, prev, cur, nxt],
        out_specs=[cur, cur],
        out_shape=[jax.ShapeDtypeStruct(shape, BF16), jax.ShapeDtypeStruct(shape, F32)],
        compiler_params=_params(("arbitrary",) * 3),
        name="band_d16",
    )(q16, k16, k16, k16, v16, v16, v16)


def _band4_kernel(q_ref, kp_ref, kc_ref, kn_ref, vp_ref, vc_ref, vn_ref, o16_ref, l16_ref, o_ref, lse_ref, *,
                  m16, step_rows):
    qr = BAND4_ROWS
    kr = qr + 2 * BAND4_HALO
    i = pl.program_id(2)
    shift_q, shift_k = qr.bit_length() - 1, kr.bit_length() - 1
    row = lax.broadcasted_iota(jnp.int32, (4 * qr, 4 * kr), 0)
    col = lax.broadcasted_iota(jnp.int32, (4 * qr, 4 * kr), 1)
    uq, rq = row >> shift_q, row & (qr - 1)
    uk, jk = col >> shift_k, col & (kr - 1)
    band = jnp.where(jnp.abs(4 * (jk - BAND4_HALO - rq) + (uk - uq)) <= HALF_WINDOW, 0.0, NEG)
    left = lax.broadcasted_iota(jnp.int32, (4 * qr, LANES), 1) < HEAD_DIM
    slabs = [slice(p * LANES, (p + 1) * LANES) for p in range(A_W // LANES)]
    window = lambda refs, u, sl: jnp.concatenate([r[0, u, 0, :, sl] for r in refs], axis=0)
    kwin = [[window([kp_ref, kc_ref, kn_ref], u, sl) for u in range(4)] for sl in slabs]
    vwin = [[window([vp_ref, vc_ref, vn_ref], u, sl) for u in range(4)] for sl in slabs]
    for blk in range(step_rows // qr):
        r0 = blk * qr
        rows = slice(r0, r0 + qr)
        kpos = i * step_rows + r0 - BAND4_HALO + jk
        bias = jnp.where(kpos >= 0, band, NEG)
        bias = jnp.where(kpos < m16, bias, NEG)
        classes = lambda ref, sl: jnp.concatenate([ref[0, u, 0, rows, sl] for u in range(4)], axis=0)
        branch4 = _band_groups([classes(q_ref, sl) for sl in slabs],
                               [jnp.concatenate([w[r0:r0 + kr] for w in kw], axis=0) for kw in kwin],
                               [jnp.concatenate([w[r0:r0 + kr] for w in vw], axis=0) for vw in vwin], bias, left)
        for sl, (o4, l4) in zip(slabs, branch4):
            o16 = classes(o16_ref, sl).astype(F32)
            l16 = classes(l16_ref, sl)
            mx = jnp.maximum(l4, l16)
            w4, w16 = jnp.exp(l4 - mx), jnp.exp(l16 - mx)
            tot = w4 + w16
            om = (w4 * o4 + w16 * o16) * (1.0 / tot)
            lm = mx + jnp.log(tot)
            for u in range(4):
                o_ref[0, u, 0, rows, sl] = om[u * qr:(u + 1) * qr].astype(BF16)
                lse_ref[0, u, 0, rows, sl] = lm[u * qr:(u + 1) * qr]


def _band4(q16, k16, v16, o16, l16, batch, seq):
    m16 = seq // DIL_CLASSES
    step_rows = min(BAND4_STEP, m16)
    ratio = step_rows // BAND4_HALO
    nh = m16 // BAND4_HALO
    view = lambda t: t.reshape(batch, 4, 4, m16, A_W)
    cur = pl.BlockSpec((1, 4, 1, step_rows, A_W), lambda b, c, i: (b, 0, c, i, 0))
    prev = pl.BlockSpec((1, 4, 1, BAND4_HALO, A_W), lambda b, c, i: (b, 0, c, jnp.maximum(i * ratio - 1, 0), 0))
    nxt = pl.BlockSpec((1, 4, 1, BAND4_HALO, A_W), lambda b, c, i: (b, 0, c, jnp.minimum((i + 1) * ratio, nh - 1), 0))
    shape = (batch, 4, 4, m16, A_W)
    o, lse = pl.pallas_call(
        functools.partial(_band4_kernel, m16=m16, step_rows=step_rows),
        grid=(batch, 4, m16 // step_rows),
        in_specs=[cur, prev, cur, nxt, prev, cur, nxt, cur, cur],
        out_specs=[cur, cur],
        out_shape=[jax.ShapeDtypeStruct(shape, BF16), jax.ShapeDtypeStruct(shape, F32)],
        compiler_params=_params(("arbitrary",) * 3),
        name="band_d4",
    )(view(q16), view(k16), view(k16), view(k16), view(v16), view(v16), view(v16), view(o16), view(l16))
    return o.reshape(batch, DIL_CLASSES, m16, A_W), lse.reshape(batch, DIL_CLASSES, m16, A_W)


def _gqa_kernel(q_ref, k_ref, v_ref, o_ref, qs_ref, m_ref, acc_ref, s_ref, vt_ref, *, seq):
    left = lax.broadcasted_iota(jnp.int32, (GQA_TQ, LANES), 1) < HEAD_DIM
    for pr in range(2):
        qp = q_ref[0, :, pr * LANES:(pr + 1) * LANES]
        zero = jnp.zeros_like(qp)
        qs_ref[(2 * pr) * GQA_TQ:(2 * pr + 1) * GQA_TQ, :] = jnp.where(left, qp, zero)
        qs_ref[(2 * pr + 1) * GQA_TQ:(2 * pr + 2) * GQA_TQ, :] = jnp.where(left, zero, qp)
    m_ref[...] = jnp.full(m_ref.shape, NEG, F32)
    acc_ref[...] = jnp.zeros(acc_ref.shape, F32)
    n_tiles = seq // GQA_TK

    @pl.when(pl.program_id(2) == 0)
    def _():
        ones_row = lax.broadcasted_iota(jnp.int32, (LANES, GQA_TK), 0) == HEAD_DIM

        def transpose_tile(t, carry):
            off = pl.multiple_of(t * GQA_TK, GQA_TK)
            vt = v_ref[0, pl.ds(off, GQA_TK), :].astype(F32).T
            vt_ref[t] = jnp.where(ones_row, 1.0, vt).astype(BF16)
            return carry

        lax.fori_loop(0, n_tiles, transpose_tile, 0)

    chains = [slice(c * GQA_CHAIN, (c + 1) * GQA_CHAIN) for c in range(4 * GQA_TQ // GQA_CHAIN)]

    def scores(tile, buf, cols):
        off = pl.multiple_of(tile * GQA_TK, GQA_TK)
        k = k_ref[0, pl.ds(off, GQA_TK), :]
        s_ref[buf, :, cols] = lax.dot_general(k, qs_ref[cols, :], _NT, preferred_element_type=F32)

    def softmax_pv(tile, buf, cols):
        s = s_ref[buf, :, cols]
        m_prev = m_ref[:, cols]
        m_new = jnp.maximum(m_prev, jnp.max(s, axis=0, keepdims=True))
        alpha = jnp.exp2(m_prev - m_new)
        p = jnp.exp2((s - m_new).astype(BF16))
        pv = jnp.dot(vt_ref[tile], p, preferred_element_type=F32)
        acc_ref[:, cols] = alpha * acc_ref[:, cols] + pv
        m_ref[:, cols] = m_new

    def step(tile, buf, prefetch):
        for cols in chains:
            if prefetch:
                scores(tile + 1, 1 - buf, cols)
            softmax_pv(tile, buf, cols)

    def pair(j, carry):
        step(2 * j, 0, True)
        step(2 * j + 1, 1, True)
        return carry

    for cols in chains:
        scores(0, 0, cols)
    lax.fori_loop(0, n_tiles // 2 - 1, pair, 0)
    step(n_tiles - 2, 0, True)
    step(n_tiles - 1, 1, False)
    acc = acc_ref[...]
    a = (acc * (1.0 / acc[HEAD_DIM:HEAD_DIM + 1, :])).T
    for pr in range(2):
        lo = a[(2 * pr) * GQA_TQ:(2 * pr + 1) * GQA_TQ]
        hi = pltpu.roll(a[(2 * pr + 1) * GQA_TQ:(2 * pr + 2) * GQA_TQ], HEAD_DIM, 1)
        o_ref[0, :, pr * LANES:(pr + 1) * LANES] = jnp.where(left, lo, hi).astype(BF16)


def _gqa(qb, kd, vd, batch, seq):
    gw = B_QW // B_KV_HEADS
    cols = 4 * GQA_TQ
    out = pl.pallas_call(
        functools.partial(_gqa_kernel, seq=seq),
        grid=(batch, B_KV_HEADS, seq // GQA_TQ),
        in_specs=[pl.BlockSpec((1, GQA_TQ, gw), lambda b, g, qi: (b, qi, g)),
                  pl.BlockSpec((1, seq, LANES), lambda b, g, qi: (b, 0, g)),
                  pl.BlockSpec((1, seq, LANES), lambda b, g, qi: (b, 0, g))],
        out_specs=pl.BlockSpec((1, GQA_TQ, gw), lambda b, g, qi: (b, qi, g)),
        out_shape=jax.ShapeDtypeStruct((batch, seq, B_QW), BF16),
        scratch_shapes=[pltpu.VMEM((cols, LANES), BF16), pltpu.VMEM((1, cols), F32),
                        pltpu.VMEM((LANES, cols), F32), pltpu.VMEM((2, GQA_TK, cols), F32),
                        pltpu.VMEM((seq // GQA_TK, LANES, GQA_TK), BF16)],
        compiler_params=_params(("arbitrary",) * 3, VMEM_LIMIT),
        name="gqa",
    )(qb.reshape(batch, seq, B_QW), kd.reshape(batch, seq, 2 * B_KVW), vd.reshape(batch, seq, 2 * B_KVW))
    return out.reshape(batch * seq, B_QW)


def _ab_out_kernel(o1_ref, l1_ref, om_ref, lm_ref, yb_ref, x_ref, w_ref, g_ref, b_ref, out_ref, os_ref, ls_ref):
    for c in range(DIL_CLASSES):
        for g in range(A_W // LANES):
            sl = slice(g * LANES, (g + 1) * LANES)
            rows = pl.ds(c, TM // DIL_CLASSES, stride=DIL_CLASSES)
            os_ref[g, rows, :] = om_ref[0, c, :, sl].astype(F32)
            ls_ref[g, rows, :] = lm_ref[0, c, :, sl]
    ya = []
    for g in range(A_W // LANES):
        sl = slice(g * LANES, (g + 1) * LANES)
        l1, lm = l1_ref[:, sl], ls_ref[g]
        mx = jnp.maximum(l1, lm)
        e1, em = jnp.exp(l1 - mx), jnp.exp(lm - mx)
        ya.append(((e1 * o1_ref[:, sl].astype(F32) + em * os_ref[g]) * (1.0 / (e1 + em))).astype(BF16))
    y = _row_block_dot(jnp.concatenate(ya + [yb_ref[...]], axis=1), w_ref[...])
    out_ref[...] = _layer_norm(ALPHA * x_ref[...] + y, g_ref[...], b_ref[...])


def _ab_out(o1, l1, om, lm, yb, x2d, w, g, b, seq):
    t = x2d.shape[0]
    nt = seq // TM
    row = lambda n: pl.BlockSpec((TM, n), lambda i: (i, 0))
    cls = pl.BlockSpec((1, DIL_CLASSES, TM // DIL_CLASSES, A_W), lambda i: (i // nt, 0, i % nt, 0))
    return pl.pallas_call(
        _ab_out_kernel,
        grid=(t // TM,),
        in_specs=[row(A_W), row(A_W), cls, cls, row(B_QW), row(D_MODEL), _const_spec((A_W + B_QW, D_MODEL)),
                  _const_spec((1, D_MODEL)), _const_spec((1, D_MODEL))],
        out_specs=row(D_MODEL),
        out_shape=jax.ShapeDtypeStruct((t, D_MODEL), F32),
        scratch_shapes=[pltpu.VMEM((A_W // LANES, TM, LANES), F32)] * 2,
        compiler_params=_params(("arbitrary",), VMEM_LIMIT),
        name="ab_out",
    )(o1, l1, om, lm, yb, x2d, w, g, b)


def _gelu(x):
    return 0.5 * x * (1.0 + jnp.tanh(0.7978845608028654 * (x + 0.044715 * (x * x * x))))


def _ffn_kernel(x_ref, xp_ref, xn_ref, wu_ref, cw_ref, cb_ref, wd_ref, g_ref, b_ref, out_ref, *, nt):
    i = pl.program_id(0)
    keep_prev = (i % nt != 0).astype(F32)
    keep_next = (i % nt != nt - 1).astype(F32)
    rows = lax.broadcasted_iota(jnp.int32, (TM, 1), 0)
    offsets = [sum(FF_CHUNKS[:c]) for c in range(len(FF_CHUNKS))]
    operands, acc = {}, {}

    def tile_operands(j):
        if j not in operands:
            r0 = j * TM
            xb = x_ref[r0:r0 + TM, :].astype(BF16)
            before = xp_ref[...] if j == 0 else x_ref[r0 - 8:r0, :]
            after = xn_ref[...] if j == FFN_SUB - 1 else x_ref[r0 + TM:r0 + TM + 8, :]
            halo = jnp.concatenate([before, after], axis=0).astype(BF16)
            operands[j] = (xb, jnp.concatenate([xb, halo], axis=0))
        return operands[j]

    def up(j, c):
        a, n = offsets[c], FF_CHUNKS[c]
        xb, xe = tile_operands(j)
        u = _row_block_dot(xb, wu_ref[:, a:a + n], FFN_ROWS)
        ge = _row_block_dot(xe, wu_ref[:, D_FF + a:D_FF + a + n], FFN_ROWS)
        return u, ge

    def finish(j, c, pending):
        u, ge = pending
        a, n = offsets[c], FF_CHUNKS[c]
        gm = ge[0:TM]
        g_before = ge[TM + 7:TM + 8]
        g_after = ge[TM + 8:TM + 9]
        if j == 0:
            g_before = g_before * keep_prev
        if j == FFN_SUB - 1:
            g_after = g_after * keep_next
        gp = jnp.where(rows == 0, g_before, pltpu.roll(gm, 1, 0))
        gn = jnp.where(rows == TM - 1, g_after, pltpu.roll(gm, TM - 1, 0))
        cw = cw_ref[:, a:a + n]
        gc = gp * cw[0:1] + gm * cw[1:2] + gn * cw[2:3] + cb_ref[:, a:a + n]
        act = (_gelu(gc) * u).astype(BF16)
        part = _row_block_dot(act, wd_ref[a:a + n, :], FFN_ROWS)
        acc[j] = part if c == 0 else acc[j] + part
        if c == len(FF_CHUNKS) - 1:
            sl = slice(j * TM, (j + 1) * TM)
            out_ref[sl, :] = _layer_norm(ALPHA * x_ref[sl, :] + acc[j], g_ref[...], b_ref[...])

    _pipelined([(functools.partial(up, j, c), functools.partial(finish, j, c))
                for j in range(FFN_SUB) for c in range(len(FF_CHUNKS))])


def _ffn(x2d, wu, cw, cb, wd, g, b, seq):
    t = x2d.shape[0]
    step = FFN_SUB * TM
    nt = seq // step
    r8 = step // 8
    return pl.pallas_call(
        functools.partial(_ffn_kernel, nt=nt),
        grid=(t // step,),
        in_specs=[pl.BlockSpec((step, D_MODEL), lambda i: (i, 0)),
                  pl.BlockSpec((8, D_MODEL), lambda i: (jnp.maximum(i * r8 - 1, 0), 0)),
                  pl.BlockSpec((8, D_MODEL), lambda i: (jnp.minimum((i + 1) * r8, t // 8 - 1), 0)),
                  _const_spec((D_MODEL, 2 * D_FF), single=True),
                  _const_spec((3, D_FF)), _const_spec((1, D_FF)),
                  _const_spec((D_FF, D_MODEL), single=True),
                  _const_spec((1, D_MODEL)), _const_spec((1, D_MODEL))],
        out_specs=pl.BlockSpec((step, D_MODEL), lambda i: (i, 0)),
        out_shape=jax.ShapeDtypeStruct((t, D_MODEL), F32),
        compiler_params=_params(("arbitrary",), VMEM_LIMIT),
        name="ffn",
    )(x2d, x2d, x2d, wu, cw, cb, wd, g, b)


def _lower_bound(tbl, layer):
    rows = [tbl[r:r + 1] for r in range(DEPTH)]
    mx = functools.reduce(jnp.maximum, rows)
    es = [jnp.exp(r - mx) for r in rows]
    inv = 1.0 / functools.reduce(lambda a, b: a + b, es)
    ps = [e * inv for e in es]
    return functools.reduce(lambda a, b: a + b, ps[:layer + 1]) - ps[0]


def _c_in_kernel(x_ref, w_ref, lbf_ref, lbb_ref, q_ref, lf_ref, lb_ref, v_ref, g_ref, *, layer):
    half = C_W // 2
    blocks = [slice(r * PROJ_ROWS, (r + 1) * PROJ_ROWS) for r in range(TM // PROJ_ROWS)]
    xbs = [x_ref[rows, :].astype(BF16) for rows in blocks]

    def proj(r, a, n):
        return jnp.dot(xbs[r], w_ref[:, a:a + n], preferred_element_type=F32)

    def silu(z):
        return z * _sigmoid(z)

    lbf = _lower_bound(lbf_ref[...], layer)
    lbb = _lower_bound(lbb_ref[...], layer)

    def store_q(z, rows, sl):
        q_ref[rows, sl] = z.astype(BF16)

    def store_logf(tbl, out_ref):
        def fn(z, rows, sl):
            lb = tbl[:, sl]
            out_ref[rows, sl] = jnp.log(lb + (1.0 - lb) * _sigmoid(z))
        return fn

    def store_silu(out_ref):
        def fn(z, rows, sl):
            out_ref[rows, sl] = silu(z).astype(BF16)
        return fn

    sections = []
    for r, rows in enumerate(blocks):
        for c in range(2):
            a = c * half
            sl = slice(a, a + half)
            for off, fn in ((0, store_q), (C_W, store_logf(lbf, lf_ref)), (2 * C_W, store_logf(lbb, lb_ref)),
                            (3 * C_W, store_silu(v_ref)), (3 * C_W + C_VW, store_silu(g_ref))):
                sections.append((functools.partial(proj, r, off + a, half), functools.partial(fn, rows=rows, sl=sl)))
    _pipelined(sections)


def _c_in(x2d, w, lbf, lbb, layer):
    t = x2d.shape[0]
    row = pl.BlockSpec((TM, C_W), lambda i: (i, 0))
    return pl.pallas_call(
        functools.partial(_c_in_kernel, layer=layer),
        grid=(t // TM,),
        in_specs=[row, _const_spec(w.shape), _const_spec((DEPTH, C_W)), _const_spec((DEPTH, C_W))],
        out_specs=[row] * 5,
        out_shape=[jax.ShapeDtypeStruct((t, C_W), d) for d in (BF16, F32, F32, BF16, BF16)],
        compiler_params=_params(("arbitrary",), VMEM_LIMIT),
        name="c_in",
    )(x2d, w, lbf, lbb)


def _hgrn_kernel(*refs, rev, fused, heads, step_rows):
    if fused:
        (q_ref, lf_ref, v_ref, of_ref, gate_ref, gn_ref, x_ref, w_ref, lg_ref, lb_ref,
         out_ref, st_ref, y_ref) = refs
    else:
        q_ref, lf_ref, v_ref, y_ref, st_ref = refs

    @pl.when(pl.program_id(2) == 0)
    def _():
        st_ref[...] = jnp.zeros(st_ref.shape, F32)

    ri = lax.broadcasted_iota(jnp.int32, (HG_CHUNK, HG_CHUNK), 0)
    ci = lax.broadcasted_iota(jnp.int32, (HG_CHUNK, HG_CHUNK), 1)
    tri = (ci >= ri) if rev else (ci <= ri)
    trib = jnp.where(tri, 1.0, 0.0).astype(BF16)
    dotf = lambda a, b: jnp.dot(a, b, preferred_element_type=F32)
    n_chunks = step_rows // HG_CHUNK
    order = range(n_chunks - 1, -1, -1) if rev else range(n_chunks)
    units = [(c, h) for c in order for h in range(heads)]
    window = lambda u: (0, slice(u[0] * HG_CHUNK, (u[0] + 1) * HG_CHUNK), slice(u[1] * C_KEY, (u[1] + 1) * C_KEY))

    b_all = {}
    for u in units:
        lf = lf_ref[window(u)]
        hi = lf.astype(BF16)
        lo = (lf - hi.astype(F32)).astype(BF16)
        b_all[u] = dotf(trib, hi) + dotf(trib, lo)

    qdec, kdec, etot, a_rows = {}, {}, {}, {}
    for u in units:
        lf, b = lf_ref[window(u)], b_all[u]
        qf = q_ref[window(u)].astype(F32)
        bex = b - lf
        kk = 1.0 - jnp.exp(lf)
        btot = b[0:1] if rev else b[HG_CHUNK - 1:HG_CHUNK]
        qdec[u] = (qf * jnp.exp(b)).astype(BF16)
        kdec[u] = (kk * jnp.exp(btot - b)).astype(BF16)
        etot[u] = jnp.exp(btot)
        rows = []
        for blk in range(HG_CHUNK // HG_SUB):
            r0 = blk * HG_SUB
            ref = bex[r0 + HG_SUB - 1:r0 + HG_SUB] if rev else bex[r0:r0 + 1]
            qt = (qf[r0:r0 + HG_SUB] * jnp.exp(b[r0:r0 + HG_SUB] - ref)).astype(BF16)
            lo_r, hi_r = (r0, HG_CHUNK) if rev else (0, r0 + HG_SUB)
            ks = (kk[lo_r:hi_r] * jnp.exp(ref - b[lo_r:hi_r])).astype(BF16)
            pieces = []
            if lo_r > 0:
                pieces.append(jnp.zeros((lo_r, C_KEY), BF16))
            pieces.append(ks)
            if hi_r < HG_CHUNK:
                pieces.append(jnp.zeros((HG_CHUNK - hi_r, C_KEY), BF16))
            kfull = jnp.concatenate(pieces, axis=0) if len(pieces) > 1 else ks
            rows.append(lax.dot_general(qt, kfull, _NT, preferred_element_type=F32))
        a_rows[u] = rows
    attn = {u: jnp.where(tri, jnp.concatenate(a_rows[u], axis=0), 0.0).astype(BF16) for u in units}

    states = [st_ref[h] for h in range(heads)]
    for u in units:
        h = u[1]
        vb = v_ref[window(u)]
        o = lax.dot_general(qdec[u], states[h].astype(BF16), _NT, preferred_element_type=F32)
        o = o + dotf(attn[u], vb)
        states[h] = states[h] * etot[u] + lax.dot_general(vb, kdec[u], _TN, preferred_element_type=F32)
        if fused:
            tot = of_ref[window(u)] + o
            inv = lax.rsqrt(jnp.mean(tot * tot, -1, keepdims=True) + RMS_EPS)
            y = tot * inv * gn_ref[:, window(u)[2]] * gate_ref[window(u)].astype(F32)
            y_ref[window(u)[1:]] = y.astype(BF16)
        else:
            y_ref[window(u)] = o
    for h in range(heads):
        st_ref[h] = states[h]
    if fused:
        proj = _row_block_dot(y_ref[...], w_ref[...])
        out_ref[0] = _layer_norm(ALPHA * x_ref[0] + proj, lg_ref[...], lb_ref[...])


def _hgrn_specs(batch, seq, rev, heads, step_rows):
    nb = seq // step_rows
    blk = (lambda b, h, i: (b, nb - 1 - i, h)) if rev else (lambda b, h, i: (b, i, h))
    return pl.BlockSpec((1, step_rows, heads * C_KEY), blk), (batch, C_HEADS // heads, nb)


def _hgrn_fwd(q, lf, v, batch, seq):
    tile, grid = _hgrn_specs(batch, seq, False, HG_HP, HG_TB)
    view = lambda t: t.reshape(batch, seq, C_W)
    out = pl.pallas_call(
        functools.partial(_hgrn_kernel, rev=False, fused=False, heads=HG_HP, step_rows=HG_TB),
        grid=grid,
        in_specs=[tile, tile, tile],
        out_specs=tile,
        out_shape=jax.ShapeDtypeStruct((batch, seq, C_VW), F32),
        scratch_shapes=[pltpu.VMEM((HG_HP, C_VAL, C_KEY), F32)],
        compiler_params=_params(("arbitrary",) * 3),
        name="hgrn_fwd",
    )(view(q), view(lf), view(v))
    return out.reshape(batch * seq, C_VW)


def _hgrn_bwd_out(q, lf, v, o_f, gate, gn, x2d, w, g, b, batch, seq):
    tile, grid = _hgrn_specs(batch, seq, True, C_HEADS, HG_TB_OUT)
    view = lambda t: t.reshape(batch, seq, C_W)
    out = pl.pallas_call(
        functools.partial(_hgrn_kernel, rev=True, fused=True, heads=C_HEADS, step_rows=HG_TB_OUT),
        grid=grid,
        in_specs=[tile] * 5 + [_const_spec((1, C_VW)), tile, _const_spec((C_VW, D_MODEL), single=True),
                               _const_spec((1, D_MODEL)), _const_spec((1, D_MODEL))],
        out_specs=tile,
        out_shape=jax.ShapeDtypeStruct((batch, seq, D_MODEL), F32),
        scratch_shapes=[pltpu.VMEM((C_HEADS, C_VAL, C_KEY), F32), pltpu.VMEM((HG_TB_OUT, C_VW), BF16)],
        compiler_params=_params(("arbitrary",) * 3, VMEM_LIMIT),
        name="hgrn_bwd_out",
    )(view(q), view(lf), view(v), view(o_f), view(gate), gn, x2d.reshape(batch, seq, D_MODEL), w, g, b)
    return out.reshape(batch * seq, D_MODEL)


def _prep_weights(w_in_ab, w_out_ab, qn_ab, kn_ab, w_in_c, w_out_c, gn_c, ffn_w_up, ffn_w_down):
    scale = HEAD_DIM ** -0.5
    ab = []
    for j in range(w_in_ab.shape[0]):
        w = w_in_ab[j]
        o3 = 3 * A_W
        kb = w[:, o3 + B_QW:o3 + B_QW + B_KVW]
        vb = w[:, o3 + B_QW + B_KVW:]
        dup = lambda t: jnp.concatenate([t[:, :HEAD_DIM], t[:, :HEAD_DIM], t[:, HEAD_DIM:], t[:, HEAD_DIM:]], 1)
        w_ext = jnp.concatenate([w[:, :A_W] * scale, w[:, A_W:o3 + B_QW], dup(kb), dup(vb)], 1).astype(BF16)
        qg = (jnp.concatenate([qn_ab[j], qn_ab[j]]) * (scale * LOG2E)).reshape(1, LANES)
        kg = jnp.concatenate([kn_ab[j], kn_ab[j]]).reshape(1, LANES)
        ab.append((w_ext, qg, kg, w_out_ab[j].astype(BF16)))
    cc = [(w_in_c[j].astype(BF16), w_out_c[j].astype(BF16), gn_c[j].reshape(1, C_VW)) for j in range(w_in_c.shape[0])]
    return ab, cc, ffn_w_up.astype(BF16), ffn_w_down.astype(BF16)


def _trunk(x, prep, lb_fwd, lb_bwd, ln_mix_g, ln_mix_b, ln_ffn_g, ln_ffn_b, ffn_conv_w, ffn_conv_b):
    ab, cc, wu, wd = prep
    batch, seq, _ = x.shape
    x2d = x.reshape(batch * seq, D_MODEL)
    tabs = _rope_tables(seq)
    blk = jnp.arange(LANES) // HEAD_DIM
    ones2 = (blk[:, None] == blk[None, :]).astype(BF16)
    vec = lambda t: t.reshape(1, -1)
    for l in range(DEPTH):
        j = l // 2
        if l % 2 == 0:
            w_ext, qg, kg, w_out = ab[j]
            assert [d for _, d in A_PATTERNS] == [1, 4, DIL_CLASSES]
            assert all(window // (2 * d) == HALF_WINDOW for window, d in A_PATTERNS)
            qa, ka, va, qb, kd, vd, qa16, ka16, va16 = _ab_in(x2d, w_ext, tabs, qg, kg, ones2, batch, seq)
            o1, l1 = _band1(qa, ka, va, batch, seq)
            o16, l16 = _band16(qa16, ka16, va16, batch, seq)
            om, lm = _band4(qa16, ka16, va16, o16, l16, batch, seq)
            yb = _gqa(qb, kd, vd, batch, seq)
            x2d = _ab_out(o1, l1, om, lm, yb, x2d, w_out, vec(ln_mix_g[l]), vec(ln_mix_b[l]), seq)
        else:
            w_in, w_out, gn = cc[j]
            q, lf, lb, v, gate = _c_in(x2d, w_in, lb_fwd, lb_bwd, l)
            o_f = _hgrn_fwd(q, lf, v, batch, seq)
            x2d = _hgrn_bwd_out(q, lb, v, o_f, gate, gn, x2d, w_out, vec(ln_mix_g[l]), vec(ln_mix_b[l]), batch, seq)
        x2d = _ffn(x2d, wu[l], ffn_conv_w[l], vec(ffn_conv_b[l]), wd[l], vec(ln_ffn_g[l]), vec(ln_ffn_b[l]), seq)
    return x2d.reshape(batch, seq, D_MODEL)


def kernel(x_prompt, x_sample, w_in_ab, w_out_ab, qn_ab, kn_ab, w_in_c, w_out_c, lb_fwd, lb_bwd, gn_c, ln_mix_g, ln_mix_b, ln_ffn_g, ln_ffn_b, ffn_w_up, ffn_conv_w, ffn_conv_b, ffn_w_down):
    prep = _prep_weights(w_in_ab, w_out_ab, qn_ab, kn_ab, w_in_c, w_out_c, gn_c, ffn_w_up, ffn_w_down)
    rest = (lb_fwd, lb_bwd, ln_mix_g, ln_mix_b, ln_ffn_g, ln_ffn_b, ffn_conv_w, ffn_conv_b)
    return (_trunk(x_prompt, prep, *rest), _trunk(x_sample, prep, *rest))
```

```python
import functools

import jax
import jax.numpy as jnp
from jax import lax
from jax.experimental import pallas as pl
from jax.experimental.pallas import tpu as pltpu

F32 = jnp.float32
BF16 = jnp.bfloat16

D_MODEL = 1024
DEPTH = 2
HEAD_DIM = 64
A_HEADS = 8
A_PATTERNS = ((128, 1), (512, 4), (2048, 16))
B_HEADS = 8
B_KV_HEADS = 2
GRID_W = 64
ROPE_THETA = 500000.0
ROPE_DIM = HEAD_DIM // 4
AXIAL_THETA = 10000.0
C_HEADS = 8
C_KEY = 128
C_VAL = 128
D_FF = 2816
ALPHA = (2 * DEPTH) ** 0.25
LN_EPS = 1e-5
RMS_EPS = 1e-6
A_W = A_HEADS * HEAD_DIM
B_QW = B_HEADS * HEAD_DIM
B_KVW = B_KV_HEADS * HEAD_DIM
C_W = C_HEADS * C_KEY
C_VW = C_HEADS * C_VAL

LANES = 128
HALF_WINDOW = 64
TM = 512
PROJ_ROWS = 128
FFN_ROWS = 256
BAND_L = 128
BAND_STEP = 512
BAND4_STEP = 128
DIL_CLASSES = 16
BAND4_ROWS = 32
BAND4_HALO = 16
GQA_TQ = 512
GQA_TK = 512
GQA_CHAIN = 512
FF_CHUNKS = (768, 768, 768, 512)
FFN_SUB = 1
HG_CHUNK = 64
HG_SUB = 16
HG_TB = 512
HG_HP = 4
HG_TB_OUT = 256
NEG = -1e30
LOG2E = 1.4426950408889634
VMEM_LIMIT = 56 * 1024 * 1024

_NT = (((1,), (1,)), ((), ()))
_TN = (((0,), (0,)), ((), ()))


def _params(sem, vmem=None):
    return pltpu.CompilerParams(dimension_semantics=sem, vmem_limit_bytes=vmem)


def _const_spec(shape, single=False):
    nd = len(shape)
    if single:
        return pl.BlockSpec(shape, lambda *_: (0,) * nd, pipeline_mode=pl.Buffered(1))
    return pl.BlockSpec(shape, lambda *_: (0,) * nd)


def _layer_norm(z, g, b):
    mu = jnp.mean(z, -1, keepdims=True)
    d = z - mu
    var = jnp.mean(d * d, -1, keepdims=True)
    return d * lax.rsqrt(var + LN_EPS) * g + b


def _sigmoid(z):
    return 1.0 / (1.0 + jnp.exp(-z))


def _row_block_dot(x, w, rows=PROJ_ROWS):
    m = x.shape[0]
    starts = list(range(0, m - m % rows, rows)) or [0]
    ends = starts[1:] + [m]
    parts = [jnp.dot(x[a:b], w, preferred_element_type=F32) for a, b in zip(starts, ends)]
    return parts[0] if len(parts) == 1 else jnp.concatenate(parts, axis=0)


def _pipelined(sections):
    pending = sections[0][0]()
    for i, (_, epilogue) in enumerate(sections):
        upcoming = sections[i + 1][0]() if i + 1 < len(sections) else None
        epilogue(pending)
        pending = upcoming


def _rope_tables(seq):
    pos = jnp.arange(seq, dtype=F32)
    d = jnp.arange(LANES) % HEAD_DIM

    def angles(p, dim, theta):
        freqs = theta ** (-(jnp.arange(0, dim, 2, dtype=F32) / dim))
        return p[:, None] * freqs[None, :]

    h = ROPE_DIM // 2
    ang = angles(pos, ROPE_DIM, ROPE_THETA)
    a = ang[:, d % h]
    lo, hi = d < h, (d >= h) & (d < ROPE_DIM)
    pc = jnp.where(lo | hi, jnp.cos(a), 1.0)
    psa = jnp.where(lo, -jnp.sin(a), 0.0)
    psb = jnp.where(hi, jnp.sin(a), 0.0)

    q = HEAD_DIM // 4
    row = jnp.floor(pos / GRID_W)
    col = pos - row * GRID_W
    ar = angles(row, HEAD_DIM // 2, AXIAL_THETA)[:, d % q]
    ac_ = angles(col, HEAD_DIM // 2, AXIAL_THETA)[:, d % q]
    a2 = jnp.where(d < HEAD_DIM // 2, ar, ac_)
    first = (d % (HEAD_DIM // 2)) < q
    ac = jnp.cos(a2)
    asa = jnp.where(first, -jnp.sin(a2), 0.0)
    asb = jnp.where(first, 0.0, jnp.sin(a2))
    return [t.astype(F32) for t in (pc, psa, psb, ac, asa, asb)]


def _rope(seg, c, sa, sb, shift):
    return seg * c + pltpu.roll(seg, LANES - shift, 1) * sa + pltpu.roll(seg, shift, 1) * sb


def _ab_in_kernel(x_ref, w_ref, pc_ref, psa_ref, psb_ref, ac_ref, asa_ref, asb_ref, qg_ref, kg_ref,
                  ones_ref, qa_ref, ka_ref, va_ref, qb_ref, kd_ref, vd_ref, qa16_ref, ka16_ref, va16_ref,
                  qs_ref, ks_ref, vs_ref):
    xb = x_ref[...].astype(BF16)

    def proj(a, n):
        return jnp.dot(xb, w_ref[:, a:a + n], preferred_element_type=F32)

    pc, psa, psb = pc_ref[...], psa_ref[...], psb_ref[...]
    ac, asa, asb = ac_ref[...], asa_ref[...], asb_ref[...]

    def dilated_a(h, rope, out_ref, out16_ref, slab_ref):
        for g in range(A_W // LANES):
            sl = slice(g * LANES, (g + 1) * LANES)
            y = _rope(h[:, sl], pc, psa, psb, ROPE_DIM // 2) if rope else h[:, sl]
            out_ref[:, sl] = y.astype(BF16)
            slab_ref[g] = y
        for c in range(DIL_CLASSES):
            for g in range(A_W // LANES):
                rows = slab_ref[g, pl.ds(c, TM // DIL_CLASSES, stride=DIL_CLASSES), :]
                out16_ref[0, c, :, g * LANES:(g + 1) * LANES] = rows.astype(BF16)

    def norm_rope(h, gain, out_ref):
        for g in range(h.shape[1] // LANES):
            sl = slice(g * LANES, (g + 1) * LANES)
            seg = h[:, sl]
            sq = seg * seg
            hi = sq.astype(BF16)
            lo = (sq - hi.astype(F32)).astype(BF16)
            ss = (jnp.dot(hi, ones_ref[...], preferred_element_type=F32)
                  + jnp.dot(lo, ones_ref[...], preferred_element_type=F32))
            y = seg * lax.rsqrt(ss * (1.0 / HEAD_DIM) + RMS_EPS) * gain
            out_ref[:, sl] = _rope(y, ac, asa, asb, HEAD_DIM // 4).astype(BF16)

    def cast_to(out_ref):
        def fn(h):
            out_ref[...] = h.astype(BF16)
        return fn

    o_qb = 3 * A_W
    o_kd = o_qb + B_QW
    o_vd = o_kd + 2 * B_KVW
    _pipelined([
        (functools.partial(proj, 0, A_W),
         functools.partial(dilated_a, rope=True, out_ref=qa_ref, out16_ref=qa16_ref, slab_ref=qs_ref)),
        (functools.partial(proj, A_W, A_W),
         functools.partial(dilated_a, rope=True, out_ref=ka_ref, out16_ref=ka16_ref, slab_ref=ks_ref)),
        (functools.partial(proj, 2 * A_W, A_W),
         functools.partial(dilated_a, rope=False, out_ref=va_ref, out16_ref=va16_ref, slab_ref=vs_ref)),
        (functools.partial(proj, o_qb, B_QW), functools.partial(norm_rope, gain=qg_ref[...], out_ref=qb_ref)),
        (functools.partial(proj, o_kd, 2 * B_KVW), functools.partial(norm_rope, gain=kg_ref[...], out_ref=kd_ref)),
        (functools.partial(proj, o_vd, 2 * B_KVW), cast_to(vd_ref)),
    ])


def _ab_in(x2d, w_ext, tabs, qg, kg, ones2, batch, seq):
    t = x2d.shape[0]
    nt = seq // TM
    tab = pl.BlockSpec((TM, LANES), lambda i: (i % nt, 0))
    row = lambda n: pl.BlockSpec((TM, n), lambda i: (i, 0))
    cls = pl.BlockSpec((1, DIL_CLASSES, TM // DIL_CLASSES, A_W), lambda i: (i // nt, 0, i % nt, 0))
    cls_shape = jax.ShapeDtypeStruct((batch, DIL_CLASSES, seq // DIL_CLASSES, A_W), BF16)
    wn = w_ext.shape[1]
    return pl.pallas_call(
        _ab_in_kernel,
        grid=(t // TM,),
        in_specs=[row(D_MODEL), _const_spec((D_MODEL, wn))] + [tab] * 6
        + [_const_spec((1, LANES)), _const_spec((1, LANES)), _const_spec((LANES, LANES))],
        out_specs=[row(A_W)] * 3 + [row(B_QW), row(2 * B_KVW), row(2 * B_KVW)] + [cls] * 3,
        out_shape=[jax.ShapeDtypeStruct((t, A_W), BF16)] * 3
        + [jax.ShapeDtypeStruct((t, B_QW), BF16)]
        + [jax.ShapeDtypeStruct((t, 2 * B_KVW), BF16)] * 2 + [cls_shape] * 3,
        scratch_shapes=[pltpu.VMEM((A_W // LANES, TM, LANES), F32)] * 3,
        compiler_params=_params(("arbitrary",), VMEM_LIMIT),
        name="ab_in",
    )(x2d, w_ext, *tabs, qg, kg, ones2)


def _band_groups(qs, ks, vs, bias, left):
    chains = [(g, sel) for g in range(len(qs)) for sel in (left, jnp.logical_not(left))]
    scores = []
    for g, sel in chains:
        qm = jnp.where(sel, qs[g], jnp.zeros_like(qs[g]))
        scores.append(lax.dot_general(qm, ks[g], _NT, preferred_element_type=F32) + bias)
    probs = []
    for s in scores:
        mx = jnp.max(s, axis=1, keepdims=True)
        pe = jnp.exp(s - mx)
        probs.append((mx, jnp.sum(pe, axis=1, keepdims=True), pe.astype(BF16)))
    outs = []
    for (g, _), (mx, l, pe) in zip(chains, probs):
        o = jnp.dot(pe, vs[g], preferred_element_type=F32)
        outs.append((o * (1.0 / l), mx + jnp.log(l)))
    return [(jnp.where(left, outs[2 * g][0], outs[2 * g + 1][0]), jnp.where(left, outs[2 * g][1], outs[2 * g + 1][1]))
            for g in range(len(qs))]


def _band_kernel(q_ref, kp_ref, kc_ref, kn_ref, vp_ref, vc_ref, vn_ref, o_ref, lse_ref, *, m, step_rows):
    nk = BAND_L + 2 * HALF_WINDOW
    base = pl.program_id(len(q_ref.shape) - 2) * step_rows
    r = lax.broadcasted_iota(jnp.int32, (BAND_L, nk), 0)
    j = lax.broadcasted_iota(jnp.int32, (BAND_L, nk), 1)
    band = jnp.where(jnp.abs(j - HALF_WINDOW - r) <= HALF_WINDOW, 0.0, NEG)
    left = lax.broadcasted_iota(jnp.int32, (BAND_L, LANES), 1) < HEAD_DIM
    pre = (0,) * (len(q_ref.shape) - 2)
    lanes = [slice(p * LANES, (p + 1) * LANES) for p in range(A_W // LANES)]
    at = lambda rows, sl: pre + (rows, sl)
    whole = slice(None)
    ks = [jnp.concatenate([kp_ref[at(whole, sl)], kc_ref[at(whole, sl)], kn_ref[at(whole, sl)]], axis=0) for sl in lanes]
    vs = [jnp.concatenate([vp_ref[at(whole, sl)], vc_ref[at(whole, sl)], vn_ref[at(whole, sl)]], axis=0) for sl in lanes]
    for blk in range(step_rows // BAND_L):
        r0 = blk * BAND_L
        rows = slice(r0, r0 + BAND_L)
        kpos = base + r0 - HALF_WINDOW + j
        bias = jnp.where(kpos >= 0, band, NEG)
        bias = jnp.where(kpos < m, bias, NEG)
        res = _band_groups([q_ref[at(rows, sl)] for sl in lanes], [kk[r0:r0 + nk] for kk in ks],
                           [vv[r0:r0 + nk] for vv in vs], bias, left)
        for sl, (o, lse) in zip(lanes, res):
            o_ref[at(rows, sl)] = o.astype(BF16)
            lse_ref[at(rows, sl)] = lse


def _band_specs(lead, rows, step_rows):
    nh = rows // HALF_WINDOW
    ratio = step_rows // HALF_WINDOW
    ones = (1,) * lead
    cur = pl.BlockSpec(ones + (step_rows, A_W), lambda *g: g[:lead] + (g[-1], 0))
    prev = pl.BlockSpec(ones + (HALF_WINDOW, A_W), lambda *g: g[:lead] + (jnp.maximum(g[-1] * ratio - 1, 0), 0))
    nxt = pl.BlockSpec(ones + (HALF_WINDOW, A_W), lambda *g: g[:lead] + (jnp.minimum((g[-1] + 1) * ratio, nh - 1), 0))
    return cur, prev, nxt


def _band1(q, k, v, batch, seq):
    step_rows = min(BAND_STEP, seq)
    cur, prev, nxt = _band_specs(1, seq, step_rows)
    view = lambda t: t.reshape(batch, seq, A_W)
    o, lse = pl.pallas_call(
        functools.partial(_band_kernel, m=seq, step_rows=step_rows),
        grid=(batch, seq // step_rows),
        in_specs=[cur, prev, cur, nxt, prev, cur, nxt],
        out_specs=[cur, cur],
        out_shape=[jax.ShapeDtypeStruct((batch, seq, A_W), BF16), jax.ShapeDtypeStruct((batch, seq, A_W), F32)],
        compiler_params=_params(("arbitrary",) * 2),
        name="band_d1",
    )(view(q), view(k), view(k), view(k), view(v), view(v), view(v))
    return o.reshape(batch * seq, A_W), lse.reshape(batch * seq, A_W)


def _band16(q16, k16, v16, batch, seq):
    m = seq // DIL_CLASSES
    step_rows = min(BAND_STEP, m)
    cur, prev, nxt = _band_specs(2, m, step_rows)
    shape = (batch, DIL_CLASSES, m, A_W)
    return pl.pallas_call(
        functools.partial(_band_kernel, m=m, step_rows=step_rows),
        grid=(batch, DIL_CLASSES, m // step_rows),
        in_specs=[cur, prev, cur, nxt, prev, cur, nxt],
        out_specs=[cur, cur],
        out_shape=[jax.ShapeDtypeStruct(shape, BF16), jax.ShapeDtypeStruct(shape, F32)],
        compiler_params=_params(("arbitrary",) * 3),
        name="band_d16",
    )(q16, k16, k16, k16, v16, v16, v16)


def _band4_kernel(q_ref, kp_ref, kc_ref, kn_ref, vp_ref, vc_ref, vn_ref, o16_ref, l16_ref, o_ref, lse_ref, *,
                  m16, step_rows):
    qr = BAND4_ROWS
    kr = qr + 2 * BAND4_HALO
    i = pl.program_id(2)
    shift_q, shift_k = qr.bit_length() - 1, kr.bit_length() - 1
    row = lax.broadcasted_iota(jnp.int32, (4 * qr, 4 * kr), 0)
    col = lax.broadcasted_iota(jnp.int32, (4 * qr, 4 * kr), 1)
    uq, rq = row >> shift_q, row & (qr - 1)
    uk, jk = col >> shift_k, col & (kr - 1)
    band = jnp.where(jnp.abs(4 * (jk - BAND4_HALO - rq) + (uk - uq)) <= HALF_WINDOW, 0.0, NEG)
    left = lax.broadcasted_iota(jnp.int32, (4 * qr, LANES), 1) < HEAD_DIM
    slabs = [slice(p * LANES, (p + 1) * LANES) for p in range(A_W // LANES)]
    window = lambda refs, u, sl: jnp.concatenate([r[0, u, 0, :, sl] for r in refs], axis=0)
    kwin = [[window([kp_ref, kc_ref, kn_ref], u, sl) for u in range(4)] for sl in slabs]
    vwin = [[window([vp_ref, vc_ref, vn_ref], u, sl) for u in range(4)] for sl in slabs]
    for blk in range(step_rows // qr):
        r0 = blk * qr
        rows = slice(r0, r0 + qr)
        kpos = i * step_rows + r0 - BAND4_HALO + jk
        bias = jnp.where(kpos >= 0, band, NEG)
        bias = jnp.where(kpos < m16, bias, NEG)
        classes = lambda ref, sl: jnp.concatenate([ref[0, u, 0, rows, sl] for u in range(4)], axis=0)
        branch4 = _band_groups([classes(q_ref, sl) for sl in slabs],
                               [jnp.concatenate([w[r0:r0 + kr] for w in kw], axis=0) for kw in kwin],
                               [jnp.concatenate([w[r0:r0 + kr] for w in vw], axis=0) for vw in vwin], bias, left)
        for sl, (o4, l4) in zip(slabs, branch4):
            o16 = classes(o16_ref, sl).astype(F32)
            l16 = classes(l16_ref, sl)
            mx = jnp.maximum(l4, l16)
            w4, w16 = jnp.exp(l4 - mx), jnp.exp(l16 - mx)
            tot = w4 + w16
            om = (w4 * o4 + w16 * o16) * (1.0 / tot)
            lm = mx + jnp.log(tot)
            for u in range(4):
                o_ref[0, u, 0, rows, sl] = om[u * qr:(u + 1) * qr].astype(BF16)
                lse_ref[0, u, 0, rows, sl] = lm[u * qr:(u + 1) * qr]


def _band4(q16, k16, v16, o16, l16, batch, seq):
    m16 = seq // DIL_CLASSES
    step_rows = min(BAND4_STEP, m16)
    ratio = step_rows // BAND4_HALO
    nh = m16 // BAND4_HALO
    view = lambda t: t.reshape(batch, 4, 4, m16, A_W)
    cur = pl.BlockSpec((1, 4, 1, step_rows, A_W), lambda b, c, i: (b, 0, c, i, 0))
    prev = pl.BlockSpec((1, 4, 1, BAND4_HALO, A_W), lambda b, c, i: (b, 0, c, jnp.maximum(i * ratio - 1, 0), 0))
    nxt = pl.BlockSpec((1, 4, 1, BAND4_HALO, A_W), lambda b, c, i: (b, 0, c, jnp.minimum((i + 1) * ratio, nh - 1), 0))
    shape = (batch, 4, 4, m16, A_W)
    o, lse = pl.pallas_call(
        functools.partial(_band4_kernel, m16=m16, step_rows=step_rows),
        grid=(batch, 4, m16 // step_rows),
        in_specs=[cur, prev, cur, nxt, prev, cur, nxt, cur, cur],
        out_specs=[cur, cur],
        out_shape=[jax.ShapeDtypeStruct(shape, BF16), jax.ShapeDtypeStruct(shape, F32)],
        compiler_params=_params(("arbitrary",) * 3),
        name="band_d4",
    )(view(q16), view(k16), view(k16), view(k16), view(v16), view(v16), view(v16), view(o16), view(l16))
    return o.reshape(batch, DIL_CLASSES, m16, A_W), lse.reshape(batch, DIL_CLASSES, m16, A_W)


def _gqa_kernel(q_ref, k_ref, v_ref, o_ref, qs_ref, m_ref, acc_ref, s_ref, vt_ref, *, seq):
    left = lax.broadcasted_iota(jnp.int32, (GQA_TQ, LANES), 1) < HEAD_DIM
    top = lax.broadcasted_iota(jnp.int32, (LANES, GQA_TQ), 0) < HEAD_DIM
    for pr in range(2):
        qt = q_ref[0, :, pr * LANES:(pr + 1) * LANES].astype(F32).T
        qs_ref[:, (2 * pr) * GQA_TQ:(2 * pr + 1) * GQA_TQ] = jnp.where(top, qt, 0.0).astype(BF16)
        qs_ref[:, (2 * pr + 1) * GQA_TQ:(2 * pr + 2) * GQA_TQ] = jnp.where(top, 0.0, qt).astype(BF16)
    m_ref[...] = jnp.full(m_ref.shape, NEG, F32)
    acc_ref[...] = jnp.zeros(acc_ref.shape, F32)
    n_tiles = seq // GQA_TK

    @pl.when(pl.program_id(2) == 0)
    def _():
        ones_row = lax.broadcasted_iota(jnp.int32, (LANES, GQA_TK), 0) == HEAD_DIM

        def transpose_tile(t, carry):
            off = pl.multiple_of(t * GQA_TK, GQA_TK)
            vt = v_ref[0, pl.ds(off, GQA_TK), :].astype(F32).T
            vt_ref[t] = jnp.where(ones_row, 1.0, vt).astype(BF16)
            return carry

        lax.fori_loop(0, n_tiles, transpose_tile, 0)

    chains = [slice(c * GQA_CHAIN, (c + 1) * GQA_CHAIN) for c in range(4 * GQA_TQ // GQA_CHAIN)]

    def scores(tile, buf, cols):
        off = pl.multiple_of(tile * GQA_TK, GQA_TK)
        k = k_ref[0, pl.ds(off, GQA_TK), :]
        s_ref[buf, :, cols] = jnp.dot(k, qs_ref[:, cols], preferred_element_type=F32)

    def softmax_pv(tile, buf, cols):
        s = s_ref[buf, :, cols]
        m_prev = m_ref[:, cols]
        m_new = jnp.maximum(m_prev, jnp.max(s, axis=0, keepdims=True))
        alpha = jnp.exp2(m_prev - m_new)
        p = jnp.exp2((s - m_new).astype(BF16))
        pv = jnp.dot(vt_ref[tile], p, preferred_element_type=F32)
        acc_ref[:, cols] = alpha * acc_ref[:, cols] + pv
        m_ref[:, cols] = m_new

    def step(tile, buf, prefetch):
        for cols in chains:
            if prefetch:
                scores(tile + 1, 1 - buf, cols)
            softmax_pv(tile, buf, cols)

    def pair(j, carry):
        step(2 * j, 0, True)
        step(2 * j + 1, 1, True)
        return carry

    for cols in chains:
        scores(0, 0, cols)
    lax.fori_loop(0, n_tiles // 2 - 1, pair, 0)
    step(n_tiles - 2, 0, True)
    step(n_tiles - 1, 1, False)
    acc = acc_ref[...]
    a = (acc * (1.0 / acc[HEAD_DIM:HEAD_DIM + 1, :])).T
    for pr in range(2):
        lo = a[(2 * pr) * GQA_TQ:(2 * pr + 1) * GQA_TQ]
        hi = pltpu.roll(a[(2 * pr + 1) * GQA_TQ:(2 * pr + 2) * GQA_TQ], HEAD_DIM, 1)
        o_ref[0, :, pr * LANES:(pr + 1) * LANES] = jnp.where(left, lo, hi).astype(BF16)


def _gqa(qb, kd, vd, batch, seq):
    gw = B_QW // B_KV_HEADS
    cols = 4 * GQA_TQ
    out = pl.pallas_call(
        functools.partial(_gqa_kernel, seq=seq),
        grid=(batch, B_KV_HEADS, seq // GQA_TQ),
        in_specs=[pl.BlockSpec((1, GQA_TQ, gw), lambda b, g, qi: (b, qi, g)),
                  pl.BlockSpec((1, seq, LANES), lambda b, g, qi: (b, 0, g)),
                  pl.BlockSpec((1, seq, LANES), lambda b, g, qi: (b, 0, g))],
        out_specs=pl.BlockSpec((1, GQA_TQ, gw), lambda b, g, qi: (b, qi, g)),
        out_shape=jax.ShapeDtypeStruct((batch, seq, B_QW), BF16),
        scratch_shapes=[pltpu.VMEM((LANES, cols), BF16), pltpu.VMEM((1, cols), F32),
                        pltpu.VMEM((LANES, cols), F32), pltpu.VMEM((2, GQA_TK, cols), F32),
                        pltpu.VMEM((seq // GQA_TK, LANES, GQA_TK), BF16)],
        compiler_params=_params(("arbitrary",) * 3, VMEM_LIMIT),
        name="gqa",
    )(qb.reshape(batch, seq, B_QW), kd.reshape(batch, seq, 2 * B_KVW), vd.reshape(batch, seq, 2 * B_KVW))
    return out.reshape(batch * seq, B_QW)


def _ab_out_kernel(o1_ref, l1_ref, om_ref, lm_ref, yb_ref, x_ref, w_ref, g_ref, b_ref, out_ref, os_ref, ls_ref):
    for c in range(DIL_CLASSES):
        for g in range(A_W // LANES):
            sl = slice(g * LANES, (g + 1) * LANES)
            rows = pl.ds(c, TM // DIL_CLASSES, stride=DIL_CLASSES)
            os_ref[g, rows, :] = om_ref[0, c, :, sl].astype(F32)
            ls_ref[g, rows, :] = lm_ref[0, c, :, sl]
    ya = []
    for g in range(A_W // LANES):
        sl = slice(g * LANES, (g + 1) * LANES)
        l1, lm = l1_ref[:, sl], ls_ref[g]
        mx = jnp.maximum(l1, lm)
        e1, em = jnp.exp(l1 - mx), jnp.exp(lm - mx)
        ya.append(((e1 * o1_ref[:, sl].astype(F32) + em * os_ref[g]) * (1.0 / (e1 + em))).astype(BF16))
    y = _row_block_dot(jnp.concatenate(ya + [yb_ref[...]], axis=1), w_ref[...])
    out_ref[...] = _layer_norm(ALPHA * x_ref[...] + y, g_ref[...], b_ref[...])


def _ab_out(o1, l1, om, lm, yb, x2d, w, g, b, seq):
    t = x2d.shape[0]
    nt = seq // TM
    row = lambda n: pl.BlockSpec((TM, n), lambda i: (i, 0))
    cls = pl.BlockSpec((1, DIL_CLASSES, TM // DIL_CLASSES, A_W), lambda i: (i // nt, 0, i % nt, 0))
    return pl.pallas_call(
        _ab_out_kernel,
        grid=(t // TM,),
        in_specs=[row(A_W), row(A_W), cls, cls, row(B_QW), row(D_MODEL), _const_spec((A_W + B_QW, D_MODEL)),
                  _const_spec((1, D_MODEL)), _const_spec((1, D_MODEL))],
        out_specs=row(D_MODEL),
        out_shape=jax.ShapeDtypeStruct((t, D_MODEL), F32),
        scratch_shapes=[pltpu.VMEM((A_W // LANES, TM, LANES), F32)] * 2,
        compiler_params=_params(("arbitrary",), VMEM_LIMIT),
        name="ab_out",
    )(o1, l1, om, lm, yb, x2d, w, g, b)


def _gelu(x):
    return 0.5 * x * (1.0 + jnp.tanh(0.7978845608028654 * (x + 0.044715 * (x * x * x))))


def _ffn_kernel(x_ref, xp_ref, xn_ref, wu_ref, cw_ref, cb_ref, wd_ref, g_ref, b_ref, out_ref, *, nt):
    i = pl.program_id(0)
    keep_prev = (i % nt != 0).astype(F32)
    keep_next = (i % nt != nt - 1).astype(F32)
    rows = lax.broadcasted_iota(jnp.int32, (TM, 1), 0)
    offsets = [sum(FF_CHUNKS[:c]) for c in range(len(FF_CHUNKS))]
    operands, acc = {}, {}

    def tile_operands(j):
        if j not in operands:
            r0 = j * TM
            xb = x_ref[r0:r0 + TM, :].astype(BF16)
            before = xp_ref[...] if j == 0 else x_ref[r0 - 8:r0, :]
            after = xn_ref[...] if j == FFN_SUB - 1 else x_ref[r0 + TM:r0 + TM + 8, :]
            halo = jnp.concatenate([before, after], axis=0).astype(BF16)
            operands[j] = (xb, jnp.concatenate([xb, halo], axis=0))
        return operands[j]

    def up(j, c):
        a, n = offsets[c], FF_CHUNKS[c]
        xb, xe = tile_operands(j)
        u = _row_block_dot(xb, wu_ref[:, a:a + n], FFN_ROWS)
        ge = _row_block_dot(xe, wu_ref[:, D_FF + a:D_FF + a + n], FFN_ROWS)
        return u, ge

    def finish(j, c, pending):
        u, ge = pending
        a, n = offsets[c], FF_CHUNKS[c]
        gm = ge[0:TM]
        g_before = ge[TM + 7:TM + 8]
        g_after = ge[TM + 8:TM + 9]
        if j == 0:
            g_before = g_before * keep_prev
        if j == FFN_SUB - 1:
            g_after = g_after * keep_next
        gp = jnp.where(rows == 0, g_before, pltpu.roll(gm, 1, 0))
        gn = jnp.where(rows == TM - 1, g_after, pltpu.roll(gm, TM - 1, 0))
        cw = cw_ref[:, a:a + n]
        gc = gp * cw[0:1] + gm * cw[1:2] + gn * cw[2:3] + cb_ref[:, a:a + n]
        act = (_gelu(gc) * u).astype(BF16)
        part = _row_block_dot(act, wd_ref[a:a + n, :], FFN_ROWS)
        acc[j] = part if c == 0 else acc[j] + part
        if c == len(FF_CHUNKS) - 1:
            sl = slice(j * TM, (j + 1) * TM)
            out_ref[sl, :] = _layer_norm(ALPHA * x_ref[sl, :] + acc[j], g_ref[...], b_ref[...])

    _pipelined([(functools.partial(up, j, c), functools.partial(finish, j, c))
                for j in range(FFN_SUB) for c in range(len(FF_CHUNKS))])


def _ffn(x2d, wu, cw, cb, wd, g, b, seq):
    t = x2d.shape[0]
    step = FFN_SUB * TM
    nt = seq // step
    r8 = step // 8
    return pl.pallas_call(
        functools.partial(_ffn_kernel, nt=nt),
        grid=(t // step,),
        in_specs=[pl.BlockSpec((step, D_MODEL), lambda i: (i, 0)),
                  pl.BlockSpec((8, D_MODEL), lambda i: (jnp.maximum(i * r8 - 1, 0), 0)),
                  pl.BlockSpec((8, D_MODEL), lambda i: (jnp.minimum((i + 1) * r8, t // 8 - 1), 0)),
                  _const_spec((D_MODEL, 2 * D_FF), single=True),
                  _const_spec((3, D_FF)), _const_spec((1, D_FF)),
                  _const_spec((D_FF, D_MODEL), single=True),
                  _const_spec((1, D_MODEL)), _const_spec((1, D_MODEL))],
        out_specs=pl.BlockSpec((step, D_MODEL), lambda i: (i, 0)),
        out_shape=jax.ShapeDtypeStruct((t, D_MODEL), F32),
        compiler_params=_params(("arbitrary",), VMEM_LIMIT),
        name="ffn",
    )(x2d, x2d, x2d, wu, cw, cb, wd, g, b)


def _lower_bound(tbl, layer):
    rows = [tbl[r:r + 1] for r in range(DEPTH)]
    mx = functools.reduce(jnp.maximum, rows)
    es = [jnp.exp(r - mx) for r in rows]
    inv = 1.0 / functools.reduce(lambda a, b: a + b, es)
    ps = [e * inv for e in es]
    return functools.reduce(lambda a, b: a + b, ps[:layer + 1]) - ps[0]


def _c_in_kernel(x_ref, w_ref, lbf_ref, lbb_ref, q_ref, lf_ref, lb_ref, v_ref, g_ref, *, layer):
    half = C_W // 2
    blocks = [slice(r * PROJ_ROWS, (r + 1) * PROJ_ROWS) for r in range(TM // PROJ_ROWS)]
    xbs = [x_ref[rows, :].astype(BF16) for rows in blocks]

    def proj(r, a, n):
        return jnp.dot(xbs[r], w_ref[:, a:a + n], preferred_element_type=F32)

    def silu(z):
        return z * _sigmoid(z)

    lbf = _lower_bound(lbf_ref[...], layer)
    lbb = _lower_bound(lbb_ref[...], layer)

    def store_q(z, rows, sl):
        q_ref[rows, sl] = z.astype(BF16)

    def store_logf(tbl, out_ref):
        def fn(z, rows, sl):
            lb = tbl[:, sl]
            out_ref[rows, sl] = jnp.log(lb + (1.0 - lb) * _sigmoid(z))
        return fn

    def store_silu(out_ref):
        def fn(z, rows, sl):
            out_ref[rows, sl] = silu(z).astype(BF16)
        return fn

    sections = []
    for r, rows in enumerate(blocks):
        for c in range(2):
            a = c * half
            sl = slice(a, a + half)
            for off, fn in ((0, store_q), (C_W, store_logf(lbf, lf_ref)), (2 * C_W, store_logf(lbb, lb_ref)),
                            (3 * C_W, store_silu(v_ref)), (3 * C_W + C_VW, store_silu(g_ref))):
                sections.append((functools.partial(proj, r, off + a, half), functools.partial(fn, rows=rows, sl=sl)))
    _pipelined(sections)


def _c_in(x2d, w, lbf, lbb, layer):
    t = x2d.shape[0]
    row = pl.BlockSpec((TM, C_W), lambda i: (i, 0))
    return pl.pallas_call(
        functools.partial(_c_in_kernel, layer=layer),
        grid=(t // TM,),
        in_specs=[row, _const_spec(w.shape), _const_spec((DEPTH, C_W)), _const_spec((DEPTH, C_W))],
        out_specs=[row] * 5,
        out_shape=[jax.ShapeDtypeStruct((t, C_W), d) for d in (BF16, F32, F32, BF16, BF16)],
        compiler_params=_params(("arbitrary",), VMEM_LIMIT),
        name="c_in",
    )(x2d, w, lbf, lbb)


def _hgrn_kernel(*refs, rev, fused, heads, step_rows):
    if fused:
        (q_ref, lf_ref, v_ref, of_ref, gate_ref, gn_ref, x_ref, w_ref, lg_ref, lb_ref,
         out_ref, st_ref, y_ref) = refs
    else:
        q_ref, lf_ref, v_ref, y_ref, st_ref = refs

    @pl.when(pl.program_id(2) == 0)
    def _():
        st_ref[...] = jnp.zeros(st_ref.shape, F32)

    ri = lax.broadcasted_iota(jnp.int32, (HG_CHUNK, HG_CHUNK), 0)
    ci = lax.broadcasted_iota(jnp.int32, (HG_CHUNK, HG_CHUNK), 1)
    tri = (ci >= ri) if rev else (ci <= ri)
    trib = jnp.where(tri, 1.0, 0.0).astype(BF16)
    dotf = lambda a, b: jnp.dot(a, b, preferred_element_type=F32)
    n_chunks = step_rows // HG_CHUNK
    order = range(n_chunks - 1, -1, -1) if rev else range(n_chunks)
    units = [(c, h) for c in order for h in range(heads)]
    window = lambda u: (0, slice(u[0] * HG_CHUNK, (u[0] + 1) * HG_CHUNK), slice(u[1] * C_KEY, (u[1] + 1) * C_KEY))

    b_all = {}
    for u in units:
        lf = lf_ref[window(u)]
        hi = lf.astype(BF16)
        lo = (lf - hi.astype(F32)).astype(BF16)
        b_all[u] = dotf(trib, hi) + dotf(trib, lo)

    qdec, kdec, etot, a_rows = {}, {}, {}, {}
    for u in units:
        lf, b = lf_ref[window(u)], b_all[u]
        qf = q_ref[window(u)].astype(F32)
        bex = b - lf
        kk = 1.0 - jnp.exp(lf)
        btot = b[0:1] if rev else b[HG_CHUNK - 1:HG_CHUNK]
        qdec[u] = (qf * jnp.exp(b)).astype(BF16)
        kdec[u] = (kk * jnp.exp(btot - b)).astype(BF16)
        etot[u] = jnp.exp(btot)
        rows = []
        for blk in range(HG_CHUNK // HG_SUB):
            r0 = blk * HG_SUB
            ref = bex[r0 + HG_SUB - 1:r0 + HG_SUB] if rev else bex[r0:r0 + 1]
            qt = (qf[r0:r0 + HG_SUB] * jnp.exp(b[r0:r0 + HG_SUB] - ref)).astype(BF16)
            lo_r, hi_r = (r0, HG_CHUNK) if rev else (0, r0 + HG_SUB)
            ks = (kk[lo_r:hi_r] * jnp.exp(ref - b[lo_r:hi_r])).astype(BF16)
            pieces = []
            if lo_r > 0:
                pieces.append(jnp.zeros((lo_r, C_KEY), BF16))
            pieces.append(ks)
            if hi_r < HG_CHUNK:
                pieces.append(jnp.zeros((HG_CHUNK - hi_r, C_KEY), BF16))
            kfull = jnp.concatenate(pieces, axis=0) if len(pieces) > 1 else ks
            rows.append(lax.dot_general(qt, kfull, _NT, preferred_element_type=F32))
        a_rows[u] = rows
    attn = {u: jnp.where(tri, jnp.concatenate(a_rows[u], axis=0), 0.0).astype(BF16) for u in units}

    states = [st_ref[h] for h in range(heads)]
    for u in units:
        h = u[1]
        vb = v_ref[window(u)]
        o = lax.dot_general(qdec[u], states[h].astype(BF16), _NT, preferred_element_type=F32)
        o = o + dotf(attn[u], vb)
        states[h] = states[h] * etot[u] + lax.dot_general(vb, kdec[u], _TN, preferred_element_type=F32)
        if fused:
            tot = of_ref[window(u)] + o
            inv = lax.rsqrt(jnp.mean(tot * tot, -1, keepdims=True) + RMS_EPS)
            y = tot * inv * gn_ref[:, window(u)[2]] * gate_ref[window(u)].astype(F32)
            y_ref[window(u)[1:]] = y.astype(BF16)
        else:
            y_ref[window(u)] = o
    for h in range(heads):
        st_ref[h] = states[h]
    if fused:
        proj = _row_block_dot(y_ref[...], w_ref[...])
        out_ref[0] = _layer_norm(ALPHA * x_ref[0] + proj, lg_ref[...], lb_ref[...])


def _hgrn_specs(batch, seq, rev, heads, step_rows):
    nb = seq // step_rows
    blk = (lambda b, h, i: (b, nb - 1 - i, h)) if rev else (lambda b, h, i: (b, i, h))
    return pl.BlockSpec((1, step_rows, heads * C_KEY), blk), (batch, C_HEADS // heads, nb)


def _hgrn_fwd(q, lf, v, batch, seq):
    tile, grid = _hgrn_specs(batch, seq, False, HG_HP, HG_TB)
    view = lambda t: t.reshape(batch, seq, C_W)
    out = pl.pallas_call(
        functools.partial(_hgrn_kernel, rev=False, fused=False, heads=HG_HP, step_rows=HG_TB),
        grid=grid,
        in_specs=[tile, tile, tile],
        out_specs=tile,
        out_shape=jax.ShapeDtypeStruct((batch, seq, C_VW), F32),
        scratch_shapes=[pltpu.VMEM((HG_HP, C_VAL, C_KEY), F32)],
        compiler_params=_params(("arbitrary",) * 3),
        name="hgrn_fwd",
    )(view(q), view(lf), view(v))
    return out.reshape(batch * seq, C_VW)


def _hgrn_bwd_out(q, lf, v, o_f, gate, gn, x2d, w, g, b, batch, seq):
    tile, grid = _hgrn_specs(batch, seq, True, C_HEADS, HG_TB_OUT)
    view = lambda t: t.reshape(batch, seq, C_W)
    out = pl.pallas_call(
        functools.partial(_hgrn_kernel, rev=True, fused=True, heads=C_HEADS, step_rows=HG_TB_OUT),
        grid=grid,
        in_specs=[tile] * 5 + [_const_spec((1, C_VW)), tile, _const_spec((C_VW, D_MODEL), single=True),
                               _const_spec((1, D_MODEL)), _const_spec((1, D_MODEL))],
        out_specs=tile,
        out_shape=jax.ShapeDtypeStruct((batch, seq, D_MODEL), F32),
        scratch_shapes=[pltpu.VMEM((C_HEADS, C_VAL, C_KEY), F32), pltpu.VMEM((HG_TB_OUT, C_VW), BF16)],
        compiler_params=_params(("arbitrary",) * 3, VMEM_LIMIT),
        name="hgrn_bwd_out",
    )(view(q), view(lf), view(v), view(o_f), view(gate), gn, x2d.reshape(batch, seq, D_MODEL), w, g, b)
    return out.reshape(batch * seq, D_MODEL)


def _prep_weights(w_in_ab, w_out_ab, qn_ab, kn_ab, w_in_c, w_out_c, gn_c, ffn_w_up, ffn_w_down):
    scale = HEAD_DIM ** -0.5
    ab = []
    for j in range(w_in_ab.shape[0]):
        w = w_in_ab[j]
        o3 = 3 * A_W
        kb = w[:, o3 + B_QW:o3 + B_QW + B_KVW]
        vb = w[:, o3 + B_QW + B_KVW:]
        dup = lambda t: jnp.concatenate([t[:, :HEAD_DIM], t[:, :HEAD_DIM], t[:, HEAD_DIM:], t[:, HEAD_DIM:]], 1)
        w_ext = jnp.concatenate([w[:, :A_W] * scale, w[:, A_W:o3 + B_QW], dup(kb), dup(vb)], 1).astype(BF16)
        qg = (jnp.concatenate([qn_ab[j], qn_ab[j]]) * (scale * LOG2E)).reshape(1, LANES)
        kg = jnp.concatenate([kn_ab[j], kn_ab[j]]).reshape(1, LANES)
        ab.append((w_ext, qg, kg, w_out_ab[j].astype(BF16)))
    cc = [(w_in_c[j].astype(BF16), w_out_c[j].astype(BF16), gn_c[j].reshape(1, C_VW)) for j in range(w_in_c.shape[0])]
    return ab, cc, ffn_w_up.astype(BF16), ffn_w_down.astype(BF16)


def _trunk(x, prep, lb_fwd, lb_bwd, ln_mix_g, ln_mix_b, ln_ffn_g, ln_ffn_b, ffn_conv_w, ffn_conv_b):
    ab, cc, wu, wd = prep
    batch, seq, _ = x.shape
    x2d = x.reshape(batch * seq, D_MODEL)
    tabs = _rope_tables(seq)
    blk = jnp.arange(LANES) // HEAD_DIM
    ones2 = (blk[:, None] == blk[None, :]).astype(BF16)
    vec = lambda t: t.reshape(1, -1)
    for l in range(DEPTH):
        j = l // 2
        if l % 2 == 0:
            w_ext, qg, kg, w_out = ab[j]
            assert [d for _, d in A_PATTERNS] == [1, 4, DIL_CLASSES]
            assert all(window // (2 * d) == HALF_WINDOW for window, d in A_PATTERNS)
            qa, ka, va, qb, kd, vd, qa16, ka16, va16 = _ab_in(x2d, w_ext, tabs, qg, kg, ones2, batch, seq)
            o1, l1 = _band1(qa, ka, va, batch, seq)
            o16, l16 = _band16(qa16, ka16, va16, batch, seq)
            om, lm = _band4(qa16, ka16, va16, o16, l16, batch, seq)
            yb = _gqa(qb, kd, vd, batch, seq)
            x2d = _ab_out(o1, l1, om, lm, yb, x2d, w_out, vec(ln_mix_g[l]), vec(ln_mix_b[l]), seq)
        else:
            w_in, w_out, gn = cc[j]
            q, lf, lb, v, gate = _c_in(x2d, w_in, lb_fwd, lb_bwd, l)
            o_f = _hgrn_fwd(q, lf, v, batch, seq)
            x2d = _hgrn_bwd_out(q, lb, v, o_f, gate, gn, x2d, w_out, vec(ln_mix_g[l]), vec(ln_mix_b[l]), batch, seq)
        x2d = _ffn(x2d, wu[l], ffn_conv_w[l], vec(ffn_conv_b[l]), wd[l], vec(ln_ffn_g[l]), vec(ln_ffn_b[l]), seq)
    return x2d.reshape(batch, seq, D_MODEL)


def kernel(x_prompt, x_sample, w_in_ab, w_out_ab, qn_ab, kn_ab, w_in_c, w_out_c, lb_fwd, lb_bwd, gn_c, ln_mix_g, ln_mix_b, ln_ffn_g, ln_ffn_b, ffn_w_up, ffn_conv_w, ffn_conv_b, ffn_w_down):
    prep = _prep_weights(w_in_ab, w_out_ab, qn_ab, kn_ab, w_in_c, w_out_c, gn_c, ffn_w_up, ffn_w_down)
    rest = (lb_fwd, lb_bwd, ln_mix_g, ln_mix_b, ln_ffn_g, ln_ffn_b, ffn_conv_w, ffn_conv_b)
    return (_trunk(x_prompt, prep, *rest), _trunk(x_sample, prep, *rest))
```

```python
import functools

import jax
import jax.numpy as jnp
from jax import lax
from jax.experimental import pallas as pl
from jax.experimental.pallas import tpu as pltpu

F32 = jnp.float32
BF16 = jnp.bfloat16

D_MODEL = 1024
DEPTH = 2
HEAD_DIM = 64
A_HEADS = 8
A_PATTERNS = ((128, 1), (512, 4), (2048, 16))
B_HEADS = 8
B_KV_HEADS = 2
GRID_W = 64
ROPE_THETA = 500000.0
ROPE_DIM = HEAD_DIM // 4
AXIAL_THETA = 10000.0
C_HEADS = 8
C_KEY = 128
C_VAL = 128
D_FF = 2816
ALPHA = (2 * DEPTH) ** 0.25
LN_EPS = 1e-5
RMS_EPS = 1e-6
A_W = A_HEADS * HEAD_DIM
B_QW = B_HEADS * HEAD_DIM
B_KVW = B_KV_HEADS * HEAD_DIM
C_W = C_HEADS * C_KEY
C_VW = C_HEADS * C_VAL

LANES = 128
HALF_WINDOW = 64
TM = 512
PROJ_ROWS = 128
FFN_ROWS = 256
BAND_L = 128
BAND_STEP = 512
BAND4_STEP = 128
DIL_CLASSES = 16
BAND4_ROWS = 32
BAND4_HALO = 16
GQA_TQ = 512
GQA_TK = 512
GQA_CHAIN = 1024
FF_CHUNKS = (768, 768, 768, 512)
FFN_SUB = 1
HG_CHUNK = 64
HG_SUB = 16
HG_TB = 512
HG_HP = 4
HG_TB_OUT = 256
NEG = -1e30
LOG2E = 1.4426950408889634
VMEM_LIMIT = 56 * 1024 * 1024

_NT = (((1,), (1,)), ((), ()))
_TN = (((0,), (0,)), ((), ()))


def _params(sem, vmem=None):
    return pltpu.CompilerParams(dimension_semantics=sem, vmem_limit_bytes=vmem)


def _const_spec(shape, single=False):
    nd = len(shape)
    if single:
        return pl.BlockSpec(shape, lambda *_: (0,) * nd, pipeline_mode=pl.Buffered(1))
    return pl.BlockSpec(shape, lambda *_: (0,) * nd)


def _layer_norm(z, g, b):
    mu = jnp.mean(z, -1, keepdims=True)
    d = z - mu
    var = jnp.mean(d * d, -1, keepdims=True)
    return d * lax.rsqrt(var + LN_EPS) * g + b


def _sigmoid(z):
    return 1.0 / (1.0 + jnp.exp(-z))


def _row_block_dot(x, w, rows=PROJ_ROWS):
    m = x.shape[0]
    starts = list(range(0, m - m % rows, rows)) or [0]
    ends = starts[1:] + [m]
    parts = [jnp.dot(x[a:b], w, preferred_element_type=F32) for a, b in zip(starts, ends)]
    return parts[0] if len(parts) == 1 else jnp.concatenate(parts, axis=0)


def _pipelined(sections):
    pending = sections[0][0]()
    for i, (_, epilogue) in enumerate(sections):
        upcoming = sections[i + 1][0]() if i + 1 < len(sections) else None
        epilogue(pending)
        pending = upcoming


def _rope_tables(seq):
    pos = jnp.arange(seq, dtype=F32)
    d = jnp.arange(LANES) % HEAD_DIM

    def angles(p, dim, theta):
        freqs = theta ** (-(jnp.arange(0, dim, 2, dtype=F32) / dim))
        return p[:, None] * freqs[None, :]

    h = ROPE_DIM // 2
    ang = angles(pos, ROPE_DIM, ROPE_THETA)
    a = ang[:, d % h]
    lo, hi = d < h, (d >= h) & (d < ROPE_DIM)
    pc = jnp.where(lo | hi, jnp.cos(a), 1.0)
    psa = jnp.where(lo, -jnp.sin(a), 0.0)
    psb = jnp.where(hi, jnp.sin(a), 0.0)

    q = HEAD_DIM // 4
    row = jnp.floor(pos / GRID_W)
    col = pos - row * GRID_W
    ar = angles(row, HEAD_DIM // 2, AXIAL_THETA)[:, d % q]
    ac_ = angles(col, HEAD_DIM // 2, AXIAL_THETA)[:, d % q]
    a2 = jnp.where(d < HEAD_DIM // 2, ar, ac_)
    first = (d % (HEAD_DIM // 2)) < q
    ac = jnp.cos(a2)
    asa = jnp.where(first, -jnp.sin(a2), 0.0)
    asb = jnp.where(first, 0.0, jnp.sin(a2))
    return [t.astype(F32) for t in (pc, psa, psb, ac, asa, asb)]


def _rope(seg, c, sa, sb, shift):
    return seg * c + pltpu.roll(seg, LANES - shift, 1) * sa + pltpu.roll(seg, shift, 1) * sb


def _ab_in_kernel(x_ref, w_ref, pc_ref, psa_ref, psb_ref, ac_ref, asa_ref, asb_ref, qg_ref, kg_ref,
                  ones_ref, qa_ref, ka_ref, va_ref, qb_ref, kd_ref, vd_ref, qa16_ref, ka16_ref, va16_ref,
                  qs_ref, ks_ref, vs_ref):
    xb = x_ref[...].astype(BF16)

    def proj(a, n):
        return jnp.dot(xb, w_ref[:, a:a + n], preferred_element_type=F32)

    pc, psa, psb = pc_ref[...], psa_ref[...], psb_ref[...]
    ac, asa, asb = ac_ref[...], asa_ref[...], asb_ref[...]

    def dilated_a(h, rope, out_ref, out16_ref, slab_ref):
        for g in range(A_W // LANES):
            sl = slice(g * LANES, (g + 1) * LANES)
            y = _rope(h[:, sl], pc, psa, psb, ROPE_DIM // 2) if rope else h[:, sl]
            out_ref[:, sl] = y.astype(BF16)
            slab_ref[g] = y
        for c in range(DIL_CLASSES):
            for g in range(A_W // LANES):
                rows = slab_ref[g, pl.ds(c, TM // DIL_CLASSES, stride=DIL_CLASSES), :]
                out16_ref[0, c, :, g * LANES:(g + 1) * LANES] = rows.astype(BF16)

    def norm_rope(h, gain, out_ref):
        for g in range(h.shape[1] // LANES):
            sl = slice(g * LANES, (g + 1) * LANES)
            seg = h[:, sl]
            sq = seg * seg
            hi = sq.astype(BF16)
            lo = (sq - hi.astype(F32)).astype(BF16)
            ss = (jnp.dot(hi, ones_ref[...], preferred_element_type=F32)
                  + jnp.dot(lo, ones_ref[...], preferred_element_type=F32))
            y = seg * lax.rsqrt(ss * (1.0 / HEAD_DIM) + RMS_EPS) * gain
            out_ref[:, sl] = _rope(y, ac, asa, asb, HEAD_DIM // 4).astype(BF16)

    def cast_to(out_ref):
        def fn(h):
            out_ref[...] = h.astype(BF16)
        return fn

    o_qb = 3 * A_W
    o_kd = o_qb + B_QW
    o_vd = o_kd + 2 * B_KVW
    _pipelined([
        (functools.partial(proj, 0, A_W),
         functools.partial(dilated_a, rope=True, out_ref=qa_ref, out16_ref=qa16_ref, slab_ref=qs_ref)),
        (functools.partial(proj, A_W, A_W),
         functools.partial(dilated_a, rope=True, out_ref=ka_ref, out16_ref=ka16_ref, slab_ref=ks_ref)),
        (functools.partial(proj, 2 * A_W, A_W),
         functools.partial(dilated_a, rope=False, out_ref=va_ref, out16_ref=va16_ref, slab_ref=vs_ref)),
        (functools.partial(proj, o_qb, B_QW), functools.partial(norm_rope, gain=qg_ref[...], out_ref=qb_ref)),
        (functools.partial(proj, o_kd, 2 * B_KVW), functools.partial(norm_rope, gain=kg_ref[...], out_ref=kd_ref)),
        (functools.partial(proj, o_vd, 2 * B_KVW), cast_to(vd_ref)),
    ])


def _ab_in(x2d, w_ext, tabs, qg, kg, ones2, batch, seq):
    t = x2d.shape[0]
    nt = seq // TM
    tab = pl.BlockSpec((TM, LANES), lambda i: (i % nt, 0))
    row = lambda n: pl.BlockSpec((TM, n), lambda i: (i, 0))
    cls = pl.BlockSpec((1, DIL_CLASSES, TM // DIL_CLASSES, A_W), lambda i: (i // nt, 0, i % nt, 0))
    cls_shape = jax.ShapeDtypeStruct((batch, DIL_CLASSES, seq // DIL_CLASSES, A_W), BF16)
    wn = w_ext.shape[1]
    return pl.pallas_call(
        _ab_in_kernel,
        grid=(t // TM,),
        in_specs=[row(D_MODEL), _const_spec((D_MODEL, wn))] + [tab] * 6
        + [_const_spec((1, LANES)), _const_spec((1, LANES)), _const_spec((LANES, LANES))],
        out_specs=[row(A_W)] * 3 + [row(B_QW), row(2 * B_KVW), row(2 * B_KVW)] + [cls] * 3,
        out_shape=[jax.ShapeDtypeStruct((t, A_W), BF16)] * 3
        + [jax.ShapeDtypeStruct((t, B_QW), BF16)]
        + [jax.ShapeDtypeStruct((t, 2 * B_KVW), BF16)] * 2 + [cls_shape] * 3,
        scratch_shapes=[pltpu.VMEM((A_W // LANES, TM, LANES), F32)] * 3,
        compiler_params=_params(("arbitrary",), VMEM_LIMIT),
        name="ab_in",
    )(x2d, w_ext, *tabs, qg, kg, ones2)


def _band_groups(qs, ks, vs, bias, left):
    chains = [(g, sel) for g in range(len(qs)) for sel in (left, jnp.logical_not(left))]
    scores = []
    for g, sel in chains:
        qm = jnp.where(sel, qs[g], jnp.zeros_like(qs[g]))
        scores.append(lax.dot_general(qm, ks[g], _NT, preferred_element_type=F32) + bias)
    probs = []
    for s in scores:
        mx = jnp.max(s, axis=1, keepdims=True)
        pe = jnp.exp(s - mx)
        probs.append((mx, jnp.sum(pe, axis=1, keepdims=True), pe.astype(BF16)))
    outs = []
    for (g, _), (mx, l, pe) in zip(chains, probs):
        o = jnp.dot(pe, vs[g], preferred_element_type=F32)
        outs.append((o * (1.0 / l), mx + jnp.log(l)))
    return [(jnp.where(left, outs[2 * g][0], outs[2 * g + 1][0]), jnp.where(left, outs[2 * g][1], outs[2 * g + 1][1]))
            for g in range(len(qs))]


def _band_kernel(q_ref, kp_ref, kc_ref, kn_ref, vp_ref, vc_ref, vn_ref, o_ref, lse_ref, *, m, step_rows):
    nk = BAND_L + 2 * HALF_WINDOW
    base = pl.program_id(len(q_ref.shape) - 2) * step_rows
    r = lax.broadcasted_iota(jnp.int32, (BAND_L, nk), 0)
    j = lax.broadcasted_iota(jnp.int32, (BAND_L, nk), 1)
    band = jnp.where(jnp.abs(j - HALF_WINDOW - r) <= HALF_WINDOW, 0.0, NEG)
    left = lax.broadcasted_iota(jnp.int32, (BAND_L, LANES), 1) < HEAD_DIM
    pre = (0,) * (len(q_ref.shape) - 2)
    lanes = [slice(p * LANES, (p + 1) * LANES) for p in range(A_W // LANES)]
    at = lambda rows, sl: pre + (rows, sl)
    whole = slice(None)
    ks = [jnp.concatenate([kp_ref[at(whole, sl)], kc_ref[at(whole, sl)], kn_ref[at(whole, sl)]], axis=0) for sl in lanes]
    vs = [jnp.concatenate([vp_ref[at(whole, sl)], vc_ref[at(whole, sl)], vn_ref[at(whole, sl)]], axis=0) for sl in lanes]
    for blk in range(step_rows // BAND_L):
        r0 = blk * BAND_L
        rows = slice(r0, r0 + BAND_L)
        kpos = base + r0 - HALF_WINDOW + j
        bias = jnp.where(kpos >= 0, band, NEG)
        bias = jnp.where(kpos < m, bias, NEG)
        res = _band_groups([q_ref[at(rows, sl)] for sl in lanes], [kk[r0:r0 + nk] for kk in ks],
                           [vv[r0:r0 + nk] for vv in vs], bias, left)
        for sl, (o, lse) in zip(lanes, res):
            o_ref[at(rows, sl)] = o.astype(BF16)
            lse_ref[at(rows, sl)] = lse


def _band_specs(lead, rows, step_rows):
    nh = rows // HALF_WINDOW
    ratio = step_rows // HALF_WINDOW
    ones = (1,) * lead
    cur = pl.BlockSpec(ones + (step_rows, A_W), lambda *g: g[:lead] + (g[-1], 0))
    prev = pl.BlockSpec(ones + (HALF_WINDOW, A_W), lambda *g: g[:lead] + (jnp.maximum(g[-1] * ratio - 1, 0), 0))
    nxt = pl.BlockSpec(ones + (HALF_WINDOW, A_W), lambda *g: g[:lead] + (jnp.minimum((g[-1] + 1) * ratio, nh - 1), 0))
    return cur, prev, nxt


def _band1(q, k, v, batch, seq):
    step_rows = min(BAND_STEP, seq)
    cur, prev, nxt = _band_specs(1, seq, step_rows)
    view = lambda t: t.reshape(batch, seq, A_W)
    o, lse = pl.pallas_call(
        functools.partial(_band_kernel, m=seq, step_rows=step_rows),
        grid=(batch, seq // step_rows),
        in_specs=[cur, prev, cur, nxt, prev, cur, nxt],
        out_specs=[cur, cur],
        out_shape=[jax.ShapeDtypeStruct((batch, seq, A_W), BF16), jax.ShapeDtypeStruct((batch, seq, A_W), F32)],
        compiler_params=_params(("arbitrary",) * 2),
        name="band_d1",
    )(view(q), view(k), view(k), view(k), view(v), view(v), view(v))
    return o.reshape(batch * seq, A_W), lse.reshape(batch * seq, A_W)


def _band16(q16, k16, v16, batch, seq):
    m = seq // DIL_CLASSES
    step_rows = min(BAND_STEP, m)
    cur, prev, nxt = _band_specs(2, m, step_rows)
    shape = (batch, DIL_CLASSES, m, A_W)
    return pl.pallas_call(
        functools.partial(_band_kernel, m=m, step_rows=step_rows),
        grid=(batch, DIL_CLASSES, m // step_rows),
        in_specs=[cur, prev, cur, nxt, prev, cur, nxt],
        out_specs=[cur, cur],
        out_shape=[jax.ShapeDtypeStruct(shape, BF16), jax.ShapeDtypeStruct(shape, F32)],
        compiler_params=_params(("arbitrary",) * 3),
        name="band_d16",
    )(q16, k16, k16, k16, v16, v16, v16)


def _band4_kernel(q_ref, kp_ref, kc_ref, kn_ref, vp_ref, vc_ref, vn_ref, o16_ref, l16_ref, o_ref, lse_ref, *,
                  m16, step_rows):
    qr = BAND4_ROWS
    kr = qr + 2 * BAND4_HALO
    i = pl.program_id(2)
    shift_q, shift_k = qr.bit_length() - 1, kr.bit_length() - 1
    row = lax.broadcasted_iota(jnp.int32, (4 * qr, 4 * kr), 0)
    col = lax.broadcasted_iota(jnp.int32, (4 * qr, 4 * kr), 1)
    uq, rq = row >> shift_q, row & (qr - 1)
    uk, jk = col >> shift_k, col & (kr - 1)
    band = jnp.where(jnp.abs(4 * (jk - BAND4_HALO - rq) + (uk - uq)) <= HALF_WINDOW, 0.0, NEG)
    left = lax.broadcasted_iota(jnp.int32, (4 * qr, LANES), 1) < HEAD_DIM
    slabs = [slice(p * LANES, (p + 1) * LANES) for p in range(A_W // LANES)]
    window = lambda refs, u, sl: jnp.concatenate([r[0, u, 0, :, sl] for r in refs], axis=0)
    kwin = [[window([kp_ref, kc_ref, kn_ref], u, sl) for u in range(4)] for sl in slabs]
    vwin = [[window([vp_ref, vc_ref, vn_ref], u, sl) for u in range(4)] for sl in slabs]
    for blk in range(step_rows // qr):
        r0 = blk * qr
        rows = slice(r0, r0 + qr)
        kpos = i * step_rows + r0 - BAND4_HALO + jk
        bias = jnp.where(kpos >= 0, band, NEG)
        bias = jnp.where(kpos < m16, bias, NEG)
        classes = lambda ref, sl: jnp.concatenate([ref[0, u, 0, rows, sl] for u in range(4)], axis=0)
        branch4 = _band_groups([classes(q_ref, sl) for sl in slabs],
                               [jnp.concatenate([w[r0:r0 + kr] for w in kw], axis=0) for kw in kwin],
                               [jnp.concatenate([w[r0:r0 + kr] for w in vw], axis=0) for vw in vwin], bias, left)
        for sl, (o4, l4) in zip(slabs, branch4):
            o16 = classes(o16_ref, sl).astype(F32)
            l16 = classes(l16_ref, sl)
            mx = jnp.maximum(l4, l16)
            w4, w16 = jnp.exp(l4 - mx), jnp.exp(l16 - mx)
            tot = w4 + w16
            om = (w4 * o4 + w16 * o16) * (1.0 / tot)
            lm = mx + jnp.log(tot)
            for u in range(4):
                o_ref[0, u, 0, rows, sl] = om[u * qr:(u + 1) * qr].astype(BF16)
                lse_ref[0, u, 0, rows, sl] = lm[u * qr:(u + 1) * qr]


def _band4(q16, k16, v16, o16, l16, batch, seq):
    m16 = seq // DIL_CLASSES
    step_rows = min(BAND4_STEP, m16)
    ratio = step_rows // BAND4_HALO
    nh = m16 // BAND4_HALO
    view = lambda t: t.reshape(batch, 4, 4, m16, A_W)
    cur = pl.BlockSpec((1, 4, 1, step_rows, A_W), lambda b, c, i: (b, 0, c, i, 0))
    prev = pl.BlockSpec((1, 4, 1, BAND4_HALO, A_W), lambda b, c, i: (b, 0, c, jnp.maximum(i * ratio - 1, 0), 0))
    nxt = pl.BlockSpec((1, 4, 1, BAND4_HALO, A_W), lambda b, c, i: (b, 0, c, jnp.minimum((i + 1) * ratio, nh - 1), 0))
    shape = (batch, 4, 4, m16, A_W)
    o, lse = pl.pallas_call(
        functools.partial(_band4_kernel, m16=m16, step_rows=step_rows),
        grid=(batch, 4, m16 // step_rows),
        in_specs=[cur, prev, cur, nxt, prev, cur, nxt, cur, cur],
        out_specs=[cur, cur],
        out_shape=[jax.ShapeDtypeStruct(shape, BF16), jax.ShapeDtypeStruct(shape, F32)],
        compiler_params=_params(("arbitrary",) * 3),
        name="band_d4",
    )(view(q16), view(k16), view(k16), view(k16), view(v16), view(v16), view(v16), view(o16), view(l16))
    return o.reshape(batch, DIL_CLASSES, m16, A_W), lse.reshape(batch, DIL_CLASSES, m16, A_W)


def _gqa_kernel(q_ref, k_ref, v_ref, o_ref, qs_ref, m_ref, acc_ref, s_ref, vt_ref, *, seq):
    left = lax.broadcasted_iota(jnp.int32, (GQA_TQ, LANES), 1) < HEAD_DIM
    top = lax.broadcasted_iota(jnp.int32, (LANES, GQA_TQ), 0) < HEAD_DIM
    for pr in range(2):
        qt = q_ref[0, :, pr * LANES:(pr + 1) * LANES].astype(F32).T
        qs_ref[:, (2 * pr) * GQA_TQ:(2 * pr + 1) * GQA_TQ] = jnp.where(top, qt, 0.0).astype(BF16)
        qs_ref[:, (2 * pr + 1) * GQA_TQ:(2 * pr + 2) * GQA_TQ] = jnp.where(top, 0.0, qt).astype(BF16)
    m_ref[...] = jnp.full(m_ref.shape, NEG, F32)
    acc_ref[...] = jnp.zeros(acc_ref.shape, F32)
    n_tiles = seq // GQA_TK

    @pl.when(pl.program_id(2) == 0)
    def _():
        ones_row = lax.broadcasted_iota(jnp.int32, (LANES, GQA_TK), 0) == HEAD_DIM

        def transpose_tile(t, carry):
            off = pl.multiple_of(t * GQA_TK, GQA_TK)
            vt = v_ref[0, pl.ds(off, GQA_TK), :].astype(F32).T
            vt_ref[t] = jnp.where(ones_row, 1.0, vt).astype(BF16)
            return carry

        lax.fori_loop(0, n_tiles, transpose_tile, 0)

    chains = [slice(c * GQA_CHAIN, (c + 1) * GQA_CHAIN) for c in range(4 * GQA_TQ // GQA_CHAIN)]

    def scores(tile, buf, cols):
        off = pl.multiple_of(tile * GQA_TK, GQA_TK)
        k = k_ref[0, pl.ds(off, GQA_TK), :]
        s_ref[buf, :, cols] = jnp.dot(k, qs_ref[:, cols], preferred_element_type=F32)

    def softmax_pv(tile, buf, cols):
        s = s_ref[buf, :, cols]
        m_prev = m_ref[:, cols]
        m_new = jnp.maximum(m_prev, jnp.max(s, axis=0, keepdims=True))
        alpha = jnp.exp2(m_prev - m_new)
        p = jnp.exp2((s - m_new).astype(BF16))
        pv = jnp.dot(vt_ref[tile], p, preferred_element_type=F32)
        acc_ref[:, cols] = alpha * acc_ref[:, cols] + pv
        m_ref[:, cols] = m_new

    def step(tile, buf, prefetch):
        for cols in chains:
            if prefetch:
                scores(tile + 1, 1 - buf, cols)
            softmax_pv(tile, buf, cols)

    def pair(j, carry):
        step(2 * j, 0, True)
        step(2 * j + 1, 1, True)
        return carry

    for cols in chains:
        scores(0, 0, cols)
    lax.fori_loop(0, n_tiles // 2 - 1, pair, 0)
    step(n_tiles - 2, 0, True)
    step(n_tiles - 1, 1, False)
    acc = acc_ref[...]
    a = (acc * (1.0 / acc[HEAD_DIM:HEAD_DIM + 1, :])).T
    for pr in range(2):
        lo = a[(2 * pr) * GQA_TQ:(2 * pr + 1) * GQA_TQ]
        hi = pltpu.roll(a[(2 * pr + 1) * GQA_TQ:(2 * pr + 2) * GQA_TQ], HEAD_DIM, 1)
        o_ref[0, :, pr * LANES:(pr + 1) * LANES] = jnp.where(left, lo, hi).astype(BF16)


def _gqa(qb, kd, vd, batch, seq):
    gw = B_QW // B_KV_HEADS
    cols = 4 * GQA_TQ
    out = pl.pallas_call(
        functools.partial(_gqa_kernel, seq=seq),
        grid=(batch, B_KV_HEADS, seq // GQA_TQ),
        in_specs=[pl.BlockSpec((1, GQA_TQ, gw), lambda b, g, qi: (b, qi, g)),
                  pl.BlockSpec((1, seq, LANES), lambda b, g, qi: (b, 0, g)),
                  pl.BlockSpec((1, seq, LANES), lambda b, g, qi: (b, 0, g))],
        out_specs=pl.BlockSpec((1, GQA_TQ, gw), lambda b, g, qi: (b, qi, g)),
        out_shape=jax.ShapeDtypeStruct((batch, seq, B_QW), BF16),
        scratch_shapes=[pltpu.VMEM((LANES, cols), BF16), pltpu.VMEM((1, cols), F32),
                        pltpu.VMEM((LANES, cols), F32), pltpu.VMEM((2, GQA_TK, cols), F32),
                        pltpu.VMEM((seq // GQA_TK, LANES, GQA_TK), BF16)],
        compiler_params=_params(("arbitrary",) * 3, VMEM_LIMIT),
        name="gqa",
    )(qb.reshape(batch, seq, B_QW), kd.reshape(batch, seq, 2 * B_KVW), vd.reshape(batch, seq, 2 * B_KVW))
    return out.reshape(batch * seq, B_QW)


def _ab_out_kernel(o1_ref, l1_ref, om_ref, lm_ref, yb_ref, x_ref, w_ref, g_ref, b_ref, out_ref, os_ref, ls_ref):
    for c in range(DIL_CLASSES):
        for g in range(A_W // LANES):
            sl = slice(g * LANES, (g + 1) * LANES)
            rows = pl.ds(c, TM // DIL_CLASSES, stride=DIL_CLASSES)
            os_ref[g, rows, :] = om_ref[0, c, :, sl].astype(F32)
            ls_ref[g, rows, :] = lm_ref[0, c, :, sl]
    ya = []
    for g in range(A_W // LANES):
        sl = slice(g * LANES, (g + 1) * LANES)
        l1, lm = l1_ref[:, sl], ls_ref[g]
        mx = jnp.maximum(l1, lm)
        e1, em = jnp.exp(l1 - mx), jnp.exp(lm - mx)
        ya.append(((e1 * o1_ref[:, sl].astype(F32) + em * os_ref[g]) * (1.0 / (e1 + em))).astype(BF16))
    y = _row_block_dot(jnp.concatenate(ya + [yb_ref[...]], axis=1), w_ref[...])
    out_ref[...] = _layer_norm(ALPHA * x_ref[...] + y, g_ref[...], b_ref[...])


def _ab_out(o1, l1, om, lm, yb, x2d, w, g, b, seq):
    t = x2d.shape[0]
    nt = seq // TM
    row = lambda n: pl.BlockSpec((TM, n), lambda i: (i, 0))
    cls = pl.BlockSpec((1, DIL_CLASSES, TM // DIL_CLASSES, A_W), lambda i: (i // nt, 0, i % nt, 0))
    return pl.pallas_call(
        _ab_out_kernel,
        grid=(t // TM,),
        in_specs=[row(A_W), row(A_W), cls, cls, row(B_QW), row(D_MODEL), _const_spec((A_W + B_QW, D_MODEL)),
                  _const_spec((1, D_MODEL)), _const_spec((1, D_MODEL))],
        out_specs=row(D_MODEL),
        out_shape=jax.ShapeDtypeStruct((t, D_MODEL), F32),
        scratch_shapes=[pltpu.VMEM((A_W // LANES, TM, LANES), F32)] * 2,
        compiler_params=_params(("arbitrary",), VMEM_LIMIT),
        name="ab_out",
    )(o1, l1, om, lm, yb, x2d, w, g, b)


def _gelu(x):
    return 0.5 * x * (1.0 + jnp.tanh(0.7978845608028654 * (x + 0.044715 * (x * x * x))))


def _ffn_kernel(x_ref, xp_ref, xn_ref, wu_ref, cw_ref, cb_ref, wd_ref, g_ref, b_ref, out_ref, *, nt):
    i = pl.program_id(0)
    keep_prev = (i % nt != 0).astype(F32)
    keep_next = (i % nt != nt - 1).astype(F32)
    rows = lax.broadcasted_iota(jnp.int32, (TM, 1), 0)
    offsets = [sum(FF_CHUNKS[:c]) for c in range(len(FF_CHUNKS))]
    operands, acc = {}, {}

    def tile_operands(j):
        if j not in operands:
            r0 = j * TM
            xb = x_ref[r0:r0 + TM, :].astype(BF16)
            before = xp_ref[...] if j == 0 else x_ref[r0 - 8:r0, :]
            after = xn_ref[...] if j == FFN_SUB - 1 else x_ref[r0 + TM:r0 + TM + 8, :]
            halo = jnp.concatenate([before, after], axis=0).astype(BF16)
            operands[j] = (xb, jnp.concatenate([xb, halo], axis=0))
        return operands[j]

    def up(j, c):
        a, n = offsets[c], FF_CHUNKS[c]
        xb, xe = tile_operands(j)
        u = _row_block_dot(xb, wu_ref[:, a:a + n], FFN_ROWS)
        ge = _row_block_dot(xe, wu_ref[:, D_FF + a:D_FF + a + n], FFN_ROWS)
        return u, ge

    def finish(j, c, pending):
        u, ge = pending
        a, n = offsets[c], FF_CHUNKS[c]
        gm = ge[0:TM]
        g_before = ge[TM + 7:TM + 8]
        g_after = ge[TM + 8:TM + 9]
        if j == 0:
            g_before = g_before * keep_prev
        if j == FFN_SUB - 1:
            g_after = g_after * keep_next
        gp = jnp.where(rows == 0, g_before, pltpu.roll(gm, 1, 0))
        gn = jnp.where(rows == TM - 1, g_after, pltpu.roll(gm, TM - 1, 0))
        cw = cw_ref[:, a:a + n]
        gc = gp * cw[0:1] + gm * cw[1:2] + gn * cw[2:3] + cb_ref[:, a:a + n]
        act = (_gelu(gc) * u).astype(BF16)
        part = _row_block_dot(act, wd_ref[a:a + n, :], FFN_ROWS)
        acc[j] = part if c == 0 else acc[j] + part
        if c == len(FF_CHUNKS) - 1:
            sl = slice(j * TM, (j + 1) * TM)
            out_ref[sl, :] = _layer_norm(ALPHA * x_ref[sl, :] + acc[j], g_ref[...], b_ref[...])

    _pipelined([(functools.partial(up, j, c), functools.partial(finish, j, c))
                for j in range(FFN_SUB) for c in range(len(FF_CHUNKS))])


def _ffn(x2d, wu, cw, cb, wd, g, b, seq):
    t = x2d.shape[0]
    step = FFN_SUB * TM
    nt = seq // step
    r8 = step // 8
    return pl.pallas_call(
        functools.partial(_ffn_kernel, nt=nt),
        grid=(t // step,),
        in_specs=[pl.BlockSpec((step, D_MODEL), lambda i: (i, 0)),
                  pl.BlockSpec((8, D_MODEL), lambda i: (jnp.maximum(i * r8 - 1, 0), 0)),
                  pl.BlockSpec((8, D_MODEL), lambda i: (jnp.minimum((i + 1) * r8, t // 8 - 1), 0)),
                  _const_spec((D_MODEL, 2 * D_FF), single=True),
                  _const_spec((3, D_FF)), _const_spec((1, D_FF)),
                  _const_spec((D_FF, D_MODEL), single=True),
                  _const_spec((1, D_MODEL)), _const_spec((1, D_MODEL))],
        out_specs=pl.BlockSpec((step, D_MODEL), lambda i: (i, 0)),
        out_shape=jax.ShapeDtypeStruct((t, D_MODEL), F32),
        compiler_params=_params(("arbitrary",), VMEM_LIMIT),
        name="ffn",
    )(x2d, x2d, x2d, wu, cw, cb, wd, g, b)


def _lower_bound(tbl, layer):
    rows = [tbl[r:r + 1] for r in range(DEPTH)]
    mx = functools.reduce(jnp.maximum, rows)
    es = [jnp.exp(r - mx) for r in rows]
    inv = 1.0 / functools.reduce(lambda a, b: a + b, es)
    ps = [e * inv for e in es]
    return functools.reduce(lambda a, b: a + b, ps[:layer + 1]) - ps[0]


def _c_in_kernel(x_ref, w_ref, lbf_ref, lbb_ref, q_ref, lf_ref, lb_ref, v_ref, g_ref, *, layer):
    half = C_W // 2
    blocks = [slice(r * PROJ_ROWS, (r + 1) * PROJ_ROWS) for r in range(TM // PROJ_ROWS)]
    xbs = [x_ref[rows, :].astype(BF16) for rows in blocks]

    def proj(r, a, n):
        return jnp.dot(xbs[r], w_ref[:, a:a + n], preferred_element_type=F32)

    def silu(z):
        return z * _sigmoid(z)

    lbf = _lower_bound(lbf_ref[...], layer)
    lbb = _lower_bound(lbb_ref[...], layer)

    def store_q(z, rows, sl):
        q_ref[rows, sl] = z.astype(BF16)

    def store_logf(tbl, out_ref):
        def fn(z, rows, sl):
            lb = tbl[:, sl]
            out_ref[rows, sl] = jnp.log(lb + (1.0 - lb) * _sigmoid(z))
        return fn

    def store_silu(out_ref):
        def fn(z, rows, sl):
            out_ref[rows, sl] = silu(z).astype(BF16)
        return fn

    sections = []
    for r, rows in enumerate(blocks):
        for c in range(2):
            a = c * half
            sl = slice(a, a + half)
            for off, fn in ((0, store_q), (C_W, store_logf(lbf, lf_ref)), (2 * C_W, store_logf(lbb, lb_ref)),
                            (3 * C_W, store_silu(v_ref)), (3 * C_W + C_VW, store_silu(g_ref))):
                sections.append((functools.partial(proj, r, off + a, half), functools.partial(fn, rows=rows, sl=sl)))
    _pipelined(sections)


def _c_in(x2d, w, lbf, lbb, layer):
    t = x2d.shape[0]
    row = pl.BlockSpec((TM, C_W), lambda i: (i, 0))
    return pl.pallas_call(
        functools.partial(_c_in_kernel, layer=layer),
        grid=(t // TM,),
        in_specs=[row, _const_spec(w.shape), _const_spec((DEPTH, C_W)), _const_spec((DEPTH, C_W))],
        out_specs=[row] * 5,
        out_shape=[jax.ShapeDtypeStruct((t, C_W), d) for d in (BF16, F32, F32, BF16, BF16)],
        compiler_params=_params(("arbitrary",), VMEM_LIMIT),
        name="c_in",
    )(x2d, w, lbf, lbb)


def _hgrn_kernel(*refs, rev, fused, heads, step_rows):
    if fused:
        (q_ref, lf_ref, v_ref, of_ref, gate_ref, gn_ref, x_ref, w_ref, lg_ref, lb_ref,
         out_ref, st_ref, y_ref) = refs
    else:
        q_ref, lf_ref, v_ref, y_ref, st_ref = refs

    @pl.when(pl.program_id(2) == 0)
    def _():
        st_ref[...] = jnp.zeros(st_ref.shape, F32)

    ri = lax.broadcasted_iota(jnp.int32, (HG_CHUNK, HG_CHUNK), 0)
    ci = lax.broadcasted_iota(jnp.int32, (HG_CHUNK, HG_CHUNK), 1)
    tri = (ci >= ri) if rev else (ci <= ri)
    trib = jnp.where(tri, 1.0, 0.0).astype(BF16)
    dotf = lambda a, b: jnp.dot(a, b, preferred_element_type=F32)
    n_chunks = step_rows // HG_CHUNK
    order = range(n_chunks - 1, -1, -1) if rev else range(n_chunks)
    units = [(c, h) for c in order for h in range(heads)]
    window = lambda u: (0, slice(u[0] * HG_CHUNK, (u[0] + 1) * HG_CHUNK), slice(u[1] * C_KEY, (u[1] + 1) * C_KEY))

    b_all = {}
    for u in units:
        lf = lf_ref[window(u)]
        hi = lf.astype(BF16)
        lo = (lf - hi.astype(F32)).astype(BF16)
        b_all[u] = dotf(trib, hi) + dotf(trib, lo)

    qdec, kdec, etot, a_rows = {}, {}, {}, {}
    for u in units:
        lf, b = lf_ref[window(u)], b_all[u]
        qf = q_ref[window(u)].astype(F32)
        bex = b - lf
        kk = 1.0 - jnp.exp(lf)
        btot = b[0:1] if rev else b[HG_CHUNK - 1:HG_CHUNK]
        qdec[u] = (qf * jnp.exp(b)).astype(BF16)
        kdec[u] = (kk * jnp.exp(btot - b)).astype(BF16)
        etot[u] = jnp.exp(btot)
        rows = []
        for blk in range(HG_CHUNK // HG_SUB):
            r0 = blk * HG_SUB
            ref = bex[r0 + HG_SUB - 1:r0 + HG_SUB] if rev else bex[r0:r0 + 1]
            qt = (qf[r0:r0 + HG_SUB] * jnp.exp(b[r0:r0 + HG_SUB] - ref)).astype(BF16)
            lo_r, hi_r = (r0, HG_CHUNK) if rev else (0, r0 + HG_SUB)
            ks = (kk[lo_r:hi_r] * jnp.exp(ref - b[lo_r:hi_r])).astype(BF16)
            pieces = []
            if lo_r > 0:
                pieces.append(jnp.zeros((lo_r, C_KEY), BF16))
            pieces.append(ks)
            if hi_r < HG_CHUNK:
                pieces.append(jnp.zeros((HG_CHUNK - hi_r, C_KEY), BF16))
            kfull = jnp.concatenate(pieces, axis=0) if len(pieces) > 1 else ks
            rows.append(lax.dot_general(qt, kfull, _NT, preferred_element_type=F32))
        a_rows[u] = rows
    attn = {u: jnp.where(tri, jnp.concatenate(a_rows[u], axis=0), 0.0).astype(BF16) for u in units}

    states = [st_ref[h] for h in range(heads)]
    for u in units:
        h = u[1]
        vb = v_ref[window(u)]
        o = lax.dot_general(qdec[u], states[h].astype(BF16), _NT, preferred_element_type=F32)
        o = o + dotf(attn[u], vb)
        states[h] = states[h] * etot[u] + lax.dot_general(vb, kdec[u], _TN, preferred_element_type=F32)
        if fused:
            tot = of_ref[window(u)] + o
            inv = lax.rsqrt(jnp.mean(tot * tot, -1, keepdims=True) + RMS_EPS)
            y = tot * inv * gn_ref[:, window(u)[2]] * gate_ref[window(u)].astype(F32)
            y_ref[window(u)[1:]] = y.astype(BF16)
        else:
            y_ref[window(u)] = o
    for h in range(heads):
        st_ref[h] = states[h]
    if fused:
        proj = _row_block_dot(y_ref[...], w_ref[...])
        out_ref[0] = _layer_norm(ALPHA * x_ref[0] + proj, lg_ref[...], lb_ref[...])


def _hgrn_specs(batch, seq, rev, heads, step_rows):
    nb = seq // step_rows
    blk = (lambda b, h, i: (b, nb - 1 - i, h)) if rev else (lambda b, h, i: (b, i, h))
    return pl.BlockSpec((1, step_rows, heads * C_KEY), blk), (batch, C_HEADS // heads, nb)


def _hgrn_fwd(q, lf, v, batch, seq):
    tile, grid = _hgrn_specs(batch, seq, False, HG_HP, HG_TB)
    view = lambda t: t.reshape(batch, seq, C_W)
    out = pl.pallas_call(
        functools.partial(_hgrn_kernel, rev=False, fused=False, heads=HG_HP, step_rows=HG_TB),
        grid=grid,
        in_specs=[tile, tile, tile],
        out_specs=tile,
        out_shape=jax.ShapeDtypeStruct((batch, seq, C_VW), F32),
        scratch_shapes=[pltpu.VMEM((HG_HP, C_VAL, C_KEY), F32)],
        compiler_params=_params(("arbitrary",) * 3),
        name="hgrn_fwd",
    )(view(q), view(lf), view(v))
    return out.reshape(batch * seq, C_VW)


def _hgrn_bwd_out(q, lf, v, o_f, gate, gn, x2d, w, g, b, batch, seq):
    tile, grid = _hgrn_specs(batch, seq, True, C_HEADS, HG_TB_OUT)
    view = lambda t: t.reshape(batch, seq, C_W)
    out = pl.pallas_call(
        functools.partial(_hgrn_kernel, rev=True, fused=True, heads=C_HEADS, step_rows=HG_TB_OUT),
        grid=grid,
        in_specs=[tile] * 5 + [_const_spec((1, C_VW)), tile, _const_spec((C_VW, D_MODEL), single=True),
                               _const_spec((1, D_MODEL)), _const_spec((1, D_MODEL))],
        out_specs=tile,
        out_shape=jax.ShapeDtypeStruct((batch, seq, D_MODEL), F32),
        scratch_shapes=[pltpu.VMEM((C_HEADS, C_VAL, C_KEY), F32), pltpu.VMEM((HG_TB_OUT, C_VW), BF16)],
        compiler_params=_params(("arbitrary",) * 3, VMEM_LIMIT),
        name="hgrn_bwd_out",
    )(view(q), view(lf), view(v), view(o_f), view(gate), gn, x2d.reshape(batch, seq, D_MODEL), w, g, b)
    return out.reshape(batch * seq, D_MODEL)


def _prep_weights(w_in_ab, w_out_ab, qn_ab, kn_ab, w_in_c, w_out_c, gn_c, ffn_w_up, ffn_w_down):
    scale = HEAD_DIM ** -0.5
    ab = []
    for j in range(w_in_ab.shape[0]):
        w = w_in_ab[j]
        o3 = 3 * A_W
        kb = w[:, o3 + B_QW:o3 + B_QW + B_KVW]
        vb = w[:, o3 + B_QW + B_KVW:]
        dup = lambda t: jnp.concatenate([t[:, :HEAD_DIM], t[:, :HEAD_DIM], t[:, HEAD_DIM:], t[:, HEAD_DIM:]], 1)
        w_ext = jnp.concatenate([w[:, :A_W] * scale, w[:, A_W:o3 + B_QW], dup(kb), dup(vb)], 1).astype(BF16)
        qg = (jnp.concatenate([qn_ab[j], qn_ab[j]]) * (scale * LOG2E)).reshape(1, LANES)
        kg = jnp.concatenate([kn_ab[j], kn_ab[j]]).reshape(1, LANES)
        ab.append((w_ext, qg, kg, w_out_ab[j].astype(BF16)))
    cc = [(w_in_c[j].astype(BF16), w_out_c[j].astype(BF16), gn_c[j].reshape(1, C_VW)) for j in range(w_in_c.shape[0])]
    return ab, cc, ffn_w_up.astype(BF16), ffn_w_down.astype(BF16)


def _trunk(x, prep, lb_fwd, lb_bwd, ln_mix_g, ln_mix_b, ln_ffn_g, ln_ffn_b, ffn_conv_w, ffn_conv_b):
    ab, cc, wu, wd = prep
    batch, seq, _ = x.shape
    x2d = x.reshape(batch * seq, D_MODEL)
    tabs = _rope_tables(seq)
    blk = jnp.arange(LANES) // HEAD_DIM
    ones2 = (blk[:, None] == blk[None, :]).astype(BF16)
    vec = lambda t: t.reshape(1, -1)
    for l in range(DEPTH):
        j = l // 2
        if l % 2 == 0:
            w_ext, qg, kg, w_out = ab[j]
            assert [d for _, d in A_PATTERNS] == [1, 4, DIL_CLASSES]
            assert all(window // (2 * d) == HALF_WINDOW for window, d in A_PATTERNS)
            qa, ka, va, qb, kd, vd, qa16, ka16, va16 = _ab_in(x2d, w_ext, tabs, qg, kg, ones2, batch, seq)
            o1, l1 = _band1(qa, ka, va, batch, seq)
            o16, l16 = _band16(qa16, ka16, va16, batch, seq)
            om, lm = _band4(qa16, ka16, va16, o16, l16, batch, seq)
            yb = _gqa(qb, kd, vd, batch, seq)
            x2d = _ab_out(o1, l1, om, lm, yb, x2d, w_out, vec(ln_mix_g[l]), vec(ln_mix_b[l]), seq)
        else:
            w_in, w_out, gn = cc[j]
            q, lf, lb, v, gate = _c_in(x2d, w_in, lb_fwd, lb_bwd, l)
            o_f = _hgrn_fwd(q, lf, v, batch, seq)
            x2d = _hgrn_bwd_out(q, lb, v, o_f, gate, gn, x2d, w_out, vec(ln_mix_g[l]), vec(ln_mix_b[l]), batch, seq)
        x2d = _ffn(x2d, wu[l], ffn_conv_w[l], vec(ffn_conv_b[l]), wd[l], vec(ln_ffn_g[l]), vec(ln_ffn_b[l]), seq)
    return x2d.reshape(batch, seq, D_MODEL)


def kernel(x_prompt, x_sample, w_in_ab, w_out_ab, qn_ab, kn_ab, w_in_c, w_out_c, lb_fwd, lb_bwd, gn_c, ln_mix_g, ln_mix_b, ln_ffn_g, ln_ffn_b, ffn_w_up, ffn_conv_w, ffn_conv_b, ffn_w_down):
    prep = _prep_weights(w_in_ab, w_out_ab, qn_ab, kn_ab, w_in_c, w_out_c, gn_c, ffn_w_up, ffn_w_down)
    rest = (lb_fwd, lb_bwd, ln_mix_g, ln_mix_b, ln_ffn_g, ln_ffn_b, ffn_conv_w, ffn_conv_b)
    return (_trunk(x_prompt, prep, *rest), _trunk(x_sample, prep, *rest))
```

```python
import functools

import jax
import jax.numpy as jnp
from jax import lax
from jax.experimental import pallas as pl
from jax.experimental.pallas import tpu as pltpu

F32 = jnp.float32
BF16 = jnp.bfloat16

D_MODEL = 1024
DEPTH = 2
HEAD_DIM = 64
A_HEADS = 8
A_PATTERNS = ((128, 1), (512, 4), (2048, 16))
B_HEADS = 8
B_KV_HEADS = 2
GRID_W = 64
ROPE_THETA = 500000.0
ROPE_DIM = HEAD_DIM // 4
AXIAL_THETA = 10000.0
C_HEADS = 8
C_KEY = 128
C_VAL = 128
D_FF = 2816
ALPHA = (2 * DEPTH) ** 0.25
LN_EPS = 1e-5
RMS_EPS = 1e-6
A_W = A_HEADS * HEAD_DIM
B_QW = B_HEADS * HEAD_DIM
B_KVW = B_KV_HEADS * HEAD_DIM
C_W = C_HEADS * C_KEY
C_VW = C_HEADS * C_VAL

LANES = 128
HALF_WINDOW = 64
TM = 512
PROJ_ROWS = 128
FFN_ROWS = 256
BAND_L = 128
BAND_STEP = 512
BAND4_STEP = 128
DIL_CLASSES = 16
BAND4_ROWS = 32
BAND4_HALO = 16
GQA_TQ = 512
GQA_TK = 1024
GQA_CHAIN = 512
FF_CHUNKS = (768, 768, 768, 512)
FFN_SUB = 1
HG_CHUNK = 64
HG_SUB = 16
HG_TB = 512
HG_HP = 4
HG_TB_OUT = 256
NEG = -1e30
LOG2E = 1.4426950408889634
VMEM_LIMIT = 56 * 1024 * 1024

_NT = (((1,), (1,)), ((), ()))
_TN = (((0,), (0,)), ((), ()))


def _params(sem, vmem=None):
    return pltpu.CompilerParams(dimension_semantics=sem, vmem_limit_bytes=vmem)


def _const_spec(shape, single=False):
    nd = len(shape)
    if single:
        return pl.BlockSpec(shape, lambda *_: (0,) * nd, pipeline_mode=pl.Buffered(1))
    return pl.BlockSpec(shape, lambda *_: (0,) * nd)


def _layer_norm(z, g, b):
    mu = jnp.mean(z, -1, keepdims=True)
    d = z - mu
    var = jnp.mean(d * d, -1, keepdims=True)
    return d * lax.rsqrt(var + LN_EPS) * g + b


def _sigmoid(z):
    return 1.0 / (1.0 + jnp.exp(-z))


def _row_block_dot(x, w, rows=PROJ_ROWS):
    m = x.shape[0]
    starts = list(range(0, m - m % rows, rows)) or [0]
    ends = starts[1:] + [m]
    parts = [jnp.dot(x[a:b], w, preferred_element_type=F32) for a, b in zip(starts, ends)]
    return parts[0] if len(parts) == 1 else jnp.concatenate(parts, axis=0)


def _pipelined(sections):
    pending = sections[0][0]()
    for i, (_, epilogue) in enumerate(sections):
        upcoming = sections[i + 1][0]() if i + 1 < len(sections) else None
        epilogue(pending)
        pending = upcoming


def _rope_tables(seq):
    pos = jnp.arange(seq, dtype=F32)
    d = jnp.arange(LANES) % HEAD_DIM

    def angles(p, dim, theta):
        freqs = theta ** (-(jnp.arange(0, dim, 2, dtype=F32) / dim))
        return p[:, None] * freqs[None, :]

    h = ROPE_DIM // 2
    ang = angles(pos, ROPE_DIM, ROPE_THETA)
    a = ang[:, d % h]
    lo, hi = d < h, (d >= h) & (d < ROPE_DIM)
    pc = jnp.where(lo | hi, jnp.cos(a), 1.0)
    psa = jnp.where(lo, -jnp.sin(a), 0.0)
    psb = jnp.where(hi, jnp.sin(a), 0.0)

    q = HEAD_DIM // 4
    row = jnp.floor(pos / GRID_W)
    col = pos - row * GRID_W
    ar = angles(row, HEAD_DIM // 2, AXIAL_THETA)[:, d % q]
    ac_ = angles(col, HEAD_DIM // 2, AXIAL_THETA)[:, d % q]
    a2 = jnp.where(d < HEAD_DIM // 2, ar, ac_)
    first = (d % (HEAD_DIM // 2)) < q
    ac = jnp.cos(a2)
    asa = jnp.where(first, -jnp.sin(a2), 0.0)
    asb = jnp.where(first, 0.0, jnp.sin(a2))
    return [t.astype(F32) for t in (pc, psa, psb, ac, asa, asb)]


def _rope(seg, c, sa, sb, shift):
    return seg * c + pltpu.roll(seg, LANES - shift, 1) * sa + pltpu.roll(seg, shift, 1) * sb


def _ab_in_kernel(x_ref, w_ref, pc_ref, psa_ref, psb_ref, ac_ref, asa_ref, asb_ref, qg_ref, kg_ref,
                  ones_ref, qa_ref, ka_ref, va_ref, qb_ref, kd_ref, vd_ref, qa16_ref, ka16_ref, va16_ref,
                  qs_ref, ks_ref, vs_ref):
    xb = x_ref[...].astype(BF16)

    def proj(a, n):
        return jnp.dot(xb, w_ref[:, a:a + n], preferred_element_type=F32)

    pc, psa, psb = pc_ref[...], psa_ref[...], psb_ref[...]
    ac, asa, asb = ac_ref[...], asa_ref[...], asb_ref[...]

    def dilated_a(h, rope, out_ref, out16_ref, slab_ref):
        for g in range(A_W // LANES):
            sl = slice(g * LANES, (g + 1) * LANES)
            y = _rope(h[:, sl], pc, psa, psb, ROPE_DIM // 2) if rope else h[:, sl]
            out_ref[:, sl] = y.astype(BF16)
            slab_ref[g] = y
        for c in range(DIL_CLASSES):
            for g in range(A_W // LANES):
                rows = slab_ref[g, pl.ds(c, TM // DIL_CLASSES, stride=DIL_CLASSES), :]
                out16_ref[0, c, :, g * LANES:(g + 1) * LANES] = rows.astype(BF16)

    def norm_rope(h, gain, out_ref):
        for g in range(h.shape[1] // LANES):
            sl = slice(g * LANES, (g + 1) * LANES)
            seg = h[:, sl]
            sq = seg * seg
            hi = sq.astype(BF16)
            lo = (sq - hi.astype(F32)).astype(BF16)
            ss = (jnp.dot(hi, ones_ref[...], preferred_element_type=F32)
                  + jnp.dot(lo, ones_ref[...], preferred_element_type=F32))
            y = seg * lax.rsqrt(ss * (1.0 / HEAD_DIM) + RMS_EPS) * gain
            out_ref[:, sl] = _rope(y, ac, asa, asb, HEAD_DIM // 4).astype(BF16)

    def cast_to(out_ref):
        def fn(h):
            out_ref[...] = h.astype(BF16)
        return fn

    o_qb = 3 * A_W
    o_kd = o_qb + B_QW
    o_vd = o_kd + 2 * B_KVW
    _pipelined([
        (functools.partial(proj, 0, A_W),
         functools.partial(dilated_a, rope=True, out_ref=qa_ref, out16_ref=qa16_ref, slab_ref=qs_ref)),
        (functools.partial(proj, A_W, A_W),
         functools.partial(dilated_a, rope=True, out_ref=ka_ref, out16_ref=ka16_ref, slab_ref=ks_ref)),
        (functools.partial(proj, 2 * A_W, A_W),
         functools.partial(dilated_a, rope=False, out_ref=va_ref, out16_ref=va16_ref, slab_ref=vs_ref)),
        (functools.partial(proj, o_qb, B_QW), functools.partial(norm_rope, gain=qg_ref[...], out_ref=qb_ref)),
        (functools.partial(proj, o_kd, 2 * B_KVW), functools.partial(norm_rope, gain=kg_ref[...], out_ref=kd_ref)),
        (functools.partial(proj, o_vd, 2 * B_KVW), cast_to(vd_ref)),
    ])


def _ab_in(x2d, w_ext, tabs, qg, kg, ones2, batch, seq):
    t = x2d.shape[0]
    nt = seq // TM
    tab = pl.BlockSpec((TM, LANES), lambda i: (i % nt, 0))
    row = lambda n: pl.BlockSpec((TM, n), lambda i: (i, 0))
    cls = pl.BlockSpec((1, DIL_CLASSES, TM // DIL_CLASSES, A_W), lambda i: (i // nt, 0, i % nt, 0))
    cls_shape = jax.ShapeDtypeStruct((batch, DIL_CLASSES, seq // DIL_CLASSES, A_W), BF16)
    wn = w_ext.shape[1]
    return pl.pallas_call(
        _ab_in_kernel,
        grid=(t // TM,),
        in_specs=[row(D_MODEL), _const_spec((D_MODEL, wn))] + [tab] * 6
        + [_const_spec((1, LANES)), _const_spec((1, LANES)), _const_spec((LANES, LANES))],
        out_specs=[row(A_W)] * 3 + [row(B_QW), row(2 * B_KVW), row(2 * B_KVW)] + [cls] * 3,
        out_shape=[jax.ShapeDtypeStruct((t, A_W), BF16)] * 3
        + [jax.ShapeDtypeStruct((t, B_QW), BF16)]
        + [jax.ShapeDtypeStruct((t, 2 * B_KVW), BF16)] * 2 + [cls_shape] * 3,
        scratch_shapes=[pltpu.VMEM((A_W // LANES, TM, LANES), F32)] * 3,
        compiler_params=_params(("arbitrary",), VMEM_LIMIT),
        name="ab_in",
    )(x2d, w_ext, *tabs, qg, kg, ones2)


def _band_groups(qs, ks, vs, bias, left):
    chains = [(g, sel) for g in range(len(qs)) for sel in (left, jnp.logical_not(left))]
    scores = []
    for g, sel in chains:
        qm = jnp.where(sel, qs[g], jnp.zeros_like(qs[g]))
        scores.append(lax.dot_general(qm, ks[g], _NT, preferred_element_type=F32) + bias)
    probs = []
    for s in scores:
        mx = jnp.max(s, axis=1, keepdims=True)
        pe = jnp.exp(s - mx)
        probs.append((mx, jnp.sum(pe, axis=1, keepdims=True), pe.astype(BF16)))
    outs = []
    for (g, _), (mx, l, pe) in zip(chains, probs):
        o = jnp.dot(pe, vs[g], preferred_element_type=F32)
        outs.append((o * (1.0 / l), mx + jnp.log(l)))
    return [(jnp.where(left, outs[2 * g][0], outs[2 * g + 1][0]), jnp.where(left, outs[2 * g][1], outs[2 * g + 1][1]))
            for g in range(len(qs))]


def _band_kernel(q_ref, kp_ref, kc_ref, kn_ref, vp_ref, vc_ref, vn_ref, o_ref, lse_ref, *, m, step_rows):
    nk = BAND_L + 2 * HALF_WINDOW
    base = pl.program_id(len(q_ref.shape) - 2) * step_rows
    r = lax.broadcasted_iota(jnp.int32, (BAND_L, nk), 0)
    j = lax.broadcasted_iota(jnp.int32, (BAND_L, nk), 1)
    band = jnp.where(jnp.abs(j - HALF_WINDOW - r) <= HALF_WINDOW, 0.0, NEG)
    left = lax.broadcasted_iota(jnp.int32, (BAND_L, LANES), 1) < HEAD_DIM
    pre = (0,) * (len(q_ref.shape) - 2)
    lanes = [slice(p * LANES, (p + 1) * LANES) for p in range(A_W // LANES)]
    at = lambda rows, sl: pre + (rows, sl)
    whole = slice(None)
    ks = [jnp.concatenate([kp_ref[at(whole, sl)], kc_ref[at(whole, sl)], kn_ref[at(whole, sl)]], axis=0) for sl in lanes]
    vs = [jnp.concatenate([vp_ref[at(whole, sl)], vc_ref[at(whole, sl)], vn_ref[at(whole, sl)]], axis=0) for sl in lanes]
    for blk in range(step_rows // BAND_L):
        r0 = blk * BAND_L
        rows = slice(r0, r0 + BAND_L)
        kpos = base + r0 - HALF_WINDOW + j
        bias = jnp.where(kpos >= 0, band, NEG)
        bias = jnp.where(kpos < m, bias, NEG)
        res = _band_groups([q_ref[at(rows, sl)] for sl in lanes], [kk[r0:r0 + nk] for kk in ks],
                           [vv[r0:r0 + nk] for vv in vs], bias, left)
        for sl, (o, lse) in zip(lanes, res):
            o_ref[at(rows, sl)] = o.astype(BF16)
            lse_ref[at(rows, sl)] = lse


def _band_specs(lead, rows, step_rows):
    nh = rows // HALF_WINDOW
    ratio = step_rows // HALF_WINDOW
    ones = (1,) * lead
    cur = pl.BlockSpec(ones + (step_rows, A_W), lambda *g: g[:lead] + (g[-1], 0))
    prev = pl.BlockSpec(ones + (HALF_WINDOW, A_W), lambda *g: g[:lead] + (jnp.maximum(g[-1] * ratio - 1, 0), 0))
    nxt = pl.BlockSpec(ones + (HALF_WINDOW, A_W), lambda *g: g[:lead] + (jnp.minimum((g[-1] + 1) * ratio, nh - 1), 0))
    return cur, prev, nxt


def _band1(q, k, v, batch, seq):
    step_rows = min(BAND_STEP, seq)
    cur, prev, nxt = _band_specs(1, seq, step_rows)
    view = lambda t: t.reshape(batch, seq, A_W)
    o, lse = pl.pallas_call(
        functools.partial(_band_kernel, m=seq, step_rows=step_rows),
        grid=(batch, seq // step_rows),
        in_specs=[cur, prev, cur, nxt, prev, cur, nxt],
        out_specs=[cur, cur],
        out_shape=[jax.ShapeDtypeStruct((batch, seq, A_W), BF16), jax.ShapeDtypeStruct((batch, seq, A_W), F32)],
        compiler_params=_params(("arbitrary",) * 2),
        name="band_d1",
    )(view(q), view(k), view(k), view(k), view(v), view(v), view(v))
    return o.reshape(batch * seq, A_W), lse.reshape(batch * seq, A_W)


def _band16(q16, k16, v16, batch, seq):
    m = seq // DIL_CLASSES
    step_rows = min(BAND_STEP, m)
    cur, prev, nxt = _band_specs(2, m, step_rows)
    shape = (batch, DIL_CLASSES, m, A_W)
    return pl.pallas_call(
        functools.partial(_band_kernel, m=m, step_rows=step_rows),
        grid=(batch, DIL_CLASSES, m // step_rows),
        in_specs=[cur, prev, cur, nxt, prev, cur, nxt],
        out_specs=[cur, cur],
        out_shape=[jax.ShapeDtypeStruct(shape, BF16), jax.ShapeDtypeStruct(shape, F32)],
        compiler_params=_params(("arbitrary",) * 3),
        name="band_d16",
    )(q16, k16, k16, k16, v16, v16, v16)


def _band4_kernel(q_ref, kp_ref, kc_ref, kn_ref, vp_ref, vc_ref, vn_ref, o16_ref, l16_ref, o_ref, lse_ref, *,
                  m16, step_rows):
    qr = BAND4_ROWS
    kr = qr + 2 * BAND4_HALO
    i = pl.program_id(2)
    shift_q, shift_k = qr.bit_length() - 1, kr.bit_length() - 1
    row = lax.broadcasted_iota(jnp.int32, (4 * qr, 4 * kr), 0)
    col = lax.broadcasted_iota(jnp.int32, (4 * qr, 4 * kr), 1)
    uq, rq = row >> shift_q, row & (qr - 1)
    uk, jk = col >> shift_k, col & (kr - 1)
    band = jnp.where(jnp.abs(4 * (jk - BAND4_HALO - rq) + (uk - uq)) <= HALF_WINDOW, 0.0, NEG)
    left = lax.broadcasted_iota(jnp.int32, (4 * qr, LANES), 1) < HEAD_DIM
    slabs = [slice(p * LANES, (p + 1) * LANES) for p in range(A_W // LANES)]
    window = lambda refs, u, sl: jnp.concatenate([r[0, u, 0, :, sl] for r in refs], axis=0)
    kwin = [[window([kp_ref, kc_ref, kn_ref], u, sl) for u in range(4)] for sl in slabs]
    vwin = [[window([vp_ref, vc_ref, vn_ref], u, sl) for u in range(4)] for sl in slabs]
    for blk in range(step_rows // qr):
        r0 = blk * qr
        rows = slice(r0, r0 + qr)
        kpos = i * step_rows + r0 - BAND4_HALO + jk
        bias = jnp.where(kpos >= 0, band, NEG)
        bias = jnp.where(kpos < m16, bias, NEG)
        classes = lambda ref, sl: jnp.concatenate([ref[0, u, 0, rows, sl] for u in range(4)], axis=0)
        branch4 = _band_groups([classes(q_ref, sl) for sl in slabs],
                               [jnp.concatenate([w[r0:r0 + kr] for w in kw], axis=0) for kw in kwin],
                               [jnp.concatenate([w[r0:r0 + kr] for w in vw], axis=0) for vw in vwin], bias, left)
        for sl, (o4, l4) in zip(slabs, branch4):
            o16 = classes(o16_ref, sl).astype(F32)
            l16 = classes(l16_ref, sl)
            mx = jnp.maximum(l4, l16)
            w4, w16 = jnp.exp(l4 - mx), jnp.exp(l16 - mx)
            tot = w4 + w16
            om = (w4 * o4 + w16 * o16) * (1.0 / tot)
            lm = mx + jnp.log(tot)
            for u in range(4):
                o_ref[0, u, 0, rows, sl] = om[u * qr:(u + 1) * qr].astype(BF16)
                lse_ref[0, u, 0, rows, sl] = lm[u * qr:(u + 1) * qr]


def _band4(q16, k16, v16, o16, l16, batch, seq):
    m16 = seq // DIL_CLASSES
    step_rows = min(BAND4_STEP, m16)
    ratio = step_rows // BAND4_HALO
    nh = m16 // BAND4_HALO
    view = lambda t: t.reshape(batch, 4, 4, m16, A_W)
    cur = pl.BlockSpec((1, 4, 1, step_rows, A_W), lambda b, c, i: (b, 0, c, i, 0))
    prev = pl.BlockSpec((1, 4, 1, BAND4_HALO, A_W), lambda b, c, i: (b, 0, c, jnp.maximum(i * ratio - 1, 0), 0))
    nxt = pl.BlockSpec((1, 4, 1, BAND4_HALO, A_W), lambda b, c, i: (b, 0, c, jnp.minimum((i + 1) * ratio, nh - 1), 0))
    shape = (batch, 4, 4, m16, A_W)
    o, lse = pl.pallas_call(
        functools.partial(_band4_kernel, m16=m16, step_rows=step_rows),
        grid=(batch, 4, m16 // step_rows),
        in_specs=[cur, prev, cur, nxt, prev, cur, nxt, cur, cur],
        out_specs=[cur, cur],
        out_shape=[jax.ShapeDtypeStruct(shape, BF16), jax.ShapeDtypeStruct(shape, F32)],
        compiler_params=_params(("arbitrary",) * 3),
        name="band_d4",
    )(view(q16), view(k16), view(k16), view(k16), view(v16), view(v16), view(v16), view(o16), view(l16))
    return o.reshape(batch, DIL_CLASSES, m16, A_W), lse.reshape(batch, DIL_CLASSES, m16, A_W)


def _gqa_kernel(q_ref, k_ref, v_ref, o_ref, qs_ref, m_ref, acc_ref, s_ref, vt_ref, *, seq):
    left = lax.broadcasted_iota(jnp.int32, (GQA_TQ, LANES), 1) < HEAD_DIM
    top = lax.broadcasted_iota(jnp.int32, (LANES, GQA_TQ), 0) < HEAD_DIM
    for pr in range(2):
        qt = q_ref[0, :, pr * LANES:(pr + 1) * LANES].astype(F32).T
        qs_ref[:, (2 * pr) * GQA_TQ:(2 * pr + 1) * GQA_TQ] = jnp.where(top, qt, 0.0).astype(BF16)
        qs_ref[:, (2 * pr + 1) * GQA_TQ:(2 * pr + 2) * GQA_TQ] = jnp.where(top, 0.0, qt).astype(BF16)
    m_ref[...] = jnp.full(m_ref.shape, NEG, F32)
    acc_ref[...] = jnp.zeros(acc_ref.shape, F32)
    n_tiles = seq // GQA_TK

    @pl.when(pl.program_id(2) == 0)
    def _():
        ones_row = lax.broadcasted_iota(jnp.int32, (LANES, GQA_TK), 0) == HEAD_DIM

        def transpose_tile(t, carry):
            off = pl.multiple_of(t * GQA_TK, GQA_TK)
            vt = v_ref[0, pl.ds(off, GQA_TK), :].astype(F32).T
            vt_ref[t] = jnp.where(ones_row, 1.0, vt).astype(BF16)
            return carry

        lax.fori_loop(0, n_tiles, transpose_tile, 0)

    chains = [slice(c * GQA_CHAIN, (c + 1) * GQA_CHAIN) for c in range(4 * GQA_TQ // GQA_CHAIN)]

    def scores(tile, buf, cols):
        off = pl.multiple_of(tile * GQA_TK, GQA_TK)
        k = k_ref[0, pl.ds(off, GQA_TK), :]
        s_ref[buf, :, cols] = jnp.dot(k, qs_ref[:, cols], preferred_element_type=F32)

    def softmax_pv(tile, buf, cols):
        s = s_ref[buf, :, cols]
        m_prev = m_ref[:, cols]
        m_new = jnp.maximum(m_prev, jnp.max(s, axis=0, keepdims=True))
        alpha = jnp.exp2(m_prev - m_new)
        p = jnp.exp2((s - m_new).astype(BF16))
        pv = jnp.dot(vt_ref[tile], p, preferred_element_type=F32)
        acc_ref[:, cols] = alpha * acc_ref[:, cols] + pv
        m_ref[:, cols] = m_new

    def step(tile, buf, prefetch):
        for cols in chains:
            if prefetch:
                scores(tile + 1, 1 - buf, cols)
            softmax_pv(tile, buf, cols)

    def pair(j, carry):
        step(2 * j, 0, True)
        step(2 * j + 1, 1, True)
        return carry

    for cols in chains:
        scores(0, 0, cols)
    lax.fori_loop(0, n_tiles // 2 - 1, pair, 0)
    step(n_tiles - 2, 0, True)
    step(n_tiles - 1, 1, False)
    acc = acc_ref[...]
    a = (acc * (1.0 / acc[HEAD_DIM:HEAD_DIM + 1, :])).T
    for pr in range(2):
        lo = a[(2 * pr) * GQA_TQ:(2 * pr + 1) * GQA_TQ]
        hi = pltpu.roll(a[(2 * pr + 1) * GQA_TQ:(2 * pr + 2) * GQA_TQ], HEAD_DIM, 1)
        o_ref[0, :, pr * LANES:(pr + 1) * LANES] = jnp.where(left, lo, hi).astype(BF16)


def _gqa(qb, kd, vd, batch, seq):
    gw = B_QW // B_KV_HEADS
    cols = 4 * GQA_TQ
    out = pl.pallas_call(
        functools.partial(_gqa_kernel, seq=seq),
        grid=(batch, B_KV_HEADS, seq // GQA_TQ),
        in_specs=[pl.BlockSpec((1, GQA_TQ, gw), lambda b, g, qi: (b, qi, g)),
                  pl.BlockSpec((1, seq, LANES), lambda b, g, qi: (b, 0, g)),
                  pl.BlockSpec((1, seq, LANES), lambda b, g, qi: (b, 0, g))],
        out_specs=pl.BlockSpec((1, GQA_TQ, gw), lambda b, g, qi: (b, qi, g)),
        out_shape=jax.ShapeDtypeStruct((batch, seq, B_QW), BF16),
        scratch_shapes=[pltpu.VMEM((LANES, cols), BF16), pltpu.VMEM((1, cols), F32),
                        pltpu.VMEM((LANES, cols), F32), pltpu.VMEM((2, GQA_TK, cols), F32),
                        pltpu.VMEM((seq // GQA_TK, LANES, GQA_TK), BF16)],
        compiler_params=_params(("arbitrary",) * 3, VMEM_LIMIT),
        name="gqa",
    )(qb.reshape(batch, seq, B_QW), kd.reshape(batch, seq, 2 * B_KVW), vd.reshape(batch, seq, 2 * B_KVW))
    return out.reshape(batch * seq, B_QW)


def _ab_out_kernel(o1_ref, l1_ref, om_ref, lm_ref, yb_ref, x_ref, w_ref, g_ref, b_ref, out_ref, os_ref, ls_ref):
    for c in range(DIL_CLASSES):
        for g in range(A_W // LANES):
            sl = slice(g * LANES, (g + 1) * LANES)
            rows = pl.ds(c, TM // DIL_CLASSES, stride=DIL_CLASSES)
            os_ref[g, rows, :] = om_ref[0, c, :, sl].astype(F32)
            ls_ref[g, rows, :] = lm_ref[0, c, :, sl]
    ya = []
    for g in range(A_W // LANES):
        sl = slice(g * LANES, (g + 1) * LANES)
        l1, lm = l1_ref[:, sl], ls_ref[g]
        mx = jnp.maximum(l1, lm)
        e1, em = jnp.exp(l1 - mx), jnp.exp(lm - mx)
        ya.append(((e1 * o1_ref[:, sl].astype(F32) + em * os_ref[g]) * (1.0 / (e1 + em))).astype(BF16))
    y = _row_block_dot(jnp.concatenate(ya + [yb_ref[...]], axis=1), w_ref[...])
    out_ref[...] = _layer_norm(ALPHA * x_ref[...] + y, g_ref[...], b_ref[...])


def _ab_out(o1, l1, om, lm, yb, x2d, w, g, b, seq):
    t = x2d.shape[0]
    nt = seq // TM
    row = lambda n: pl.BlockSpec((TM, n), lambda i: (i, 0))
    cls = pl.BlockSpec((1, DIL_CLASSES, TM // DIL_CLASSES, A_W), lambda i: (i // nt, 0, i % nt, 0))
    return pl.pallas_call(
        _ab_out_kernel,
        grid=(t // TM,),
        in_specs=[row(A_W), row(A_W), cls, cls, row(B_QW), row(D_MODEL), _const_spec((A_W + B_QW, D_MODEL)),
                  _const_spec((1, D_MODEL)), _const_spec((1, D_MODEL))],
        out_specs=row(D_MODEL),
        out_shape=jax.ShapeDtypeStruct((t, D_MODEL), F32),
        scratch_shapes=[pltpu.VMEM((A_W // LANES, TM, LANES), F32)] * 2,
        compiler_params=_params(("arbitrary",), VMEM_LIMIT),
        name="ab_out",
    )(o1, l1, om, lm, yb, x2d, w, g, b)


def _gelu(x):
    return 0.5 * x * (1.0 + jnp.tanh(0.7978845608028654 * (x + 0.044715 * (x * x * x))))


def _ffn_kernel(x_ref, xp_ref, xn_ref, wu_ref, cw_ref, cb_ref, wd_ref, g_ref, b_ref, out_ref, *, nt):
    i = pl.program_id(0)
    keep_prev = (i % nt != 0).astype(F32)
    keep_next = (i % nt != nt - 1).astype(F32)
    rows = lax.broadcasted_iota(jnp.int32, (TM, 1), 0)
    offsets = [sum(FF_CHUNKS[:c]) for c in range(len(FF_CHUNKS))]
    operands, acc = {}, {}

    def tile_operands(j):
        if j not in operands:
            r0 = j * TM
            xb = x_ref[r0:r0 + TM, :].astype(BF16)
            before = xp_ref[...] if j == 0 else x_ref[r0 - 8:r0, :]
            after = xn_ref[...] if j == FFN_SUB - 1 else x_ref[r0 + TM:r0 + TM + 8, :]
            halo = jnp.concatenate([before, after], axis=0).astype(BF16)
            operands[j] = (xb, jnp.concatenate([xb, halo], axis=0))
        return operands[j]

    def up(j, c):
        a, n = offsets[c], FF_CHUNKS[c]
        xb, xe = tile_operands(j)
        u = _row_block_dot(xb, wu_ref[:, a:a + n], FFN_ROWS)
        ge = _row_block_dot(xe, wu_ref[:, D_FF + a:D_FF + a + n], FFN_ROWS)
        return u, ge

    def finish(j, c, pending):
        u, ge = pending
        a, n = offsets[c], FF_CHUNKS[c]
        gm = ge[0:TM]
        g_before = ge[TM + 7:TM + 8]
        g_after = ge[TM + 8:TM + 9]
        if j == 0:
            g_before = g_before * keep_prev
        if j == FFN_SUB - 1:
            g_after = g_after * keep_next
        gp = jnp.where(rows == 0, g_before, pltpu.roll(gm, 1, 0))
        gn = jnp.where(rows == TM - 1, g_after, pltpu.roll(gm, TM - 1, 0))
        cw = cw_ref[:, a:a + n]
        gc = gp * cw[0:1] + gm * cw[1:2] + gn * cw[2:3] + cb_ref[:, a:a + n]
        act = (_gelu(gc) * u).astype(BF16)
        part = _row_block_dot(act, wd_ref[a:a + n, :], FFN_ROWS)
        acc[j] = part if c == 0 else acc[j] + part
        if c == len(FF_CHUNKS) - 1:
            sl = slice(j * TM, (j + 1) * TM)
            out_ref[sl, :] = _layer_norm(ALPHA * x_ref[sl, :] + acc[j], g_ref[...], b_ref[...])

    _pipelined([(functools.partial(up, j, c), functools.partial(finish, j, c))
                for j in range(FFN_SUB) for c in range(len(FF_CHUNKS))])


def _ffn(x2d, wu, cw, cb, wd, g, b, seq):
    t = x2d.shape[0]
    step = FFN_SUB * TM
    nt = seq // step
    r8 = step // 8
    return pl.pallas_call(
        functools.partial(_ffn_kernel, nt=nt),
        grid=(t // step,),
        in_specs=[pl.BlockSpec((step, D_MODEL), lambda i: (i, 0)),
                  pl.BlockSpec((8, D_MODEL), lambda i: (jnp.maximum(i * r8 - 1, 0), 0)),
                  pl.BlockSpec((8, D_MODEL), lambda i: (jnp.minimum((i + 1) * r8, t // 8 - 1), 0)),
                  _const_spec((D_MODEL, 2 * D_FF), single=True),
                  _const_spec((3, D_FF)), _const_spec((1, D_FF)),
                  _const_spec((D_FF, D_MODEL), single=True),
                  _const_spec((1, D_MODEL)), _const_spec((1, D_MODEL))],
        out_specs=pl.BlockSpec((step, D_MODEL), lambda i: (i, 0)),
        out_shape=jax.ShapeDtypeStruct((t, D_MODEL), F32),
        compiler_params=_params(("arbitrary",), VMEM_LIMIT),
        name="ffn",
    )(x2d, x2d, x2d, wu, cw, cb, wd, g, b)


def _lower_bound(tbl, layer):
    rows = [tbl[r:r + 1] for r in range(DEPTH)]
    mx = functools.reduce(jnp.maximum, rows)
    es = [jnp.exp(r - mx) for r in rows]
    inv = 1.0 / functools.reduce(lambda a, b: a + b, es)
    ps = [e * inv for e in es]
    return functools.reduce(lambda a, b: a + b, ps[:layer + 1]) - ps[0]


def _c_in_kernel(x_ref, w_ref, lbf_ref, lbb_ref, q_ref, lf_ref, lb_ref, v_ref, g_ref, *, layer):
    half = C_W // 2
    blocks = [slice(r * PROJ_ROWS, (r + 1) * PROJ_ROWS) for r in range(TM // PROJ_ROWS)]
    xbs = [x_ref[rows, :].astype(BF16) for rows in blocks]

    def proj(r, a, n):
        return jnp.dot(xbs[r], w_ref[:, a:a + n], preferred_element_type=F32)

    def silu(z):
        return z * _sigmoid(z)

    lbf = _lower_bound(lbf_ref[...], layer)
    lbb = _lower_bound(lbb_ref[...], layer)

    def store_q(z, rows, sl):
        q_ref[rows, sl] = z.astype(BF16)

    def store_logf(tbl, out_ref):
        def fn(z, rows, sl):
            lb = tbl[:, sl]
            out_ref[rows, sl] = jnp.log(lb + (1.0 - lb) * _sigmoid(z))
        return fn

    def store_silu(out_ref):
        def fn(z, rows, sl):
            out_ref[rows, sl] = silu(z).astype(BF16)
        return fn

    sections = []
    for r, rows in enumerate(blocks):
        for c in range(2):
            a = c * half
            sl = slice(a, a + half)
            for off, fn in ((0, store_q), (C_W, store_logf(lbf, lf_ref)), (2 * C_W, store_logf(lbb, lb_ref)),
                            (3 * C_W, store_silu(v_ref)), (3 * C_W + C_VW, store_silu(g_ref))):
                sections.append((functools.partial(proj, r, off + a, half), functools.partial(fn, rows=rows, sl=sl)))
    _pipelined(sections)


def _c_in(x2d, w, lbf, lbb, layer):
    t = x2d.shape[0]
    row = pl.BlockSpec((TM, C_W), lambda i: (i, 0))
    return pl.pallas_call(
        functools.partial(_c_in_kernel, layer=layer),
        grid=(t // TM,),
        in_specs=[row, _const_spec(w.shape), _const_spec((DEPTH, C_W)), _const_spec((DEPTH, C_W))],
        out_specs=[row] * 5,
        out_shape=[jax.ShapeDtypeStruct((t, C_W), d) for d in (BF16, F32, F32, BF16, BF16)],
        compiler_params=_params(("arbitrary",), VMEM_LIMIT),
        name="c_in",
    )(x2d, w, lbf, lbb)


def _hgrn_kernel(*refs, rev, fused, heads, step_rows):
    if fused:
        (q_ref, lf_ref, v_ref, of_ref, gate_ref, gn_ref, x_ref, w_ref, lg_ref, lb_ref,
         out_ref, st_ref, y_ref) = refs
    else:
        q_ref, lf_ref, v_ref, y_ref, st_ref = refs

    @pl.when(pl.program_id(2) == 0)
    def _():
        st_ref[...] = jnp.zeros(st_ref.shape, F32)

    ri = lax.broadcasted_iota(jnp.int32, (HG_CHUNK, HG_CHUNK), 0)
    ci = lax.broadcasted_iota(jnp.int32, (HG_CHUNK, HG_CHUNK), 1)
    tri = (ci >= ri) if rev else (ci <= ri)
    trib = jnp.where(tri, 1.0, 0.0).astype(BF16)
    dotf = lambda a, b: jnp.dot(a, b, preferred_element_type=F32)
    n_chunks = step_rows // HG_CHUNK
    order = range(n_chunks - 1, -1, -1) if rev else range(n_chunks)
    units = [(c, h) for c in order for h in range(heads)]
    window = lambda u: (0, slice(u[0] * HG_CHUNK, (u[0] + 1) * HG_CHUNK), slice(u[1] * C_KEY, (u[1] + 1) * C_KEY))

    b_all = {}
    for u in units:
        lf = lf_ref[window(u)]
        hi = lf.astype(BF16)
        lo = (lf - hi.astype(F32)).astype(BF16)
        b_all[u] = dotf(trib, hi) + dotf(trib, lo)

    qdec, kdec, etot, a_rows = {}, {}, {}, {}
    for u in units:
        lf, b = lf_ref[window(u)], b_all[u]
        qf = q_ref[window(u)].astype(F32)
        bex = b - lf
        kk = 1.0 - jnp.exp(lf)
        btot = b[0:1] if rev else b[HG_CHUNK - 1:HG_CHUNK]
        qdec[u] = (qf * jnp.exp(b)).astype(BF16)
        kdec[u] = (kk * jnp.exp(btot - b)).astype(BF16)
        etot[u] = jnp.exp(btot)
        rows = []
        for blk in range(HG_CHUNK // HG_SUB):
            r0 = blk * HG_SUB
            ref = bex[r0 + HG_SUB - 1:r0 + HG_SUB] if rev else bex[r0:r0 + 1]
            qt = (qf[r0:r0 + HG_SUB] * jnp.exp(b[r0:r0 + HG_SUB] - ref)).astype(BF16)
            lo_r, hi_r = (r0, HG_CHUNK) if rev else (0, r0 + HG_SUB)
            ks = (kk[lo_r:hi_r] * jnp.exp(ref - b[lo_r:hi_r])).astype(BF16)
            pieces = []
            if lo_r > 0:
                pieces.append(jnp.zeros((lo_r, C_KEY), BF16))
            pieces.append(ks)
            if hi_r < HG_CHUNK:
                pieces.append(jnp.zeros((HG_CHUNK - hi_r, C_KEY), BF16))
            kfull = jnp.concatenate(pieces, axis=0) if len(pieces) > 1 else ks
            rows.append(lax.dot_general(qt, kfull, _NT, preferred_element_type=F32))
        a_rows[u] = rows
    attn = {u: jnp.where(tri, jnp.concatenate(a_rows[u], axis=0), 0.0).astype(BF16) for u in units}

    states = [st_ref[h] for h in range(heads)]
    for u in units:
        h = u[1]
        vb = v_ref[window(u)]
        o = lax.dot_general(qdec[u], states[h].astype(BF16), _NT, preferred_element_type=F32)
        o = o + dotf(attn[u], vb)
        states[h] = states[h] * etot[u] + lax.dot_general(vb, kdec[u], _TN, preferred_element_type=F32)
        if fused:
            tot = of_ref[window(u)] + o
            inv = lax.rsqrt(jnp.mean(tot * tot, -1, keepdims=True) + RMS_EPS)
            y = tot * inv * gn_ref[:, window(u)[2]] * gate_ref[window(u)].astype(F32)
            y_ref[window(u)[1:]] = y.astype(BF16)
        else:
            y_ref[window(u)] = o
    for h in range(heads):
        st_ref[h] = states[h]
    if fused:
        proj = _row_block_dot(y_ref[...], w_ref[...])
        out_ref[0] = _layer_norm(ALPHA * x_ref[0] + proj, lg_ref[...], lb_ref[...])


def _hgrn_specs(batch, seq, rev, heads, step_rows):
    nb = seq // step_rows
    blk = (lambda b, h, i: (b, nb - 1 - i, h)) if rev else (lambda b, h, i: (b, i, h))
    return pl.BlockSpec((1, step_rows, heads * C_KEY), blk), (batch, C_HEADS // heads, nb)


def _hgrn_fwd(q, lf, v, batch, seq):
    tile, grid = _hgrn_specs(batch, seq, False, HG_HP, HG_TB)
    view = lambda t: t.reshape(batch, seq, C_W)
    out = pl.pallas_call(
        functools.partial(_hgrn_kernel, rev=False, fused=False, heads=HG_HP, step_rows=HG_TB),
        grid=grid,
        in_specs=[tile, tile, tile],
        out_specs=tile,
        out_shape=jax.ShapeDtypeStruct((batch, seq, C_VW), F32),
        scratch_shapes=[pltpu.VMEM((HG_HP, C_VAL, C_KEY), F32)],
        compiler_params=_params(("arbitrary",) * 3),
        name="hgrn_fwd",
    )(view(q), view(lf), view(v))
    return out.reshape(batch * seq, C_VW)


def _hgrn_bwd_out(q, lf, v, o_f, gate, gn, x2d, w, g, b, batch, seq):
    tile, grid = _hgrn_specs(batch, seq, True, C_HEADS, HG_TB_OUT)
    view = lambda t: t.reshape(batch, seq, C_W)
    out = pl.pallas_call(
        functools.partial(_hgrn_kernel, rev=True, fused=True, heads=C_HEADS, step_rows=HG_TB_OUT),
        grid=grid,
        in_specs=[tile] * 5 + [_const_spec((1, C_VW)), tile, _const_spec((C_VW, D_MODEL), single=True),
                               _const_spec((1, D_MODEL)), _const_spec((1, D_MODEL))],
        out_specs=tile,
        out_shape=jax.ShapeDtypeStruct((batch, seq, D_MODEL), F32),
        scratch_shapes=[pltpu.VMEM((C_HEADS, C_VAL, C_KEY), F32), pltpu.VMEM((HG_TB_OUT, C_VW), BF16)],
        compiler_params=_params(("arbitrary",) * 3, VMEM_LIMIT),
        name="hgrn_bwd_out",
    )(view(q), view(lf), view(v), view(o_f), view(gate), gn, x2d.reshape(batch, seq, D_MODEL), w, g, b)
    return out.reshape(batch * seq, D_MODEL)


def _prep_weights(w_in_ab, w_out_ab, qn_ab, kn_ab, w_in_c, w_out_c, gn_c, ffn_w_up, ffn_w_down):
    scale = HEAD_DIM ** -0.5
    ab = []
    for j in range(w_in_ab.shape[0]):
        w = w_in_ab[j]
        o3 = 3 * A_W
        kb = w[:, o3 + B_QW:o3 + B_QW + B_KVW]
        vb = w[:, o3 + B_QW + B_KVW:]
        dup = lambda t: jnp.concatenate([t[:, :HEAD_DIM], t[:, :HEAD_DIM], t[:, HEAD_DIM:], t[:, HEAD_DIM:]], 1)
        w_ext = jnp.concatenate([w[:, :A_W] * scale, w[:, A_W:o3 + B_QW], dup(kb), dup(vb)], 1).astype(BF16)
        qg = (jnp.concatenate([qn_ab[j], qn_ab[j]]) * (scale * LOG2E)).reshape(1, LANES)
        kg = jnp.concatenate([kn_ab[j], kn_ab[j]]).reshape(1, LANES)
        ab.append((w_ext, qg, kg, w_out_ab[j].astype(BF16)))
    cc = [(w_in_c[j].astype(BF16), w_out_c[j].astype(BF16), gn_c[j].reshape(1, C_VW)) for j in range(w_in_c.shape[0])]
    return ab, cc, ffn_w_up.astype(BF16), ffn_w_down.astype(BF16)


def _trunk(x, prep, lb_fwd, lb_bwd, ln_mix_g, ln_mix_b, ln_ffn_g, ln_ffn_b, ffn_conv_w, ffn_conv_b):
    ab, cc, wu, wd = prep
    batch, seq, _ = x.shape
    x2d = x.reshape(batch * seq, D_MODEL)
    tabs = _rope_tables(seq)
    blk = jnp.arange(LANES) // HEAD_DIM
    ones2 = (blk[:, None] == blk[None, :]).astype(BF16)
    vec = lambda t: t.reshape(1, -1)
    for l in range(DEPTH):
        j = l // 2
        if l % 2 == 0:
            w_ext, qg, kg, w_out = ab[j]
            assert [d for _, d in A_PATTERNS] == [1, 4, DIL_CLASSES]
            assert all(window // (2 * d) == HALF_WINDOW for window, d in A_PATTERNS)
            qa, ka, va, qb, kd, vd, qa16, ka16, va16 = _ab_in(x2d, w_ext, tabs, qg, kg, ones2, batch, seq)
            o1, l1 = _band1(qa, ka, va, batch, seq)
            o16, l16 = _band16(qa16, ka16, va16, batch, seq)
            om, lm = _band4(qa16, ka16, va16, o16, l16, batch, seq)
            yb = _gqa(qb, kd, vd, batch, seq)
            x2d = _ab_out(o1, l1, om, lm, yb, x2d, w_out, vec(ln_mix_g[l]), vec(ln_mix_b[l]), seq)
        else:
            w_in, w_out, gn = cc[j]
            q, lf, lb, v, gate = _c_in(x2d, w_in, lb_fwd, lb_bwd, l)
            o_f = _hgrn_fwd(q, lf, v, batch, seq)
            x2d = _hgrn_bwd_out(q, lb, v, o_f, gate, gn, x2d, w_out, vec(ln_mix_g[l]), vec(ln_mix_b[l]), batch, seq)
        x2d = _ffn(x2d, wu[l], ffn_conv_w[l], vec(ffn_conv_b[l]), wd[l], vec(ln_ffn_g[l]), vec(ln_ffn_b[l]), seq)
    return x2d.reshape(batch, seq, D_MODEL)


def kernel(x_prompt, x_sample, w_in_ab, w_out_ab, qn_ab, kn_ab, w_in_c, w_out_c, lb_fwd, lb_bwd, gn_c, ln_mix_g, ln_mix_b, ln_ffn_g, ln_ffn_b, ffn_w_up, ffn_conv_w, ffn_conv_b, ffn_w_down):
    prep = _prep_weights(w_in_ab, w_out_ab, qn_ab, kn_ab, w_in_c, w_out_c, gn_c, ffn_w_up, ffn_w_down)
    rest = (lb_fwd, lb_bwd, ln_mix_g, ln_mix_b, ln_ffn_g, ln_ffn_b, ffn_conv_w, ffn_conv_b)
    return (_trunk(x_prompt, prep, *rest), _trunk(x_sample, prep, *rest))
```

```python
import functools

import jax
import jax.numpy as jnp
from jax import lax
from jax.experimental import pallas as pl
from jax.experimental.pallas import tpu as pltpu

F32 = jnp.float32
BF16 = jnp.bfloat16

D_MODEL = 1024
DEPTH = 2
HEAD_DIM = 64
A_HEADS = 8
A_PATTERNS = ((128, 1), (512, 4), (2048, 16))
B_HEADS = 8
B_KV_HEADS = 2
GRID_W = 64
ROPE_THETA = 500000.0
ROPE_DIM = HEAD_DIM // 4
AXIAL_THETA = 10000.0
C_HEADS = 8
C_KEY = 128
C_VAL = 128
D_FF = 2816
ALPHA = (2 * DEPTH) ** 0.25
LN_EPS = 1e-5
RMS_EPS = 1e-6
A_W = A_HEADS * HEAD_DIM
B_QW = B_HEADS * HEAD_DIM
B_KVW = B_KV_HEADS * HEAD_DIM
C_W = C_HEADS * C_KEY
C_VW = C_HEADS * C_VAL

LANES = 128
HALF_WINDOW = 64
TM = 512
PROJ_ROWS = 128
FFN_ROWS = 256
BAND_L = 128
BAND_STEP = 512
BAND4_STEP = 128
DIL_CLASSES = 16
BAND4_ROWS = 32
BAND4_HALO = 16
GQA_TQ = 512
GQA_TK = 512
GQA_CHAIN = 512
GQA_V_ROWS = 80
FF_CHUNKS = (768, 768, 768, 512)
FFN_SUB = 1
HG_CHUNK = 64
HG_SUB = 16
HG_TB = 512
HG_HP = 4
HG_TB_OUT = 256
NEG = -1e30
LOG2E = 1.4426950408889634
VMEM_LIMIT = 56 * 1024 * 1024

_NT = (((1,), (1,)), ((), ()))
_TN = (((0,), (0,)), ((), ()))


def _params(sem, vmem=None):
    return pltpu.CompilerParams(dimension_semantics=sem, vmem_limit_bytes=vmem)


def _const_spec(shape, single=False):
    nd = len(shape)
    if single:
        return pl.BlockSpec(shape, lambda *_: (0,) * nd, pipeline_mode=pl.Buffered(1))
    return pl.BlockSpec(shape, lambda *_: (0,) * nd)


def _layer_norm(z, g, b):
    mu = jnp.mean(z, -1, keepdims=True)
    d = z - mu
    var = jnp.mean(d * d, -1, keepdims=True)
    return d * lax.rsqrt(var + LN_EPS) * g + b


def _sigmoid(z):
    return 1.0 / (1.0 + jnp.exp(-z))


def _row_block_dot(x, w, rows=PROJ_ROWS):
    m = x.shape[0]
    starts = list(range(0, m - m % rows, rows)) or [0]
    ends = starts[1:] + [m]
    parts = [jnp.dot(x[a:b], w, preferred_element_type=F32) for a, b in zip(starts, ends)]
    return parts[0] if len(parts) == 1 else jnp.concatenate(parts, axis=0)


def _pipelined(sections):
    pending = sections[0][0]()
    for i, (_, epilogue) in enumerate(sections):
        upcoming = sections[i + 1][0]() if i + 1 < len(sections) else None
        epilogue(pending)
        pending = upcoming


def _rope_tables(seq):
    pos = jnp.arange(seq, dtype=F32)
    d = jnp.arange(LANES) % HEAD_DIM

    def angles(p, dim, theta):
        freqs = theta ** (-(jnp.arange(0, dim, 2, dtype=F32) / dim))
        return p[:, None] * freqs[None, :]

    h = ROPE_DIM // 2
    ang = angles(pos, ROPE_DIM, ROPE_THETA)
    a = ang[:, d % h]
    lo, hi = d < h, (d >= h) & (d < ROPE_DIM)
    pc = jnp.where(lo | hi, jnp.cos(a), 1.0)
    psa = jnp.where(lo, -jnp.sin(a), 0.0)
    psb = jnp.where(hi, jnp.sin(a), 0.0)

    q = HEAD_DIM // 4
    row = jnp.floor(pos / GRID_W)
    col = pos - row * GRID_W
    ar = angles(row, HEAD_DIM // 2, AXIAL_THETA)[:, d % q]
    ac_ = angles(col, HEAD_DIM // 2, AXIAL_THETA)[:, d % q]
    a2 = jnp.where(d < HEAD_DIM // 2, ar, ac_)
    first = (d % (HEAD_DIM // 2)) < q
    ac = jnp.cos(a2)
    asa = jnp.where(first, -jnp.sin(a2), 0.0)
    asb = jnp.where(first, 0.0, jnp.sin(a2))
    return [t.astype(F32) for t in (pc, psa, psb, ac, asa, asb)]


def _rope(seg, c, sa, sb, shift):
    return seg * c + pltpu.roll(seg, LANES - shift, 1) * sa + pltpu.roll(seg, shift, 1) * sb


def _ab_in_kernel(x_ref, w_ref, pc_ref, psa_ref, psb_ref, ac_ref, asa_ref, asb_ref, qg_ref, kg_ref,
                  ones_ref, qa_ref, ka_ref, va_ref, qb_ref, kd_ref, vd_ref, qa16_ref, ka16_ref, va16_ref,
                  qs_ref, ks_ref, vs_ref):
    xb = x_ref[...].astype(BF16)

    def proj(a, n):
        return jnp.dot(xb, w_ref[:, a:a + n], preferred_element_type=F32)

    pc, psa, psb = pc_ref[...], psa_ref[...], psb_ref[...]
    ac, asa, asb = ac_ref[...], asa_ref[...], asb_ref[...]

    def dilated_a(h, rope, out_ref, out16_ref, slab_ref):
        for g in range(A_W // LANES):
            sl = slice(g * LANES, (g + 1) * LANES)
            y = _rope(h[:, sl], pc, psa, psb, ROPE_DIM // 2) if rope else h[:, sl]
            out_ref[:, sl] = y.astype(BF16)
            slab_ref[g] = y
        for c in range(DIL_CLASSES):
            for g in range(A_W // LANES):
                rows = slab_ref[g, pl.ds(c, TM // DIL_CLASSES, stride=DIL_CLASSES), :]
                out16_ref[0, c, :, g * LANES:(g + 1) * LANES] = rows.astype(BF16)

    def norm_rope(h, gain, out_ref):
        for g in range(h.shape[1] // LANES):
            sl = slice(g * LANES, (g + 1) * LANES)
            seg = h[:, sl]
            sq = seg * seg
            hi = sq.astype(BF16)
            lo = (sq - hi.astype(F32)).astype(BF16)
            ss = (jnp.dot(hi, ones_ref[...], preferred_element_type=F32)
                  + jnp.dot(lo, ones_ref[...], preferred_element_type=F32))
            y = seg * lax.rsqrt(ss * (1.0 / HEAD_DIM) + RMS_EPS) * gain
            out_ref[:, sl] = _rope(y, ac, asa, asb, HEAD_DIM // 4).astype(BF16)

    def cast_to(out_ref):
        def fn(h):
            out_ref[...] = h.astype(BF16)
        return fn

    o_qb = 3 * A_W
    o_kd = o_qb + B_QW
    o_vd = o_kd + 2 * B_KVW
    _pipelined([
        (functools.partial(proj, 0, A_W),
         functools.partial(dilated_a, rope=True, out_ref=qa_ref, out16_ref=qa16_ref, slab_ref=qs_ref)),
        (functools.partial(proj, A_W, A_W),
         functools.partial(dilated_a, rope=True, out_ref=ka_ref, out16_ref=ka16_ref, slab_ref=ks_ref)),
        (functools.partial(proj, 2 * A_W, A_W),
         functools.partial(dilated_a, rope=False, out_ref=va_ref, out16_ref=va16_ref, slab_ref=vs_ref)),
        (functools.partial(proj, o_qb, B_QW), functools.partial(norm_rope, gain=qg_ref[...], out_ref=qb_ref)),
        (functools.partial(proj, o_kd, 2 * B_KVW), functools.partial(norm_rope, gain=kg_ref[...], out_ref=kd_ref)),
        (functools.partial(proj, o_vd, 2 * B_KVW), cast_to(vd_ref)),
    ])


def _ab_in(x2d, w_ext, tabs, qg, kg, ones2, batch, seq):
    t = x2d.shape[0]
    nt = seq // TM
    tab = pl.BlockSpec((TM, LANES), lambda i: (i % nt, 0))
    row = lambda n: pl.BlockSpec((TM, n), lambda i: (i, 0))
    cls = pl.BlockSpec((1, DIL_CLASSES, TM // DIL_CLASSES, A_W), lambda i: (i // nt, 0, i % nt, 0))
    cls_shape = jax.ShapeDtypeStruct((batch, DIL_CLASSES, seq // DIL_CLASSES, A_W), BF16)
    wn = w_ext.shape[1]
    return pl.pallas_call(
        _ab_in_kernel,
        grid=(t // TM,),
        in_specs=[row(D_MODEL), _const_spec((D_MODEL, wn))] + [tab] * 6
        + [_const_spec((1, LANES)), _const_spec((1, LANES)), _const_spec((LANES, LANES))],
        out_specs=[row(A_W)] * 3 + [row(B_QW), row(2 * B_KVW), row(2 * B_KVW)] + [cls] * 3,
        out_shape=[jax.ShapeDtypeStruct((t, A_W), BF16)] * 3
        + [jax.ShapeDtypeStruct((t, B_QW), BF16)]
        + [jax.ShapeDtypeStruct((t, 2 * B_KVW), BF16)] * 2 + [cls_shape] * 3,
        scratch_shapes=[pltpu.VMEM((A_W // LANES, TM, LANES), F32)] * 3,
        compiler_params=_params(("arbitrary",), VMEM_LIMIT),
        name="ab_in",
    )(x2d, w_ext, *tabs, qg, kg, ones2)


def _band_groups(qs, ks, vs, bias, left):
    chains = [(g, sel) for g in range(len(qs)) for sel in (left, jnp.logical_not(left))]
    scores = []
    for g, sel in chains:
        qm = jnp.where(sel, qs[g], jnp.zeros_like(qs[g]))
        scores.append(lax.dot_general(qm, ks[g], _NT, preferred_element_type=F32) + bias)
    probs = []
    for s in scores:
        mx = jnp.max(s, axis=1, keepdims=True)
        pe = jnp.exp(s - mx)
        probs.append((mx, jnp.sum(pe, axis=1, keepdims=True), pe.astype(BF16)))
    outs = []
    for (g, _), (mx, l, pe) in zip(chains, probs):
        o = jnp.dot(pe, vs[g], preferred_element_type=F32)
        outs.append((o * (1.0 / l), mx + jnp.log(l)))
    return [(jnp.where(left, outs[2 * g][0], outs[2 * g + 1][0]), jnp.where(left, outs[2 * g][1], outs[2 * g + 1][1]))
            for g in range(len(qs))]


def _band_kernel(q_ref, kp_ref, kc_ref, kn_ref, vp_ref, vc_ref, vn_ref, o_ref, lse_ref, *, m, step_rows):
    nk = BAND_L + 2 * HALF_WINDOW
    base = pl.program_id(len(q_ref.shape) - 2) * step_rows
    r = lax.broadcasted_iota(jnp.int32, (BAND_L, nk), 0)
    j = lax.broadcasted_iota(jnp.int32, (BAND_L, nk), 1)
    band = jnp.where(jnp.abs(j - HALF_WINDOW - r) <= HALF_WINDOW, 0.0, NEG)
    left = lax.broadcasted_iota(jnp.int32, (BAND_L, LANES), 1) < HEAD_DIM
    pre = (0,) * (len(q_ref.shape) - 2)
    lanes = [slice(p * LANES, (p + 1) * LANES) for p in range(A_W // LANES)]
    at = lambda rows, sl: pre + (rows, sl)
    whole = slice(None)
    ks = [jnp.concatenate([kp_ref[at(whole, sl)], kc_ref[at(whole, sl)], kn_ref[at(whole, sl)]], axis=0) for sl in lanes]
    vs = [jnp.concatenate([vp_ref[at(whole, sl)], vc_ref[at(whole, sl)], vn_ref[at(whole, sl)]], axis=0) for sl in lanes]
    for blk in range(step_rows // BAND_L):
        r0 = blk * BAND_L
        rows = slice(r0, r0 + BAND_L)
        kpos = base + r0 - HALF_WINDOW + j
        bias = jnp.where(kpos >= 0, band, NEG)
        bias = jnp.where(kpos < m, bias, NEG)
        res = _band_groups([q_ref[at(rows, sl)] for sl in lanes], [kk[r0:r0 + nk] for kk in ks],
                           [vv[r0:r0 + nk] for vv in vs], bias, left)
        for sl, (o, lse) in zip(lanes, res):
            o_ref[at(rows, sl)] = o.astype(BF16)
            lse_ref[at(rows, sl)] = lse


def _band_specs(lead, rows, step_rows):
    nh = rows // HALF_WINDOW
    ratio = step_rows // HALF_WINDOW
    ones = (1,) * lead
    cur = pl.BlockSpec(ones + (step_rows, A_W), lambda *g: g[:lead] + (g[-1], 0))
    prev = pl.BlockSpec(ones + (HALF_WINDOW, A_W), lambda *g: g[:lead] + (jnp.maximum(g[-1] * ratio - 1, 0), 0))
    nxt = pl.BlockSpec(ones + (HALF_WINDOW, A_W), lambda *g: g[:lead] + (jnp.minimum((g[-1] + 1) * ratio, nh - 1), 0))
    return cur, prev, nxt


def _band1(q, k, v, batch, seq):
    step_rows = min(BAND_STEP, seq)
    cur, prev, nxt = _band_specs(1, seq, step_rows)
    view = lambda t: t.reshape(batch, seq, A_W)
    o, lse = pl.pallas_call(
        functools.partial(_band_kernel, m=seq, step_rows=step_rows),
        grid=(batch, seq // step_rows),
        in_specs=[cur, prev, cur, nxt, prev, cur, nxt],
        out_specs=[cur, cur],
        out_shape=[jax.ShapeDtypeStruct((batch, seq, A_W), BF16), jax.ShapeDtypeStruct((batch, seq, A_W), F32)],
        compiler_params=_params(("arbitrary",) * 2),
        name="band_d1",
    )(view(q), view(k), view(k), view(k), view(v), view(v), view(v))
    return o.reshape(batch * seq, A_W), lse.reshape(batch * seq, A_W)


def _band16(q16, k16, v16, batch, seq):
    m = seq // DIL_CLASSES
    step_rows = min(BAND_STEP, m)
    cur, prev, nxt = _band_specs(2, m, step_rows)
    shape = (batch, DIL_CLASSES, m, A_W)
    return pl.pallas_call(
        functools.partial(_band_kernel, m=m, step_rows=step_rows),
        grid=(batch, DIL_CLASSES, m // step_rows),
        in_specs=[cur, prev, cur, nxt, prev, cur, nxt],
        out_specs=[cur, cur],
        out_shape=[jax.ShapeDtypeStruct(shape, BF16), jax.ShapeDtypeStruct(shape, F32)],
        compiler_params=_params(("arbitrary",) * 3),
        name="band_d16",
    )(q16, k16, k16, k16, v16, v16, v16)


def _band4_kernel(q_ref, kp_ref, kc_ref, kn_ref, vp_ref, vc_ref, vn_ref, o16_ref, l16_ref, o_ref, lse_ref, *,
                  m16, step_rows):
    qr = BAND4_ROWS
    kr = qr + 2 * BAND4_HALO
    i = pl.program_id(2)
    shift_q, shift_k = qr.bit_length() - 1, kr.bit_length() - 1
    row = lax.broadcasted_iota(jnp.int32, (4 * qr, 4 * kr), 0)
    col = lax.broadcasted_iota(jnp.int32, (4 * qr, 4 * kr), 1)
    uq, rq = row >> shift_q, row & (qr - 1)
    uk, jk = col >> shift_k, col & (kr - 1)
    band = jnp.where(jnp.abs(4 * (jk - BAND4_HALO - rq) + (uk - uq)) <= HALF_WINDOW, 0.0, NEG)
    left = lax.broadcasted_iota(jnp.int32, (4 * qr, LANES), 1) < HEAD_DIM
    slabs = [slice(p * LANES, (p + 1) * LANES) for p in range(A_W // LANES)]
    window = lambda refs, u, sl: jnp.concatenate([r[0, u, 0, :, sl] for r in refs], axis=0)
    kwin = [[window([kp_ref, kc_ref, kn_ref], u, sl) for u in range(4)] for sl in slabs]
    vwin = [[window([vp_ref, vc_ref, vn_ref], u, sl) for u in range(4)] for sl in slabs]
    for blk in range(step_rows // qr):
        r0 = blk * qr
        rows = slice(r0, r0 + qr)
        kpos = i * step_rows + r0 - BAND4_HALO + jk
        bias = jnp.where(kpos >= 0, band, NEG)
        bias = jnp.where(kpos < m16, bias, NEG)
        classes = lambda ref, sl: jnp.concatenate([ref[0, u, 0, rows, sl] for u in range(4)], axis=0)
        branch4 = _band_groups([classes(q_ref, sl) for sl in slabs],
                               [jnp.concatenate([w[r0:r0 + kr] for w in kw], axis=0) for kw in kwin],
                               [jnp.concatenate([w[r0:r0 + kr] for w in vw], axis=0) for vw in vwin], bias, left)
        for sl, (o4, l4) in zip(slabs, branch4):
            o16 = classes(o16_ref, sl).astype(F32)
            l16 = classes(l16_ref, sl)
            mx = jnp.maximum(l4, l16)
            w4, w16 = jnp.exp(l4 - mx), jnp.exp(l16 - mx)
            tot = w4 + w16
            om = (w4 * o4 + w16 * o16) * (1.0 / tot)
            lm = mx + jnp.log(tot)
            for u in range(4):
                o_ref[0, u, 0, rows, sl] = om[u * qr:(u + 1) * qr].astype(BF16)
                lse_ref[0, u, 0, rows, sl] = lm[u * qr:(u + 1) * qr]


def _band4(q16, k16, v16, o16, l16, batch, seq):
    m16 = seq // DIL_CLASSES
    step_rows = min(BAND4_STEP, m16)
    ratio = step_rows // BAND4_HALO
    nh = m16 // BAND4_HALO
    view = lambda t: t.reshape(batch, 4, 4, m16, A_W)
    cur = pl.BlockSpec((1, 4, 1, step_rows, A_W), lambda b, c, i: (b, 0, c, i, 0))
    prev = pl.BlockSpec((1, 4, 1, BAND4_HALO, A_W), lambda b, c, i: (b, 0, c, jnp.maximum(i * ratio - 1, 0), 0))
    nxt = pl.BlockSpec((1, 4, 1, BAND4_HALO, A_W), lambda b, c, i: (b, 0, c, jnp.minimum((i + 1) * ratio, nh - 1), 0))
    shape = (batch, 4, 4, m16, A_W)
    o, lse = pl.pallas_call(
        functools.partial(_band4_kernel, m16=m16, step_rows=step_rows),
        grid=(batch, 4, m16 // step_rows),
        in_specs=[cur, prev, cur, nxt, prev, cur, nxt, cur, cur],
        out_specs=[cur, cur],
        out_shape=[jax.ShapeDtypeStruct(shape, BF16), jax.ShapeDtypeStruct(shape, F32)],
        compiler_params=_params(("arbitrary",) * 3),
        name="band_d4",
    )(view(q16), view(k16), view(k16), view(k16), view(v16), view(v16), view(v16), view(o16), view(l16))
    return o.reshape(batch, DIL_CLASSES, m16, A_W), lse.reshape(batch, DIL_CLASSES, m16, A_W)


def _gqa_kernel(q_ref, k_ref, v_ref, o_ref, qs_ref, m_ref, acc_ref, s_ref, vt_ref, *, seq):
    for pr in range(2):
        qt = q_ref[0, :, pr * LANES:(pr + 1) * LANES].astype(F32).T
        qs_ref[:, (2 * pr) * GQA_TQ:(2 * pr + 1) * GQA_TQ] = qt[0:HEAD_DIM].astype(BF16)
        qs_ref[:, (2 * pr + 1) * GQA_TQ:(2 * pr + 2) * GQA_TQ] = qt[HEAD_DIM:2 * HEAD_DIM].astype(BF16)
    m_ref[...] = jnp.full(m_ref.shape, NEG, F32)
    acc_ref[...] = jnp.zeros(acc_ref.shape, F32)
    n_tiles = seq // GQA_TK

    @pl.when(pl.program_id(2) == 0)
    def _():
        ones_row = lax.broadcasted_iota(jnp.int32, (GQA_V_ROWS, GQA_TK), 0) == HEAD_DIM

        def transpose_tile(t, carry):
            off = pl.multiple_of(t * GQA_TK, GQA_TK)
            vt = v_ref[0, pl.ds(off, GQA_TK), :].astype(F32).T[0:GQA_V_ROWS]
            vt_ref[t] = jnp.where(ones_row, 1.0, vt).astype(BF16)
            return carry

        lax.fori_loop(0, n_tiles, transpose_tile, 0)

    chains = [slice(c * GQA_CHAIN, (c + 1) * GQA_CHAIN) for c in range(4 * GQA_TQ // GQA_CHAIN)]

    def scores(tile, buf, cols):
        off = pl.multiple_of(tile * GQA_TK, GQA_TK)
        k = k_ref[0, pl.ds(off, GQA_TK), 0:HEAD_DIM]
        s_ref[buf, :, cols] = jnp.dot(k, qs_ref[:, cols], preferred_element_type=F32)

    def softmax_pv(tile, buf, cols):
        s = s_ref[buf, :, cols]
        m_prev = m_ref[:, cols]
        m_new = jnp.maximum(m_prev, jnp.max(s, axis=0, keepdims=True))
        alpha = jnp.exp2(m_prev - m_new)
        p = jnp.exp2((s - m_new).astype(BF16))
        pv = jnp.dot(vt_ref[tile], p, preferred_element_type=F32)
        acc_ref[:, cols] = alpha * acc_ref[:, cols] + pv
        m_ref[:, cols] = m_new

    def step(tile, buf, prefetch):
        for cols in chains:
            if prefetch:
                scores(tile + 1, 1 - buf, cols)
            softmax_pv(tile, buf, cols)

    def pair(j, carry):
        step(2 * j, 0, True)
        step(2 * j + 1, 1, True)
        return carry

    for cols in chains:
        scores(0, 0, cols)
    lax.fori_loop(0, n_tiles // 2 - 1, pair, 0)
    step(n_tiles - 2, 0, True)
    step(n_tiles - 1, 1, False)
    acc = acc_ref[...]
    a = acc[0:HEAD_DIM] * (1.0 / acc[HEAD_DIM:HEAD_DIM + 1, :])
    for pr in range(2):
        two_heads = a[:, (2 * pr) * GQA_TQ:(2 * pr + 2) * GQA_TQ]
        stacked = jnp.concatenate([two_heads[:, :GQA_TQ], two_heads[:, GQA_TQ:]], axis=0)
        o_ref[0, :, pr * LANES:(pr + 1) * LANES] = stacked.T.astype(BF16)


def _gqa(qb, kd, vd, batch, seq):
    gw = B_QW // B_KV_HEADS
    cols = 4 * GQA_TQ
    out = pl.pallas_call(
        functools.partial(_gqa_kernel, seq=seq),
        grid=(batch, B_KV_HEADS, seq // GQA_TQ),
        in_specs=[pl.BlockSpec((1, GQA_TQ, gw), lambda b, g, qi: (b, qi, g)),
                  pl.BlockSpec((1, seq, LANES), lambda b, g, qi: (b, 0, g)),
                  pl.BlockSpec((1, seq, LANES), lambda b, g, qi: (b, 0, g))],
        out_specs=pl.BlockSpec((1, GQA_TQ, gw), lambda b, g, qi: (b, qi, g)),
        out_shape=jax.ShapeDtypeStruct((batch, seq, B_QW), BF16),
        scratch_shapes=[pltpu.VMEM((HEAD_DIM, cols), BF16), pltpu.VMEM((1, cols), F32),
                        pltpu.VMEM((GQA_V_ROWS, cols), F32), pltpu.VMEM((2, GQA_TK, cols), F32),
                        pltpu.VMEM((seq // GQA_TK, GQA_V_ROWS, GQA_TK), BF16)],
        compiler_params=_params(("arbitrary",) * 3, VMEM_LIMIT),
        name="gqa",
    )(qb.reshape(batch, seq, B_QW), kd.reshape(batch, seq, 2 * B_KVW), vd.reshape(batch, seq, 2 * B_KVW))
    return out.reshape(batch * seq, B_QW)


def _ab_out_kernel(o1_ref, l1_ref, om_ref, lm_ref, yb_ref, x_ref, w_ref, g_ref, b_ref, out_ref, os_ref, ls_ref):
    for c in range(DIL_CLASSES):
        for g in range(A_W // LANES):
            sl = slice(g * LANES, (g + 1) * LANES)
            rows = pl.ds(c, TM // DIL_CLASSES, stride=DIL_CLASSES)
            os_ref[g, rows, :] = om_ref[0, c, :, sl].astype(F32)
            ls_ref[g, rows, :] = lm_ref[0, c, :, sl]
    ya = []
    for g in range(A_W // LANES):
        sl = slice(g * LANES, (g + 1) * LANES)
        l1, lm = l1_ref[:, sl], ls_ref[g]
        mx = jnp.maximum(l1, lm)
        e1, em = jnp.exp(l1 - mx), jnp.exp(lm - mx)
        ya.append(((e1 * o1_ref[:, sl].astype(F32) + em * os_ref[g]) * (1.0 / (e1 + em))).astype(BF16))
    y = _row_block_dot(jnp.concatenate(ya + [yb_ref[...]], axis=1), w_ref[...])
    out_ref[...] = _layer_norm(ALPHA * x_ref[...] + y, g_ref[...], b_ref[...])


def _ab_out(o1, l1, om, lm, yb, x2d, w, g, b, seq):
    t = x2d.shape[0]
    nt = seq // TM
    row = lambda n: pl.BlockSpec((TM, n), lambda i: (i, 0))
    cls = pl.BlockSpec((1, DIL_CLASSES, TM // DIL_CLASSES, A_W), lambda i: (i // nt, 0, i % nt, 0))
    return pl.pallas_call(
        _ab_out_kernel,
        grid=(t // TM,),
        in_specs=[row(A_W), row(A_W), cls, cls, row(B_QW), row(D_MODEL), _const_spec((A_W + B_QW, D_MODEL)),
                  _const_spec((1, D_MODEL)), _const_spec((1, D_MODEL))],
        out_specs=row(D_MODEL),
        out_shape=jax.ShapeDtypeStruct((t, D_MODEL), F32),
        scratch_shapes=[pltpu.VMEM((A_W // LANES, TM, LANES), F32)] * 2,
        compiler_params=_params(("arbitrary",), VMEM_LIMIT),
        name="ab_out",
    )(o1, l1, om, lm, yb, x2d, w, g, b)


def _gelu(x):
    return 0.5 * x * (1.0 + jnp.tanh(0.7978845608028654 * (x + 0.044715 * (x * x * x))))


def _ffn_kernel(x_ref, xp_ref, xn_ref, wu_ref, cw_ref, cb_ref, wd_ref, g_ref, b_ref, out_ref, *, nt):
    i = pl.program_id(0)
    keep_prev = (i % nt != 0).astype(F32)
    keep_next = (i % nt != nt - 1).astype(F32)
    rows = lax.broadcasted_iota(jnp.int32, (TM, 1), 0)
    offsets = [sum(FF_CHUNKS[:c]) for c in range(len(FF_CHUNKS))]
    operands, acc = {}, {}

    def tile_operands(j):
        if j not in operands:
            r0 = j * TM
            xb = x_ref[r0:r0 + TM, :].astype(BF16)
            before = xp_ref[...] if j == 0 else x_ref[r0 - 8:r0, :]
            after = xn_ref[...] if j == FFN_SUB - 1 else x_ref[r0 + TM:r0 + TM + 8, :]
            halo = jnp.concatenate([before, after], axis=0).astype(BF16)
            operands[j] = (xb, jnp.concatenate([xb, halo], axis=0))
        return operands[j]

    def up(j, c):
        a, n = offsets[c], FF_CHUNKS[c]
        xb, xe = tile_operands(j)
        u = _row_block_dot(xb, wu_ref[:, a:a + n], FFN_ROWS)
        ge = _row_block_dot(xe, wu_ref[:, D_FF + a:D_FF + a + n], FFN_ROWS)
        return u, ge

    def finish(j, c, pending):
        u, ge = pending
        a, n = offsets[c], FF_CHUNKS[c]
        gm = ge[0:TM]
        g_before = ge[TM + 7:TM + 8]
        g_after = ge[TM + 8:TM + 9]
        if j == 0:
            g_before = g_before * keep_prev
        if j == FFN_SUB - 1:
            g_after = g_after * keep_next
        gp = jnp.where(rows == 0, g_before, pltpu.roll(gm, 1, 0))
        gn = jnp.where(rows == TM - 1, g_after, pltpu.roll(gm, TM - 1, 0))
        cw = cw_ref[:, a:a + n]
        gc = gp * cw[0:1] + gm * cw[1:2] + gn * cw[2:3] + cb_ref[:, a:a + n]
        act = (_gelu(gc) * u).astype(BF16)
        part = _row_block_dot(act, wd_ref[a:a + n, :], FFN_ROWS)
        acc[j] = part if c == 0 else acc[j] + part
        if c == len(FF_CHUNKS) - 1:
            sl = slice(j * TM, (j + 1) * TM)
            out_ref[sl, :] = _layer_norm(ALPHA * x_ref[sl, :] + acc[j], g_ref[...], b_ref[...])

    _pipelined([(functools.partial(up, j, c), functools.partial(finish, j, c))
                for j in range(FFN_SUB) for c in range(len(FF_CHUNKS))])


def _ffn(x2d, wu, cw, cb, wd, g, b, seq):
    t = x2d.shape[0]
    step = FFN_SUB * TM
    nt = seq // step
    r8 = step // 8
    return pl.pallas_call(
        functools.partial(_ffn_kernel, nt=nt),
        grid=(t // step,),
        in_specs=[pl.BlockSpec((step, D_MODEL), lambda i: (i, 0)),
                  pl.BlockSpec((8, D_MODEL), lambda i: (jnp.maximum(i * r8 - 1, 0), 0)),
                  pl.BlockSpec((8, D_MODEL), lambda i: (jnp.minimum((i + 1) * r8, t // 8 - 1), 0)),
                  _const_spec((D_MODEL, 2 * D_FF), single=True),
                  _const_spec((3, D_FF)), _const_spec((1, D_FF)),
                  _const_spec((D_FF, D_MODEL), single=True),
                  _const_spec((1, D_MODEL)), _const_spec((1, D_MODEL))],
        out_specs=pl.BlockSpec((step, D_MODEL), lambda i: (i, 0)),
        out_shape=jax.ShapeDtypeStruct((t, D_MODEL), F32),
        compiler_params=_params(("arbitrary",), VMEM_LIMIT),
        name="ffn",
    )(x2d, x2d, x2d, wu, cw, cb, wd, g, b)


def _lower_bound(tbl, layer):
    rows = [tbl[r:r + 1] for r in range(DEPTH)]
    mx = functools.reduce(jnp.maximum, rows)
    es = [jnp.exp(r - mx) for r in rows]
    inv = 1.0 / functools.reduce(lambda a, b: a + b, es)
    ps = [e * inv for e in es]
    return functools.reduce(lambda a, b: a + b, ps[:layer + 1]) - ps[0]


def _c_in_kernel(x_ref, w_ref, lbf_ref, lbb_ref, q_ref, lf_ref, lb_ref, v_ref, g_ref, *, layer):
    half = C_W // 2
    blocks = [slice(r * PROJ_ROWS, (r + 1) * PROJ_ROWS) for r in range(TM // PROJ_ROWS)]
    xbs = [x_ref[rows, :].astype(BF16) for rows in blocks]

    def proj(r, a, n):
        return jnp.dot(xbs[r], w_ref[:, a:a + n], preferred_element_type=F32)

    def silu(z):
        return z * _sigmoid(z)

    lbf = _lower_bound(lbf_ref[...], layer)
    lbb = _lower_bound(lbb_ref[...], layer)

    def store_q(z, rows, sl):
        q_ref[rows, sl] = z.astype(BF16)

    def store_logf(tbl, out_ref):
        def fn(z, rows, sl):
            lb = tbl[:, sl]
            out_ref[rows, sl] = jnp.log(lb + (1.0 - lb) * _sigmoid(z))
        return fn

    def store_silu(out_ref):
        def fn(z, rows, sl):
            out_ref[rows, sl] = silu(z).astype(BF16)
        return fn

    sections = []
    for r, rows in enumerate(blocks):
        for c in range(2):
            a = c * half
            sl = slice(a, a + half)
            for off, fn in ((0, store_q), (C_W, store_logf(lbf, lf_ref)), (2 * C_W, store_logf(lbb, lb_ref)),
                            (3 * C_W, store_silu(v_ref)), (3 * C_W + C_VW, store_silu(g_ref))):
                sections.append((functools.partial(proj, r, off + a, half), functools.partial(fn, rows=rows, sl=sl)))
    _pipelined(sections)


def _c_in(x2d, w, lbf, lbb, layer):
    t = x2d.shape[0]
    row = pl.BlockSpec((TM, C_W), lambda i: (i, 0))
    return pl.pallas_call(
        functools.partial(_c_in_kernel, layer=layer),
        grid=(t // TM,),
        in_specs=[row, _const_spec(w.shape), _const_spec((DEPTH, C_W)), _const_spec((DEPTH, C_W))],
        out_specs=[row] * 5,
        out_shape=[jax.ShapeDtypeStruct((t, C_W), d) for d in (BF16, F32, F32, BF16, BF16)],
        compiler_params=_params(("arbitrary",), VMEM_LIMIT),
        name="c_in",
    )(x2d, w, lbf, lbb)


def _hgrn_kernel(*refs, rev, fused, heads, step_rows):
    if fused:
        (q_ref, lf_ref, v_ref, of_ref, gate_ref, gn_ref, x_ref, w_ref, lg_ref, lb_ref,
         out_ref, st_ref, y_ref) = refs
    else:
        q_ref, lf_ref, v_ref, y_ref, st_ref = refs

    @pl.when(pl.program_id(2) == 0)
    def _():
        st_ref[...] = jnp.zeros(st_ref.shape, F32)

    ri = lax.broadcasted_iota(jnp.int32, (HG_CHUNK, HG_CHUNK), 0)
    ci = lax.broadcasted_iota(jnp.int32, (HG_CHUNK, HG_CHUNK), 1)
    tri = (ci >= ri) if rev else (ci <= ri)
    trib = jnp.where(tri, 1.0, 0.0).astype(BF16)
    dotf = lambda a, b: jnp.dot(a, b, preferred_element_type=F32)
    n_chunks = step_rows // HG_CHUNK
    order = range(n_chunks - 1, -1, -1) if rev else range(n_chunks)
    units = [(c, h) for c in order for h in range(heads)]
    window = lambda u: (0, slice(u[0] * HG_CHUNK, (u[0] + 1) * HG_CHUNK), slice(u[1] * C_KEY, (u[1] + 1) * C_KEY))

    b_all = {}
    for u in units:
        lf = lf_ref[window(u)]
        hi = lf.astype(BF16)
        lo = (lf - hi.astype(F32)).astype(BF16)
        b_all[u] = dotf(trib, hi) + dotf(trib, lo)

    qdec, kdec, etot, a_rows = {}, {}, {}, {}
    for u in units:
        lf, b = lf_ref[window(u)], b_all[u]
        qf = q_ref[window(u)].astype(F32)
        bex = b - lf
        kk = 1.0 - jnp.exp(lf)
        btot = b[0:1] if rev else b[HG_CHUNK - 1:HG_CHUNK]
        qdec[u] = (qf * jnp.exp(b)).astype(BF16)
        kdec[u] = (kk * jnp.exp(btot - b)).astype(BF16)
        etot[u] = jnp.exp(btot)
        rows = []
        for blk in range(HG_CHUNK // HG_SUB):
            r0 = blk * HG_SUB
            ref = bex[r0 + HG_SUB - 1:r0 + HG_SUB] if rev else bex[r0:r0 + 1]
            qt = (qf[r0:r0 + HG_SUB] * jnp.exp(b[r0:r0 + HG_SUB] - ref)).astype(BF16)
            lo_r, hi_r = (r0, HG_CHUNK) if rev else (0, r0 + HG_SUB)
            ks = (kk[lo_r:hi_r] * jnp.exp(ref - b[lo_r:hi_r])).astype(BF16)
            pieces = []
            if lo_r > 0:
                pieces.append(jnp.zeros((lo_r, C_KEY), BF16))
            pieces.append(ks)
            if hi_r < HG_CHUNK:
                pieces.append(jnp.zeros((HG_CHUNK - hi_r, C_KEY), BF16))
            kfull = jnp.concatenate(pieces, axis=0) if len(pieces) > 1 else ks
            rows.append(lax.dot_general(qt, kfull, _NT, preferred_element_type=F32))
        a_rows[u] = rows
    attn = {u: jnp.where(tri, jnp.concatenate(a_rows[u], axis=0), 0.0).astype(BF16) for u in units}

    states = [st_ref[h] for h in range(heads)]
    for u in units:
        h = u[1]
        vb = v_ref[window(u)]
        o = lax.dot_general(qdec[u], states[h].astype(BF16), _NT, preferred_element_type=F32)
        o = o + dotf(attn[u], vb)
        states[h] = states[h] * etot[u] + lax.dot_general(vb, kdec[u], _TN, preferred_element_type=F32)
        if fused:
            tot = of_ref[window(u)] + o
            inv = lax.rsqrt(jnp.mean(tot * tot, -1, keepdims=True) + RMS_EPS)
            y = tot * inv * gn_ref[:, window(u)[2]] * gate_ref[window(u)].astype(F32)
            y_ref[window(u)[1:]] = y.astype(BF16)
        else:
            y_ref[window(u)] = o
    for h in range(heads):
        st_ref[h] = states[h]
    if fused:
        proj = _row_block_dot(y_ref[...], w_ref[...])
        out_ref[0] = _layer_norm(ALPHA * x_ref[0] + proj, lg_ref[...], lb_ref[...])


def _hgrn_specs(batch, seq, rev, heads, step_rows):
    nb = seq // step_rows
    blk = (lambda b, h, i: (b, nb - 1 - i, h)) if rev else (lambda b, h, i: (b, i, h))
    return pl.BlockSpec((1, step_rows, heads * C_KEY), blk), (batch, C_HEADS // heads, nb)


def _hgrn_fwd(q, lf, v, batch, seq):
    tile, grid = _hgrn_specs(batch, seq, False, HG_HP, HG_TB)
    view = lambda t: t.reshape(batch, seq, C_W)
    out = pl.pallas_call(
        functools.partial(_hgrn_kernel, rev=False, fused=False, heads=HG_HP, step_rows=HG_TB),
        grid=grid,
        in_specs=[tile, tile, tile],
        out_specs=tile,
        out_shape=jax.ShapeDtypeStruct((batch, seq, C_VW), F32),
        scratch_shapes=[pltpu.VMEM((HG_HP, C_VAL, C_KEY), F32)],
        compiler_params=_params(("arbitrary",) * 3),
        name="hgrn_fwd",
    )(view(q), view(lf), view(v))
    return out.reshape(batch * seq, C_VW)


def _hgrn_bwd_out(q, lf, v, o_f, gate, gn, x2d, w, g, b, batch, seq):
    tile, grid = _hgrn_specs(batch, seq, True, C_HEADS, HG_TB_OUT)
    view = lambda t: t.reshape(batch, seq, C_W)
    out = pl.pallas_call(
        functools.partial(_hgrn_kernel, rev=True, fused=True, heads=C_HEADS, step_rows=HG_TB_OUT),
        grid=grid,
        in_specs=[tile] * 5 + [_const_spec((1, C_VW)), tile, _const_spec((C_VW, D_MODEL), single=True),
                               _const_spec((1, D_MODEL)), _const_spec((1, D_MODEL))],
        out_specs=tile,
        out_shape=jax.ShapeDtypeStruct((batch, seq, D_MODEL), F32),
        scratch_shapes=[pltpu.VMEM((C_HEADS, C_VAL, C_KEY), F32), pltpu.VMEM((HG_TB_OUT, C_VW), BF16)],
        compiler_params=_params(("arbitrary",) * 3, VMEM_LIMIT),
        name="hgrn_bwd_out",
    )(view(q), view(lf), view(v), view(o_f), view(gate), gn, x2d.reshape(batch, seq, D_MODEL), w, g, b)
    return out.reshape(batch * seq, D_MODEL)


def _prep_weights(w_in_ab, w_out_ab, qn_ab, kn_ab, w_in_c, w_out_c, gn_c, ffn_w_up, ffn_w_down):
    scale = HEAD_DIM ** -0.5
    ab = []
    for j in range(w_in_ab.shape[0]):
        w = w_in_ab[j]
        o3 = 3 * A_W
        kb = w[:, o3 + B_QW:o3 + B_QW + B_KVW]
        vb = w[:, o3 + B_QW + B_KVW:]
        dup = lambda t: jnp.concatenate([t[:, :HEAD_DIM], t[:, :HEAD_DIM], t[:, HEAD_DIM:], t[:, HEAD_DIM:]], 1)
        w_ext = jnp.concatenate([w[:, :A_W] * scale, w[:, A_W:o3 + B_QW], dup(kb), dup(vb)], 1).astype(BF16)
        qg = (jnp.concatenate([qn_ab[j], qn_ab[j]]) * (scale * LOG2E)).reshape(1, LANES)
        kg = jnp.concatenate([kn_ab[j], kn_ab[j]]).reshape(1, LANES)
        ab.append((w_ext, qg, kg, w_out_ab[j].astype(BF16)))
    cc = [(w_in_c[j].astype(BF16), w_out_c[j].astype(BF16), gn_c[j].reshape(1, C_VW)) for j in range(w_in_c.shape[0])]
    return ab, cc, ffn_w_up.astype(BF16), ffn_w_down.astype(BF16)


def _trunk(x, prep, lb_fwd, lb_bwd, ln_mix_g, ln_mix_b, ln_ffn_g, ln_ffn_b, ffn_conv_w, ffn_conv_b):
    ab, cc, wu, wd = prep
    batch, seq, _ = x.shape
    x2d = x.reshape(batch * seq, D_MODEL)
    tabs = _rope_tables(seq)
    blk = jnp.arange(LANES) // HEAD_DIM
    ones2 = (blk[:, None] == blk[None, :]).astype(BF16)
    vec = lambda t: t.reshape(1, -1)
    for l in range(DEPTH):
        j = l // 2
        if l % 2 == 0:
            w_ext, qg, kg, w_out = ab[j]
            assert [d for _, d in A_PATTERNS] == [1, 4, DIL_CLASSES]
            assert all(window // (2 * d) == HALF_WINDOW for window, d in A_PATTERNS)
            qa, ka, va, qb, kd, vd, qa16, ka16, va16 = _ab_in(x2d, w_ext, tabs, qg, kg, ones2, batch, seq)
            o1, l1 = _band1(qa, ka, va, batch, seq)
            o16, l16 = _band16(qa16, ka16, va16, batch, seq)
            om, lm = _band4(qa16, ka16, va16, o16, l16, batch, seq)
            yb = _gqa(qb, kd, vd, batch, seq)
            x2d = _ab_out(o1, l1, om, lm, yb, x2d, w_out, vec(ln_mix_g[l]), vec(ln_mix_b[l]), seq)
        else:
            w_in, w_out, gn = cc[j]
            q, lf, lb, v, gate = _c_in(x2d, w_in, lb_fwd, lb_bwd, l)
            o_f = _hgrn_fwd(q, lf, v, batch, seq)
            x2d = _hgrn_bwd_out(q, lb, v, o_f, gate, gn, x2d, w_out, vec(ln_mix_g[l]), vec(ln_mix_b[l]), batch, seq)
        x2d = _ffn(x2d, wu[l], ffn_conv_w[l], vec(ffn_conv_b[l]), wd[l], vec(ln_ffn_g[l]), vec(ln_ffn_b[l]), seq)
    return x2d.reshape(batch, seq, D_MODEL)


def kernel(x_prompt, x_sample, w_in_ab, w_out_ab, qn_ab, kn_ab, w_in_c, w_out_c, lb_fwd, lb_bwd, gn_c, ln_mix_g, ln_mix_b, ln_ffn_g, ln_ffn_b, ffn_w_up, ffn_conv_w, ffn_conv_b, ffn_w_down):
    prep = _prep_weights(w_in_ab, w_out_ab, qn_ab, kn_ab, w_in_c, w_out_c, gn_c, ffn_w_up, ffn_w_down)
    rest = (lb_fwd, lb_bwd, ln_mix_g, ln_mix_b, ln_ffn_g, ln_ffn_b, ffn_conv_w, ffn_conv_b)
    return (_trunk(x_prompt, prep, *rest), _trunk(x_sample, prep, *rest))
```

```python
import functools

import jax
import jax.numpy as jnp
from jax import lax
from jax.experimental import pallas as pl
from jax.experimental.pallas import tpu as pltpu

F32 = jnp.float32
BF16 = jnp.bfloat16

D_MODEL = 1024
DEPTH = 2
HEAD_DIM = 64
A_HEADS = 8
A_PATTERNS = ((128, 1), (512, 4), (2048, 16))
B_HEADS = 8
B_KV_HEADS = 2
GRID_W = 64
ROPE_THETA = 500000.0
ROPE_DIM = HEAD_DIM // 4
AXIAL_THETA = 10000.0
C_HEADS = 8
C_KEY = 128
C_VAL = 128
D_FF = 2816
ALPHA = (2 * DEPTH) ** 0.25
LN_EPS = 1e-5
RMS_EPS = 1e-6
A_W = A_HEADS * HEAD_DIM
B_QW = B_HEADS * HEAD_DIM
B_KVW = B_KV_HEADS * HEAD_DIM
C_W = C_HEADS * C_KEY
C_VW = C_HEADS * C_VAL

LANES = 128
HALF_WINDOW = 64
TM = 512
PROJ_ROWS = 128
FFN_ROWS = 256
BAND_L = 128
BAND_STEP = 512
BAND4_STEP = 128
DIL_CLASSES = 16
BAND4_ROWS = 32
BAND4_HALO = 16
GQA_TQ = 512
GQA_TK = 512
GQA_CHAIN = 512
GQA_V_ROWS = 80
FF_CHUNKS = (768, 768, 768, 512)
HG_CHUNK = 64
HG_SUB = 16
HG_TB = 512
HG_HP = 4
HG_TB_OUT = 256
NEG = -1e30
LOG2E = 1.4426950408889634
VMEM_LIMIT = 56 * 1024 * 1024

_NT = (((1,), (1,)), ((), ()))
_TN = (((0,), (0,)), ((), ()))


def _params(sem, vmem=None):
    return pltpu.CompilerParams(dimension_semantics=sem, vmem_limit_bytes=vmem)


def _const_spec(shape, single=False):
    nd = len(shape)
    if single:
        return pl.BlockSpec(shape, lambda *_: (0,) * nd, pipeline_mode=pl.Buffered(1))
    return pl.BlockSpec(shape, lambda *_: (0,) * nd)


def _layer_norm(z, g, b):
    mu = jnp.mean(z, -1, keepdims=True)
    d = z - mu
    var = jnp.mean(d * d, -1, keepdims=True)
    return d * lax.rsqrt(var + LN_EPS) * g + b


def _sigmoid(z):
    return 1.0 / (1.0 + jnp.exp(-z))


def _row_block_dot(x, w, rows=PROJ_ROWS):
    m = x.shape[0]
    starts = list(range(0, m - m % rows, rows)) or [0]
    ends = starts[1:] + [m]
    parts = [jnp.dot(x[a:b], w, preferred_element_type=F32) for a, b in zip(starts, ends)]
    return parts[0] if len(parts) == 1 else jnp.concatenate(parts, axis=0)


def _pipelined(sections):
    pending = sections[0][0]()
    for i, (_, epilogue) in enumerate(sections):
        upcoming = sections[i + 1][0]() if i + 1 < len(sections) else None
        epilogue(pending)
        pending = upcoming


def _rope_tables(seq):
    pos = jnp.arange(seq, dtype=F32)
    d = jnp.arange(LANES) % HEAD_DIM

    def angles(p, dim, theta):
        freqs = theta ** (-(jnp.arange(0, dim, 2, dtype=F32) / dim))
        return p[:, None] * freqs[None, :]

    h = ROPE_DIM // 2
    ang = angles(pos, ROPE_DIM, ROPE_THETA)
    a = ang[:, d % h]
    lo, hi = d < h, (d >= h) & (d < ROPE_DIM)
    pc = jnp.where(lo | hi, jnp.cos(a), 1.0)
    psa = jnp.where(lo, -jnp.sin(a), 0.0)
    psb = jnp.where(hi, jnp.sin(a), 0.0)

    q = HEAD_DIM // 4
    row = jnp.floor(pos / GRID_W)
    col = pos - row * GRID_W
    ar = angles(row, HEAD_DIM // 2, AXIAL_THETA)[:, d % q]
    ac_ = angles(col, HEAD_DIM // 2, AXIAL_THETA)[:, d % q]
    a2 = jnp.where(d < HEAD_DIM // 2, ar, ac_)
    first = (d % (HEAD_DIM // 2)) < q
    ac = jnp.cos(a2)
    asa = jnp.where(first, -jnp.sin(a2), 0.0)
    asb = jnp.where(first, 0.0, jnp.sin(a2))
    return [t.astype(F32) for t in (pc, psa, psb, ac, asa, asb)]


def _rope(seg, c, sa, sb, shift):
    return seg * c + pltpu.roll(seg, LANES - shift, 1) * sa + pltpu.roll(seg, shift, 1) * sb


def _ab_in_kernel(x_ref, w_ref, pc_ref, psa_ref, psb_ref, ac_ref, asa_ref, asb_ref, qg_ref, kg_ref,
                  ones_ref, qa_ref, ka_ref, va_ref, qb_ref, kd_ref, vd_ref, qa16_ref, ka16_ref, va16_ref,
                  qs_ref, ks_ref, vs_ref):
    xb = x_ref[...].astype(BF16)

    def proj(a, n):
        return jnp.dot(xb, w_ref[:, a:a + n], preferred_element_type=F32)

    pc, psa, psb = pc_ref[...], psa_ref[...], psb_ref[...]
    ac, asa, asb = ac_ref[...], asa_ref[...], asb_ref[...]

    def dilated_a(h, rope, out_ref, out16_ref, slab_ref):
        for g in range(A_W // LANES):
            sl = slice(g * LANES, (g + 1) * LANES)
            y = _rope(h[:, sl], pc, psa, psb, ROPE_DIM // 2) if rope else h[:, sl]
            out_ref[:, sl] = y.astype(BF16)
            slab_ref[g] = y
        for c in range(DIL_CLASSES):
            for g in range(A_W // LANES):
                rows = slab_ref[g, pl.ds(c, TM // DIL_CLASSES, stride=DIL_CLASSES), :]
                out16_ref[0, c, :, g * LANES:(g + 1) * LANES] = rows.astype(BF16)

    def norm_rope(h, gain, out_ref):
        for g in range(h.shape[1] // LANES):
            sl = slice(g * LANES, (g + 1) * LANES)
            seg = h[:, sl]
            sq = seg * seg
            hi = sq.astype(BF16)
            lo = (sq - hi.astype(F32)).astype(BF16)
            ss = (jnp.dot(hi, ones_ref[...], preferred_element_type=F32)
                  + jnp.dot(lo, ones_ref[...], preferred_element_type=F32))
            y = seg * lax.rsqrt(ss * (1.0 / HEAD_DIM) + RMS_EPS) * gain
            out_ref[:, sl] = _rope(y, ac, asa, asb, HEAD_DIM // 4).astype(BF16)

    def cast_to(out_ref):
        def fn(h):
            out_ref[...] = h.astype(BF16)
        return fn

    o_qb = 3 * A_W
    o_kd = o_qb + B_QW
    o_vd = o_kd + 2 * B_KVW
    _pipelined([
        (functools.partial(proj, 0, A_W),
         functools.partial(dilated_a, rope=True, out_ref=qa_ref, out16_ref=qa16_ref, slab_ref=qs_ref)),
        (functools.partial(proj, A_W, A_W),
         functools.partial(dilated_a, rope=True, out_ref=ka_ref, out16_ref=ka16_ref, slab_ref=ks_ref)),
        (functools.partial(proj, 2 * A_W, A_W),
         functools.partial(dilated_a, rope=False, out_ref=va_ref, out16_ref=va16_ref, slab_ref=vs_ref)),
        (functools.partial(proj, o_qb, B_QW), functools.partial(norm_rope, gain=qg_ref[...], out_ref=qb_ref)),
        (functools.partial(proj, o_kd, 2 * B_KVW), functools.partial(norm_rope, gain=kg_ref[...], out_ref=kd_ref)),
        (functools.partial(proj, o_vd, 2 * B_KVW), cast_to(vd_ref)),
    ])


def _ab_in(x2d, w_ext, tabs, qg, kg, ones2, batch, seq):
    t = x2d.shape[0]
    nt = seq // TM
    tab = pl.BlockSpec((TM, LANES), lambda i: (i % nt, 0))
    row = lambda n: pl.BlockSpec((TM, n), lambda i: (i, 0))
    cls = pl.BlockSpec((1, DIL_CLASSES, TM // DIL_CLASSES, A_W), lambda i: (i // nt, 0, i % nt, 0))
    cls_shape = jax.ShapeDtypeStruct((batch, DIL_CLASSES, seq // DIL_CLASSES, A_W), BF16)
    wn = w_ext.shape[1]
    return pl.pallas_call(
        _ab_in_kernel,
        grid=(t // TM,),
        in_specs=[row(D_MODEL), _const_spec((D_MODEL, wn))] + [tab] * 6
        + [_const_spec((1, LANES)), _const_spec((1, LANES)), _const_spec((LANES, LANES))],
        out_specs=[row(A_W)] * 3 + [row(B_QW), row(2 * B_KVW), row(2 * B_KVW)] + [cls] * 3,
        out_shape=[jax.ShapeDtypeStruct((t, A_W), BF16)] * 3
        + [jax.ShapeDtypeStruct((t, B_QW), BF16)]
        + [jax.ShapeDtypeStruct((t, 2 * B_KVW), BF16)] * 2 + [cls_shape] * 3,
        scratch_shapes=[pltpu.VMEM((A_W // LANES, TM, LANES), F32)] * 3,
        compiler_params=_params(("arbitrary",), VMEM_LIMIT),
        name="ab_in",
    )(x2d, w_ext, *tabs, qg, kg, ones2)


def _band_groups(qs, ks, vs, bias, left):
    chains = [(g, sel) for g in range(len(qs)) for sel in (left, jnp.logical_not(left))]
    scores = []
    for g, sel in chains:
        qm = jnp.where(sel, qs[g], jnp.zeros_like(qs[g]))
        scores.append(lax.dot_general(qm, ks[g], _NT, preferred_element_type=F32) + bias)
    probs = []
    for s in scores:
        mx = jnp.max(s, axis=1, keepdims=True)
        pe = jnp.exp(s - mx)
        probs.append((mx, jnp.sum(pe, axis=1, keepdims=True), pe.astype(BF16)))
    outs = []
    for (g, _), (mx, l, pe) in zip(chains, probs):
        o = jnp.dot(pe, vs[g], preferred_element_type=F32)
        outs.append((o * (1.0 / l), mx + jnp.log(l)))
    return [(jnp.where(left, outs[2 * g][0], outs[2 * g + 1][0]), jnp.where(left, outs[2 * g][1], outs[2 * g + 1][1]))
            for g in range(len(qs))]


def _band_kernel(q_ref, kp_ref, kc_ref, kn_ref, vp_ref, vc_ref, vn_ref, o_ref, lse_ref, *, m, step_rows):
    nk = BAND_L + 2 * HALF_WINDOW
    base = pl.program_id(len(q_ref.shape) - 2) * step_rows
    r = lax.broadcasted_iota(jnp.int32, (BAND_L, nk), 0)
    j = lax.broadcasted_iota(jnp.int32, (BAND_L, nk), 1)
    band = jnp.where(jnp.abs(j - HALF_WINDOW - r) <= HALF_WINDOW, 0.0, NEG)
    left = lax.broadcasted_iota(jnp.int32, (BAND_L, LANES), 1) < HEAD_DIM
    pre = (0,) * (len(q_ref.shape) - 2)
    lanes = [slice(p * LANES, (p + 1) * LANES) for p in range(A_W // LANES)]
    at = lambda rows, sl: pre + (rows, sl)
    whole = slice(None)
    ks = [jnp.concatenate([kp_ref[at(whole, sl)], kc_ref[at(whole, sl)], kn_ref[at(whole, sl)]], axis=0) for sl in lanes]
    vs = [jnp.concatenate([vp_ref[at(whole, sl)], vc_ref[at(whole, sl)], vn_ref[at(whole, sl)]], axis=0) for sl in lanes]
    for blk in range(step_rows // BAND_L):
        r0 = blk * BAND_L
        rows = slice(r0, r0 + BAND_L)
        kpos = base + r0 - HALF_WINDOW + j
        bias = jnp.where(kpos >= 0, band, NEG)
        bias = jnp.where(kpos < m, bias, NEG)
        res = _band_groups([q_ref[at(rows, sl)] for sl in lanes], [kk[r0:r0 + nk] for kk in ks],
                           [vv[r0:r0 + nk] for vv in vs], bias, left)
        for sl, (o, lse) in zip(lanes, res):
            o_ref[at(rows, sl)] = o.astype(BF16)
            lse_ref[at(rows, sl)] = lse


def _band_specs(lead, rows, step_rows):
    nh = rows // HALF_WINDOW
    ratio = step_rows // HALF_WINDOW
    ones = (1,) * lead
    cur = pl.BlockSpec(ones + (step_rows, A_W), lambda *g: g[:lead] + (g[-1], 0))
    prev = pl.BlockSpec(ones + (HALF_WINDOW, A_W), lambda *g: g[:lead] + (jnp.maximum(g[-1] * ratio - 1, 0), 0))
    nxt = pl.BlockSpec(ones + (HALF_WINDOW, A_W), lambda *g: g[:lead] + (jnp.minimum((g[-1] + 1) * ratio, nh - 1), 0))
    return cur, prev, nxt


def _band16(q16, k16, v16, batch, seq):
    m = seq // DIL_CLASSES
    step_rows = min(BAND_STEP, m)
    cur, prev, nxt = _band_specs(2, m, step_rows)
    shape = (batch, DIL_CLASSES, m, A_W)
    return pl.pallas_call(
        functools.partial(_band_kernel, m=m, step_rows=step_rows),
        grid=(batch, DIL_CLASSES, m // step_rows),
        in_specs=[cur, prev, cur, nxt, prev, cur, nxt],
        out_specs=[cur, cur],
        out_shape=[jax.ShapeDtypeStruct(shape, BF16), jax.ShapeDtypeStruct(shape, F32)],
        compiler_params=_params(("arbitrary",) * 3),
        name="band_d16",
    )(q16, k16, k16, k16, v16, v16, v16)


def _band4_kernel(q_ref, kp_ref, kc_ref, kn_ref, vp_ref, vc_ref, vn_ref, o16_ref, l16_ref, o_ref, lse_ref, *,
                  m16, step_rows):
    qr = BAND4_ROWS
    kr = qr + 2 * BAND4_HALO
    i = pl.program_id(2)
    shift_q, shift_k = qr.bit_length() - 1, kr.bit_length() - 1
    row = lax.broadcasted_iota(jnp.int32, (4 * qr, 4 * kr), 0)
    col = lax.broadcasted_iota(jnp.int32, (4 * qr, 4 * kr), 1)
    uq, rq = row >> shift_q, row & (qr - 1)
    uk, jk = col >> shift_k, col & (kr - 1)
    band = jnp.where(jnp.abs(4 * (jk - BAND4_HALO - rq) + (uk - uq)) <= HALF_WINDOW, 0.0, NEG)
    left = lax.broadcasted_iota(jnp.int32, (4 * qr, LANES), 1) < HEAD_DIM
    slabs = [slice(p * LANES, (p + 1) * LANES) for p in range(A_W // LANES)]
    window = lambda refs, u, sl: jnp.concatenate([r[0, u, 0, :, sl] for r in refs], axis=0)
    kwin = [[window([kp_ref, kc_ref, kn_ref], u, sl) for u in range(4)] for sl in slabs]
    vwin = [[window([vp_ref, vc_ref, vn_ref], u, sl) for u in range(4)] for sl in slabs]
    for blk in range(step_rows // qr):
        r0 = blk * qr
        rows = slice(r0, r0 + qr)
        kpos = i * step_rows + r0 - BAND4_HALO + jk
        bias = jnp.where(kpos >= 0, band, NEG)
        bias = jnp.where(kpos < m16, bias, NEG)
        classes = lambda ref, sl: jnp.concatenate([ref[0, u, 0, rows, sl] for u in range(4)], axis=0)
        branch4 = _band_groups([classes(q_ref, sl) for sl in slabs],
                               [jnp.concatenate([w[r0:r0 + kr] for w in kw], axis=0) for kw in kwin],
                               [jnp.concatenate([w[r0:r0 + kr] for w in vw], axis=0) for vw in vwin], bias, left)
        for sl, (o4, l4) in zip(slabs, branch4):
            o16 = classes(o16_ref, sl).astype(F32)
            l16 = classes(l16_ref, sl)
            mx = jnp.maximum(l4, l16)
            w4, w16 = jnp.exp(l4 - mx), jnp.exp(l16 - mx)
            tot = w4 + w16
            om = (w4 * o4 + w16 * o16) * (1.0 / tot)
            lm = mx + jnp.log(tot)
            for u in range(4):
                o_ref[0, u, 0, rows, sl] = om[u * qr:(u + 1) * qr].astype(BF16)
                lse_ref[0, u, 0, rows, sl] = lm[u * qr:(u + 1) * qr]


def _band4(q16, k16, v16, o16, l16, batch, seq):
    m16 = seq // DIL_CLASSES
    step_rows = min(BAND4_STEP, m16)
    ratio = step_rows // BAND4_HALO
    nh = m16 // BAND4_HALO
    view = lambda t: t.reshape(batch, 4, 4, m16, A_W)
    cur = pl.BlockSpec((1, 4, 1, step_rows, A_W), lambda b, c, i: (b, 0, c, i, 0))
    prev = pl.BlockSpec((1, 4, 1, BAND4_HALO, A_W), lambda b, c, i: (b, 0, c, jnp.maximum(i * ratio - 1, 0), 0))
    nxt = pl.BlockSpec((1, 4, 1, BAND4_HALO, A_W), lambda b, c, i: (b, 0, c, jnp.minimum((i + 1) * ratio, nh - 1), 0))
    shape = (batch, 4, 4, m16, A_W)
    o, lse = pl.pallas_call(
        functools.partial(_band4_kernel, m16=m16, step_rows=step_rows),
        grid=(batch, 4, m16 // step_rows),
        in_specs=[cur, prev, cur, nxt, prev, cur, nxt, cur, cur],
        out_specs=[cur, cur],
        out_shape=[jax.ShapeDtypeStruct(shape, BF16), jax.ShapeDtypeStruct(shape, F32)],
        compiler_params=_params(("arbitrary",) * 3),
        name="band_d4",
    )(view(q16), view(k16), view(k16), view(k16), view(v16), view(v16), view(v16), view(o16), view(l16))
    return o.reshape(batch, DIL_CLASSES, m16, A_W), lse.reshape(batch, DIL_CLASSES, m16, A_W)


def _gqa_kernel(q_ref, k_ref, v_ref, o_ref, qs_ref, m_ref, acc_ref, s_ref, vt_ref, *, seq):
    for pr in range(2):
        qt = q_ref[0, :, pr * LANES:(pr + 1) * LANES].astype(F32).T
        qs_ref[:, (2 * pr) * GQA_TQ:(2 * pr + 1) * GQA_TQ] = qt[0:HEAD_DIM].astype(BF16)
        qs_ref[:, (2 * pr + 1) * GQA_TQ:(2 * pr + 2) * GQA_TQ] = qt[HEAD_DIM:2 * HEAD_DIM].astype(BF16)
    m_ref[...] = jnp.full(m_ref.shape, NEG, F32)
    acc_ref[...] = jnp.zeros(acc_ref.shape, F32)
    n_tiles = seq // GQA_TK

    @pl.when(pl.program_id(2) == 0)
    def _():
        ones_row = lax.broadcasted_iota(jnp.int32, (GQA_V_ROWS, GQA_TK), 0) == HEAD_DIM

        def transpose_tile(t, carry):
            off = pl.multiple_of(t * GQA_TK, GQA_TK)
            vt = v_ref[0, pl.ds(off, GQA_TK), :].astype(F32).T[0:GQA_V_ROWS]
            vt_ref[t] = jnp.where(ones_row, 1.0, vt).astype(BF16)
            return carry

        lax.fori_loop(0, n_tiles, transpose_tile, 0)

    chains = [slice(c * GQA_CHAIN, (c + 1) * GQA_CHAIN) for c in range(4 * GQA_TQ // GQA_CHAIN)]

    def scores(tile, buf, cols):
        off = pl.multiple_of(tile * GQA_TK, GQA_TK)
        k = k_ref[0, pl.ds(off, GQA_TK), 0:HEAD_DIM]
        s_ref[buf, :, cols] = jnp.dot(k, qs_ref[:, cols], preferred_element_type=F32)

    def softmax_pv(tile, buf, cols):
        s = s_ref[buf, :, cols]
        m_prev = m_ref[:, cols]
        m_new = jnp.maximum(m_prev, jnp.max(s, axis=0, keepdims=True))
        alpha = jnp.exp2(m_prev - m_new)
        p = jnp.exp2((s - m_new).astype(BF16))
        pv = jnp.dot(vt_ref[tile], p, preferred_element_type=F32)
        acc_ref[:, cols] = alpha * acc_ref[:, cols] + pv
        m_ref[:, cols] = m_new

    def step(tile, buf, prefetch):
        for cols in chains:
            if prefetch:
                scores(tile + 1, 1 - buf, cols)
            softmax_pv(tile, buf, cols)

    def pair(j, carry):
        step(2 * j, 0, True)
        step(2 * j + 1, 1, True)
        return carry

    for cols in chains:
        scores(0, 0, cols)
    lax.fori_loop(0, n_tiles // 2 - 1, pair, 0)
    step(n_tiles - 2, 0, True)
    step(n_tiles - 1, 1, False)
    acc = acc_ref[...]
    a = acc[0:HEAD_DIM] * (1.0 / acc[HEAD_DIM:HEAD_DIM + 1, :])
    for pr in range(2):
        two_heads = a[:, (2 * pr) * GQA_TQ:(2 * pr + 2) * GQA_TQ]
        stacked = jnp.concatenate([two_heads[:, :GQA_TQ], two_heads[:, GQA_TQ:]], axis=0)
        o_ref[0, :, pr * LANES:(pr + 1) * LANES] = stacked.T.astype(BF16)


def _gqa(qb, kd, vd, batch, seq):
    gw = B_QW // B_KV_HEADS
    cols = 4 * GQA_TQ
    out = pl.pallas_call(
        functools.partial(_gqa_kernel, seq=seq),
        grid=(batch, B_KV_HEADS, seq // GQA_TQ),
        in_specs=[pl.BlockSpec((1, GQA_TQ, gw), lambda b, g, qi: (b, qi, g)),
                  pl.BlockSpec((1, seq, LANES), lambda b, g, qi: (b, 0, g)),
                  pl.BlockSpec((1, seq, LANES), lambda b, g, qi: (b, 0, g))],
        out_specs=pl.BlockSpec((1, GQA_TQ, gw), lambda b, g, qi: (b, qi, g)),
        out_shape=jax.ShapeDtypeStruct((batch, seq, B_QW), BF16),
        scratch_shapes=[pltpu.VMEM((HEAD_DIM, cols), BF16), pltpu.VMEM((1, cols), F32),
                        pltpu.VMEM((GQA_V_ROWS, cols), F32), pltpu.VMEM((2, GQA_TK, cols), F32),
                        pltpu.VMEM((seq // GQA_TK, GQA_V_ROWS, GQA_TK), BF16)],
        compiler_params=_params(("arbitrary",) * 3, VMEM_LIMIT),
        name="gqa",
    )(qb.reshape(batch, seq, B_QW), kd.reshape(batch, seq, 2 * B_KVW), vd.reshape(batch, seq, 2 * B_KVW))
    return out.reshape(batch * seq, B_QW)


def _ab_out_kernel(q_ref, kp_ref, kc_ref, kn_ref, vp_ref, vc_ref, vn_ref, om_ref, lm_ref, yb_ref, x_ref, w_ref,
                   g_ref, b_ref, out_ref, os_ref, ls_ref, *, nt, seq):
    for c in range(DIL_CLASSES):
        for g in range(A_W // LANES):
            sl = slice(g * LANES, (g + 1) * LANES)
            rows = pl.ds(c, TM // DIL_CLASSES, stride=DIL_CLASSES)
            os_ref[g, rows, :] = om_ref[0, c, :, sl].astype(F32)
            ls_ref[g, rows, :] = lm_ref[0, c, :, sl]
    nk = BAND_L + 2 * HALF_WINDOW
    base = (pl.program_id(0) % nt) * TM
    r = lax.broadcasted_iota(jnp.int32, (BAND_L, nk), 0)
    j = lax.broadcasted_iota(jnp.int32, (BAND_L, nk), 1)
    band = jnp.where(jnp.abs(j - HALF_WINDOW - r) <= HALF_WINDOW, 0.0, NEG)
    left = lax.broadcasted_iota(jnp.int32, (BAND_L, LANES), 1) < HEAD_DIM
    lanes = [slice(p * LANES, (p + 1) * LANES) for p in range(A_W // LANES)]
    ks = [jnp.concatenate([kp_ref[:, sl], kc_ref[:, sl], kn_ref[:, sl]], axis=0) for sl in lanes]
    vs = [jnp.concatenate([vp_ref[:, sl], vc_ref[:, sl], vn_ref[:, sl]], axis=0) for sl in lanes]
    for blk in range(TM // BAND_L):
        r0 = blk * BAND_L
        rows = slice(r0, r0 + BAND_L)
        kpos = base + r0 - HALF_WINDOW + j
        bias = jnp.where(kpos >= 0, band, NEG)
        bias = jnp.where(kpos < seq, bias, NEG)
        branch1 = _band_groups([q_ref[rows, sl] for sl in lanes], [kk[r0:r0 + nk] for kk in ks],
                               [vv[r0:r0 + nk] for vv in vs], bias, left)
        ya = []
        for g, (o1, l1) in enumerate(branch1):
            lm = ls_ref[g, rows, :]
            mx = jnp.maximum(l1, lm)
            e1, em = jnp.exp(l1 - mx), jnp.exp(lm - mx)
            ya.append(((e1 * o1 + em * os_ref[g, rows, :]) * (1.0 / (e1 + em))).astype(BF16))
        y = jnp.dot(jnp.concatenate(ya + [yb_ref[rows, :]], axis=1), w_ref[...], preferred_element_type=F32)
        out_ref[rows, :] = _layer_norm(ALPHA * x_ref[rows, :] + y, g_ref[...], b_ref[...])


def _ab_out(qa, ka, va, om, lm, yb, x2d, w, g, b, seq):
    t = x2d.shape[0]
    nt = seq // TM
    ratio = TM // HALF_WINDOW
    row = lambda n: pl.BlockSpec((TM, n), lambda i: (i, 0))
    prev = pl.BlockSpec((HALF_WINDOW, A_W), lambda i: (jnp.maximum(i * ratio - 1, 0), 0))
    nxt = pl.BlockSpec((HALF_WINDOW, A_W), lambda i: (jnp.minimum((i + 1) * ratio, t // HALF_WINDOW - 1), 0))
    cls = pl.BlockSpec((1, DIL_CLASSES, TM // DIL_CLASSES, A_W), lambda i: (i // nt, 0, i % nt, 0))
    return pl.pallas_call(
        functools.partial(_ab_out_kernel, nt=nt, seq=seq),
        grid=(t // TM,),
        in_specs=[row(A_W), prev, row(A_W), nxt, prev, row(A_W), nxt, cls, cls, row(B_QW), row(D_MODEL),
                  _const_spec((A_W + B_QW, D_MODEL)), _const_spec((1, D_MODEL)), _const_spec((1, D_MODEL))],
        out_specs=row(D_MODEL),
        out_shape=jax.ShapeDtypeStruct((t, D_MODEL), F32),
        scratch_shapes=[pltpu.VMEM((A_W // LANES, TM, LANES), F32)] * 2,
        compiler_params=_params(("arbitrary",), VMEM_LIMIT),
        name="ab_out",
    )(qa, ka, ka, ka, va, va, va, om, lm, yb, x2d, w, g, b)


def _gelu(x):
    return 0.5 * x * (1.0 + jnp.tanh(0.7978845608028654 * (x + 0.044715 * (x * x * x))))


def _ffn_kernel(x_ref, xp_ref, xn_ref, wu_ref, cw_ref, cb_ref, wd_ref, g_ref, b_ref, out_ref, *, nt):
    i = pl.program_id(0)
    keep_prev = (i % nt != 0).astype(F32)
    keep_next = (i % nt != nt - 1).astype(F32)
    rows = lax.broadcasted_iota(jnp.int32, (TM, 1), 0)
    offsets = [sum(FF_CHUNKS[:c]) for c in range(len(FF_CHUNKS))]
    xb = x_ref[...].astype(BF16)
    halo = jnp.concatenate([xp_ref[...], xn_ref[...]], axis=0).astype(BF16)
    xe = jnp.concatenate([xb, halo], axis=0)
    acc = []

    def up(c):
        a, n = offsets[c], FF_CHUNKS[c]
        u = _row_block_dot(xb, wu_ref[:, a:a + n], FFN_ROWS)
        ge = _row_block_dot(xe, wu_ref[:, D_FF + a:D_FF + a + n], FFN_ROWS)
        return u, ge

    def finish(c, pending):
        u, ge = pending
        a, n = offsets[c], FF_CHUNKS[c]
        gm = ge[0:TM]
        g_before = ge[TM + 7:TM + 8] * keep_prev
        g_after = ge[TM + 8:TM + 9] * keep_next
        gp = jnp.where(rows == 0, g_before, pltpu.roll(gm, 1, 0))
        gn = jnp.where(rows == TM - 1, g_after, pltpu.roll(gm, TM - 1, 0))
        cw = cw_ref[:, a:a + n]
        gc = gp * cw[0:1] + gm * cw[1:2] + gn * cw[2:3] + cb_ref[:, a:a + n]
        act = (_gelu(gc) * u).astype(BF16)
        part = _row_block_dot(act, wd_ref[a:a + n, :], FFN_ROWS)
        acc[:] = [part if not acc else acc[0] + part]

    _pipelined([(functools.partial(up, c), functools.partial(finish, c)) for c in range(len(FF_CHUNKS))])
    out_ref[...] = _layer_norm(ALPHA * x_ref[...] + acc[0], g_ref[...], b_ref[...])


def _ffn(x2d, wu, cw, cb, wd, g, b, seq):
    t = x2d.shape[0]
    r8 = TM // 8
    return pl.pallas_call(
        functools.partial(_ffn_kernel, nt=seq // TM),
        grid=(t // TM,),
        in_specs=[pl.BlockSpec((TM, D_MODEL), lambda i: (i, 0)),
                  pl.BlockSpec((8, D_MODEL), lambda i: (jnp.maximum(i * r8 - 1, 0), 0)),
                  pl.BlockSpec((8, D_MODEL), lambda i: (jnp.minimum((i + 1) * r8, t // 8 - 1), 0)),
                  _const_spec((D_MODEL, 2 * D_FF), single=True),
                  _const_spec((3, D_FF)), _const_spec((1, D_FF)),
                  _const_spec((D_FF, D_MODEL), single=True),
                  _const_spec((1, D_MODEL)), _const_spec((1, D_MODEL))],
        out_specs=pl.BlockSpec((TM, D_MODEL), lambda i: (i, 0)),
        out_shape=jax.ShapeDtypeStruct((t, D_MODEL), F32),
        compiler_params=_params(("arbitrary",), VMEM_LIMIT),
        name="ffn",
    )(x2d, x2d, x2d, wu, cw, cb, wd, g, b)


def _lower_bound(tbl, layer):
    rows = [tbl[r:r + 1] for r in range(DEPTH)]
    mx = functools.reduce(jnp.maximum, rows)
    es = [jnp.exp(r - mx) for r in rows]
    inv = 1.0 / functools.reduce(lambda a, b: a + b, es)
    ps = [e * inv for e in es]
    return functools.reduce(lambda a, b: a + b, ps[:layer + 1]) - ps[0]


def _c_in_kernel(x_ref, w_ref, lbf_ref, lbb_ref, q_ref, lf_ref, lb_ref, v_ref, g_ref, *, layer):
    half = C_W // 2
    blocks = [slice(r * PROJ_ROWS, (r + 1) * PROJ_ROWS) for r in range(TM // PROJ_ROWS)]
    xbs = [x_ref[rows, :].astype(BF16) for rows in blocks]

    def proj(r, a, n):
        return jnp.dot(xbs[r], w_ref[:, a:a + n], preferred_element_type=F32)

    def silu(z):
        return z * _sigmoid(z)

    lbf = _lower_bound(lbf_ref[...], layer)
    lbb = _lower_bound(lbb_ref[...], layer)

    def store_q(z, rows, sl):
        q_ref[rows, sl] = z.astype(BF16)

    def store_logf(tbl, out_ref):
        def fn(z, rows, sl):
            lb = tbl[:, sl]
            out_ref[rows, sl] = jnp.log(lb + (1.0 - lb) * _sigmoid(z))
        return fn

    def store_silu(out_ref):
        def fn(z, rows, sl):
            out_ref[rows, sl] = silu(z).astype(BF16)
        return fn

    sections = []
    for r, rows in enumerate(blocks):
        for c in range(2):
            a = c * half
            sl = slice(a, a + half)
            for off, fn in ((0, store_q), (C_W, store_logf(lbf, lf_ref)), (2 * C_W, store_logf(lbb, lb_ref)),
                            (3 * C_W, store_silu(v_ref)), (3 * C_W + C_VW, store_silu(g_ref))):
                sections.append((functools.partial(proj, r, off + a, half), functools.partial(fn, rows=rows, sl=sl)))
    _pipelined(sections)


def _c_in(x2d, w, lbf, lbb, layer):
    t = x2d.shape[0]
    row = pl.BlockSpec((TM, C_W), lambda i: (i, 0))
    return pl.pallas_call(
        functools.partial(_c_in_kernel, layer=layer),
        grid=(t // TM,),
        in_specs=[row, _const_spec(w.shape), _const_spec((DEPTH, C_W)), _const_spec((DEPTH, C_W))],
        out_specs=[row] * 5,
        out_shape=[jax.ShapeDtypeStruct((t, C_W), d) for d in (BF16, F32, F32, BF16, BF16)],
        compiler_params=_params(("arbitrary",), VMEM_LIMIT),
        name="c_in",
    )(x2d, w, lbf, lbb)


def _hgrn_kernel(*refs, rev, fused, heads, step_rows):
    if fused:
        (q_ref, lf_ref, v_ref, of_ref, gate_ref, gn_ref, x_ref, w_ref, lg_ref, lb_ref,
         out_ref, st_ref, y_ref) = refs
    else:
        q_ref, lf_ref, v_ref, y_ref, st_ref = refs

    @pl.when(pl.program_id(2) == 0)
    def _():
        st_ref[...] = jnp.zeros(st_ref.shape, F32)

    ri = lax.broadcasted_iota(jnp.int32, (HG_CHUNK, HG_CHUNK), 0)
    ci = lax.broadcasted_iota(jnp.int32, (HG_CHUNK, HG_CHUNK), 1)
    tri = (ci >= ri) if rev else (ci <= ri)
    trib = jnp.where(tri, 1.0, 0.0).astype(BF16)
    dotf = lambda a, b: jnp.dot(a, b, preferred_element_type=F32)
    n_chunks = step_rows // HG_CHUNK
    order = range(n_chunks - 1, -1, -1) if rev else range(n_chunks)
    units = [(c, h) for c in order for h in range(heads)]
    window = lambda u: (0, slice(u[0] * HG_CHUNK, (u[0] + 1) * HG_CHUNK), slice(u[1] * C_KEY, (u[1] + 1) * C_KEY))

    b_all = {}
    for u in units:
        lf = lf_ref[window(u)]
        hi = lf.astype(BF16)
        lo = (lf - hi.astype(F32)).astype(BF16)
        b_all[u] = dotf(trib, hi) + dotf(trib, lo)

    qdec, kdec, etot, a_rows = {}, {}, {}, {}
    for u in units:
        lf, b = lf_ref[window(u)], b_all[u]
        qf = q_ref[window(u)].astype(F32)
        bex = b - lf
        kk = 1.0 - jnp.exp(lf)
        btot = b[0:1] if rev else b[HG_CHUNK - 1:HG_CHUNK]
        qdec[u] = (qf * jnp.exp(b)).astype(BF16)
        kdec[u] = (kk * jnp.exp(btot - b)).astype(BF16)
        etot[u] = jnp.exp(btot)
        rows = []
        for blk in range(HG_CHUNK // HG_SUB):
            r0 = blk * HG_SUB
            ref = bex[r0 + HG_SUB - 1:r0 + HG_SUB] if rev else bex[r0:r0 + 1]
            qt = (qf[r0:r0 + HG_SUB] * jnp.exp(b[r0:r0 + HG_SUB] - ref)).astype(BF16)
            lo_r, hi_r = (r0, HG_CHUNK) if rev else (0, r0 + HG_SUB)
            ks = (kk[lo_r:hi_r] * jnp.exp(ref - b[lo_r:hi_r])).astype(BF16)
            pieces = []
            if lo_r > 0:
                pieces.append(jnp.zeros((lo_r, C_KEY), BF16))
            pieces.append(ks)
            if hi_r < HG_CHUNK:
                pieces.append(jnp.zeros((HG_CHUNK - hi_r, C_KEY), BF16))
            kfull = jnp.concatenate(pieces, axis=0) if len(pieces) > 1 else ks
            rows.append(lax.dot_general(qt, kfull, _NT, preferred_element_type=F32))
        a_rows[u] = rows
    attn = {u: jnp.where(tri, jnp.concatenate(a_rows[u], axis=0), 0.0).astype(BF16) for u in units}

    states = [st_ref[h] for h in range(heads)]
    for u in units:
        h = u[1]
        vb = v_ref[window(u)]
        o = lax.dot_general(qdec[u], states[h].astype(BF16), _NT, preferred_element_type=F32)
        o = o + dotf(attn[u], vb)
        states[h] = states[h] * etot[u] + lax.dot_general(vb, kdec[u], _TN, preferred_element_type=F32)
        if fused:
            tot = of_ref[window(u)] + o
            inv = lax.rsqrt(jnp.mean(tot * tot, -1, keepdims=True) + RMS_EPS)
            y = tot * inv * gn_ref[:, window(u)[2]] * gate_ref[window(u)].astype(F32)
            y_ref[window(u)[1:]] = y.astype(BF16)
        else:
            y_ref[window(u)] = o
    for h in range(heads):
        st_ref[h] = states[h]
    if fused:
        proj = _row_block_dot(y_ref[...], w_ref[...])
        out_ref[0] = _layer_norm(ALPHA * x_ref[0] + proj, lg_ref[...], lb_ref[...])


def _hgrn_specs(batch, seq, rev, heads, step_rows):
    nb = seq // step_rows
    blk = (lambda b, h, i: (b, nb - 1 - i, h)) if rev else (lambda b, h, i: (b, i, h))
    return pl.BlockSpec((1, step_rows, heads * C_KEY), blk), (batch, C_HEADS // heads, nb)


def _hgrn_fwd(q, lf, v, batch, seq):
    tile, grid = _hgrn_specs(batch, seq, False, HG_HP, HG_TB)
    view = lambda t: t.reshape(batch, seq, C_W)
    out = pl.pallas_call(
        functools.partial(_hgrn_kernel, rev=False, fused=False, heads=HG_HP, step_rows=HG_TB),
        grid=grid,
        in_specs=[tile, tile, tile],
        out_specs=tile,
        out_shape=jax.ShapeDtypeStruct((batch, seq, C_VW), F32),
        scratch_shapes=[pltpu.VMEM((HG_HP, C_VAL, C_KEY), F32)],
        compiler_params=_params(("arbitrary",) * 3),
        name="hgrn_fwd",
    )(view(q), view(lf), view(v))
    return out.reshape(batch * seq, C_VW)


def _hgrn_bwd_out(q, lf, v, o_f, gate, gn, x2d, w, g, b, batch, seq):
    tile, grid = _hgrn_specs(batch, seq, True, C_HEADS, HG_TB_OUT)
    view = lambda t: t.reshape(batch, seq, C_W)
    out = pl.pallas_call(
        functools.partial(_hgrn_kernel, rev=True, fused=True, heads=C_HEADS, step_rows=HG_TB_OUT),
        grid=grid,
        in_specs=[tile] * 5 + [_const_spec((1, C_VW)), tile, _const_spec((C_VW, D_MODEL), single=True),
                               _const_spec((1, D_MODEL)), _const_spec((1, D_MODEL))],
        out_specs=tile,
        out_shape=jax.ShapeDtypeStruct((batch, seq, D_MODEL), F32),
        scratch_shapes=[pltpu.VMEM((C_HEADS, C_VAL, C_KEY), F32), pltpu.VMEM((HG_TB_OUT, C_VW), BF16)],
        compiler_params=_params(("arbitrary",) * 3, VMEM_LIMIT),
        name="hgrn_bwd_out",
    )(view(q), view(lf), view(v), view(o_f), view(gate), gn, x2d.reshape(batch, seq, D_MODEL), w, g, b)
    return out.reshape(batch * seq, D_MODEL)


def _prep_weights(w_in_ab, w_out_ab, qn_ab, kn_ab, w_in_c, w_out_c, gn_c, ffn_w_up, ffn_w_down):
    scale = HEAD_DIM ** -0.5
    ab = []
    for j in range(w_in_ab.shape[0]):
        w = w_in_ab[j]
        o3 = 3 * A_W
        kb = w[:, o3 + B_QW:o3 + B_QW + B_KVW]
        vb = w[:, o3 + B_QW + B_KVW:]
        dup = lambda t: jnp.concatenate([t[:, :HEAD_DIM], t[:, :HEAD_DIM], t[:, HEAD_DIM:], t[:, HEAD_DIM:]], 1)
        w_ext = jnp.concatenate([w[:, :A_W] * scale, w[:, A_W:o3 + B_QW], dup(kb), dup(vb)], 1).astype(BF16)
        qg = (jnp.concatenate([qn_ab[j], qn_ab[j]]) * (scale * LOG2E)).reshape(1, LANES)
        kg = jnp.concatenate([kn_ab[j], kn_ab[j]]).reshape(1, LANES)
        ab.append((w_ext, qg, kg, w_out_ab[j].astype(BF16)))
    cc = [(w_in_c[j].astype(BF16), w_out_c[j].astype(BF16), gn_c[j].reshape(1, C_VW)) for j in range(w_in_c.shape[0])]
    return ab, cc, ffn_w_up.astype(BF16), ffn_w_down.astype(BF16)


def _trunk(x, prep, lb_fwd, lb_bwd, ln_mix_g, ln_mix_b, ln_ffn_g, ln_ffn_b, ffn_conv_w, ffn_conv_b):
    ab, cc, wu, wd = prep
    batch, seq, _ = x.shape
    x2d = x.reshape(batch * seq, D_MODEL)
    tabs = _rope_tables(seq)
    blk = jnp.arange(LANES) // HEAD_DIM
    ones2 = (blk[:, None] == blk[None, :]).astype(BF16)
    vec = lambda t: t.reshape(1, -1)
    for l in range(DEPTH):
        j = l // 2
        if l % 2 == 0:
            w_ext, qg, kg, w_out = ab[j]
            assert [d for _, d in A_PATTERNS] == [1, 4, DIL_CLASSES]
            assert all(window // (2 * d) == HALF_WINDOW for window, d in A_PATTERNS)
            qa, ka, va, qb, kd, vd, qa16, ka16, va16 = _ab_in(x2d, w_ext, tabs, qg, kg, ones2, batch, seq)
            o16, l16 = _band16(qa16, ka16, va16, batch, seq)
            om, lm = _band4(qa16, ka16, va16, o16, l16, batch, seq)
            yb = _gqa(qb, kd, vd, batch, seq)
            x2d = _ab_out(qa, ka, va, om, lm, yb, x2d, w_out, vec(ln_mix_g[l]), vec(ln_mix_b[l]), seq)
        else:
            w_in, w_out, gn = cc[j]
            q, lf, lb, v, gate = _c_in(x2d, w_in, lb_fwd, lb_bwd, l)
            o_f = _hgrn_fwd(q, lf, v, batch, seq)
            x2d = _hgrn_bwd_out(q, lb, v, o_f, gate, gn, x2d, w_out, vec(ln_mix_g[l]), vec(ln_mix_b[l]), batch, seq)
        x2d = _ffn(x2d, wu[l], ffn_conv_w[l], vec(ffn_conv_b[l]), wd[l], vec(ln_ffn_g[l]), vec(ln_ffn_b[l]), seq)
    return x2d.reshape(batch, seq, D_MODEL)


def kernel(x_prompt, x_sample, w_in_ab, w_out_ab, qn_ab, kn_ab, w_in_c, w_out_c, lb_fwd, lb_bwd, gn_c, ln_mix_g, ln_mix_b, ln_ffn_g, ln_ffn_b, ffn_w_up, ffn_conv_w, ffn_conv_b, ffn_w_down):
    prep = _prep_weights(w_in_ab, w_out_ab, qn_ab, kn_ab, w_in_c, w_out_c, gn_c, ffn_w_up, ffn_w_down)
    rest = (lb_fwd, lb_bwd, ln_mix_g, ln_mix_b, ln_ffn_g, ln_ffn_b, ffn_conv_w, ffn_conv_b)
    return (_trunk(x_prompt, prep, *rest), _trunk(x_sample, prep, *rest))
```

```python
import functools

import jax
import jax.numpy as jnp
from jax import lax
from jax.experimental import pallas as pl
from jax.experimental.pallas import tpu as pltpu

F32 = jnp.float32
BF16 = jnp.bfloat16

D_MODEL = 1024
DEPTH = 2
HEAD_DIM = 64
A_HEADS = 8
A_PATTERNS = ((128, 1), (512, 4), (2048, 16))
B_HEADS = 8
B_KV_HEADS = 2
GRID_W = 64
ROPE_THETA = 500000.0
ROPE_DIM = HEAD_DIM // 4
AXIAL_THETA = 10000.0
C_HEADS = 8
C_KEY = 128
C_VAL = 128
D_FF = 2816
ALPHA = (2 * DEPTH) ** 0.25
LN_EPS = 1e-5
RMS_EPS = 1e-6
A_W = A_HEADS * HEAD_DIM
B_QW = B_HEADS * HEAD_DIM
B_KVW = B_KV_HEADS * HEAD_DIM
C_W = C_HEADS * C_KEY
C_VW = C_HEADS * C_VAL

LANES = 128
HALF_WINDOW = 64
TM = 512
PROJ_ROWS = 128
FFN_ROWS = 256
BAND_L = 128
BAND_STEP = 512
BAND4_STEP = 128
DIL_CLASSES = 16
BAND4_ROWS = 32
BAND4_HALO = 16
GQA_TQ = 1024
GQA_TK = 512
GQA_CHAIN = 512
GQA_V_ROWS = 80
FF_CHUNKS = (768, 768, 768, 512)
HG_CHUNK = 64
HG_SUB = 16
HG_TB = 512
HG_HP = 4
HG_TB_OUT = 256
NEG = -1e30
LOG2E = 1.4426950408889634
VMEM_LIMIT = 56 * 1024 * 1024

_NT = (((1,), (1,)), ((), ()))
_TN = (((0,), (0,)), ((), ()))


def _params(sem, vmem=None):
    return pltpu.CompilerParams(dimension_semantics=sem, vmem_limit_bytes=vmem)


def _const_spec(shape, single=False):
    nd = len(shape)
    if single:
        return pl.BlockSpec(shape, lambda *_: (0,) * nd, pipeline_mode=pl.Buffered(1))
    return pl.BlockSpec(shape, lambda *_: (0,) * nd)


def _layer_norm(z, g, b):
    mu = jnp.mean(z, -1, keepdims=True)
    d = z - mu
    var = jnp.mean(d * d, -1, keepdims=True)
    return d * lax.rsqrt(var + LN_EPS) * g + b


def _sigmoid(z):
    return 1.0 / (1.0 + jnp.exp(-z))


def _row_block_dot(x, w, rows=PROJ_ROWS):
    m = x.shape[0]
    starts = list(range(0, m - m % rows, rows)) or [0]
    ends = starts[1:] + [m]
    parts = [jnp.dot(x[a:b], w, preferred_element_type=F32) for a, b in zip(starts, ends)]
    return parts[0] if len(parts) == 1 else jnp.concatenate(parts, axis=0)


def _pipelined(sections):
    pending = sections[0][0]()
    for i, (_, epilogue) in enumerate(sections):
        upcoming = sections[i + 1][0]() if i + 1 < len(sections) else None
        epilogue(pending)
        pending = upcoming


def _rope_tables(seq):
    pos = jnp.arange(seq, dtype=F32)
    d = jnp.arange(LANES) % HEAD_DIM

    def angles(p, dim, theta):
        freqs = theta ** (-(jnp.arange(0, dim, 2, dtype=F32) / dim))
        return p[:, None] * freqs[None, :]

    h = ROPE_DIM // 2
    ang = angles(pos, ROPE_DIM, ROPE_THETA)
    a = ang[:, d % h]
    lo, hi = d < h, (d >= h) & (d < ROPE_DIM)
    pc = jnp.where(lo | hi, jnp.cos(a), 1.0)
    psa = jnp.where(lo, -jnp.sin(a), 0.0)
    psb = jnp.where(hi, jnp.sin(a), 0.0)

    q = HEAD_DIM // 4
    row = jnp.floor(pos / GRID_W)
    col = pos - row * GRID_W
    ar = angles(row, HEAD_DIM // 2, AXIAL_THETA)[:, d % q]
    ac_ = angles(col, HEAD_DIM // 2, AXIAL_THETA)[:, d % q]
    a2 = jnp.where(d < HEAD_DIM // 2, ar, ac_)
    first = (d % (HEAD_DIM // 2)) < q
    ac = jnp.cos(a2)
    asa = jnp.where(first, -jnp.sin(a2), 0.0)
    asb = jnp.where(first, 0.0, jnp.sin(a2))
    return [t.astype(F32) for t in (pc, psa, psb, ac, asa, asb)]


def _rope(seg, c, sa, sb, shift):
    return seg * c + pltpu.roll(seg, LANES - shift, 1) * sa + pltpu.roll(seg, shift, 1) * sb


def _ab_in_kernel(x_ref, w_ref, pc_ref, psa_ref, psb_ref, ac_ref, asa_ref, asb_ref, qg_ref, kg_ref,
                  ones_ref, qa_ref, ka_ref, va_ref, qb_ref, kd_ref, vd_ref, qa16_ref, ka16_ref, va16_ref,
                  qs_ref, ks_ref, vs_ref):
    xb = x_ref[...].astype(BF16)

    def proj(a, n):
        return jnp.dot(xb, w_ref[:, a:a + n], preferred_element_type=F32)

    pc, psa, psb = pc_ref[...], psa_ref[...], psb_ref[...]
    ac, asa, asb = ac_ref[...], asa_ref[...], asb_ref[...]

    def dilated_a(h, rope, out_ref, out16_ref, slab_ref):
        for g in range(A_W // LANES):
            sl = slice(g * LANES, (g + 1) * LANES)
            y = _rope(h[:, sl], pc, psa, psb, ROPE_DIM // 2) if rope else h[:, sl]
            out_ref[:, sl] = y.astype(BF16)
            slab_ref[g] = y
        for c in range(DIL_CLASSES):
            for g in range(A_W // LANES):
                rows = slab_ref[g, pl.ds(c, TM // DIL_CLASSES, stride=DIL_CLASSES), :]
                out16_ref[0, c, :, g * LANES:(g + 1) * LANES] = rows.astype(BF16)

    def norm_rope(h, gain, out_ref):
        for g in range(h.shape[1] // LANES):
            sl = slice(g * LANES, (g + 1) * LANES)
            seg = h[:, sl]
            sq = seg * seg
            hi = sq.astype(BF16)
            lo = (sq - hi.astype(F32)).astype(BF16)
            ss = (jnp.dot(hi, ones_ref[...], preferred_element_type=F32)
                  + jnp.dot(lo, ones_ref[...], preferred_element_type=F32))
            y = seg * lax.rsqrt(ss * (1.0 / HEAD_DIM) + RMS_EPS) * gain
            out_ref[:, sl] = _rope(y, ac, asa, asb, HEAD_DIM // 4).astype(BF16)

    def cast_to(out_ref):
        def fn(h):
            out_ref[...] = h.astype(BF16)
        return fn

    o_qb = 3 * A_W
    o_kd = o_qb + B_QW
    o_vd = o_kd + 2 * B_KVW
    _pipelined([
        (functools.partial(proj, 0, A_W),
         functools.partial(dilated_a, rope=True, out_ref=qa_ref, out16_ref=qa16_ref, slab_ref=qs_ref)),
        (functools.partial(proj, A_W, A_W),
         functools.partial(dilated_a, rope=True, out_ref=ka_ref, out16_ref=ka16_ref, slab_ref=ks_ref)),
        (functools.partial(proj, 2 * A_W, A_W),
         functools.partial(dilated_a, rope=False, out_ref=va_ref, out16_ref=va16_ref, slab_ref=vs_ref)),
        (functools.partial(proj, o_qb, B_QW), functools.partial(norm_rope, gain=qg_ref[...], out_ref=qb_ref)),
        (functools.partial(proj, o_kd, 2 * B_KVW), functools.partial(norm_rope, gain=kg_ref[...], out_ref=kd_ref)),
        (functools.partial(proj, o_vd, 2 * B_KVW), cast_to(vd_ref)),
    ])


def _ab_in(x2d, w_ext, tabs, qg, kg, ones2, batch, seq):
    t = x2d.shape[0]
    nt = seq // TM
    tab = pl.BlockSpec((TM, LANES), lambda i: (i % nt, 0))
    row = lambda n: pl.BlockSpec((TM, n), lambda i: (i, 0))
    cls = pl.BlockSpec((1, DIL_CLASSES, TM // DIL_CLASSES, A_W), lambda i: (i // nt, 0, i % nt, 0))
    cls_shape = jax.ShapeDtypeStruct((batch, DIL_CLASSES, seq // DIL_CLASSES, A_W), BF16)
    wn = w_ext.shape[1]
    return pl.pallas_call(
        _ab_in_kernel,
        grid=(t // TM,),
        in_specs=[row(D_MODEL), _const_spec((D_MODEL, wn))] + [tab] * 6
        + [_const_spec((1, LANES)), _const_spec((1, LANES)), _const_spec((LANES, LANES))],
        out_specs=[row(A_W)] * 3 + [row(B_QW), row(2 * B_KVW), row(2 * B_KVW)] + [cls] * 3,
        out_shape=[jax.ShapeDtypeStruct((t, A_W), BF16)] * 3
        + [jax.ShapeDtypeStruct((t, B_QW), BF16)]
        + [jax.ShapeDtypeStruct((t, 2 * B_KVW), BF16)] * 2 + [cls_shape] * 3,
        scratch_shapes=[pltpu.VMEM((A_W // LANES, TM, LANES), F32)] * 3,
        compiler_params=_params(("arbitrary",), VMEM_LIMIT),
        name="ab_in",
    )(x2d, w_ext, *tabs, qg, kg, ones2)


def _band_groups(qs, ks, vs, bias, left):
    chains = [(g, sel) for g in range(len(qs)) for sel in (left, jnp.logical_not(left))]
    scores = []
    for g, sel in chains:
        qm = jnp.where(sel, qs[g], jnp.zeros_like(qs[g]))
        scores.append(lax.dot_general(qm, ks[g], _NT, preferred_element_type=F32) + bias)
    probs = []
    for s in scores:
        mx = jnp.max(s, axis=1, keepdims=True)
        pe = jnp.exp(s - mx)
        probs.append((mx, jnp.sum(pe, axis=1, keepdims=True), pe.astype(BF16)))
    outs = []
    for (g, _), (mx, l, pe) in zip(chains, probs):
        o = jnp.dot(pe, vs[g], preferred_element_type=F32)
        outs.append((o * (1.0 / l), mx + jnp.log(l)))
    return [(jnp.where(left, outs[2 * g][0], outs[2 * g + 1][0]), jnp.where(left, outs[2 * g][1], outs[2 * g + 1][1]))
            for g in range(len(qs))]


def _band_kernel(q_ref, kp_ref, kc_ref, kn_ref, vp_ref, vc_ref, vn_ref, o_ref, lse_ref, *, m, step_rows):
    nk = BAND_L + 2 * HALF_WINDOW
    base = pl.program_id(len(q_ref.shape) - 2) * step_rows
    r = lax.broadcasted_iota(jnp.int32, (BAND_L, nk), 0)
    j = lax.broadcasted_iota(jnp.int32, (BAND_L, nk), 1)
    band = jnp.where(jnp.abs(j - HALF_WINDOW - r) <= HALF_WINDOW, 0.0, NEG)
    left = lax.broadcasted_iota(jnp.int32, (BAND_L, LANES), 1) < HEAD_DIM
    pre = (0,) * (len(q_ref.shape) - 2)
    lanes = [slice(p * LANES, (p + 1) * LANES) for p in range(A_W // LANES)]
    at = lambda rows, sl: pre + (rows, sl)
    whole = slice(None)
    ks = [jnp.concatenate([kp_ref[at(whole, sl)], kc_ref[at(whole, sl)], kn_ref[at(whole, sl)]], axis=0) for sl in lanes]
    vs = [jnp.concatenate([vp_ref[at(whole, sl)], vc_ref[at(whole, sl)], vn_ref[at(whole, sl)]], axis=0) for sl in lanes]
    for blk in range(step_rows // BAND_L):
        r0 = blk * BAND_L
        rows = slice(r0, r0 + BAND_L)
        kpos = base + r0 - HALF_WINDOW + j
        bias = jnp.where(kpos >= 0, band, NEG)
        bias = jnp.where(kpos < m, bias, NEG)
        res = _band_groups([q_ref[at(rows, sl)] for sl in lanes], [kk[r0:r0 + nk] for kk in ks],
                           [vv[r0:r0 + nk] for vv in vs], bias, left)
        for sl, (o, lse) in zip(lanes, res):
            o_ref[at(rows, sl)] = o.astype(BF16)
            lse_ref[at(rows, sl)] = lse


def _band_specs(lead, rows, step_rows):
    nh = rows // HALF_WINDOW
    ratio = step_rows // HALF_WINDOW
    ones = (1,) * lead
    cur = pl.BlockSpec(ones + (step_rows, A_W), lambda *g: g[:lead] + (g[-1], 0))
    prev = pl.BlockSpec(ones + (HALF_WINDOW, A_W), lambda *g: g[:lead] + (jnp.maximum(g[-1] * ratio - 1, 0), 0))
    nxt = pl.BlockSpec(ones + (HALF_WINDOW, A_W), lambda *g: g[:lead] + (jnp.minimum((g[-1] + 1) * ratio, nh - 1), 0))
    return cur, prev, nxt


def _band1(q, k, v, batch, seq):
    step_rows = min(BAND_STEP, seq)
    cur, prev, nxt = _band_specs(1, seq, step_rows)
    view = lambda t: t.reshape(batch, seq, A_W)
    o, lse = pl.pallas_call(
        functools.partial(_band_kernel, m=seq, step_rows=step_rows),
        grid=(batch, seq // step_rows),
        in_specs=[cur, prev, cur, nxt, prev, cur, nxt],
        out_specs=[cur, cur],
        out_shape=[jax.ShapeDtypeStruct((batch, seq, A_W), BF16), jax.ShapeDtypeStruct((batch, seq, A_W), F32)],
        compiler_params=_params(("arbitrary",) * 2),
        name="band_d1",
    )(view(q), view(k), view(k), view(k), view(v), view(v), view(v))
    return o.reshape(batch * seq, A_W), lse.reshape(batch * seq, A_W)


def _band16(q16, k16, v16, batch, seq):
    m = seq // DIL_CLASSES
    step_rows = min(BAND_STEP, m)
    cur, prev, nxt = _band_specs(2, m, step_rows)
    shape = (batch, DIL_CLASSES, m, A_W)
    return pl.pallas_call(
        functools.partial(_band_kernel, m=m, step_rows=step_rows),
        grid=(batch, DIL_CLASSES, m // step_rows),
        in_specs=[cur, prev, cur, nxt, prev, cur, nxt],
        out_specs=[cur, cur],
        out_shape=[jax.ShapeDtypeStruct(shape, BF16), jax.ShapeDtypeStruct(shape, F32)],
        compiler_params=_params(("arbitrary",) * 3),
        name="band_d16",
    )(q16, k16, k16, k16, v16, v16, v16)


def _band4_kernel(q_ref, kp_ref, kc_ref, kn_ref, vp_ref, vc_ref, vn_ref, o16_ref, l16_ref, o_ref, lse_ref, *,
                  m16, step_rows):
    qr = BAND4_ROWS
    kr = qr + 2 * BAND4_HALO
    i = pl.program_id(2)
    shift_q, shift_k = qr.bit_length() - 1, kr.bit_length() - 1
    row = lax.broadcasted_iota(jnp.int32, (4 * qr, 4 * kr), 0)
    col = lax.broadcasted_iota(jnp.int32, (4 * qr, 4 * kr), 1)
    uq, rq = row >> shift_q, row & (qr - 1)
    uk, jk = col >> shift_k, col & (kr - 1)
    band = jnp.where(jnp.abs(4 * (jk - BAND4_HALO - rq) + (uk - uq)) <= HALF_WINDOW, 0.0, NEG)
    left = lax.broadcasted_iota(jnp.int32, (4 * qr, LANES), 1) < HEAD_DIM
    slabs = [slice(p * LANES, (p + 1) * LANES) for p in range(A_W // LANES)]
    window = lambda refs, u, sl: jnp.concatenate([r[0, u, 0, :, sl] for r in refs], axis=0)
    kwin = [[window([kp_ref, kc_ref, kn_ref], u, sl) for u in range(4)] for sl in slabs]
    vwin = [[window([vp_ref, vc_ref, vn_ref], u, sl) for u in range(4)] for sl in slabs]
    for blk in range(step_rows // qr):
        r0 = blk * qr
        rows = slice(r0, r0 + qr)
        kpos = i * step_rows + r0 - BAND4_HALO + jk
        bias = jnp.where(kpos >= 0, band, NEG)
        bias = jnp.where(kpos < m16, bias, NEG)
        classes = lambda ref, sl: jnp.concatenate([ref[0, u, 0, rows, sl] for u in range(4)], axis=0)
        branch4 = _band_groups([classes(q_ref, sl) for sl in slabs],
                               [jnp.concatenate([w[r0:r0 + kr] for w in kw], axis=0) for kw in kwin],
                               [jnp.concatenate([w[r0:r0 + kr] for w in vw], axis=0) for vw in vwin], bias, left)
        for sl, (o4, l4) in zip(slabs, branch4):
            o16 = classes(o16_ref, sl).astype(F32)
            l16 = classes(l16_ref, sl)
            mx = jnp.maximum(l4, l16)
            w4, w16 = jnp.exp(l4 - mx), jnp.exp(l16 - mx)
            tot = w4 + w16
            om = (w4 * o4 + w16 * o16) * (1.0 / tot)
            lm = mx + jnp.log(tot)
            for u in range(4):
                o_ref[0, u, 0, rows, sl] = om[u * qr:(u + 1) * qr].astype(BF16)
                lse_ref[0, u, 0, rows, sl] = lm[u * qr:(u + 1) * qr]


def _band4(q16, k16, v16, o16, l16, batch, seq):
    m16 = seq // DIL_CLASSES
    step_rows = min(BAND4_STEP, m16)
    ratio = step_rows // BAND4_HALO
    nh = m16 // BAND4_HALO
    view = lambda t: t.reshape(batch, 4, 4, m16, A_W)
    cur = pl.BlockSpec((1, 4, 1, step_rows, A_W), lambda b, c, i: (b, 0, c, i, 0))
    prev = pl.BlockSpec((1, 4, 1, BAND4_HALO, A_W), lambda b, c, i: (b, 0, c, jnp.maximum(i * ratio - 1, 0), 0))
    nxt = pl.BlockSpec((1, 4, 1, BAND4_HALO, A_W), lambda b, c, i: (b, 0, c, jnp.minimum((i + 1) * ratio, nh - 1), 0))
    shape = (batch, 4, 4, m16, A_W)
    o, lse = pl.pallas_call(
        functools.partial(_band4_kernel, m16=m16, step_rows=step_rows),
        grid=(batch, 4, m16 // step_rows),
        in_specs=[cur, prev, cur, nxt, prev, cur, nxt, cur, cur],
        out_specs=[cur, cur],
        out_shape=[jax.ShapeDtypeStruct(shape, BF16), jax.ShapeDtypeStruct(shape, F32)],
        compiler_params=_params(("arbitrary",) * 3),
        name="band_d4",
    )(view(q16), view(k16), view(k16), view(k16), view(v16), view(v16), view(v16), view(o16), view(l16))
    return o.reshape(batch, DIL_CLASSES, m16, A_W), lse.reshape(batch, DIL_CLASSES, m16, A_W)


def _gqa_kernel(q_ref, k_ref, v_ref, o_ref, qs_ref, m_ref, acc_ref, s_ref, vt_ref, *, seq):
    for pr in range(2):
        qt = q_ref[0, :, pr * LANES:(pr + 1) * LANES].astype(F32).T
        qs_ref[:, (2 * pr) * GQA_TQ:(2 * pr + 1) * GQA_TQ] = qt[0:HEAD_DIM].astype(BF16)
        qs_ref[:, (2 * pr + 1) * GQA_TQ:(2 * pr + 2) * GQA_TQ] = qt[HEAD_DIM:2 * HEAD_DIM].astype(BF16)
    m_ref[...] = jnp.full(m_ref.shape, NEG, F32)
    acc_ref[...] = jnp.zeros(acc_ref.shape, F32)
    n_tiles = seq // GQA_TK

    @pl.when(pl.program_id(2) == 0)
    def _():
        ones_row = lax.broadcasted_iota(jnp.int32, (GQA_V_ROWS, GQA_TK), 0) == HEAD_DIM

        def transpose_tile(t, carry):
            off = pl.multiple_of(t * GQA_TK, GQA_TK)
            vt = v_ref[0, pl.ds(off, GQA_TK), :].astype(F32).T[0:GQA_V_ROWS]
            vt_ref[t] = jnp.where(ones_row, 1.0, vt).astype(BF16)
            return carry

        lax.fori_loop(0, n_tiles, transpose_tile, 0)

    chains = [slice(c * GQA_CHAIN, (c + 1) * GQA_CHAIN) for c in range(4 * GQA_TQ // GQA_CHAIN)]

    def scores(tile, buf, cols):
        off = pl.multiple_of(tile * GQA_TK, GQA_TK)
        k = k_ref[0, pl.ds(off, GQA_TK), 0:HEAD_DIM]
        s_ref[buf, :, cols] = jnp.dot(k, qs_ref[:, cols], preferred_element_type=F32)

    def softmax_pv(tile, buf, cols):
        s = s_ref[buf, :, cols]
        m_prev = m_ref[:, cols]
        m_new = jnp.maximum(m_prev, jnp.max(s, axis=0, keepdims=True))
        alpha = jnp.exp2(m_prev - m_new)
        p = jnp.exp2((s - m_new).astype(BF16))
        pv = jnp.dot(vt_ref[tile], p, preferred_element_type=F32)
        acc_ref[:, cols] = alpha * acc_ref[:, cols] + pv
        m_ref[:, cols] = m_new

    def step(tile, buf, prefetch):
        for cols in chains:
            if prefetch:
                scores(tile + 1, 1 - buf, cols)
            softmax_pv(tile, buf, cols)

    def pair(j, carry):
        step(2 * j, 0, True)
        step(2 * j + 1, 1, True)
        return carry

    for cols in chains:
        scores(0, 0, cols)
    lax.fori_loop(0, n_tiles // 2 - 1, pair, 0)
    step(n_tiles - 2, 0, True)
    step(n_tiles - 1, 1, False)
    acc = acc_ref[...]
    a = acc[0:HEAD_DIM] * (1.0 / acc[HEAD_DIM:HEAD_DIM + 1, :])
    for pr in range(2):
        two_heads = a[:, (2 * pr) * GQA_TQ:(2 * pr + 2) * GQA_TQ]
        stacked = jnp.concatenate([two_heads[:, :GQA_TQ], two_heads[:, GQA_TQ:]], axis=0)
        o_ref[0, :, pr * LANES:(pr + 1) * LANES] = stacked.T.astype(BF16)


def _gqa(qb, kd, vd, batch, seq):
    gw = B_QW // B_KV_HEADS
    cols = 4 * GQA_TQ
    out = pl.pallas_call(
        functools.partial(_gqa_kernel, seq=seq),
        grid=(batch, B_KV_HEADS, seq // GQA_TQ),
        in_specs=[pl.BlockSpec((1, GQA_TQ, gw), lambda b, g, qi: (b, qi, g)),
                  pl.BlockSpec((1, seq, LANES), lambda b, g, qi: (b, 0, g)),
                  pl.BlockSpec((1, seq, LANES), lambda b, g, qi: (b, 0, g))],
        out_specs=pl.BlockSpec((1, GQA_TQ, gw), lambda b, g, qi: (b, qi, g)),
        out_shape=jax.ShapeDtypeStruct((batch, seq, B_QW), BF16),
        scratch_shapes=[pltpu.VMEM((HEAD_DIM, cols), BF16), pltpu.VMEM((1, cols), F32),
                        pltpu.VMEM((GQA_V_ROWS, cols), F32), pltpu.VMEM((2, GQA_TK, cols), F32),
                        pltpu.VMEM((seq // GQA_TK, GQA_V_ROWS, GQA_TK), BF16)],
        compiler_params=_params(("arbitrary",) * 3, VMEM_LIMIT),
        name="gqa",
    )(qb.reshape(batch, seq, B_QW), kd.reshape(batch, seq, 2 * B_KVW), vd.reshape(batch, seq, 2 * B_KVW))
    return out.reshape(batch * seq, B_QW)


def _ab_out_kernel(o1_ref, l1_ref, om_ref, lm_ref, yb_ref, x_ref, w_ref, g_ref, b_ref, out_ref, os_ref, ls_ref):
    for c in range(DIL_CLASSES):
        for g in range(A_W // LANES):
            sl = slice(g * LANES, (g + 1) * LANES)
            rows = pl.ds(c, TM // DIL_CLASSES, stride=DIL_CLASSES)
            os_ref[g, rows, :] = om_ref[0, c, :, sl].astype(F32)
            ls_ref[g, rows, :] = lm_ref[0, c, :, sl]
    ya = []
    for g in range(A_W // LANES):
        sl = slice(g * LANES, (g + 1) * LANES)
        l1, lm = l1_ref[:, sl], ls_ref[g]
        mx = jnp.maximum(l1, lm)
        e1, em = jnp.exp(l1 - mx), jnp.exp(lm - mx)
        ya.append(((e1 * o1_ref[:, sl].astype(F32) + em * os_ref[g]) * (1.0 / (e1 + em))).astype(BF16))
    y = _row_block_dot(jnp.concatenate(ya + [yb_ref[...]], axis=1), w_ref[...])
    out_ref[...] = _layer_norm(ALPHA * x_ref[...] + y, g_ref[...], b_ref[...])


def _ab_out(o1, l1, om, lm, yb, x2d, w, g, b, seq):
    t = x2d.shape[0]
    nt = seq // TM
    row = lambda n: pl.BlockSpec((TM, n), lambda i: (i, 0))
    cls = pl.BlockSpec((1, DIL_CLASSES, TM // DIL_CLASSES, A_W), lambda i: (i // nt, 0, i % nt, 0))
    return pl.pallas_call(
        _ab_out_kernel,
        grid=(t // TM,),
        in_specs=[row(A_W), row(A_W), cls, cls, row(B_QW), row(D_MODEL), _const_spec((A_W + B_QW, D_MODEL)),
                  _const_spec((1, D_MODEL)), _const_spec((1, D_MODEL))],
        out_specs=row(D_MODEL),
        out_shape=jax.ShapeDtypeStruct((t, D_MODEL), F32),
        scratch_shapes=[pltpu.VMEM((A_W // LANES, TM, LANES), F32)] * 2,
        compiler_params=_params(("arbitrary",), VMEM_LIMIT),
        name="ab_out",
    )(o1, l1, om, lm, yb, x2d, w, g, b)


def _gelu(x):
    return 0.5 * x * (1.0 + jnp.tanh(0.7978845608028654 * (x + 0.044715 * (x * x * x))))


def _ffn_kernel(x_ref, xp_ref, xn_ref, wu_ref, cw_ref, cb_ref, wd_ref, g_ref, b_ref, out_ref, *, nt):
    i = pl.program_id(0)
    keep_prev = (i % nt != 0).astype(F32)
    keep_next = (i % nt != nt - 1).astype(F32)
    rows = lax.broadcasted_iota(jnp.int32, (TM, 1), 0)
    offsets = [sum(FF_CHUNKS[:c]) for c in range(len(FF_CHUNKS))]
    xb = x_ref[...].astype(BF16)
    halo = jnp.concatenate([xp_ref[...], xn_ref[...]], axis=0).astype(BF16)
    xe = jnp.concatenate([xb, halo], axis=0)
    acc = []

    def up(c):
        a, n = offsets[c], FF_CHUNKS[c]
        u = _row_block_dot(xb, wu_ref[:, a:a + n], FFN_ROWS)
        ge = _row_block_dot(xe, wu_ref[:, D_FF + a:D_FF + a + n], FFN_ROWS)
        return u, ge

    def finish(c, pending):
        u, ge = pending
        a, n = offsets[c], FF_CHUNKS[c]
        gm = ge[0:TM]
        g_before = ge[TM + 7:TM + 8] * keep_prev
        g_after = ge[TM + 8:TM + 9] * keep_next
        gp = jnp.where(rows == 0, g_before, pltpu.roll(gm, 1, 0))
        gn = jnp.where(rows == TM - 1, g_after, pltpu.roll(gm, TM - 1, 0))
        cw = cw_ref[:, a:a + n]
        gc = gp * cw[0:1] + gm * cw[1:2] + gn * cw[2:3] + cb_ref[:, a:a + n]
        act = (_gelu(gc) * u).astype(BF16)
        part = _row_block_dot(act, wd_ref[a:a + n, :], FFN_ROWS)
        acc[:] = [part if not acc else acc[0] + part]

    _pipelined([(functools.partial(up, c), functools.partial(finish, c)) for c in range(len(FF_CHUNKS))])
    out_ref[...] = _layer_norm(ALPHA * x_ref[...] + acc[0], g_ref[...], b_ref[...])


def _ffn(x2d, wu, cw, cb, wd, g, b, seq):
    t = x2d.shape[0]
    r8 = TM // 8
    return pl.pallas_call(
        functools.partial(_ffn_kernel, nt=seq // TM),
        grid=(t // TM,),
        in_specs=[pl.BlockSpec((TM, D_MODEL), lambda i: (i, 0)),
                  pl.BlockSpec((8, D_MODEL), lambda i: (jnp.maximum(i * r8 - 1, 0), 0)),
                  pl.BlockSpec((8, D_MODEL), lambda i: (jnp.minimum((i + 1) * r8, t // 8 - 1), 0)),
                  _const_spec((D_MODEL, 2 * D_FF), single=True),
                  _const_spec((3, D_FF)), _const_spec((1, D_FF)),
                  _const_spec((D_FF, D_MODEL), single=True),
                  _const_spec((1, D_MODEL)), _const_spec((1, D_MODEL))],
        out_specs=pl.BlockSpec((TM, D_MODEL), lambda i: (i, 0)),
        out_shape=jax.ShapeDtypeStruct((t, D_MODEL), F32),
        compiler_params=_params(("arbitrary",), VMEM_LIMIT),
        name="ffn",
    )(x2d, x2d, x2d, wu, cw, cb, wd, g, b)


def _lower_bound(tbl, layer):
    rows = [tbl[r:r + 1] for r in range(DEPTH)]
    mx = functools.reduce(jnp.maximum, rows)
    es = [jnp.exp(r - mx) for r in rows]
    inv = 1.0 / functools.reduce(lambda a, b: a + b, es)
    ps = [e * inv for e in es]
    return functools.reduce(lambda a, b: a + b, ps[:layer + 1]) - ps[0]


def _c_in_kernel(x_ref, w_ref, lbf_ref, lbb_ref, q_ref, lf_ref, lb_ref, v_ref, g_ref, *, layer):
    half = C_W // 2
    blocks = [slice(r * PROJ_ROWS, (r + 1) * PROJ_ROWS) for r in range(TM // PROJ_ROWS)]
    xbs = [x_ref[rows, :].astype(BF16) for rows in blocks]

    def proj(r, a, n):
        return jnp.dot(xbs[r], w_ref[:, a:a + n], preferred_element_type=F32)

    def silu(z):
        return z * _sigmoid(z)

    lbf = _lower_bound(lbf_ref[...], layer)
    lbb = _lower_bound(lbb_ref[...], layer)

    def store_q(z, rows, sl):
        q_ref[rows, sl] = z.astype(BF16)

    def store_logf(tbl, out_ref):
        def fn(z, rows, sl):
            lb = tbl[:, sl]
            out_ref[rows, sl] = jnp.log(lb + (1.0 - lb) * _sigmoid(z))
        return fn

    def store_silu(out_ref):
        def fn(z, rows, sl):
            out_ref[rows, sl] = silu(z).astype(BF16)
        return fn

    sections = []
    for r, rows in enumerate(blocks):
        for c in range(2):
            a = c * half
            sl = slice(a, a + half)
            for off, fn in ((0, store_q), (C_W, store_logf(lbf, lf_ref)), (2 * C_W, store_logf(lbb, lb_ref)),
                            (3 * C_W, store_silu(v_ref)), (3 * C_W + C_VW, store_silu(g_ref))):
                sections.append((functools.partial(proj, r, off + a, half), functools.partial(fn, rows=rows, sl=sl)))
    _pipelined(sections)


def _c_in(x2d, w, lbf, lbb, layer):
    t = x2d.shape[0]
    row = pl.BlockSpec((TM, C_W), lambda i: (i, 0))
    return pl.pallas_call(
        functools.partial(_c_in_kernel, layer=layer),
        grid=(t // TM,),
        in_specs=[row, _const_spec(w.shape), _const_spec((DEPTH, C_W)), _const_spec((DEPTH, C_W))],
        out_specs=[row] * 5,
        out_shape=[jax.ShapeDtypeStruct((t, C_W), d) for d in (BF16, F32, F32, BF16, BF16)],
        compiler_params=_params(("arbitrary",), VMEM_LIMIT),
        name="c_in",
    )(x2d, w, lbf, lbb)


def _hgrn_kernel(*refs, rev, fused, heads, step_rows):
    if fused:
        (q_ref, lf_ref, v_ref, of_ref, gate_ref, gn_ref, x_ref, w_ref, lg_ref, lb_ref,
         out_ref, st_ref, y_ref) = refs
    else:
        q_ref, lf_ref, v_ref, y_ref, st_ref = refs

    @pl.when(pl.program_id(2) == 0)
    def _():
        st_ref[...] = jnp.zeros(st_ref.shape, F32)

    ri = lax.broadcasted_iota(jnp.int32, (HG_CHUNK, HG_CHUNK), 0)
    ci = lax.broadcasted_iota(jnp.int32, (HG_CHUNK, HG_CHUNK), 1)
    tri = (ci >= ri) if rev else (ci <= ri)
    trib = jnp.where(tri, 1.0, 0.0).astype(BF16)
    dotf = lambda a, b: jnp.dot(a, b, preferred_element_type=F32)
    n_chunks = step_rows // HG_CHUNK
    order = range(n_chunks - 1, -1, -1) if rev else range(n_chunks)
    units = [(c, h) for c in order for h in range(heads)]
    window = lambda u: (0, slice(u[0] * HG_CHUNK, (u[0] + 1) * HG_CHUNK), slice(u[1] * C_KEY, (u[1] + 1) * C_KEY))

    b_all = {}
    for u in units:
        lf = lf_ref[window(u)]
        hi = lf.astype(BF16)
        lo = (lf - hi.astype(F32)).astype(BF16)
        b_all[u] = dotf(trib, hi) + dotf(trib, lo)

    qdec, kdec, etot, a_rows = {}, {}, {}, {}
    for u in units:
        lf, b = lf_ref[window(u)], b_all[u]
        qf = q_ref[window(u)].astype(F32)
        bex = b - lf
        kk = 1.0 - jnp.exp(lf)
        btot = b[0:1] if rev else b[HG_CHUNK - 1:HG_CHUNK]
        qdec[u] = (qf * jnp.exp(b)).astype(BF16)
        kdec[u] = (kk * jnp.exp(btot - b)).astype(BF16)
        etot[u] = jnp.exp(btot)
        rows = []
        for blk in range(HG_CHUNK // HG_SUB):
            r0 = blk * HG_SUB
            ref = bex[r0 + HG_SUB - 1:r0 + HG_SUB] if rev else bex[r0:r0 + 1]
            qt = (qf[r0:r0 + HG_SUB] * jnp.exp(b[r0:r0 + HG_SUB] - ref)).astype(BF16)
            lo_r, hi_r = (r0, HG_CHUNK) if rev else (0, r0 + HG_SUB)
            ks = (kk[lo_r:hi_r] * jnp.exp(ref - b[lo_r:hi_r])).astype(BF16)
            pieces = []
            if lo_r > 0:
                pieces.append(jnp.zeros((lo_r, C_KEY), BF16))
            pieces.append(ks)
            if hi_r < HG_CHUNK:
                pieces.append(jnp.zeros((HG_CHUNK - hi_r, C_KEY), BF16))
            kfull = jnp.concatenate(pieces, axis=0) if len(pieces) > 1 else ks
            rows.append(lax.dot_general(qt, kfull, _NT, preferred_element_type=F32))
        a_rows[u] = rows
    attn = {u: jnp.where(tri, jnp.concatenate(a_rows[u], axis=0), 0.0).astype(BF16) for u in units}

    states = [st_ref[h] for h in range(heads)]
    for u in units:
        h = u[1]
        vb = v_ref[window(u)]
        o = lax.dot_general(qdec[u], states[h].astype(BF16), _NT, preferred_element_type=F32)
        o = o + dotf(attn[u], vb)
        states[h] = states[h] * etot[u] + lax.dot_general(vb, kdec[u], _TN, preferred_element_type=F32)
        if fused:
            tot = of_ref[window(u)] + o
            inv = lax.rsqrt(jnp.mean(tot * tot, -1, keepdims=True) + RMS_EPS)
            y = tot * inv * gn_ref[:, window(u)[2]] * gate_ref[window(u)].astype(F32)
            y_ref[window(u)[1:]] = y.astype(BF16)
        else:
            y_ref[window(u)] = o
    for h in range(heads):
        st_ref[h] = states[h]
    if fused:
        proj = _row_block_dot(y_ref[...], w_ref[...])
        out_ref[0] = _layer_norm(ALPHA * x_ref[0] + proj, lg_ref[...], lb_ref[...])


def _hgrn_specs(batch, seq, rev, heads, step_rows):
    nb = seq // step_rows
    blk = (lambda b, h, i: (b, nb - 1 - i, h)) if rev else (lambda b, h, i: (b, i, h))
    return pl.BlockSpec((1, step_rows, heads * C_KEY), blk), (batch, C_HEADS // heads, nb)


def _hgrn_fwd(q, lf, v, batch, seq):
    tile, grid = _hgrn_specs(batch, seq, False, HG_HP, HG_TB)
    view = lambda t: t.reshape(batch, seq, C_W)
    out = pl.pallas_call(
        functools.partial(_hgrn_kernel, rev=False, fused=False, heads=HG_HP, step_rows=HG_TB),
        grid=grid,
        in_specs=[tile, tile, tile],
        out_specs=tile,
        out_shape=jax.ShapeDtypeStruct((batch, seq, C_VW), F32),
        scratch_shapes=[pltpu.VMEM((HG_HP, C_VAL, C_KEY), F32)],
        compiler_params=_params(("arbitrary",) * 3),
        name="hgrn_fwd",
    )(view(q), view(lf), view(v))
    return out.reshape(batch * seq, C_VW)


def _hgrn_bwd_out(q, lf, v, o_f, gate, gn, x2d, w, g, b, batch, seq):
    tile, grid = _hgrn_specs(batch, seq, True, C_HEADS, HG_TB_OUT)
    view = lambda t: t.reshape(batch, seq, C_W)
    out = pl.pallas_call(
        functools.partial(_hgrn_kernel, rev=True, fused=True, heads=C_HEADS, step_rows=HG_TB_OUT),
        grid=grid,
        in_specs=[tile] * 5 + [_const_spec((1, C_VW)), tile, _const_spec((C_VW, D_MODEL), single=True),
                               _const_spec((1, D_MODEL)), _const_spec((1, D_MODEL))],
        out_specs=tile,
        out_shape=jax.ShapeDtypeStruct((batch, seq, D_MODEL), F32),
        scratch_shapes=[pltpu.VMEM((C_HEADS, C_VAL, C_KEY), F32), pltpu.VMEM((HG_TB_OUT, C_VW), BF16)],
        compiler_params=_params(("arbitrary",) * 3, VMEM_LIMIT),
        name="hgrn_bwd_out",
    )(view(q), view(lf), view(v), view(o_f), view(gate), gn, x2d.reshape(batch, seq, D_MODEL), w, g, b)
    return out.reshape(batch * seq, D_MODEL)


def _prep_weights(w_in_ab, w_out_ab, qn_ab, kn_ab, w_in_c, w_out_c, gn_c, ffn_w_up, ffn_w_down):
    scale = HEAD_DIM ** -0.5
    ab = []
    for j in range(w_in_ab.shape[0]):
        w = w_in_ab[j]
        o3 = 3 * A_W
        kb = w[:, o3 + B_QW:o3 + B_QW + B_KVW]
        vb = w[:, o3 + B_QW + B_KVW:]
        dup = lambda t: jnp.concatenate([t[:, :HEAD_DIM], t[:, :HEAD_DIM], t[:, HEAD_DIM:], t[:, HEAD_DIM:]], 1)
        w_ext = jnp.concatenate([w[:, :A_W] * scale, w[:, A_W:o3 + B_QW], dup(kb), dup(vb)], 1).astype(BF16)
        qg = (jnp.concatenate([qn_ab[j], qn_ab[j]]) * (scale * LOG2E)).reshape(1, LANES)
        kg = jnp.concatenate([kn_ab[j], kn_ab[j]]).reshape(1, LANES)
        ab.append((w_ext, qg, kg, w_out_ab[j].astype(BF16)))
    cc = [(w_in_c[j].astype(BF16), w_out_c[j].astype(BF16), gn_c[j].reshape(1, C_VW)) for j in range(w_in_c.shape[0])]
    return ab, cc, ffn_w_up.astype(BF16), ffn_w_down.astype(BF16)


def _trunk(x, prep, lb_fwd, lb_bwd, ln_mix_g, ln_mix_b, ln_ffn_g, ln_ffn_b, ffn_conv_w, ffn_conv_b):
    ab, cc, wu, wd = prep
    batch, seq, _ = x.shape
    x2d = x.reshape(batch * seq, D_MODEL)
    tabs = _rope_tables(seq)
    blk = jnp.arange(LANES) // HEAD_DIM
    ones2 = (blk[:, None] == blk[None, :]).astype(BF16)
    vec = lambda t: t.reshape(1, -1)
    for l in range(DEPTH):
        j = l // 2
        if l % 2 == 0:
            w_ext, qg, kg, w_out = ab[j]
            assert [d for _, d in A_PATTERNS] == [1, 4, DIL_CLASSES]
            assert all(window // (2 * d) == HALF_WINDOW for window, d in A_PATTERNS)
            qa, ka, va, qb, kd, vd, qa16, ka16, va16 = _ab_in(x2d, w_ext, tabs, qg, kg, ones2, batch, seq)
            o1, l1 = _band1(qa, ka, va, batch, seq)
            o16, l16 = _band16(qa16, ka16, va16, batch, seq)
            om, lm = _band4(qa16, ka16, va16, o16, l16, batch, seq)
            yb = _gqa(qb, kd, vd, batch, seq)
            x2d = _ab_out(o1, l1, om, lm, yb, x2d, w_out, vec(ln_mix_g[l]), vec(ln_mix_b[l]), seq)
        else:
            w_in, w_out, gn = cc[j]
            q, lf, lb, v, gate = _c_in(x2d, w_in, lb_fwd, lb_bwd, l)
            o_f = _hgrn_fwd(q, lf, v, batch, seq)
            x2d = _hgrn_bwd_out(q, lb, v, o_f, gate, gn, x2d, w_out, vec(ln_mix_g[l]), vec(ln_mix_b[l]), batch, seq)
        x2d = _ffn(x2d, wu[l], ffn_conv_w[l], vec(ffn_conv_b[l]), wd[l], vec(ln_ffn_g[l]), vec(ln_ffn_b[l]), seq)
    return x2d.reshape(batch, seq, D_MODEL)


def kernel(x_prompt, x_sample, w_in_ab, w_out_ab, qn_ab, kn_ab, w_in_c, w_out_c, lb_fwd, lb_bwd, gn_c, ln_mix_g, ln_mix_b, ln_ffn_g, ln_ffn_b, ffn_w_up, ffn_conv_w, ffn_conv_b, ffn_w_down):
    prep = _prep_weights(w_in_ab, w_out_ab, qn_ab, kn_ab, w_in_c, w_out_c, gn_c, ffn_w_up, ffn_w_down)
    rest = (lb_fwd, lb_bwd, ln_mix_g, ln_mix_b, ln_ffn_g, ln_ffn_b, ffn_conv_w, ffn_conv_b)
    return (_trunk(x_prompt, prep, *rest), _trunk(x_sample, prep, *rest))
```

```python
import functools

import jax
import jax.numpy as jnp
from jax import lax
from jax.experimental import pallas as pl
from jax.experimental.pallas import tpu as pltpu

F32 = jnp.float32
BF16 = jnp.bfloat16

D_MODEL = 1024
DEPTH = 2
HEAD_DIM = 64
A_HEADS = 8
A_PATTERNS = ((128, 1), (512, 4), (2048, 16))
B_HEADS = 8
B_KV_HEADS = 2
GRID_W = 64
ROPE_THETA = 500000.0
ROPE_DIM = HEAD_DIM // 4
AXIAL_THETA = 10000.0
C_HEADS = 8
C_KEY = 128
C_VAL = 128
D_FF = 2816
ALPHA = (2 * DEPTH) ** 0.25
LN_EPS = 1e-5
RMS_EPS = 1e-6
A_W = A_HEADS * HEAD_DIM
B_QW = B_HEADS * HEAD_DIM
B_KVW = B_KV_HEADS * HEAD_DIM
C_W = C_HEADS * C_KEY
C_VW = C_HEADS * C_VAL

LANES = 128
HALF_WINDOW = 64
TM = 512
PROJ_ROWS = 128
FFN_ROWS = 256
BAND_L = 128
BAND_STEP = 512
BAND4_STEP = 128
DIL_CLASSES = 16
BAND4_ROWS = 32
BAND4_HALO = 16
GQA_TQ = 2048
GQA_TK = 512
GQA_CHAIN = 512
GQA_V_ROWS = 80
FF_CHUNKS = (768, 768, 768, 512)
HG_CHUNK = 64
HG_SUB = 16
HG_TB = 512
HG_HP = 4
HG_TB_OUT = 256
NEG = -1e30
LOG2E = 1.4426950408889634
VMEM_LIMIT = 56 * 1024 * 1024

_NT = (((1,), (1,)), ((), ()))
_TN = (((0,), (0,)), ((), ()))


def _params(sem, vmem=None):
    return pltpu.CompilerParams(dimension_semantics=sem, vmem_limit_bytes=vmem)


def _const_spec(shape, single=False):
    nd = len(shape)
    if single:
        return pl.BlockSpec(shape, lambda *_: (0,) * nd, pipeline_mode=pl.Buffered(1))
    return pl.BlockSpec(shape, lambda *_: (0,) * nd)


def _layer_norm(z, g, b):
    mu = jnp.mean(z, -1, keepdims=True)
    d = z - mu
    var = jnp.mean(d * d, -1, keepdims=True)
    return d * lax.rsqrt(var + LN_EPS) * g + b


def _sigmoid(z):
    return 1.0 / (1.0 + jnp.exp(-z))


def _row_block_dot(x, w, rows=PROJ_ROWS):
    m = x.shape[0]
    starts = list(range(0, m - m % rows, rows)) or [0]
    ends = starts[1:] + [m]
    parts = [jnp.dot(x[a:b], w, preferred_element_type=F32) for a, b in zip(starts, ends)]
    return parts[0] if len(parts) == 1 else jnp.concatenate(parts, axis=0)


def _pipelined(sections):
    pending = sections[0][0]()
    for i, (_, epilogue) in enumerate(sections):
        upcoming = sections[i + 1][0]() if i + 1 < len(sections) else None
        epilogue(pending)
        pending = upcoming


def _rope_tables(seq):
    pos = jnp.arange(seq, dtype=F32)
    d = jnp.arange(LANES) % HEAD_DIM

    def angles(p, dim, theta):
        freqs = theta ** (-(jnp.arange(0, dim, 2, dtype=F32) / dim))
        return p[:, None] * freqs[None, :]

    h = ROPE_DIM // 2
    ang = angles(pos, ROPE_DIM, ROPE_THETA)
    a = ang[:, d % h]
    lo, hi = d < h, (d >= h) & (d < ROPE_DIM)
    pc = jnp.where(lo | hi, jnp.cos(a), 1.0)
    psa = jnp.where(lo, -jnp.sin(a), 0.0)
    psb = jnp.where(hi, jnp.sin(a), 0.0)

    q = HEAD_DIM // 4
    row = jnp.floor(pos / GRID_W)
    col = pos - row * GRID_W
    ar = angles(row, HEAD_DIM // 2, AXIAL_THETA)[:, d % q]
    ac_ = angles(col, HEAD_DIM // 2, AXIAL_THETA)[:, d % q]
    a2 = jnp.where(d < HEAD_DIM // 2, ar, ac_)
    first = (d % (HEAD_DIM // 2)) < q
    ac = jnp.cos(a2)
    asa = jnp.where(first, -jnp.sin(a2), 0.0)
    asb = jnp.where(first, 0.0, jnp.sin(a2))
    return [t.astype(F32) for t in (pc, psa, psb, ac, asa, asb)]


def _rope(seg, c, sa, sb, shift):
    return seg * c + pltpu.roll(seg, LANES - shift, 1) * sa + pltpu.roll(seg, shift, 1) * sb


def _ab_in_kernel(x_ref, w_ref, pc_ref, psa_ref, psb_ref, ac_ref, asa_ref, asb_ref, qg_ref, kg_ref,
                  ones_ref, qa_ref, ka_ref, va_ref, qb_ref, kd_ref, vd_ref, qa16_ref, ka16_ref, va16_ref,
                  qs_ref, ks_ref, vs_ref):
    xb = x_ref[...].astype(BF16)

    def proj(a, n):
        return jnp.dot(xb, w_ref[:, a:a + n], preferred_element_type=F32)

    pc, psa, psb = pc_ref[...], psa_ref[...], psb_ref[...]
    ac, asa, asb = ac_ref[...], asa_ref[...], asb_ref[...]

    def dilated_a(h, rope, out_ref, out16_ref, slab_ref):
        for g in range(A_W // LANES):
            sl = slice(g * LANES, (g + 1) * LANES)
            y = _rope(h[:, sl], pc, psa, psb, ROPE_DIM // 2) if rope else h[:, sl]
            out_ref[:, sl] = y.astype(BF16)
            slab_ref[g] = y
        for c in range(DIL_CLASSES):
            for g in range(A_W // LANES):
                rows = slab_ref[g, pl.ds(c, TM // DIL_CLASSES, stride=DIL_CLASSES), :]
                out16_ref[0, c, :, g * LANES:(g + 1) * LANES] = rows.astype(BF16)

    def norm_rope(h, gain, out_ref):
        for g in range(h.shape[1] // LANES):
            sl = slice(g * LANES, (g + 1) * LANES)
            seg = h[:, sl]
            sq = seg * seg
            hi = sq.astype(BF16)
            lo = (sq - hi.astype(F32)).astype(BF16)
            ss = (jnp.dot(hi, ones_ref[...], preferred_element_type=F32)
                  + jnp.dot(lo, ones_ref[...], preferred_element_type=F32))
            y = seg * lax.rsqrt(ss * (1.0 / HEAD_DIM) + RMS_EPS) * gain
            out_ref[:, sl] = _rope(y, ac, asa, asb, HEAD_DIM // 4).astype(BF16)

    def cast_to(out_ref):
        def fn(h):
            out_ref[...] = h.astype(BF16)
        return fn

    o_qb = 3 * A_W
    o_kd = o_qb + B_QW
    o_vd = o_kd + 2 * B_KVW
    _pipelined([
        (functools.partial(proj, 0, A_W),
         functools.partial(dilated_a, rope=True, out_ref=qa_ref, out16_ref=qa16_ref, slab_ref=qs_ref)),
        (functools.partial(proj, A_W, A_W),
         functools.partial(dilated_a, rope=True, out_ref=ka_ref, out16_ref=ka16_ref, slab_ref=ks_ref)),
        (functools.partial(proj, 2 * A_W, A_W),
         functools.partial(dilated_a, rope=False, out_ref=va_ref, out16_ref=va16_ref, slab_ref=vs_ref)),
        (functools.partial(proj, o_qb, B_QW), functools.partial(norm_rope, gain=qg_ref[...], out_ref=qb_ref)),
        (functools.partial(proj, o_kd, 2 * B_KVW), functools.partial(norm_rope, gain=kg_ref[...], out_ref=kd_ref)),
        (functools.partial(proj, o_vd, 2 * B_KVW), cast_to(vd_ref)),
    ])


def _ab_in(x2d, w_ext, tabs, qg, kg, ones2, batch, seq):
    t = x2d.shape[0]
    nt = seq // TM
    tab = pl.BlockSpec((TM, LANES), lambda i: (i % nt, 0))
    row = lambda n: pl.BlockSpec((TM, n), lambda i: (i, 0))
    cls = pl.BlockSpec((1, DIL_CLASSES, TM // DIL_CLASSES, A_W), lambda i: (i // nt, 0, i % nt, 0))
    cls_shape = jax.ShapeDtypeStruct((batch, DIL_CLASSES, seq // DIL_CLASSES, A_W), BF16)
    wn = w_ext.shape[1]
    return pl.pallas_call(
        _ab_in_kernel,
        grid=(t // TM,),
        in_specs=[row(D_MODEL), _const_spec((D_MODEL, wn))] + [tab] * 6
        + [_const_spec((1, LANES)), _const_spec((1, LANES)), _const_spec((LANES, LANES))],
        out_specs=[row(A_W)] * 3 + [row(B_QW), row(2 * B_KVW), row(2 * B_KVW)] + [cls] * 3,
        out_shape=[jax.ShapeDtypeStruct((t, A_W), BF16)] * 3
        + [jax.ShapeDtypeStruct((t, B_QW), BF16)]
        + [jax.ShapeDtypeStruct((t, 2 * B_KVW), BF16)] * 2 + [cls_shape] * 3,
        scratch_shapes=[pltpu.VMEM((A_W // LANES, TM, LANES), F32)] * 3,
        compiler_params=_params(("arbitrary",), VMEM_LIMIT),
        name="ab_in",
    )(x2d, w_ext, *tabs, qg, kg, ones2)


def _band_groups(qs, ks, vs, bias, left):
    chains = [(g, sel) for g in range(len(qs)) for sel in (left, jnp.logical_not(left))]
    scores = []
    for g, sel in chains:
        qm = jnp.where(sel, qs[g], jnp.zeros_like(qs[g]))
        scores.append(lax.dot_general(qm, ks[g], _NT, preferred_element_type=F32) + bias)
    probs = []
    for s in scores:
        mx = jnp.max(s, axis=1, keepdims=True)
        pe = jnp.exp(s - mx)
        probs.append((mx, jnp.sum(pe, axis=1, keepdims=True), pe.astype(BF16)))
    outs = []
    for (g, _), (mx, l, pe) in zip(chains, probs):
        o = jnp.dot(pe, vs[g], preferred_element_type=F32)
        outs.append((o * (1.0 / l), mx + jnp.log(l)))
    return [(jnp.where(left, outs[2 * g][0], outs[2 * g + 1][0]), jnp.where(left, outs[2 * g][1], outs[2 * g + 1][1]))
            for g in range(len(qs))]


def _band_kernel(q_ref, kp_ref, kc_ref, kn_ref, vp_ref, vc_ref, vn_ref, o_ref, lse_ref, *, m, step_rows):
    nk = BAND_L + 2 * HALF_WINDOW
    base = pl.program_id(len(q_ref.shape) - 2) * step_rows
    r = lax.broadcasted_iota(jnp.int32, (BAND_L, nk), 0)
    j = lax.broadcasted_iota(jnp.int32, (BAND_L, nk), 1)
    band = jnp.where(jnp.abs(j - HALF_WINDOW - r) <= HALF_WINDOW, 0.0, NEG)
    left = lax.broadcasted_iota(jnp.int32, (BAND_L, LANES), 1) < HEAD_DIM
    pre = (0,) * (len(q_ref.shape) - 2)
    lanes = [slice(p * LANES, (p + 1) * LANES) for p in range(A_W // LANES)]
    at = lambda rows, sl: pre + (rows, sl)
    whole = slice(None)
    ks = [jnp.concatenate([kp_ref[at(whole, sl)], kc_ref[at(whole, sl)], kn_ref[at(whole, sl)]], axis=0) for sl in lanes]
    vs = [jnp.concatenate([vp_ref[at(whole, sl)], vc_ref[at(whole, sl)], vn_ref[at(whole, sl)]], axis=0) for sl in lanes]
    for blk in range(step_rows // BAND_L):
        r0 = blk * BAND_L
        rows = slice(r0, r0 + BAND_L)
        kpos = base + r0 - HALF_WINDOW + j
        bias = jnp.where(kpos >= 0, band, NEG)
        bias = jnp.where(kpos < m, bias, NEG)
        res = _band_groups([q_ref[at(rows, sl)] for sl in lanes], [kk[r0:r0 + nk] for kk in ks],
                           [vv[r0:r0 + nk] for vv in vs], bias, left)
        for sl, (o, lse) in zip(lanes, res):
            o_ref[at(rows, sl)] = o.astype(BF16)
            lse_ref[at(rows, sl)] = lse


def _band_specs(lead, rows, step_rows):
    nh = rows // HALF_WINDOW
    ratio = step_rows // HALF_WINDOW
    ones = (1,) * lead
    cur = pl.BlockSpec(ones + (step_rows, A_W), lambda *g: g[:lead] + (g[-1], 0))
    prev = pl.BlockSpec(ones + (HALF_WINDOW, A_W), lambda *g: g[:lead] + (jnp.maximum(g[-1] * ratio - 1, 0), 0))
    nxt = pl.BlockSpec(ones + (HALF_WINDOW, A_W), lambda *g: g[:lead] + (jnp.minimum((g[-1] + 1) * ratio, nh - 1), 0))
    return cur, prev, nxt


def _band1(q, k, v, batch, seq):
    step_rows = min(BAND_STEP, seq)
    cur, prev, nxt = _band_specs(1, seq, step_rows)
    view = lambda t: t.reshape(batch, seq, A_W)
    o, lse = pl.pallas_call(
        functools.partial(_band_kernel, m=seq, step_rows=step_rows),
        grid=(batch, seq // step_rows),
        in_specs=[cur, prev, cur, nxt, prev, cur, nxt],
        out_specs=[cur, cur],
        out_shape=[jax.ShapeDtypeStruct((batch, seq, A_W), BF16), jax.ShapeDtypeStruct((batch, seq, A_W), F32)],
        compiler_params=_params(("arbitrary",) * 2),
        name="band_d1",
    )(view(q), view(k), view(k), view(k), view(v), view(v), view(v))
    return o.reshape(batch * seq, A_W), lse.reshape(batch * seq, A_W)


def _band16(q16, k16, v16, batch, seq):
    m = seq // DIL_CLASSES
    step_rows = min(BAND_STEP, m)
    cur, prev, nxt = _band_specs(2, m, step_rows)
    shape = (batch, DIL_CLASSES, m, A_W)
    return pl.pallas_call(
        functools.partial(_band_kernel, m=m, step_rows=step_rows),
        grid=(batch, DIL_CLASSES, m // step_rows),
        in_specs=[cur, prev, cur, nxt, prev, cur, nxt],
        out_specs=[cur, cur],
        out_shape=[jax.ShapeDtypeStruct(shape, BF16), jax.ShapeDtypeStruct(shape, F32)],
        compiler_params=_params(("arbitrary",) * 3),
        name="band_d16",
    )(q16, k16, k16, k16, v16, v16, v16)


def _band4_kernel(q_ref, kp_ref, kc_ref, kn_ref, vp_ref, vc_ref, vn_ref, o16_ref, l16_ref, o_ref, lse_ref, *,
                  m16, step_rows):
    qr = BAND4_ROWS
    kr = qr + 2 * BAND4_HALO
    i = pl.program_id(2)
    shift_q, shift_k = qr.bit_length() - 1, kr.bit_length() - 1
    row = lax.broadcasted_iota(jnp.int32, (4 * qr, 4 * kr), 0)
    col = lax.broadcasted_iota(jnp.int32, (4 * qr, 4 * kr), 1)
    uq, rq = row >> shift_q, row & (qr - 1)
    uk, jk = col >> shift_k, col & (kr - 1)
    band = jnp.where(jnp.abs(4 * (jk - BAND4_HALO - rq) + (uk - uq)) <= HALF_WINDOW, 0.0, NEG)
    left = lax.broadcasted_iota(jnp.int32, (4 * qr, LANES), 1) < HEAD_DIM
    slabs = [slice(p * LANES, (p + 1) * LANES) for p in range(A_W // LANES)]
    window = lambda refs, u, sl: jnp.concatenate([r[0, u, 0, :, sl] for r in refs], axis=0)
    kwin = [[window([kp_ref, kc_ref, kn_ref], u, sl) for u in range(4)] for sl in slabs]
    vwin = [[window([vp_ref, vc_ref, vn_ref], u, sl) for u in range(4)] for sl in slabs]
    for blk in range(step_rows // qr):
        r0 = blk * qr
        rows = slice(r0, r0 + qr)
        kpos = i * step_rows + r0 - BAND4_HALO + jk
        bias = jnp.where(kpos >= 0, band, NEG)
        bias = jnp.where(kpos < m16, bias, NEG)
        classes = lambda ref, sl: jnp.concatenate([ref[0, u, 0, rows, sl] for u in range(4)], axis=0)
        branch4 = _band_groups([classes(q_ref, sl) for sl in slabs],
                               [jnp.concatenate([w[r0:r0 + kr] for w in kw], axis=0) for kw in kwin],
                               [jnp.concatenate([w[r0:r0 + kr] for w in vw], axis=0) for vw in vwin], bias, left)
        for sl, (o4, l4) in zip(slabs, branch4):
            o16 = classes(o16_ref, sl).astype(F32)
            l16 = classes(l16_ref, sl)
            mx = jnp.maximum(l4, l16)
            w4, w16 = jnp.exp(l4 - mx), jnp.exp(l16 - mx)
            tot = w4 + w16
            om = (w4 * o4 + w16 * o16) * (1.0 / tot)
            lm = mx + jnp.log(tot)
            for u in range(4):
                o_ref[0, u, 0, rows, sl] = om[u * qr:(u + 1) * qr].astype(BF16)
                lse_ref[0, u, 0, rows, sl] = lm[u * qr:(u + 1) * qr]


def _band4(q16, k16, v16, o16, l16, batch, seq):
    m16 = seq // DIL_CLASSES
    step_rows = min(BAND4_STEP, m16)
    ratio = step_rows // BAND4_HALO
    nh = m16 // BAND4_HALO
    view = lambda t: t.reshape(batch, 4, 4, m16, A_W)
    cur = pl.BlockSpec((1, 4, 1, step_rows, A_W), lambda b, c, i: (b, 0, c, i, 0))
    prev = pl.BlockSpec((1, 4, 1, BAND4_HALO, A_W), lambda b, c, i: (b, 0, c, jnp.maximum(i * ratio - 1, 0), 0))
    nxt = pl.BlockSpec((1, 4, 1, BAND4_HALO, A_W), lambda b, c, i: (b, 0, c, jnp.minimum((i + 1) * ratio, nh - 1), 0))
    shape = (batch, 4, 4, m16, A_W)
    o, lse = pl.pallas_call(
        functools.partial(_band4_kernel, m16=m16, step_rows=step_rows),
        grid=(batch, 4, m16 // step_rows),
        in_specs=[cur, prev, cur, nxt, prev, cur, nxt, cur, cur],
        out_specs=[cur, cur],
        out_shape=[jax.ShapeDtypeStruct(shape, BF16), jax.ShapeDtypeStruct(shape, F32)],
        compiler_params=_params(("arbitrary",) * 3),
        name="band_d4",
    )(view(q16), view(k16), view(k16), view(k16), view(v16), view(v16), view(v16), view(o16), view(l16))
    return o.reshape(batch, DIL_CLASSES, m16, A_W), lse.reshape(batch, DIL_CLASSES, m16, A_W)


def _gqa_kernel(q_ref, k_ref, v_ref, o_ref, qs_ref, m_ref, acc_ref, s_ref, vt_ref, *, seq):
    for pr in range(2):
        qt = q_ref[0, :, pr * LANES:(pr + 1) * LANES].astype(F32).T
        qs_ref[:, (2 * pr) * GQA_TQ:(2 * pr + 1) * GQA_TQ] = qt[0:HEAD_DIM].astype(BF16)
        qs_ref[:, (2 * pr + 1) * GQA_TQ:(2 * pr + 2) * GQA_TQ] = qt[HEAD_DIM:2 * HEAD_DIM].astype(BF16)
    m_ref[...] = jnp.full(m_ref.shape, NEG, F32)
    acc_ref[...] = jnp.zeros(acc_ref.shape, F32)
    n_tiles = seq // GQA_TK

    @pl.when(pl.program_id(2) == 0)
    def _():
        ones_row = lax.broadcasted_iota(jnp.int32, (GQA_V_ROWS, GQA_TK), 0) == HEAD_DIM

        def transpose_tile(t, carry):
            off = pl.multiple_of(t * GQA_TK, GQA_TK)
            vt = v_ref[0, pl.ds(off, GQA_TK), :].astype(F32).T[0:GQA_V_ROWS]
            vt_ref[t] = jnp.where(ones_row, 1.0, vt).astype(BF16)
            return carry

        lax.fori_loop(0, n_tiles, transpose_tile, 0)

    chains = [slice(c * GQA_CHAIN, (c + 1) * GQA_CHAIN) for c in range(4 * GQA_TQ // GQA_CHAIN)]

    def scores(tile, buf, cols):
        off = pl.multiple_of(tile * GQA_TK, GQA_TK)
        k = k_ref[0, pl.ds(off, GQA_TK), 0:HEAD_DIM]
        s_ref[buf, :, cols] = jnp.dot(k, qs_ref[:, cols], preferred_element_type=F32)

    def softmax_pv(tile, buf, cols):
        s = s_ref[buf, :, cols]
        m_prev = m_ref[:, cols]
        m_new = jnp.maximum(m_prev, jnp.max(s, axis=0, keepdims=True))
        alpha = jnp.exp2(m_prev - m_new)
        p = jnp.exp2((s - m_new).astype(BF16))
        pv = jnp.dot(vt_ref[tile], p, preferred_element_type=F32)
        acc_ref[:, cols] = alpha * acc_ref[:, cols] + pv
        m_ref[:, cols] = m_new

    def step(tile, buf, prefetch):
        for cols in chains:
            if prefetch:
                scores(tile + 1, 1 - buf, cols)
            softmax_pv(tile, buf, cols)

    def pair(j, carry):
        step(2 * j, 0, True)
        step(2 * j + 1, 1, True)
        return carry

    for cols in chains:
        scores(0, 0, cols)
    lax.fori_loop(0, n_tiles // 2 - 1, pair, 0)
    step(n_tiles - 2, 0, True)
    step(n_tiles - 1, 1, False)
    acc = acc_ref[...]
    a = acc[0:HEAD_DIM] * (1.0 / acc[HEAD_DIM:HEAD_DIM + 1, :])
    for pr in range(2):
        two_heads = a[:, (2 * pr) * GQA_TQ:(2 * pr + 2) * GQA_TQ]
        stacked = jnp.concatenate([two_heads[:, :GQA_TQ], two_heads[:, GQA_TQ:]], axis=0)
        o_ref[0, :, pr * LANES:(pr + 1) * LANES] = stacked.T.astype(BF16)


def _gqa(qb, kd, vd, batch, seq):
    gw = B_QW // B_KV_HEADS
    cols = 4 * GQA_TQ
    out = pl.pallas_call(
        functools.partial(_gqa_kernel, seq=seq),
        grid=(batch, B_KV_HEADS, seq // GQA_TQ),
        in_specs=[pl.BlockSpec((1, GQA_TQ, gw), lambda b, g, qi: (b, qi, g)),
                  pl.BlockSpec((1, seq, LANES), lambda b, g, qi: (b, 0, g)),
                  pl.BlockSpec((1, seq, LANES), lambda b, g, qi: (b, 0, g))],
        out_specs=pl.BlockSpec((1, GQA_TQ, gw), lambda b, g, qi: (b, qi, g)),
        out_shape=jax.ShapeDtypeStruct((batch, seq, B_QW), BF16),
        scratch_shapes=[pltpu.VMEM((HEAD_DIM, cols), BF16), pltpu.VMEM((1, cols), F32),
                        pltpu.VMEM((GQA_V_ROWS, cols), F32), pltpu.VMEM((2, GQA_TK, cols), F32),
                        pltpu.VMEM((seq // GQA_TK, GQA_V_ROWS, GQA_TK), BF16)],
        compiler_params=_params(("arbitrary",) * 3, VMEM_LIMIT),
        name="gqa",
    )(qb.reshape(batch, seq, B_QW), kd.reshape(batch, seq, 2 * B_KVW), vd.reshape(batch, seq, 2 * B_KVW))
    return out.reshape(batch * seq, B_QW)


def _ab_out_kernel(o1_ref, l1_ref, om_ref, lm_ref, yb_ref, x_ref, w_ref, g_ref, b_ref, out_ref, os_ref, ls_ref):
    for c in range(DIL_CLASSES):
        for g in range(A_W // LANES):
            sl = slice(g * LANES, (g + 1) * LANES)
            rows = pl.ds(c, TM // DIL_CLASSES, stride=DIL_CLASSES)
            os_ref[g, rows, :] = om_ref[0, c, :, sl].astype(F32)
            ls_ref[g, rows, :] = lm_ref[0, c, :, sl]
    ya = []
    for g in range(A_W // LANES):
        sl = slice(g * LANES, (g + 1) * LANES)
        l1, lm = l1_ref[:, sl], ls_ref[g]
        mx = jnp.maximum(l1, lm)
        e1, em = jnp.exp(l1 - mx), jnp.exp(lm - mx)
        ya.append(((e1 * o1_ref[:, sl].astype(F32) + em * os_ref[g]) * (1.0 / (e1 + em))).astype(BF16))
    y = _row_block_dot(jnp.concatenate(ya + [yb_ref[...]], axis=1), w_ref[...])
    out_ref[...] = _layer_norm(ALPHA * x_ref[...] + y, g_ref[...], b_ref[...])


def _ab_out(o1, l1, om, lm, yb, x2d, w, g, b, seq):
    t = x2d.shape[0]
    nt = seq // TM
    row = lambda n: pl.BlockSpec((TM, n), lambda i: (i, 0))
    cls = pl.BlockSpec((1, DIL_CLASSES, TM // DIL_CLASSES, A_W), lambda i: (i // nt, 0, i % nt, 0))
    return pl.pallas_call(
        _ab_out_kernel,
        grid=(t // TM,),
        in_specs=[row(A_W), row(A_W), cls, cls, row(B_QW), row(D_MODEL), _const_spec((A_W + B_QW, D_MODEL)),
                  _const_spec((1, D_MODEL)), _const_spec((1, D_MODEL))],
        out_specs=row(D_MODEL),
        out_shape=jax.ShapeDtypeStruct((t, D_MODEL), F32),
        scratch_shapes=[pltpu.VMEM((A_W // LANES, TM, LANES), F32)] * 2,
        compiler_params=_params(("arbitrary",), VMEM_LIMIT),
        name="ab_out",
    )(o1, l1, om, lm, yb, x2d, w, g, b)


def _gelu(x):
    return 0.5 * x * (1.0 + jnp.tanh(0.7978845608028654 * (x + 0.044715 * (x * x * x))))


def _ffn_kernel(x_ref, xp_ref, xn_ref, wu_ref, cw_ref, cb_ref, wd_ref, g_ref, b_ref, out_ref, *, nt):
    i = pl.program_id(0)
    keep_prev = (i % nt != 0).astype(F32)
    keep_next = (i % nt != nt - 1).astype(F32)
    rows = lax.broadcasted_iota(jnp.int32, (TM, 1), 0)
    offsets = [sum(FF_CHUNKS[:c]) for c in range(len(FF_CHUNKS))]
    xb = x_ref[...].astype(BF16)
    halo = jnp.concatenate([xp_ref[...], xn_ref[...]], axis=0).astype(BF16)
    xe = jnp.concatenate([xb, halo], axis=0)
    acc = []

    def up(c):
        a, n = offsets[c], FF_CHUNKS[c]
        u = _row_block_dot(xb, wu_ref[:, a:a + n], FFN_ROWS)
        ge = _row_block_dot(xe, wu_ref[:, D_FF + a:D_FF + a + n], FFN_ROWS)
        return u, ge

    def finish(c, pending):
        u, ge = pending
        a, n = offsets[c], FF_CHUNKS[c]
        gm = ge[0:TM]
        g_before = ge[TM + 7:TM + 8] * keep_prev
        g_after = ge[TM + 8:TM + 9] * keep_next
        gp = jnp.where(rows == 0, g_before, pltpu.roll(gm, 1, 0))
        gn = jnp.where(rows == TM - 1, g_after, pltpu.roll(gm, TM - 1, 0))
        cw = cw_ref[:, a:a + n]
        gc = gp * cw[0:1] + gm * cw[1:2] + gn * cw[2:3] + cb_ref[:, a:a + n]
        act = (_gelu(gc) * u).astype(BF16)
        part = _row_block_dot(act, wd_ref[a:a + n, :], FFN_ROWS)
        acc[:] = [part if not acc else acc[0] + part]

    _pipelined([(functools.partial(up, c), functools.partial(finish, c)) for c in range(len(FF_CHUNKS))])
    out_ref[...] = _layer_norm(ALPHA * x_ref[...] + acc[0], g_ref[...], b_ref[...])


def _ffn(x2d, wu, cw, cb, wd, g, b, seq):
    t = x2d.shape[0]
    r8 = TM // 8
    return pl.pallas_call(
        functools.partial(_ffn_kernel, nt=seq // TM),
        grid=(t // TM,),
        in_specs=[pl.BlockSpec((TM, D_MODEL), lambda i: (i, 0)),
                  pl.BlockSpec((8, D_MODEL), lambda i: (jnp.maximum(i * r8 - 1, 0), 0)),
                  pl.BlockSpec((8, D_MODEL), lambda i: (jnp.minimum((i + 1) * r8, t // 8 - 1), 0)),
                  _const_spec((D_MODEL, 2 * D_FF), single=True),
                  _const_spec((3, D_FF)), _const_spec((1, D_FF)),
                  _const_spec((D_FF, D_MODEL), single=True),
                  _const_spec((1, D_MODEL)), _const_spec((1, D_MODEL))],
        out_specs=pl.BlockSpec((TM, D_MODEL), lambda i: (i, 0)),
        out_shape=jax.ShapeDtypeStruct((t, D_MODEL), F32),
        compiler_params=_params(("arbitrary",), VMEM_LIMIT),
        name="ffn",
    )(x2d, x2d, x2d, wu, cw, cb, wd, g, b)


def _lower_bound(tbl, layer):
    rows = [tbl[r:r + 1] for r in range(DEPTH)]
    mx = functools.reduce(jnp.maximum, rows)
    es = [jnp.exp(r - mx) for r in rows]
    inv = 1.0 / functools.reduce(lambda a, b: a + b, es)
    ps = [e * inv for e in es]
    return functools.reduce(lambda a, b: a + b, ps[:layer + 1]) - ps[0]


def _c_in_kernel(x_ref, w_ref, lbf_ref, lbb_ref, q_ref, lf_ref, lb_ref, v_ref, g_ref, *, layer):
    half = C_W // 2
    blocks = [slice(r * PROJ_ROWS, (r + 1) * PROJ_ROWS) for r in range(TM // PROJ_ROWS)]
    xbs = [x_ref[rows, :].astype(BF16) for rows in blocks]

    def proj(r, a, n):
        return jnp.dot(xbs[r], w_ref[:, a:a + n], preferred_element_type=F32)

    def silu(z):
        return z * _sigmoid(z)

    lbf = _lower_bound(lbf_ref[...], layer)
    lbb = _lower_bound(lbb_ref[...], layer)

    def store_q(z, rows, sl):
        q_ref[rows, sl] = z.astype(BF16)

    def store_logf(tbl, out_ref):
        def fn(z, rows, sl):
            lb = tbl[:, sl]
            out_ref[rows, sl] = jnp.log(lb + (1.0 - lb) * _sigmoid(z))
        return fn

    def store_silu(out_ref):
        def fn(z, rows, sl):
            out_ref[rows, sl] = silu(z).astype(BF16)
        return fn

    sections = []
    for r, rows in enumerate(blocks):
        for c in range(2):
            a = c * half
            sl = slice(a, a + half)
            for off, fn in ((0, store_q), (C_W, store_logf(lbf, lf_ref)), (2 * C_W, store_logf(lbb, lb_ref)),
                            (3 * C_W, store_silu(v_ref)), (3 * C_W + C_VW, store_silu(g_ref))):
                sections.append((functools.partial(proj, r, off + a, half), functools.partial(fn, rows=rows, sl=sl)))
    _pipelined(sections)


def _c_in(x2d, w, lbf, lbb, layer):
    t = x2d.shape[0]
    row = pl.BlockSpec((TM, C_W), lambda i: (i, 0))
    return pl.pallas_call(
        functools.partial(_c_in_kernel, layer=layer),
        grid=(t // TM,),
        in_specs=[row, _const_spec(w.shape), _const_spec((DEPTH, C_W)), _const_spec((DEPTH, C_W))],
        out_specs=[row] * 5,
        out_shape=[jax.ShapeDtypeStruct((t, C_W), d) for d in (BF16, F32, F32, BF16, BF16)],
        compiler_params=_params(("arbitrary",), VMEM_LIMIT),
        name="c_in",
    )(x2d, w, lbf, lbb)


def _hgrn_kernel(*refs, rev, fused, heads, step_rows):
    if fused:
        (q_ref, lf_ref, v_ref, of_ref, gate_ref, gn_ref, x_ref, w_ref, lg_ref, lb_ref,
         out_ref, st_ref, y_ref) = refs
    else:
        q_ref, lf_ref, v_ref, y_ref, st_ref = refs

    @pl.when(pl.program_id(2) == 0)
    def _():
        st_ref[...] = jnp.zeros(st_ref.shape, F32)

    ri = lax.broadcasted_iota(jnp.int32, (HG_CHUNK, HG_CHUNK), 0)
    ci = lax.broadcasted_iota(jnp.int32, (HG_CHUNK, HG_CHUNK), 1)
    tri = (ci >= ri) if rev else (ci <= ri)
    trib = jnp.where(tri, 1.0, 0.0).astype(BF16)
    dotf = lambda a, b: jnp.dot(a, b, preferred_element_type=F32)
    n_chunks = step_rows // HG_CHUNK
    order = range(n_chunks - 1, -1, -1) if rev else range(n_chunks)
    units = [(c, h) for c in order for h in range(heads)]
    window = lambda u: (0, slice(u[0] * HG_CHUNK, (u[0] + 1) * HG_CHUNK), slice(u[1] * C_KEY, (u[1] + 1) * C_KEY))

    b_all = {}
    for u in units:
        lf = lf_ref[window(u)]
        hi = lf.astype(BF16)
        lo = (lf - hi.astype(F32)).astype(BF16)
        b_all[u] = dotf(trib, hi) + dotf(trib, lo)

    qdec, kdec, etot, a_rows = {}, {}, {}, {}
    for u in units:
        lf, b = lf_ref[window(u)], b_all[u]
        qf = q_ref[window(u)].astype(F32)
        bex = b - lf
        kk = 1.0 - jnp.exp(lf)
        btot = b[0:1] if rev else b[HG_CHUNK - 1:HG_CHUNK]
        qdec[u] = (qf * jnp.exp(b)).astype(BF16)
        kdec[u] = (kk * jnp.exp(btot - b)).astype(BF16)
        etot[u] = jnp.exp(btot)
        rows = []
        for blk in range(HG_CHUNK // HG_SUB):
            r0 = blk * HG_SUB
            ref = bex[r0 + HG_SUB - 1:r0 + HG_SUB] if rev else bex[r0:r0 + 1]
            qt = (qf[r0:r0 + HG_SUB] * jnp.exp(b[r0:r0 + HG_SUB] - ref)).astype(BF16)
            lo_r, hi_r = (r0, HG_CHUNK) if rev else (0, r0 + HG_SUB)
            ks = (kk[lo_r:hi_r] * jnp.exp(ref - b[lo_r:hi_r])).astype(BF16)
            pieces = []
            if lo_r > 0:
                pieces.append(jnp.zeros((lo_r, C_KEY), BF16))
            pieces.append(ks)
            if hi_r < HG_CHUNK:
                pieces.append(jnp.zeros((HG_CHUNK - hi_r, C_KEY), BF16))
            kfull = jnp.concatenate(pieces, axis=0) if len(pieces) > 1 else ks
            rows.append(lax.dot_general(qt, kfull, _NT, preferred_element_type=F32))
        a_rows[u] = rows
    attn = {u: jnp.where(tri, jnp.concatenate(a_rows[u], axis=0), 0.0).astype(BF16) for u in units}

    states = [st_ref[h] for h in range(heads)]
    for u in units:
        h = u[1]
        vb = v_ref[window(u)]
        o = lax.dot_general(qdec[u], states[h].astype(BF16), _NT, preferred_element_type=F32)
        o = o + dotf(attn[u], vb)
        states[h] = states[h] * etot[u] + lax.dot_general(vb, kdec[u], _TN, preferred_element_type=F32)
        if fused:
            tot = of_ref[window(u)] + o
            inv = lax.rsqrt(jnp.mean(tot * tot, -1, keepdims=True) + RMS_EPS)
            y = tot * inv * gn_ref[:, window(u)[2]] * gate_ref[window(u)].astype(F32)
            y_ref[window(u)[1:]] = y.astype(BF16)
        else:
            y_ref[window(u)] = o
    for h in range(heads):
        st_ref[h] = states[h]
    if fused:
        proj = _row_block_dot(y_ref[...], w_ref[...])
        out_ref[0] = _layer_norm(ALPHA * x_ref[0] + proj, lg_ref[...], lb_ref[...])


def _hgrn_specs(batch, seq, rev, heads, step_rows):
    nb = seq // step_rows
    blk = (lambda b, h, i: (b, nb - 1 - i, h)) if rev else (lambda b, h, i: (b, i, h))
    return pl.BlockSpec((1, step_rows, heads * C_KEY), blk), (batch, C_HEADS // heads, nb)


def _hgrn_fwd(q, lf, v, batch, seq):
    tile, grid = _hgrn_specs(batch, seq, False, HG_HP, HG_TB)
    view = lambda t: t.reshape(batch, seq, C_W)
    out = pl.pallas_call(
        functools.partial(_hgrn_kernel, rev=False, fused=False, heads=HG_HP, step_rows=HG_TB),
        grid=grid,
        in_specs=[tile, tile, tile],
        out_specs=tile,
        out_shape=jax.ShapeDtypeStruct((batch, seq, C_VW), F32),
        scratch_shapes=[pltpu.VMEM((HG_HP, C_VAL, C_KEY), F32)],
        compiler_params=_params(("arbitrary",) * 3),
        name="hgrn_fwd",
    )(view(q), view(lf), view(v))
    return out.reshape(batch * seq, C_VW)


def _hgrn_bwd_out(q, lf, v, o_f, gate, gn, x2d, w, g, b, batch, seq):
    tile, grid = _hgrn_specs(batch, seq, True, C_HEADS, HG_TB_OUT)
    view = lambda t: t.reshape(batch, seq, C_W)
    out = pl.pallas_call(
        functools.partial(_hgrn_kernel, rev=True, fused=True, heads=C_HEADS, step_rows=HG_TB_OUT),
        grid=grid,
        in_specs=[tile] * 5 + [_const_spec((1, C_VW)), tile, _const_spec((C_VW, D_MODEL), single=True),
                               _const_spec((1, D_MODEL)), _const_spec((1, D_MODEL))],
        out_specs=tile,
        out_shape=jax.ShapeDtypeStruct((batch, seq, D_MODEL), F32),
        scratch_shapes=[pltpu.VMEM((C_HEADS, C_VAL, C_KEY), F32), pltpu.VMEM((HG_TB_OUT, C_VW), BF16)],
        compiler_params=_params(("arbitrary",) * 3, VMEM_LIMIT),
        name="hgrn_bwd_out",
    )(view(q), view(lf), view(v), view(o_f), view(gate), gn, x2d.reshape(batch, seq, D_MODEL), w, g, b)
    return out.reshape(batch * seq, D_MODEL)


def _prep_weights(w_in_ab, w_out_ab, qn_ab, kn_ab, w_in_c, w_out_c, gn_c, ffn_w_up, ffn_w_down):
    scale = HEAD_DIM ** -0.5
    ab = []
    for j in range(w_in_ab.shape[0]):
        w = w_in_ab[j]
        o3 = 3 * A_W
        kb = w[:, o3 + B_QW:o3 + B_QW + B_KVW]
        vb = w[:, o3 + B_QW + B_KVW:]
        dup = lambda t: jnp.concatenate([t[:, :HEAD_DIM], t[:, :HEAD_DIM], t[:, HEAD_DIM:], t[:, HEAD_DIM:]], 1)
        w_ext = jnp.concatenate([w[:, :A_W] * scale, w[:, A_W:o3 + B_QW], dup(kb), dup(vb)], 1).astype(BF16)
        qg = (jnp.concatenate([qn_ab[j], qn_ab[j]]) * (scale * LOG2E)).reshape(1, LANES)
        kg = jnp.concatenate([kn_ab[j], kn_ab[j]]).reshape(1, LANES)
        ab.append((w_ext, qg, kg, w_out_ab[j].astype(BF16)))
    cc = [(w_in_c[j].astype(BF16), w_out_c[j].astype(BF16), gn_c[j].reshape(1, C_VW)) for j in range(w_in_c.shape[0])]
    return ab, cc, ffn_w_up.astype(BF16), ffn_w_down.astype(BF16)


def _trunk(x, prep, lb_fwd, lb_bwd, ln_mix_g, ln_mix_b, ln_ffn_g, ln_ffn_b, ffn_conv_w, ffn_conv_b):
    ab, cc, wu, wd = prep
    batch, seq, _ = x.shape
    x2d = x.reshape(batch * seq, D_MODEL)
    tabs = _rope_tables(seq)
    blk = jnp.arange(LANES) // HEAD_DIM
    ones2 = (blk[:, None] == blk[None, :]).astype(BF16)
    vec = lambda t: t.reshape(1, -1)
    for l in range(DEPTH):
        j = l // 2
        if l % 2 == 0:
            w_ext, qg, kg, w_out = ab[j]
            assert [d for _, d in A_PATTERNS] == [1, 4, DIL_CLASSES]
            assert all(window // (2 * d) == HALF_WINDOW for window, d in A_PATTERNS)
            qa, ka, va, qb, kd, vd, qa16, ka16, va16 = _ab_in(x2d, w_ext, tabs, qg, kg, ones2, batch, seq)
            o1, l1 = _band1(qa, ka, va, batch, seq)
            o16, l16 = _band16(qa16, ka16, va16, batch, seq)
            om, lm = _band4(qa16, ka16, va16, o16, l16, batch, seq)
            yb = _gqa(qb, kd, vd, batch, seq)
            x2d = _ab_out(o1, l1, om, lm, yb, x2d, w_out, vec(ln_mix_g[l]), vec(ln_mix_b[l]), seq)
        else:
            w_in, w_out, gn = cc[j]
            q, lf, lb, v, gate = _c_in(x2d, w_in, lb_fwd, lb_bwd, l)
            o_f = _hgrn_fwd(q, lf, v, batch, seq)
            x2d = _hgrn_bwd_out(q, lb, v, o_f, gate, gn, x2d, w_out, vec(ln_mix_g[l]), vec(ln_mix_b[l]), batch, seq)
        x2d = _ffn(x2d, wu[l], ffn_conv_w[l], vec(ffn_conv_b[l]), wd[l], vec(ln_ffn_g[l]), vec(ln_ffn_b[l]), seq)
    return x2d.reshape(batch, seq, D_MODEL)


def kernel(x_prompt, x_sample, w_in_ab, w_out_ab, qn_ab, kn_ab, w_in_c, w_out_c, lb_fwd, lb_bwd, gn_c, ln_mix_g, ln_mix_b, ln_ffn_g, ln_ffn_b, ffn_w_up, ffn_conv_w, ffn_conv_b, ffn_w_down):
    prep = _prep_weights(w_in_ab, w_out_ab, qn_ab, kn_ab, w_in_c, w_out_c, gn_c, ffn_w_up, ffn_w_down)
    rest = (lb_fwd, lb_bwd, ln_mix_g, ln_mix_b, ln_ffn_g, ln_ffn_b, ffn_conv_w, ffn_conv_b)
    return (_trunk(x_prompt, prep, *rest), _trunk(x_sample, prep, *rest))
```

```python
import functools

import jax
import jax.numpy as jnp
from jax import lax
from jax.experimental import pallas as pl
from jax.experimental.pallas import tpu as pltpu

F32 = jnp.float32
BF16 = jnp.bfloat16

D_MODEL = 1024
DEPTH = 2
HEAD_DIM = 64
A_HEADS = 8
A_PATTERNS = ((128, 1), (512, 4), (2048, 16))
B_HEADS = 8
B_KV_HEADS = 2
GRID_W = 64
ROPE_THETA = 500000.0
ROPE_DIM = HEAD_DIM // 4
AXIAL_THETA = 10000.0
C_HEADS = 8
C_KEY = 128
C_VAL = 128
D_FF = 2816
ALPHA = (2 * DEPTH) ** 0.25
LN_EPS = 1e-5
RMS_EPS = 1e-6
A_W = A_HEADS * HEAD_DIM
B_QW = B_HEADS * HEAD_DIM
B_KVW = B_KV_HEADS * HEAD_DIM
C_W = C_HEADS * C_KEY
C_VW = C_HEADS * C_VAL

LANES = 128
HALF_WINDOW = 64
TM = 512
PROJ_ROWS = 128
FFN_ROWS = 256
BAND_L = 128
BAND_STEP = 1024
BAND4_STEP = 256
DIL_CLASSES = 16
BAND4_ROWS = 32
BAND4_HALO = 16
GQA_TQ = 2048
GQA_TK = 512
GQA_CHAIN = 512
GQA_V_ROWS = 80
FF_CHUNKS = (768, 768, 768, 512)
HG_CHUNK = 64
HG_SUB = 16
HG_TB = 1024
HG_HP = 4
HG_TB_OUT = 512
NEG = -1e30
LOG2E = 1.4426950408889634
VMEM_LIMIT = 56 * 1024 * 1024

_NT = (((1,), (1,)), ((), ()))
_TN = (((0,), (0,)), ((), ()))


def _params(sem, vmem=None):
    return pltpu.CompilerParams(dimension_semantics=sem, vmem_limit_bytes=vmem)


def _const_spec(shape, single=False):
    nd = len(shape)
    if single:
        return pl.BlockSpec(shape, lambda *_: (0,) * nd, pipeline_mode=pl.Buffered(1))
    return pl.BlockSpec(shape, lambda *_: (0,) * nd)


def _layer_norm(z, g, b):
    mu = jnp.mean(z, -1, keepdims=True)
    d = z - mu
    var = jnp.mean(d * d, -1, keepdims=True)
    return d * lax.rsqrt(var + LN_EPS) * g + b


def _sigmoid(z):
    return 1.0 / (1.0 + jnp.exp(-z))


def _row_block_dot(x, w, rows=PROJ_ROWS):
    m = x.shape[0]
    starts = list(range(0, m - m % rows, rows)) or [0]
    ends = starts[1:] + [m]
    parts = [jnp.dot(x[a:b], w, preferred_element_type=F32) for a, b in zip(starts, ends)]
    return parts[0] if len(parts) == 1 else jnp.concatenate(parts, axis=0)


def _pipelined(sections):
    pending = sections[0][0]()
    for i, (_, epilogue) in enumerate(sections):
        upcoming = sections[i + 1][0]() if i + 1 < len(sections) else None
        epilogue(pending)
        pending = upcoming


def _rope_tables(seq):
    pos = jnp.arange(seq, dtype=F32)
    d = jnp.arange(LANES) % HEAD_DIM

    def angles(p, dim, theta):
        freqs = theta ** (-(jnp.arange(0, dim, 2, dtype=F32) / dim))
        return p[:, None] * freqs[None, :]

    h = ROPE_DIM // 2
    ang = angles(pos, ROPE_DIM, ROPE_THETA)
    a = ang[:, d % h]
    lo, hi = d < h, (d >= h) & (d < ROPE_DIM)
    pc = jnp.where(lo | hi, jnp.cos(a), 1.0)
    psa = jnp.where(lo, -jnp.sin(a), 0.0)
    psb = jnp.where(hi, jnp.sin(a), 0.0)

    q = HEAD_DIM // 4
    row = jnp.floor(pos / GRID_W)
    col = pos - row * GRID_W
    ar = angles(row, HEAD_DIM // 2, AXIAL_THETA)[:, d % q]
    ac_ = angles(col, HEAD_DIM // 2, AXIAL_THETA)[:, d % q]
    a2 = jnp.where(d < HEAD_DIM // 2, ar, ac_)
    first = (d % (HEAD_DIM // 2)) < q
    ac = jnp.cos(a2)
    asa = jnp.where(first, -jnp.sin(a2), 0.0)
    asb = jnp.where(first, 0.0, jnp.sin(a2))
    return [t.astype(F32) for t in (pc, psa, psb, ac, asa, asb)]


def _rope(seg, c, sa, sb, shift):
    return seg * c + pltpu.roll(seg, LANES - shift, 1) * sa + pltpu.roll(seg, shift, 1) * sb


def _ab_in_kernel(x_ref, w_ref, pc_ref, psa_ref, psb_ref, ac_ref, asa_ref, asb_ref, qg_ref, kg_ref,
                  ones_ref, qa_ref, ka_ref, va_ref, qb_ref, kd_ref, vd_ref, qa16_ref, ka16_ref, va16_ref,
                  qs_ref, ks_ref, vs_ref):
    xb = x_ref[...].astype(BF16)

    def proj(a, n):
        return jnp.dot(xb, w_ref[:, a:a + n], preferred_element_type=F32)

    pc, psa, psb = pc_ref[...], psa_ref[...], psb_ref[...]
    ac, asa, asb = ac_ref[...], asa_ref[...], asb_ref[...]

    def dilated_a(h, rope, out_ref, out16_ref, slab_ref):
        for g in range(A_W // LANES):
            sl = slice(g * LANES, (g + 1) * LANES)
            y = _rope(h[:, sl], pc, psa, psb, ROPE_DIM // 2) if rope else h[:, sl]
            out_ref[:, sl] = y.astype(BF16)
            slab_ref[g] = y
        for c in range(DIL_CLASSES):
            for g in range(A_W // LANES):
                rows = slab_ref[g, pl.ds(c, TM // DIL_CLASSES, stride=DIL_CLASSES), :]
                out16_ref[0, c, :, g * LANES:(g + 1) * LANES] = rows.astype(BF16)

    def norm_rope(h, gain, out_ref):
        for g in range(h.shape[1] // LANES):
            sl = slice(g * LANES, (g + 1) * LANES)
            seg = h[:, sl]
            sq = seg * seg
            hi = sq.astype(BF16)
            lo = (sq - hi.astype(F32)).astype(BF16)
            ss = (jnp.dot(hi, ones_ref[...], preferred_element_type=F32)
                  + jnp.dot(lo, ones_ref[...], preferred_element_type=F32))
            y = seg * lax.rsqrt(ss * (1.0 / HEAD_DIM) + RMS_EPS) * gain
            out_ref[:, sl] = _rope(y, ac, asa, asb, HEAD_DIM // 4).astype(BF16)

    def cast_to(out_ref):
        def fn(h):
            out_ref[...] = h.astype(BF16)
        return fn

    o_qb = 3 * A_W
    o_kd = o_qb + B_QW
    o_vd = o_kd + 2 * B_KVW
    _pipelined([
        (functools.partial(proj, 0, A_W),
         functools.partial(dilated_a, rope=True, out_ref=qa_ref, out16_ref=qa16_ref, slab_ref=qs_ref)),
        (functools.partial(proj, A_W, A_W),
         functools.partial(dilated_a, rope=True, out_ref=ka_ref, out16_ref=ka16_ref, slab_ref=ks_ref)),
        (functools.partial(proj, 2 * A_W, A_W),
         functools.partial(dilated_a, rope=False, out_ref=va_ref, out16_ref=va16_ref, slab_ref=vs_ref)),
        (functools.partial(proj, o_qb, B_QW), functools.partial(norm_rope, gain=qg_ref[...], out_ref=qb_ref)),
        (functools.partial(proj, o_kd, 2 * B_KVW), functools.partial(norm_rope, gain=kg_ref[...], out_ref=kd_ref)),
        (functools.partial(proj, o_vd, 2 * B_KVW), cast_to(vd_ref)),
    ])


def _ab_in(x2d, w_ext, tabs, qg, kg, ones2, batch, seq):
    t = x2d.shape[0]
    nt = seq // TM
    tab = pl.BlockSpec((TM, LANES), lambda i: (i % nt, 0))
    row = lambda n: pl.BlockSpec((TM, n), lambda i: (i, 0))
    cls = pl.BlockSpec((1, DIL_CLASSES, TM // DIL_CLASSES, A_W), lambda i: (i // nt, 0, i % nt, 0))
    cls_shape = jax.ShapeDtypeStruct((batch, DIL_CLASSES, seq // DIL_CLASSES, A_W), BF16)
    wn = w_ext.shape[1]
    return pl.pallas_call(
        _ab_in_kernel,
        grid=(t // TM,),
        in_specs=[row(D_MODEL), _const_spec((D_MODEL, wn))] + [tab] * 6
        + [_const_spec((1, LANES)), _const_spec((1, LANES)), _const_spec((LANES, LANES))],
        out_specs=[row(A_W)] * 3 + [row(B_QW), row(2 * B_KVW), row(2 * B_KVW)] + [cls] * 3,
        out_shape=[jax.ShapeDtypeStruct((t, A_W), BF16)] * 3
        + [jax.ShapeDtypeStruct((t, B_QW), BF16)]
        + [jax.ShapeDtypeStruct((t, 2 * B_KVW), BF16)] * 2 + [cls_shape] * 3,
        scratch_shapes=[pltpu.VMEM((A_W // LANES, TM, LANES), F32)] * 3,
        compiler_params=_params(("arbitrary",), VMEM_LIMIT),
        name="ab_in",
    )(x2d, w_ext, *tabs, qg, kg, ones2)


def _band_groups(qs, ks, vs, bias, left):
    chains = [(g, sel) for g in range(len(qs)) for sel in (left, jnp.logical_not(left))]
    scores = []
    for g, sel in chains:
        qm = jnp.where(sel, qs[g], jnp.zeros_like(qs[g]))
        scores.append(lax.dot_general(qm, ks[g], _NT, preferred_element_type=F32) + bias)
    probs = []
    for s in scores:
        mx = jnp.max(s, axis=1, keepdims=True)
        pe = jnp.exp(s - mx)
        probs.append((mx, jnp.sum(pe, axis=1, keepdims=True), pe.astype(BF16)))
    outs = []
    for (g, _), (mx, l, pe) in zip(chains, probs):
        o = jnp.dot(pe, vs[g], preferred_element_type=F32)
        outs.append((o * (1.0 / l), mx + jnp.log(l)))
    return [(jnp.where(left, outs[2 * g][0], outs[2 * g + 1][0]), jnp.where(left, outs[2 * g][1], outs[2 * g + 1][1]))
            for g in range(len(qs))]


def _band_kernel(q_ref, kp_ref, kc_ref, kn_ref, vp_ref, vc_ref, vn_ref, o_ref, lse_ref, *, m, step_rows):
    nk = BAND_L + 2 * HALF_WINDOW
    base = pl.program_id(len(q_ref.shape) - 2) * step_rows
    r = lax.broadcasted_iota(jnp.int32, (BAND_L, nk), 0)
    j = lax.broadcasted_iota(jnp.int32, (BAND_L, nk), 1)
    band = jnp.where(jnp.abs(j - HALF_WINDOW - r) <= HALF_WINDOW, 0.0, NEG)
    left = lax.broadcasted_iota(jnp.int32, (BAND_L, LANES), 1) < HEAD_DIM
    pre = (0,) * (len(q_ref.shape) - 2)
    lanes = [slice(p * LANES, (p + 1) * LANES) for p in range(A_W // LANES)]
    at = lambda rows, sl: pre + (rows, sl)
    whole = slice(None)
    ks = [jnp.concatenate([kp_ref[at(whole, sl)], kc_ref[at(whole, sl)], kn_ref[at(whole, sl)]], axis=0) for sl in lanes]
    vs = [jnp.concatenate([vp_ref[at(whole, sl)], vc_ref[at(whole, sl)], vn_ref[at(whole, sl)]], axis=0) for sl in lanes]
    for blk in range(step_rows // BAND_L):
        r0 = blk * BAND_L
        rows = slice(r0, r0 + BAND_L)
        kpos = base + r0 - HALF_WINDOW + j
        bias = jnp.where(kpos >= 0, band, NEG)
        bias = jnp.where(kpos < m, bias, NEG)
        res = _band_groups([q_ref[at(rows, sl)] for sl in lanes], [kk[r0:r0 + nk] for kk in ks],
                           [vv[r0:r0 + nk] for vv in vs], bias, left)
        for sl, (o, lse) in zip(lanes, res):
            o_ref[at(rows, sl)] = o.astype(BF16)
            lse_ref[at(rows, sl)] = lse


def _band_specs(lead, rows, step_rows):
    nh = rows // HALF_WINDOW
    ratio = step_rows // HALF_WINDOW
    ones = (1,) * lead
    cur = pl.BlockSpec(ones + (step_rows, A_W), lambda *g: g[:lead] + (g[-1], 0))
    prev = pl.BlockSpec(ones + (HALF_WINDOW, A_W), lambda *g: g[:lead] + (jnp.maximum(g[-1] * ratio - 1, 0), 0))
    nxt = pl.BlockSpec(ones + (HALF_WINDOW, A_W), lambda *g: g[:lead] + (jnp.minimum((g[-1] + 1) * ratio, nh - 1), 0))
    return cur, prev, nxt


def _band1(q, k, v, batch, seq):
    step_rows = min(BAND_STEP, seq)
    cur, prev, nxt = _band_specs(1, seq, step_rows)
    view = lambda t: t.reshape(batch, seq, A_W)
    o, lse = pl.pallas_call(
        functools.partial(_band_kernel, m=seq, step_rows=step_rows),
        grid=(batch, seq // step_rows),
        in_specs=[cur, prev, cur, nxt, prev, cur, nxt],
        out_specs=[cur, cur],
        out_shape=[jax.ShapeDtypeStruct((batch, seq, A_W), BF16), jax.ShapeDtypeStruct((batch, seq, A_W), F32)],
        compiler_params=_params(("arbitrary",) * 2),
        name="band_d1",
    )(view(q), view(k), view(k), view(k), view(v), view(v), view(v))
    return o.reshape(batch * seq, A_W), lse.reshape(batch * seq, A_W)


def _band16(q16, k16, v16, batch, seq):
    m = seq // DIL_CLASSES
    step_rows = min(BAND_STEP, m)
    cur, prev, nxt = _band_specs(2, m, step_rows)
    shape = (batch, DIL_CLASSES, m, A_W)
    return pl.pallas_call(
        functools.partial(_band_kernel, m=m, step_rows=step_rows),
        grid=(batch, DIL_CLASSES, m // step_rows),
        in_specs=[cur, prev, cur, nxt, prev, cur, nxt],
        out_specs=[cur, cur],
        out_shape=[jax.ShapeDtypeStruct(shape, BF16), jax.ShapeDtypeStruct(shape, F32)],
        compiler_params=_params(("arbitrary",) * 3),
        name="band_d16",
    )(q16, k16, k16, k16, v16, v16, v16)


def _band4_kernel(q_ref, kp_ref, kc_ref, kn_ref, vp_ref, vc_ref, vn_ref, o16_ref, l16_ref, o_ref, lse_ref, *,
                  m16, step_rows):
    qr = BAND4_ROWS
    kr = qr + 2 * BAND4_HALO
    i = pl.program_id(2)
    shift_q, shift_k = qr.bit_length() - 1, kr.bit_length() - 1
    row = lax.broadcasted_iota(jnp.int32, (4 * qr, 4 * kr), 0)
    col = lax.broadcasted_iota(jnp.int32, (4 * qr, 4 * kr), 1)
    uq, rq = row >> shift_q, row & (qr - 1)
    uk, jk = col >> shift_k, col & (kr - 1)
    band = jnp.where(jnp.abs(4 * (jk - BAND4_HALO - rq) + (uk - uq)) <= HALF_WINDOW, 0.0, NEG)
    left = lax.broadcasted_iota(jnp.int32, (4 * qr, LANES), 1) < HEAD_DIM
    slabs = [slice(p * LANES, (p + 1) * LANES) for p in range(A_W // LANES)]
    window = lambda refs, u, sl: jnp.concatenate([r[0, u, 0, :, sl] for r in refs], axis=0)
    kwin = [[window([kp_ref, kc_ref, kn_ref], u, sl) for u in range(4)] for sl in slabs]
    vwin = [[window([vp_ref, vc_ref, vn_ref], u, sl) for u in range(4)] for sl in slabs]
    for blk in range(step_rows // qr):
        r0 = blk * qr
        rows = slice(r0, r0 + qr)
        kpos = i * step_rows + r0 - BAND4_HALO + jk
        bias = jnp.where(kpos >= 0, band, NEG)
        bias = jnp.where(kpos < m16, bias, NEG)
        classes = lambda ref, sl: jnp.concatenate([ref[0, u, 0, rows, sl] for u in range(4)], axis=0)
        branch4 = _band_groups([classes(q_ref, sl) for sl in slabs],
                               [jnp.concatenate([w[r0:r0 + kr] for w in kw], axis=0) for kw in kwin],
                               [jnp.concatenate([w[r0:r0 + kr] for w in vw], axis=0) for vw in vwin], bias, left)
        for sl, (o4, l4) in zip(slabs, branch4):
            o16 = classes(o16_ref, sl).astype(F32)
            l16 = classes(l16_ref, sl)
            mx = jnp.maximum(l4, l16)
            w4, w16 = jnp.exp(l4 - mx), jnp.exp(l16 - mx)
            tot = w4 + w16
            om = (w4 * o4 + w16 * o16) * (1.0 / tot)
            lm = mx + jnp.log(tot)
            for u in range(4):
                o_ref[0, u, 0, rows, sl] = om[u * qr:(u + 1) * qr].astype(BF16)
                lse_ref[0, u, 0, rows, sl] = lm[u * qr:(u + 1) * qr]


def _band4(q16, k16, v16, o16, l16, batch, seq):
    m16 = seq // DIL_CLASSES
    step_rows = min(BAND4_STEP, m16)
    ratio = step_rows // BAND4_HALO
    nh = m16 // BAND4_HALO
    view = lambda t: t.reshape(batch, 4, 4, m16, A_W)
    cur = pl.BlockSpec((1, 4, 1, step_rows, A_W), lambda b, c, i: (b, 0, c, i, 0))
    prev = pl.BlockSpec((1, 4, 1, BAND4_HALO, A_W), lambda b, c, i: (b, 0, c, jnp.maximum(i * ratio - 1, 0), 0))
    nxt = pl.BlockSpec((1, 4, 1, BAND4_HALO, A_W), lambda b, c, i: (b, 0, c, jnp.minimum((i + 1) * ratio, nh - 1), 0))
    shape = (batch, 4, 4, m16, A_W)
    o, lse = pl.pallas_call(
        functools.partial(_band4_kernel, m16=m16, step_rows=step_rows),
        grid=(batch, 4, m16 // step_rows),
        in_specs=[cur, prev, cur, nxt, prev, cur, nxt, cur, cur],
        out_specs=[cur, cur],
        out_shape=[jax.ShapeDtypeStruct(shape, BF16), jax.ShapeDtypeStruct(shape, F32)],
        compiler_params=_params(("arbitrary",) * 3),
        name="band_d4",
    )(view(q16), view(k16), view(k16), view(k16), view(v16), view(v16), view(v16), view(o16), view(l16))
    return o.reshape(batch, DIL_CLASSES, m16, A_W), lse.reshape(batch, DIL_CLASSES, m16, A_W)


def _gqa_kernel(q_ref, k_ref, v_ref, o_ref, qs_ref, m_ref, acc_ref, s_ref, vt_ref, *, seq):
    for pr in range(2):
        qt = q_ref[0, :, pr * LANES:(pr + 1) * LANES].astype(F32).T
        qs_ref[:, (2 * pr) * GQA_TQ:(2 * pr + 1) * GQA_TQ] = qt[0:HEAD_DIM].astype(BF16)
        qs_ref[:, (2 * pr + 1) * GQA_TQ:(2 * pr + 2) * GQA_TQ] = qt[HEAD_DIM:2 * HEAD_DIM].astype(BF16)
    m_ref[...] = jnp.full(m_ref.shape, NEG, F32)
    acc_ref[...] = jnp.zeros(acc_ref.shape, F32)
    n_tiles = seq // GQA_TK

    @pl.when(pl.program_id(2) == 0)
    def _():
        ones_row = lax.broadcasted_iota(jnp.int32, (GQA_V_ROWS, GQA_TK), 0) == HEAD_DIM

        def transpose_tile(t, carry):
            off = pl.multiple_of(t * GQA_TK, GQA_TK)
            vt = v_ref[0, pl.ds(off, GQA_TK), :].astype(F32).T[0:GQA_V_ROWS]
            vt_ref[t] = jnp.where(ones_row, 1.0, vt).astype(BF16)
            return carry

        lax.fori_loop(0, n_tiles, transpose_tile, 0)

    chains = [slice(c * GQA_CHAIN, (c + 1) * GQA_CHAIN) for c in range(4 * GQA_TQ // GQA_CHAIN)]

    def scores(tile, buf, cols):
        off = pl.multiple_of(tile * GQA_TK, GQA_TK)
        k = k_ref[0, pl.ds(off, GQA_TK), 0:HEAD_DIM]
        s_ref[buf, :, cols] = jnp.dot(k, qs_ref[:, cols], preferred_element_type=F32)

    def softmax_pv(tile, buf, cols):
        s = s_ref[buf, :, cols]
        m_prev = m_ref[:, cols]
        m_new = jnp.maximum(m_prev, jnp.max(s, axis=0, keepdims=True))
        alpha = jnp.exp2(m_prev - m_new)
        p = jnp.exp2((s - m_new).astype(BF16))
        pv = jnp.dot(vt_ref[tile], p, preferred_element_type=F32)
        acc_ref[:, cols] = alpha * acc_ref[:, cols] + pv
        m_ref[:, cols] = m_new

    def step(tile, buf, prefetch):
        for cols in chains:
            if prefetch:
                scores(tile + 1, 1 - buf, cols)
            softmax_pv(tile, buf, cols)

    def pair(j, carry):
        step(2 * j, 0, True)
        step(2 * j + 1, 1, True)
        return carry

    for cols in chains:
        scores(0, 0, cols)
    lax.fori_loop(0, n_tiles // 2 - 1, pair, 0)
    step(n_tiles - 2, 0, True)
    step(n_tiles - 1, 1, False)
    acc = acc_ref[...]
    a = acc[0:HEAD_DIM] * (1.0 / acc[HEAD_DIM:HEAD_DIM + 1, :])
    for pr in range(2):
        two_heads = a[:, (2 * pr) * GQA_TQ:(2 * pr + 2) * GQA_TQ]
        stacked = jnp.concatenate([two_heads[:, :GQA_TQ], two_heads[:, GQA_TQ:]], axis=0)
        o_ref[0, :, pr * LANES:(pr + 1) * LANES] = stacked.T.astype(BF16)


def _gqa(qb, kd, vd, batch, seq):
    gw = B_QW // B_KV_HEADS
    cols = 4 * GQA_TQ
    out = pl.pallas_call(
        functools.partial(_gqa_kernel, seq=seq),
        grid=(batch, B_KV_HEADS, seq // GQA_TQ),
        in_specs=[pl.BlockSpec((1, GQA_TQ, gw), lambda b, g, qi: (b, qi, g)),
                  pl.BlockSpec((1, seq, LANES), lambda b, g, qi: (b, 0, g)),
                  pl.BlockSpec((1, seq, LANES), lambda b, g, qi: (b, 0, g))],
        out_specs=pl.BlockSpec((1, GQA_TQ, gw), lambda b, g, qi: (b, qi, g)),
        out_shape=jax.ShapeDtypeStruct((batch, seq, B_QW), BF16),
        scratch_shapes=[pltpu.VMEM((HEAD_DIM, cols), BF16), pltpu.VMEM((1, cols), F32),
                        pltpu.VMEM((GQA_V_ROWS, cols), F32), pltpu.VMEM((2, GQA_TK, cols), F32),
                        pltpu.VMEM((seq // GQA_TK, GQA_V_ROWS, GQA_TK), BF16)],
        compiler_params=_params(("arbitrary",) * 3, VMEM_LIMIT),
        name="gqa",
    )(qb.reshape(batch, seq, B_QW), kd.reshape(batch, seq, 2 * B_KVW), vd.reshape(batch, seq, 2 * B_KVW))
    return out.reshape(batch * seq, B_QW)


def _ab_out_kernel(o1_ref, l1_ref, om_ref, lm_ref, yb_ref, x_ref, w_ref, g_ref, b_ref, out_ref, os_ref, ls_ref):
    for c in range(DIL_CLASSES):
        for g in range(A_W // LANES):
            sl = slice(g * LANES, (g + 1) * LANES)
            rows = pl.ds(c, TM // DIL_CLASSES, stride=DIL_CLASSES)
            os_ref[g, rows, :] = om_ref[0, c, :, sl].astype(F32)
            ls_ref[g, rows, :] = lm_ref[0, c, :, sl]
    ya = []
    for g in range(A_W // LANES):
        sl = slice(g * LANES, (g + 1) * LANES)
        l1, lm = l1_ref[:, sl], ls_ref[g]
        mx = jnp.maximum(l1, lm)
        e1, em = jnp.exp(l1 - mx), jnp.exp(lm - mx)
        ya.append(((e1 * o1_ref[:, sl].astype(F32) + em * os_ref[g]) * (1.0 / (e1 + em))).astype(BF16))
    y = _row_block_dot(jnp.concatenate(ya + [yb_ref[...]], axis=1), w_ref[...])
    out_ref[...] = _layer_norm(ALPHA * x_ref[...] + y, g_ref[...], b_ref[...])


def _ab_out(o1, l1, om, lm, yb, x2d, w, g, b, seq):
    t = x2d.shape[0]
    nt = seq // TM
    row = lambda n: pl.BlockSpec((TM, n), lambda i: (i, 0))
    cls = pl.BlockSpec((1, DIL_CLASSES, TM // DIL_CLASSES, A_W), lambda i: (i // nt, 0, i % nt, 0))
    return pl.pallas_call(
        _ab_out_kernel,
        grid=(t // TM,),
        in_specs=[row(A_W), row(A_W), cls, cls, row(B_QW), row(D_MODEL), _const_spec((A_W + B_QW, D_MODEL)),
                  _const_spec((1, D_MODEL)), _const_spec((1, D_MODEL))],
        out_specs=row(D_MODEL),
        out_shape=jax.ShapeDtypeStruct((t, D_MODEL), F32),
        scratch_shapes=[pltpu.VMEM((A_W // LANES, TM, LANES), F32)] * 2,
        compiler_params=_params(("arbitrary",), VMEM_LIMIT),
        name="ab_out",
    )(o1, l1, om, lm, yb, x2d, w, g, b)


def _gelu(x):
    return 0.5 * x * (1.0 + jnp.tanh(0.7978845608028654 * (x + 0.044715 * (x * x * x))))


def _ffn_kernel(x_ref, xp_ref, xn_ref, wu_ref, cw_ref, cb_ref, wd_ref, g_ref, b_ref, out_ref, *, nt):
    i = pl.program_id(0)
    keep_prev = (i % nt != 0).astype(F32)
    keep_next = (i % nt != nt - 1).astype(F32)
    rows = lax.broadcasted_iota(jnp.int32, (TM, 1), 0)
    offsets = [sum(FF_CHUNKS[:c]) for c in range(len(FF_CHUNKS))]
    xb = x_ref[...].astype(BF16)
    halo = jnp.concatenate([xp_ref[...], xn_ref[...]], axis=0).astype(BF16)
    xe = jnp.concatenate([xb, halo], axis=0)
    acc = []

    def up(c):
        a, n = offsets[c], FF_CHUNKS[c]
        u = _row_block_dot(xb, wu_ref[:, a:a + n], FFN_ROWS)
        ge = _row_block_dot(xe, wu_ref[:, D_FF + a:D_FF + a + n], FFN_ROWS)
        return u, ge

    def finish(c, pending):
        u, ge = pending
        a, n = offsets[c], FF_CHUNKS[c]
        gm = ge[0:TM]
        g_before = ge[TM + 7:TM + 8] * keep_prev
        g_after = ge[TM + 8:TM + 9] * keep_next
        gp = jnp.where(rows == 0, g_before, pltpu.roll(gm, 1, 0))
        gn = jnp.where(rows == TM - 1, g_after, pltpu.roll(gm, TM - 1, 0))
        cw = cw_ref[:, a:a + n]
        gc = gp * cw[0:1] + gm * cw[1:2] + gn * cw[2:3] + cb_ref[:, a:a + n]
        act = (_gelu(gc) * u).astype(BF16)
        part = _row_block_dot(act, wd_ref[a:a + n, :], FFN_ROWS)
        acc[:] = [part if not acc else acc[0] + part]

    _pipelined([(functools.partial(up, c), functools.partial(finish, c)) for c in range(len(FF_CHUNKS))])
    out_ref[...] = _layer_norm(ALPHA * x_ref[...] + acc[0], g_ref[...], b_ref[...])


def _ffn(x2d, wu, cw, cb, wd, g, b, seq):
    t = x2d.shape[0]
    r8 = TM // 8
    return pl.pallas_call(
        functools.partial(_ffn_kernel, nt=seq // TM),
        grid=(t // TM,),
        in_specs=[pl.BlockSpec((TM, D_MODEL), lambda i: (i, 0)),
                  pl.BlockSpec((8, D_MODEL), lambda i: (jnp.maximum(i * r8 - 1, 0), 0)),
                  pl.BlockSpec((8, D_MODEL), lambda i: (jnp.minimum((i + 1) * r8, t // 8 - 1), 0)),
                  _const_spec((D_MODEL, 2 * D_FF), single=True),
                  _const_spec((3, D_FF)), _const_spec((1, D_FF)),
                  _const_spec((D_FF, D_MODEL), single=True),
                  _const_spec((1, D_MODEL)), _const_spec((1, D_MODEL))],
        out_specs=pl.BlockSpec((TM, D_MODEL), lambda i: (i, 0)),
        out_shape=jax.ShapeDtypeStruct((t, D_MODEL), F32),
        compiler_params=_params(("arbitrary",), VMEM_LIMIT),
        name="ffn",
    )(x2d, x2d, x2d, wu, cw, cb, wd, g, b)


def _lower_bound(tbl, layer):
    rows = [tbl[r:r + 1] for r in range(DEPTH)]
    mx = functools.reduce(jnp.maximum, rows)
    es = [jnp.exp(r - mx) for r in rows]
    inv = 1.0 / functools.reduce(lambda a, b: a + b, es)
    ps = [e * inv for e in es]
    return functools.reduce(lambda a, b: a + b, ps[:layer + 1]) - ps[0]


def _c_in_kernel(x_ref, w_ref, lbf_ref, lbb_ref, q_ref, lf_ref, lb_ref, v_ref, g_ref, *, layer):
    half = C_W // 2
    blocks = [slice(r * PROJ_ROWS, (r + 1) * PROJ_ROWS) for r in range(TM // PROJ_ROWS)]
    xbs = [x_ref[rows, :].astype(BF16) for rows in blocks]

    def proj(r, a, n):
        return jnp.dot(xbs[r], w_ref[:, a:a + n], preferred_element_type=F32)

    def silu(z):
        return z * _sigmoid(z)

    lbf = _lower_bound(lbf_ref[...], layer)
    lbb = _lower_bound(lbb_ref[...], layer)

    def store_q(z, rows, sl):
        q_ref[rows, sl] = z.astype(BF16)

    def store_logf(tbl, out_ref):
        def fn(z, rows, sl):
            lb = tbl[:, sl]
            out_ref[rows, sl] = jnp.log(lb + (1.0 - lb) * _sigmoid(z))
        return fn

    def store_silu(out_ref):
        def fn(z, rows, sl):
            out_ref[rows, sl] = silu(z).astype(BF16)
        return fn

    sections = []
    for r, rows in enumerate(blocks):
        for c in range(2):
            a = c * half
            sl = slice(a, a + half)
            for off, fn in ((0, store_q), (C_W, store_logf(lbf, lf_ref)), (2 * C_W, store_logf(lbb, lb_ref)),
                            (3 * C_W, store_silu(v_ref)), (3 * C_W + C_VW, store_silu(g_ref))):
                sections.append((functools.partial(proj, r, off + a, half), functools.partial(fn, rows=rows, sl=sl)))
    _pipelined(sections)


def _c_in(x2d, w, lbf, lbb, layer):
    t = x2d.shape[0]
    row = pl.BlockSpec((TM, C_W), lambda i: (i, 0))
    return pl.pallas_call(
        functools.partial(_c_in_kernel, layer=layer),
        grid=(t // TM,),
        in_specs=[row, _const_spec(w.shape), _const_spec((DEPTH, C_W)), _const_spec((DEPTH, C_W))],
        out_specs=[row] * 5,
        out_shape=[jax.ShapeDtypeStruct((t, C_W), d) for d in (BF16, F32, F32, BF16, BF16)],
        compiler_params=_params(("arbitrary",), VMEM_LIMIT),
        name="c_in",
    )(x2d, w, lbf, lbb)


def _hgrn_kernel(*refs, rev, fused, heads, step_rows):
    if fused:
        (q_ref, lf_ref, v_ref, of_ref, gate_ref, gn_ref, x_ref, w_ref, lg_ref, lb_ref,
         out_ref, st_ref, y_ref) = refs
    else:
        q_ref, lf_ref, v_ref, y_ref, st_ref = refs

    @pl.when(pl.program_id(2) == 0)
    def _():
        st_ref[...] = jnp.zeros(st_ref.shape, F32)

    ri = lax.broadcasted_iota(jnp.int32, (HG_CHUNK, HG_CHUNK), 0)
    ci = lax.broadcasted_iota(jnp.int32, (HG_CHUNK, HG_CHUNK), 1)
    tri = (ci >= ri) if rev else (ci <= ri)
    trib = jnp.where(tri, 1.0, 0.0).astype(BF16)
    dotf = lambda a, b: jnp.dot(a, b, preferred_element_type=F32)
    n_chunks = step_rows // HG_CHUNK
    order = range(n_chunks - 1, -1, -1) if rev else range(n_chunks)
    units = [(c, h) for c in order for h in range(heads)]
    window = lambda u: (0, slice(u[0] * HG_CHUNK, (u[0] + 1) * HG_CHUNK), slice(u[1] * C_KEY, (u[1] + 1) * C_KEY))

    b_all = {}
    for u in units:
        lf = lf_ref[window(u)]
        hi = lf.astype(BF16)
        lo = (lf - hi.astype(F32)).astype(BF16)
        b_all[u] = dotf(trib, hi) + dotf(trib, lo)

    qdec, kdec, etot, a_rows = {}, {}, {}, {}
    for u in units:
        lf, b = lf_ref[window(u)], b_all[u]
        qf = q_ref[window(u)].astype(F32)
        bex = b - lf
        kk = 1.0 - jnp.exp(lf)
        btot = b[0:1] if rev else b[HG_CHUNK - 1:HG_CHUNK]
        qdec[u] = (qf * jnp.exp(b)).astype(BF16)
        kdec[u] = (kk * jnp.exp(btot - b)).astype(BF16)
        etot[u] = jnp.exp(btot)
        rows = []
        for blk in range(HG_CHUNK // HG_SUB):
            r0 = blk * HG_SUB
            ref = bex[r0 + HG_SUB - 1:r0 + HG_SUB] if rev else bex[r0:r0 + 1]
            qt = (qf[r0:r0 + HG_SUB] * jnp.exp(b[r0:r0 + HG_SUB] - ref)).astype(BF16)
            lo_r, hi_r = (r0, HG_CHUNK) if rev else (0, r0 + HG_SUB)
            ks = (kk[lo_r:hi_r] * jnp.exp(ref - b[lo_r:hi_r])).astype(BF16)
            pieces = []
            if lo_r > 0:
                pieces.append(jnp.zeros((lo_r, C_KEY), BF16))
            pieces.append(ks)
            if hi_r < HG_CHUNK:
                pieces.append(jnp.zeros((HG_CHUNK - hi_r, C_KEY), BF16))
            kfull = jnp.concatenate(pieces, axis=0) if len(pieces) > 1 else ks
            rows.append(lax.dot_general(qt, kfull, _NT, preferred_element_type=F32))
        a_rows[u] = rows
    attn = {u: jnp.where(tri, jnp.concatenate(a_rows[u], axis=0), 0.0).astype(BF16) for u in units}

    states = [st_ref[h] for h in range(heads)]
    for u in units:
        h = u[1]
        vb = v_ref[window(u)]
        o = lax.dot_general(qdec[u], states[h].astype(BF16), _NT, preferred_element_type=F32)
        o = o + dotf(attn[u], vb)
        states[h] = states[h] * etot[u] + lax.dot_general(vb, kdec[u], _TN, preferred_element_type=F32)
        if fused:
            tot = of_ref[window(u)] + o
            inv = lax.rsqrt(jnp.mean(tot * tot, -1, keepdims=True) + RMS_EPS)
            y = tot * inv * gn_ref[:, window(u)[2]] * gate_ref[window(u)].astype(F32)
            y_ref[window(u)[1:]] = y.astype(BF16)
        else:
            y_ref[window(u)] = o
    for h in range(heads):
        st_ref[h] = states[h]
    if fused:
        proj = _row_block_dot(y_ref[...], w_ref[...])
        out_ref[0] = _layer_norm(ALPHA * x_ref[0] + proj, lg_ref[...], lb_ref[...])


def _hgrn_specs(batch, seq, rev, heads, step_rows):
    nb = seq // step_rows
    blk = (lambda b, h, i: (b, nb - 1 - i, h)) if rev else (lambda b, h, i: (b, i, h))
    return pl.BlockSpec((1, step_rows, heads * C_KEY), blk), (batch, C_HEADS // heads, nb)


def _hgrn_fwd(q, lf, v, batch, seq):
    tile, grid = _hgrn_specs(batch, seq, False, HG_HP, HG_TB)
    view = lambda t: t.reshape(batch, seq, C_W)
    out = pl.pallas_call(
        functools.partial(_hgrn_kernel, rev=False, fused=False, heads=HG_HP, step_rows=HG_TB),
        grid=grid,
        in_specs=[tile, tile, tile],
        out_specs=tile,
        out_shape=jax.ShapeDtypeStruct((batch, seq, C_VW), F32),
        scratch_shapes=[pltpu.VMEM((HG_HP, C_VAL, C_KEY), F32)],
        compiler_params=_params(("arbitrary",) * 3),
        name="hgrn_fwd",
    )(view(q), view(lf), view(v))
    return out.reshape(batch * seq, C_VW)


def _hgrn_bwd_out(q, lf, v, o_f, gate, gn, x2d, w, g, b, batch, seq):
    tile, grid = _hgrn_specs(batch, seq, True, C_HEADS, HG_TB_OUT)
    view = lambda t: t.reshape(batch, seq, C_W)
    out = pl.pallas_call(
        functools.partial(_hgrn_kernel, rev=True, fused=True, heads=C_HEADS, step_rows=HG_TB_OUT),
        grid=grid,
        in_specs=[tile] * 5 + [_const_spec((1, C_VW)), tile, _const_spec((C_VW, D_MODEL), single=True),
                               _const_spec((1, D_MODEL)), _const_spec((1, D_MODEL))],
        out_specs=tile,
        out_shape=jax.ShapeDtypeStruct((batch, seq, D_MODEL), F32),
        scratch_shapes=[pltpu.VMEM((C_HEADS, C_VAL, C_KEY), F32), pltpu.VMEM((HG_TB_OUT, C_VW), BF16)],
        compiler_params=_params(("arbitrary",) * 3, VMEM_LIMIT),
        name="hgrn_bwd_out",
    )(view(q), view(lf), view(v), view(o_f), view(gate), gn, x2d.reshape(batch, seq, D_MODEL), w, g, b)
    return out.reshape(batch * seq, D_MODEL)


def _prep_weights(w_in_ab, w_out_ab, qn_ab, kn_ab, w_in_c, w_out_c, gn_c, ffn_w_up, ffn_w_down):
    scale = HEAD_DIM ** -0.5
    ab = []
    for j in range(w_in_ab.shape[0]):
        w = w_in_ab[j]
        o3 = 3 * A_W
        kb = w[:, o3 + B_QW:o3 + B_QW + B_KVW]
        vb = w[:, o3 + B_QW + B_KVW:]
        dup = lambda t: jnp.concatenate([t[:, :HEAD_DIM], t[:, :HEAD_DIM], t[:, HEAD_DIM:], t[:, HEAD_DIM:]], 1)
        w_ext = jnp.concatenate([w[:, :A_W] * scale, w[:, A_W:o3 + B_QW], dup(kb), dup(vb)], 1).astype(BF16)
        qg = (jnp.concatenate([qn_ab[j], qn_ab[j]]) * (scale * LOG2E)).reshape(1, LANES)
        kg = jnp.concatenate([kn_ab[j], kn_ab[j]]).reshape(1, LANES)
        ab.append((w_ext, qg, kg, w_out_ab[j].astype(BF16)))
    cc = [(w_in_c[j].astype(BF16), w_out_c[j].astype(BF16), gn_c[j].reshape(1, C_VW)) for j in range(w_in_c.shape[0])]
    return ab, cc, ffn_w_up.astype(BF16), ffn_w_down.astype(BF16)


def _trunk(x, prep, lb_fwd, lb_bwd, ln_mix_g, ln_mix_b, ln_ffn_g, ln_ffn_b, ffn_conv_w, ffn_conv_b):
    ab, cc, wu, wd = prep
    batch, seq, _ = x.shape
    x2d = x.reshape(batch * seq, D_MODEL)
    tabs = _rope_tables(seq)
    blk = jnp.arange(LANES) // HEAD_DIM
    ones2 = (blk[:, None] == blk[None, :]).astype(BF16)
    vec = lambda t: t.reshape(1, -1)
    for l in range(DEPTH):
        j = l // 2
        if l % 2 == 0:
            w_ext, qg, kg, w_out = ab[j]
            assert [d for _, d in A_PATTERNS] == [1, 4, DIL_CLASSES]
            assert all(window // (2 * d) == HALF_WINDOW for window, d in A_PATTERNS)
            qa, ka, va, qb, kd, vd, qa16, ka16, va16 = _ab_in(x2d, w_ext, tabs, qg, kg, ones2, batch, seq)
            o1, l1 = _band1(qa, ka, va, batch, seq)
            o16, l16 = _band16(qa16, ka16, va16, batch, seq)
            om, lm = _band4(qa16, ka16, va16, o16, l16, batch, seq)
            yb = _gqa(qb, kd, vd, batch, seq)
            x2d = _ab_out(o1, l1, om, lm, yb, x2d, w_out, vec(ln_mix_g[l]), vec(ln_mix_b[l]), seq)
        else:
            w_in, w_out, gn = cc[j]
            q, lf, lb, v, gate = _c_in(x2d, w_in, lb_fwd, lb_bwd, l)
            o_f = _hgrn_fwd(q, lf, v, batch, seq)
            x2d = _hgrn_bwd_out(q, lb, v, o_f, gate, gn, x2d, w_out, vec(ln_mix_g[l]), vec(ln_mix_b[l]), batch, seq)
        x2d = _ffn(x2d, wu[l], ffn_conv_w[l], vec(ffn_conv_b[l]), wd[l], vec(ln_ffn_g[l]), vec(ln_ffn_b[l]), seq)
    return x2d.reshape(batch, seq, D_MODEL)


def kernel(x_prompt, x_sample, w_in_ab, w_out_ab, qn_ab, kn_ab, w_in_c, w_out_c, lb_fwd, lb_bwd, gn_c, ln_mix_g, ln_mix_b, ln_ffn_g, ln_ffn_b, ffn_w_up, ffn_conv_w, ffn_conv_b, ffn_w_down):
    prep = _prep_weights(w_in_ab, w_out_ab, qn_ab, kn_ab, w_in_c, w_out_c, gn_c, ffn_w_up, ffn_w_down)
    rest = (lb_fwd, lb_bwd, ln_mix_g, ln_mix_b, ln_ffn_g, ln_ffn_b, ffn_conv_w, ffn_conv_b)
    return (_trunk(x_prompt, prep, *rest), _trunk(x_sample, prep, *rest))
```
